```python
import jax, jax.numpy as jnp
from jax import lax
import numpy as np

D_MODEL = 4096
BATCH = 4
SEQ = 2048
DEPTH = 1

CHUNK = 64
LEFT_CHUNKS = 8
BAND = (LEFT_CHUNKS + 1) * CHUNK
ATTN_HEAD_DIM = 128
ATTN_HEADS = (D_MODEL // 2) // ATTN_HEAD_DIM
ATTN_WIDTH = ATTN_HEADS * ATTN_HEAD_DIM
MAX_REL_DIST = 256
RET_V_DIM = 256
RET_HEADS = (D_MODEL // 2) // RET_V_DIM
RET_QK_DIM = RET_V_DIM // 2
RET_QK_WIDTH = RET_HEADS * RET_QK_DIM
RET_WIDTH = RET_HEADS * RET_V_DIM
ROPE_BASE = 10000.0
MIX_WIDTH = ATTN_WIDTH + RET_WIDTH
IN_SIZES = (ATTN_WIDTH, ATTN_WIDTH, ATTN_WIDTH, RET_QK_WIDTH, RET_QK_WIDTH, RET_WIDTH, RET_WIDTH)
IN_WIDTH = 3 * ATTN_WIDTH + 2 * RET_QK_WIDTH + 2 * RET_WIDTH
MEM_LEN = 256
XATTN_HEADS = 4
XATTN_HEAD_DIM = D_MODEL // XATTN_HEADS
PEER_HEADS = 8
PEER_KEYS = 128
PEER_EXPERTS = PEER_KEYS * PEER_KEYS
PEER_QUERY_DIM = 256
PEER_HALF = PEER_QUERY_DIM // 2
PEER_TOPK = 16
PEER_TOKEN_BLOCK = 64

EPS = 1e-6
NEG_INF = -1e30

kernel_name = "hybrid_chunkattn_retention_peer_block"


def rms_norm(x, w):
    xf = x.astype(jnp.float32)
    y = xf * lax.rsqrt(jnp.mean(xf * xf, axis=-1, keepdims=True) + EPS)
    return (y * w.astype(jnp.float32)).astype(x.dtype)


def split_columns(t, sizes):
    parts, start = [], 0
    for size in sizes:
        parts.append(t[..., start:start + size])
        start += size
    return parts


def rotary(x, pos):
    d = x.shape[-1]
    inv_freq = 1.0 / (ROPE_BASE ** (jnp.arange(0, d, 2, dtype=jnp.float32) / d))
    ang = pos.astype(jnp.float32)[:, None] * inv_freq[None, :]
    cos = jnp.cos(ang)[None, :, None, :]
    sin = jnp.sin(ang)[None, :, None, :]
    xf = x.astype(jnp.float32)
    x1, x2 = xf[..., : d // 2], xf[..., d // 2:]
    return jnp.concatenate([x1 * cos - x2 * sin, x1 * sin + x2 * cos], axis=-1).astype(x.dtype)


def chunked_rel_attention(q, k, v, rel_bias):
    B, S, H, dh = q.shape
    nc = S // CHUNK

    def to_chunks(t):
        return t.reshape(B, nc, CHUNK, H, dh).transpose(0, 3, 1, 2, 4)

    pad = ((0, 0), (0, 0), (LEFT_CHUNKS, 0), (0, 0), (0, 0))
    qc = to_chunks(q) * (dh ** -0.5)
    kp = jnp.pad(to_chunks(k), pad)
    vp = jnp.pad(to_chunks(v), pad)
    qi = jnp.arange(CHUNK)[:, None]
    band = jnp.arange(BAND)[None, :]
    dist = LEFT_CHUNKS * CHUNK + qi - band
    bias = rel_bias[:, jnp.clip(dist, -MAX_REL_DIST, MAX_REL_DIST) + MAX_REL_DIST]
    src_chunk = jnp.arange(nc)[:, None] - LEFT_CHUNKS + jnp.arange(LEFT_CHUNKS + 1)[None, :]
    valid = jnp.repeat(src_chunk >= 0, CHUNK, axis=1)
    scores = jnp.concatenate(
        [jnp.einsum('bhncd,bhnmd->bhncm', qc, kp[:, :, j:j + nc]) for j in range(LEFT_CHUNKS + 1)],
        axis=-1).astype(jnp.float32)
    scores = scores + bias[None, :, None].astype(jnp.float32)
    scores = jnp.where(valid[None, None, :, None, :], scores, NEG_INF)
    probs = jax.nn.softmax(scores, axis=-1).astype(v.dtype)
    out = jnp.einsum('bhncm,bhnmd->bhncd', probs[..., :CHUNK], vp[:, :, 0:nc])
    for j in range(1, LEFT_CHUNKS + 1):
        out = out + jnp.einsum('bhncm,bhnmd->bhncd',
                               probs[..., j * CHUNK:(j + 1) * CHUNK], vp[:, :, j:j + nc])
    return out.transpose(0, 2, 3, 1, 4).reshape(B, S, H * dh)


def chunkwise_retention(q, k, v, gate, gn_w):
    B, S, H, dk = q.shape
    dv = v.shape[-1]
    nc = S // CHUNK
    dt = q.dtype
    log_gamma = jnp.log1p(-jnp.power(2.0, -5.0 - jnp.arange(H, dtype=jnp.float32)))
    pos = jnp.arange(CHUNK, dtype=jnp.float32)
    inner_decay = jnp.exp(log_gamma[:, None, None] * jnp.abs(pos[:, None] - pos[None, :]))
    q_decay = jnp.exp(log_gamma[:, None] * (pos + 1.0))
    k_decay = jnp.exp(log_gamma[:, None] * (CHUNK - 1.0 - pos))
    chunk_decay = jnp.exp(log_gamma * CHUNK)[None, :, None, None]

    qc = q.reshape(B, nc, CHUNK, H, dk).transpose(0, 3, 1, 2, 4)
    kc = k.reshape(B, nc, CHUNK, H, dk).transpose(0, 3, 1, 2, 4) * (dk ** -0.5)
    vc = v.reshape(B, nc, CHUNK, H, dv).transpose(0, 3, 1, 2, 4)

    inner = jnp.einsum('bhncd,bhnmd->bhncm', qc, kc) * inner_decay[None, :, None].astype(dt)
    inner = jnp.einsum('bhncm,bhnme->bhnce', inner, vc)
    kv = jnp.einsum('bhncd,bhnce->nbhde',
                    kc * k_decay[None, :, None, :, None].astype(dt), vc)

    def step(state, kv_n):
        return state * chunk_decay + kv_n, state

    _, states = lax.scan(step, jnp.zeros((B, H, dk, dv), jnp.float32), kv.astype(jnp.float32))
    cross = jnp.einsum('bhncd,nbhde->bhnce',
                       qc * q_decay[None, :, None, :, None].astype(dt), states.astype(dt))
    y = (inner + cross).transpose(0, 2, 3, 1, 4).astype(jnp.float32)
    mu = jnp.mean(y, axis=-1, keepdims=True)
    var = jnp.mean(jnp.square(y - mu), axis=-1, keepdims=True)
    yn = ((y - mu) * lax.rsqrt(var + EPS)).reshape(B, S, H * dv) * gn_w.astype(jnp.float32)
    return (jax.nn.silu(gate.astype(jnp.float32)) * yn).astype(dt)


def memory_cross_attention(h, mem_n, wq, wkv, wo):
    B, S, D = h.shape
    M = mem_n.shape[1]
    q = (h @ wq).reshape(B, S, XATTN_HEADS, XATTN_HEAD_DIM)
    k, v = split_columns(mem_n @ wkv, (D, D))
    k = k.reshape(B, M, XATTN_HEADS, XATTN_HEAD_DIM)
    v = v.reshape(B, M, XATTN_HEADS, XATTN_HEAD_DIM)
    s = jnp.einsum('bshd,bmhd->bhsm', q, k).astype(jnp.float32) * (XATTN_HEAD_DIM ** -0.5)
    p = jax.nn.softmax(s, axis=-1).astype(v.dtype)
    o = jnp.einsum('bhsm,bmhd->bshd', p, v).reshape(B, S, D)
    return o @ wo


def peer_ffn(h, wq, subkeys, u, v):
    B, S, D = h.shape
    T = B * S
    x = h.reshape(T, D)
    q = (x @ wq).reshape(T, PEER_HEADS, 2, PEER_HALF)
    s = jnp.einsum('thpd,hpkd->thpk', q, subkeys).astype(jnp.float32)
    top_s, top_i = lax.top_k(s, PEER_TOPK)
    cand_s = top_s[:, :, 0, :, None] + top_s[:, :, 1, None, :]
    cand_i = top_i[:, :, 0, :, None] * PEER_KEYS + top_i[:, :, 1, None, :]
    best_s, best_c = lax.top_k(cand_s.reshape(T, PEER_HEADS, PEER_TOPK * PEER_TOPK), PEER_TOPK)
    expert_idx = jnp.take_along_axis(
        cand_i.reshape(T, PEER_HEADS, PEER_TOPK * PEER_TOPK), best_c, axis=-1)
    gates = jax.nn.softmax(best_s, axis=-1).astype(h.dtype)
    nb = T // PEER_TOKEN_BLOCK
    xb = x.reshape(nb, PEER_TOKEN_BLOCK, D)
    ib = expert_idx.reshape(nb, PEER_TOKEN_BLOCK, PEER_HEADS * PEER_TOPK)
    gb = gates.reshape(nb, PEER_TOKEN_BLOCK, PEER_HEADS * PEER_TOPK)

    def block(args):
        xt, idx, g = args
        act = jax.nn.gelu(jnp.einsum('td,tkd->tk', xt, u[idx]), approximate=False)
        return jnp.einsum('tk,tkd->td', g * act, v[idx])

    out = lax.map(block, (xb, ib, gb))
    return out.reshape(B, S, D)


def setup_inputs(seed: int = 0) -> dict:
    key = jax.random.key(seed)
    ks = jax.random.split(key, 18)
    f32 = jnp.float32
    L = DEPTH

    def nrm(k, shape, scale):
        return jax.random.normal(k, shape, f32) * scale

    def gain(k, shape):
        return 1.0 + 0.01 * jax.random.normal(k, shape, f32)

    return {
        "x": nrm(ks[0], (BATCH, SEQ, D_MODEL), 1.0),
        "mem": nrm(ks[1], (BATCH, MEM_LEN, D_MODEL), 1.0),
        "norm1_w": gain(ks[2], (L, D_MODEL)),
        "w_in": nrm(ks[3], (L, D_MODEL, IN_WIDTH), D_MODEL ** -0.5),
        "attn_rel_bias": nrm(ks[4], (L, ATTN_HEADS, 2 * MAX_REL_DIST + 1), 0.1),
        "ret_gn_w": gain(ks[5], (L, RET_WIDTH)),
        "w_out": nrm(ks[6], (L, MIX_WIDTH, D_MODEL), MIX_WIDTH ** -0.5),
        "norm2_w": gain(ks[7], (L, D_MODEL)),
        "mem_norm_w": gain(ks[8], (L, D_MODEL)),
        "xattn_wq": nrm(ks[9], (L, D_MODEL, D_MODEL), D_MODEL ** -0.5),
        "xattn_wkv": nrm(ks[10], (L, D_MODEL, 2 * D_MODEL), D_MODEL ** -0.5),
        "xattn_wo": nrm(ks[11], (L, D_MODEL, D_MODEL), D_MODEL ** -0.5),
        "norm3_w": gain(ks[12], (L, D_MODEL)),
        "peer_wq": nrm(ks[13], (L, D_MODEL, PEER_HEADS * PEER_QUERY_DIM), D_MODEL ** -0.5),
        "peer_subkeys": nrm(ks[14], (L, PEER_HEADS, 2, PEER_KEYS, PEER_HALF), PEER_HALF ** -0.5),
        "peer_u": nrm(ks[15], (L, PEER_EXPERTS, D_MODEL), D_MODEL ** -0.5),
        "peer_v": nrm(ks[16], (L, PEER_EXPERTS, D_MODEL), PEER_HEADS ** -0.5),
        "final_norm_w": gain(ks[17], (D_MODEL,)),
    }


def reference(x, mem, norm1_w, w_in, attn_rel_bias, ret_gn_w, w_out, norm2_w, mem_norm_w,
              xattn_wq, xattn_wkv, xattn_wo, norm3_w, peer_wq, peer_subkeys, peer_u, peer_v,
              final_norm_w):
    B, S, _ = x.shape
    pos = jnp.arange(S)
    h = x
    for l in range(DEPTH):
        xn = rms_norm(h, norm1_w[l])
        aq, ak, av, rq, rk, rv, rg = split_columns(xn @ w_in[l], IN_SIZES)
        a_out = chunked_rel_attention(
            aq.reshape(B, S, ATTN_HEADS, ATTN_HEAD_DIM),
            ak.reshape(B, S, ATTN_HEADS, ATTN_HEAD_DIM),
            av.reshape(B, S, ATTN_HEADS, ATTN_HEAD_DIM),
            attn_rel_bias[l])
        r_out = chunkwise_retention(
            rotary(rq.reshape(B, S, RET_HEADS, RET_QK_DIM), pos),
            rotary(rk.reshape(B, S, RET_HEADS, RET_QK_DIM), pos),
            rv.reshape(B, S, RET_HEADS, RET_V_DIM),
            rg, ret_gn_w[l])
        h = h + jnp.concatenate([a_out, r_out], axis=-1) @ w_out[l]
        mem_n = rms_norm(mem, mem_norm_w[l])
        h = h + memory_cross_attention(rms_norm(h, norm2_w[l]), mem_n,
                                       xattn_wq[l], xattn_wkv[l], xattn_wo[l])
        h = h + peer_ffn(rms_norm(h, norm3_w[l]), peer_wq[l], peer_subkeys[l], peer_u[l], peer_v[l])
    return rms_norm(h, final_norm_w)
```

```python
import functools

import numpy as np
import jax
import jax.numpy as jnp
from jax import lax
from jax.experimental import pallas as pl
from jax.experimental.pallas import tpu as pltpu

F32 = jnp.float32
BF16 = jnp.bfloat16

CHUNK = 64
LEFT_CHUNKS = 8
LEFT = LEFT_CHUNKS * CHUNK
ATTN_HEAD_DIM = 128
MAX_REL_DIST = 256
RET_V_DIM = 256
RET_QK_DIM = 128
ROPE_BASE = 10000.0
XATTN_HEADS = 4
PEER_HEADS = 8
PEER_KEYS = 128
PEER_HALF = 128
PEER_TOPK = 16
EPS = 1e-6
NEG_INF = -1e30
NOT_RANKED = 1e9

VMEM_LIMIT_BYTES = 56 * 1024 * 1024
LANES = 128

_NT = (((1,), (1,)), ((), ()))
_TN = (((0,), (0,)), ((), ()))


def _params(n_grid_dims):
    return pltpu.CompilerParams(
        dimension_semantics=("arbitrary",) * n_grid_dims,
        vmem_limit_bytes=VMEM_LIMIT_BYTES)


def _rmsnorm_kernel(x_ref, w_ref, o_ref):
    x = x_ref[...]
    ms = jnp.mean(x * x, axis=-1, keepdims=True)
    o_ref[...] = (x * lax.rsqrt(ms + EPS) * w_ref[...]).astype(o_ref.dtype)


def _rmsnorm(x, w, out_dtype, rows=256):
    m, d = x.shape
    return pl.pallas_call(
        _rmsnorm_kernel,
        grid=(m // rows,),
        in_specs=[pl.BlockSpec((rows, d), lambda i: (i, 0)),
                  pl.BlockSpec((1, d), lambda i: (0, 0))],
        out_specs=pl.BlockSpec((rows, d), lambda i: (i, 0)),
        out_shape=jax.ShapeDtypeStruct((m, d), out_dtype),
        compiler_params=_params(1),
        name="rmsnorm",
    )(x, w.reshape(1, d))


def _add_rmsnorm_kernel(x_ref, y_ref, w_ref, o_ref):
    x = x_ref[...] + y_ref[...].astype(F32)
    ms = jnp.mean(x * x, axis=-1, keepdims=True)
    o_ref[...] = (x * lax.rsqrt(ms + EPS) * w_ref[...]).astype(o_ref.dtype)


def _add_rmsnorm(x, y, w, out_dtype, rows=256):
    m, d = x.shape
    return pl.pallas_call(
        _add_rmsnorm_kernel,
        grid=(m // rows,),
        in_specs=[pl.BlockSpec((rows, d), lambda i: (i, 0)),
                  pl.BlockSpec((rows, d), lambda i: (i, 0)),
                  pl.BlockSpec((1, d), lambda i: (0, 0))],
        out_specs=pl.BlockSpec((rows, d), lambda i: (i, 0)),
        out_shape=jax.ShapeDtypeStruct((m, d), out_dtype),
        compiler_params=_params(1),
        name="add_rmsnorm",
    )(x, y, w.reshape(1, d))


def _mm_kernel(a_ref, w_ref, o_ref):
    acc = jnp.dot(a_ref[...], w_ref[...].astype(BF16), preferred_element_type=F32)
    o_ref[...] = acc.astype(o_ref.dtype)


def _mm_res_kernel(a_ref, w_ref, r_ref, o_ref):
    acc = jnp.dot(a_ref[...], w_ref[...].astype(BF16), preferred_element_type=F32)
    o_ref[...] = (r_ref[...] + acc).astype(o_ref.dtype)


def _matmul(a, w, out_dtype, residual=None, tm=1024, tn=512, name="matmul"):
    m, k = a.shape
    n = w.shape[1]
    tm, tn = min(tm, m), min(tn, n)
    in_specs = [pl.BlockSpec((tm, k), lambda i, j: (i, 0)),
                pl.BlockSpec((k, tn), lambda i, j: (0, j))]
    args = [a, w]
    body = _mm_kernel
    if residual is not None:
        in_specs.append(pl.BlockSpec((tm, tn), lambda i, j: (i, j)))
        args.append(residual)
        body = _mm_res_kernel
    return pl.pallas_call(
        body,
        grid=(m // tm, n // tn),
        in_specs=in_specs,
        out_specs=pl.BlockSpec((tm, tn), lambda i, j: (i, j)),
        out_shape=jax.ShapeDtypeStruct((m, n), out_dtype),
        compiler_params=_params(2),
        name=name,
    )(*args)


def _attn_kernel(q_ref, k_ref, v_ref, bias_ref, o_ref, *, qb, scale):
    seq = q_ref.shape[0]
    for i in range(seq // qb):
        q0 = i * qb
        k0 = max(0, q0 - LEFT)
        kw = q0 + qb - k0
        c0 = k0 - (q0 - LEFT)
        s = lax.dot_general(q_ref[q0:q0 + qb, :], k_ref[k0:k0 + kw, :], _NT,
                            preferred_element_type=F32)
        s = s * scale + bias_ref[:, c0:c0 + kw]
        m = jnp.max(s, axis=-1, keepdims=True)
        p = jnp.exp(s - m)
        l = jnp.sum(p, axis=-1, keepdims=True)
        o = jnp.dot(p.astype(BF16), v_ref[k0:k0 + kw, :], preferred_element_type=F32)
        o_ref[q0:q0 + qb, :] = (o / l).astype(o_ref.dtype)


def _attn_bias_tile(rel_bias, qb):
    q = jnp.arange(qb)[:, None]
    c = jnp.arange(LEFT + qb)[None, :]
    dist = q - c + LEFT
    in_band = (c // CHUNK >= q // CHUNK) & (c // CHUNK <= q // CHUNK + LEFT_CHUNKS)
    b = rel_bias[:, jnp.clip(dist, -MAX_REL_DIST, MAX_REL_DIST) + MAX_REL_DIST]
    return jnp.where(in_band[None], b, NEG_INF).astype(F32)


def _chunk_attention(proj, rel_bias, n_heads, qb=256):
    b, s, _ = proj.shape
    dh = ATTN_HEAD_DIM
    bias = _attn_bias_tile(rel_bias, qb)
    kern = functools.partial(_attn_kernel, qb=qb, scale=dh ** -0.5)
    return pl.pallas_call(
        kern,
        grid=(b, n_heads),
        in_specs=[pl.BlockSpec((None, s, dh), lambda bi, h: (bi, 0, h)),
                  pl.BlockSpec((None, s, dh), lambda bi, h: (bi, 0, n_heads + h)),
                  pl.BlockSpec((None, s, dh), lambda bi, h: (bi, 0, 2 * n_heads + h)),
                  pl.BlockSpec((None, qb, LEFT + qb), lambda bi, h: (h, 0, 0))],
        out_specs=pl.BlockSpec((None, s, dh), lambda bi, h: (bi, 0, h)),
        out_shape=jax.ShapeDtypeStruct((b, s, n_heads * dh), BF16),
        compiler_params=_params(2),
        name="chunk_attention",
    )(proj, proj, proj, bias)


def _ret_kernel(q_ref, k_ref, v_ref, g_ref, cos_ref, sin_ref, dec_ref, qd_ref, kd_ref, bd_ref,
                gnw_ref, o_ref, state_ref, *, rb, scale):
    seq = q_ref.shape[0]
    state_ref[...] = jnp.zeros_like(state_ref)
    half = q_ref.shape[1] // 2

    def body(n, carry):
        r0 = pl.multiple_of(n * rb, rb)
        rows = pl.ds(r0, rb)
        cos = cos_ref[rows, :]
        sin = sin_ref[rows, :]
        q = q_ref[rows, :].astype(F32)
        k = k_ref[rows, :].astype(F32)
        q = q * cos + pltpu.roll(q, half, 1) * sin
        k = (k * cos + pltpu.roll(k, half, 1) * sin) * scale
        v = v_ref[rows, :]
        a = lax.dot_general(q.astype(BF16), k.astype(BF16), _NT, preferred_element_type=F32)
        a = a * dec_ref[...]
        st = state_ref[...]
        y = jnp.dot(a.astype(BF16), v, preferred_element_type=F32)
        y = y + jnp.dot((q * qd_ref[...]).astype(BF16), st.astype(BF16),
                        preferred_element_type=F32)
        kd = (k * kd_ref[...]).astype(BF16)
        kv = lax.dot_general(kd, v, _TN, preferred_element_type=F32)
        state_ref[...] = st * bd_ref[...] + kv
        mu = jnp.mean(y, axis=-1, keepdims=True)
        yc = y - mu
        var = jnp.mean(yc * yc, axis=-1, keepdims=True)
        yn = yc * lax.rsqrt(var + EPS) * gnw_ref[...]
        g = g_ref[rows, :].astype(F32)
        o_ref[rows, :] = (g * (1.0 / (1.0 + jnp.exp(-g))) * yn).astype(o_ref.dtype)
        return carry

    lax.fori_loop(0, seq // rb, body, 0)


def _retention(proj, gn_w, n_heads, col0, rb=256):
    b, s, _ = proj.shape
    dk, dv = RET_QK_DIM, RET_V_DIM
    qk0 = col0 // dk
    v0 = (col0 + 2 * n_heads * dk) // dv
    inv_freq = 1.0 / (ROPE_BASE ** (jnp.arange(0, dk, 2, dtype=F32) / dk))
    ang = jnp.arange(s, dtype=F32)[:, None] * inv_freq[None, :]
    cos = jnp.concatenate([jnp.cos(ang), jnp.cos(ang)], axis=-1)
    sin = jnp.concatenate([-jnp.sin(ang), jnp.sin(ang)], axis=-1)
    log_gamma = jnp.log1p(-jnp.power(2.0, -5.0 - jnp.arange(n_heads, dtype=F32)))
    pos = jnp.arange(rb, dtype=F32)
    chunk_of = jnp.arange(rb) // CHUNK
    causal = (chunk_of[None, :] <= chunk_of[:, None]).astype(F32)
    dec = jnp.exp(log_gamma[:, None, None] * jnp.abs(pos[:, None] - pos[None, :])) * causal[None]
    qd = jnp.exp(log_gamma[:, None] * (pos + 1.0))[:, :, None]
    kd = jnp.exp(log_gamma[:, None] * (rb - 1.0 - pos))[:, :, None]
    bd = jnp.exp(log_gamma * rb)[:, None, None]
    kern = functools.partial(_ret_kernel, rb=rb, scale=dk ** -0.5)
    return pl.pallas_call(
        kern,
        grid=(b, n_heads),
        in_specs=[pl.BlockSpec((None, s, dk), lambda bi, h: (bi, 0, qk0 + h)),
                  pl.BlockSpec((None, s, dk), lambda bi, h: (bi, 0, qk0 + n_heads + h)),
                  pl.BlockSpec((None, s, dv), lambda bi, h: (bi, 0, v0 + h)),
                  pl.BlockSpec((None, s, dv), lambda bi, h: (bi, 0, v0 + n_heads + h)),
                  pl.BlockSpec((s, dk), lambda bi, h: (0, 0)),
                  pl.BlockSpec((s, dk), lambda bi, h: (0, 0)),
                  pl.BlockSpec((None, rb, rb), lambda bi, h: (h, 0, 0)),
                  pl.BlockSpec((None, rb, 1), lambda bi, h: (h, 0, 0)),
                  pl.BlockSpec((None, rb, 1), lambda bi, h: (h, 0, 0)),
                  pl.BlockSpec((None, 1, 1), lambda bi, h: (h, 0, 0)),
                  pl.BlockSpec((1, dv), lambda bi, h: (0, h))],
        out_specs=pl.BlockSpec((None, s, dv), lambda bi, h: (bi, 0, h)),
        out_shape=jax.ShapeDtypeStruct((b, s, n_heads * dv), BF16),
        scratch_shapes=[pltpu.VMEM((dk, dv), F32)],
        compiler_params=_params(2),
        name="retention",
    )(proj, proj, proj, proj, cos, sin, dec, qd, kd, bd, gn_w.reshape(1, -1))


def _xattn_kernel(q_ref, kv_ref, o_ref, *, n_heads, scale):
    d = q_ref.shape[1]
    dh = d // n_heads
    for h in range(n_heads):
        cols = slice(h * dh, (h + 1) * dh)
        s = lax.dot_general(q_ref[:, cols], kv_ref[:, cols], _NT, preferred_element_type=F32)
        s = s * scale
        m = jnp.max(s, axis=-1, keepdims=True)
        p = jnp.exp(s - m)
        l = jnp.sum(p, axis=-1, keepdims=True)
        o = jnp.dot(p.astype(BF16), kv_ref[:, d + h * dh:d + (h + 1) * dh],
                    preferred_element_type=F32)
        o_ref[:, cols] = (o / l).astype(o_ref.dtype)


def _cross_attention(q, kv, n_heads, tq=512):
    b, s, d = q.shape
    m = kv.shape[1]
    kern = functools.partial(_xattn_kernel, n_heads=n_heads, scale=(d // n_heads) ** -0.5)
    return pl.pallas_call(
        kern,
        grid=(b, s // tq),
        in_specs=[pl.BlockSpec((None, tq, d), lambda bi, i: (bi, i, 0)),
                  pl.BlockSpec((None, m, 2 * d), lambda bi, i: (bi, 0, 0))],
        out_specs=pl.BlockSpec((None, tq, d), lambda bi, i: (bi, i, 0)),
        out_shape=jax.ShapeDtypeStruct((b, s, d), BF16),
        compiler_params=_params(2),
        name="cross_attention",
    )(q, kv)


def _top_rows(s, k):
    n_rows = s.shape[0]
    rows = lax.broadcasted_iota(jnp.int32, s.shape, 0)
    work = s
    rank = jnp.full(s.shape, NOT_RANKED, F32)
    vals = []
    for r in range(k):
        m = jnp.max(work, axis=0, keepdims=True)
        first = jnp.min(jnp.where(work == m, rows, n_rows), axis=0, keepdims=True)
        sel = rows == first
        rank = jnp.where(sel, float(r), rank)
        work = jnp.where(sel, -jnp.inf, work)
        vals.append(m)
    return vals, rank


def _route_kernel(q_ref, keys_ref, rank1_ref, e1_ref, cnt_ref, e0_ref):
    k = PEER_TOPK
    tl = q_ref.shape[0]
    for g in range(tl // LANES):
        q = q_ref[g * LANES:(g + 1) * LANES, :]
        s0 = lax.dot_general(keys_ref[0].astype(BF16), q[:, :PEER_HALF], _NT,
                             preferred_element_type=F32)
        s1 = lax.dot_general(keys_ref[1].astype(BF16), q[:, PEER_HALF:], _NT,
                             preferred_element_type=F32)
        a, rank0 = _top_rows(s0, k)
        b, rank1 = _top_rows(s1, k)
        b_all = jnp.concatenate(b, axis=0)
        ea = [jnp.exp(x - a[0]) for x in a]
        eb_all = jnp.exp(b_all - b[0])
        widths = [k] + [8] * (k - 1)
        cand = jnp.concatenate([a[ra] + b_all[:w] for ra, w in enumerate(widths)], axis=0)
        wgt = jnp.concatenate([ea[ra] * eb_all[:w] for ra, w in enumerate(widths)], axis=0)
        n_rows = cand.shape[0]
        rows = lax.broadcasted_iota(jnp.int32, cand.shape, 0)
        work = cand
        chosen = jnp.zeros(cand.shape, F32)
        for _ in range(k):
            m = jnp.max(work, axis=0, keepdims=True)
            first = jnp.min(jnp.where(work == m, rows, n_rows), axis=0, keepdims=True)
            sel = rows == first
            chosen = jnp.where(sel, 1.0, chosen)
            work = jnp.where(sel, -jnp.inf, work)
        z = jnp.sum(chosen * wgt, axis=0, keepdims=True)
        cnt = jnp.zeros(s0.shape, F32)
        off = 0
        for ra, w in enumerate(widths):
            n_ra = jnp.sum(chosen[off:off + w], axis=0, keepdims=True)
            cnt = jnp.where(rank0 == float(ra), n_ra, cnt)
            off += w
        lanes = slice(g * LANES, (g + 1) * LANES)
        rank1_ref[:, lanes] = rank1
        e1_ref[:, lanes] = jnp.exp(s1 - b[0]) / z
        cnt_ref[:, lanes] = cnt
        e0_ref[:, lanes] = jnp.exp(s0 - a[0])


def _peer_route(qp, subkeys, tl=256):
    t = qp.shape[0]
    hp, _, nk, half = subkeys.shape
    out = jax.ShapeDtypeStruct((hp, nk, t), F32)
    ospec = pl.BlockSpec((None, nk, tl), lambda i, h: (h, 0, i))
    return pl.pallas_call(
        _route_kernel,
        grid=(t // tl, hp),
        in_specs=[pl.BlockSpec((tl, 2 * half), lambda i, h: (i, h)),
                  pl.BlockSpec((None, 2, nk, half), lambda i, h: (h, 0, 0, 0))],
        out_specs=[ospec] * 4,
        out_shape=[out] * 4,
        compiler_params=_params(2),
        name="peer_route",
    )(qp, subkeys)


def _peer_kernel(x_ref, u_ref, v_ref, rank1_ref, e1_ref, cnt_ref, e0_ref, o_ref, acc_ref):
    e = pl.program_id(1)
    n_heads, nk, _ = rank1_ref.shape
    te = u_ref.shape[0]
    rows_per_block = te // nk

    @pl.when(e == 0)
    def _():
        acc_ref[...] = jnp.zeros_like(acc_ref)

    act = lax.dot_general(u_ref[...], x_ref[...], _NT, preferred_element_type=F32)
    act = 0.5 * act * (1.0 + lax.erf(act * np.float32(1.0 / np.sqrt(2.0))))
    pieces = []
    for ii in range(rows_per_block):
        i = e * rows_per_block + ii
        gate = None
        for h in range(n_heads):
            cnt = cnt_ref[h, pl.ds(i, 1), :]
            e0 = e0_ref[h, pl.ds(i, 1), :]
            term = e0 * jnp.where(rank1_ref[h] < cnt, e1_ref[h], 0.0)
            gate = term if gate is None else gate + term
        pieces.append(gate * act[ii * nk:(ii + 1) * nk, :])
    w = jnp.concatenate(pieces, axis=0).astype(BF16)
    acc_ref[...] += lax.dot_general(w, v_ref[...], _TN, preferred_element_type=F32)

    @pl.when(e == pl.num_programs(1) - 1)
    def _():
        o_ref[...] = acc_ref[...].astype(o_ref.dtype)


def _peer_experts(x, u, v, rank1, e1, cnt, e0, tm=512, te=256):
    t, d = x.shape
    n_exp = u.shape[0]
    hp, nk, _ = rank1.shape
    tab = pl.BlockSpec((hp, nk, tm), lambda i, e: (0, 0, i))
    return pl.pallas_call(
        _peer_kernel,
        grid=(t // tm, n_exp // te),
        in_specs=[pl.BlockSpec((tm, d), lambda i, e: (i, 0)),
                  pl.BlockSpec((te, d), lambda i, e: (e, 0)),
                  pl.BlockSpec((te, d), lambda i, e: (e, 0)),
                  tab, tab, tab, tab],
        out_specs=pl.BlockSpec((tm, d), lambda i, e: (i, 0)),
        out_shape=jax.ShapeDtypeStruct((t, d), BF16),
        scratch_shapes=[pltpu.VMEM((tm, d), F32)],
        compiler_params=_params(2),
        name="peer_experts",
    )(x, u, v, rank1, e1, cnt, e0)


def kernel(x, mem, norm1_w, w_in, attn_rel_bias, ret_gn_w, w_out, norm2_w, mem_norm_w, xattn_wq,
           xattn_wkv, xattn_wo, norm3_w, peer_wq, peer_subkeys, peer_u, peer_v, final_norm_w):
    b, s, d = x.shape
    t = b * s
    depth = w_in.shape[0]
    n_attn_heads = attn_rel_bias.shape[1]
    attn_width = n_attn_heads * ATTN_HEAD_DIM
    n_ret_heads = ret_gn_w.shape[1] // RET_V_DIM
    h = x.reshape(t, d)
    xn = _rmsnorm(h, norm1_w[0], BF16)
    for l in range(depth):
        proj = _matmul(xn, w_in[l], BF16, name="in_proj").reshape(b, s, -1)
        a_out = _chunk_attention(proj, attn_rel_bias[l], n_attn_heads)
        r_out = _retention(proj, ret_gn_w[l], n_ret_heads, 3 * attn_width)
        mix = jnp.concatenate([a_out, r_out], axis=-1).reshape(t, -1)
        h = _matmul(mix, w_out[l], F32, residual=h, name="out_proj")

        mem_n = _rmsnorm(mem.reshape(-1, d), mem_norm_w[l], BF16)
        kv = _matmul(mem_n, xattn_wkv[l], BF16, name="mem_kv").reshape(b, -1, 2 * d)
        xq = _matmul(_rmsnorm(h, norm2_w[l], BF16), xattn_wq[l], BF16, name="xattn_q")
        xo = _cross_attention(xq.reshape(b, s, d), kv, XATTN_HEADS).reshape(t, d)
        h = _matmul(xo, xattn_wo[l], F32, residual=h, name="xattn_o")

        xn3 = _rmsnorm(h, norm3_w[l], BF16)
        qp = _matmul(xn3, peer_wq[l], BF16, name="peer_q")
        rank1, e1, cnt, e0 = _peer_route(qp, peer_subkeys[l])
        delta = _peer_experts(xn3, peer_u[l].astype(BF16), peer_v[l].astype(BF16),
                              rank1, e1, cnt, e0)
        if l + 1 < depth:
            h = h + delta.astype(F32)
            xn = _rmsnorm(h, norm1_w[l + 1], BF16)
    return _add_rmsnorm(h, delta, final_norm_w, F32).reshape(b, s, d)
```

```python
import functools

import numpy as np
import jax
import jax.numpy as jnp
from jax import lax
from jax.experimental import pallas as pl
from jax.experimental.pallas import tpu as pltpu

F32 = jnp.float32
BF16 = jnp.bfloat16

CHUNK = 64
LEFT_CHUNKS = 8
LEFT = LEFT_CHUNKS * CHUNK
ATTN_HEAD_DIM = 128
MAX_REL_DIST = 256
RET_V_DIM = 256
RET_QK_DIM = 128
ROPE_BASE = 10000.0
XATTN_HEADS = 4
PEER_HEADS = 8
PEER_KEYS = 128
PEER_HALF = 128
PEER_TOPK = 16
EPS = 1e-6
NEG_INF = -1e30
NOT_RANKED = 1e9

VMEM_LIMIT_BYTES = 56 * 1024 * 1024
LANES = 128

_NT = (((1,), (1,)), ((), ()))
_TN = (((0,), (0,)), ((), ()))


def _params(n_grid_dims, flags=None):
    return pltpu.CompilerParams(
        dimension_semantics=("arbitrary",) * n_grid_dims,
        vmem_limit_bytes=VMEM_LIMIT_BYTES,
        flags=flags)


def _rmsnorm_kernel(x_ref, w_ref, o_ref):
    x = x_ref[...]
    ms = jnp.mean(x * x, axis=-1, keepdims=True)
    o_ref[...] = (x * lax.rsqrt(ms + EPS) * w_ref[...]).astype(o_ref.dtype)


def _rmsnorm(x, w, out_dtype, rows=256):
    m, d = x.shape
    return pl.pallas_call(
        _rmsnorm_kernel,
        grid=(m // rows,),
        in_specs=[pl.BlockSpec((rows, d), lambda i: (i, 0)),
                  pl.BlockSpec((1, d), lambda i: (0, 0))],
        out_specs=pl.BlockSpec((rows, d), lambda i: (i, 0)),
        out_shape=jax.ShapeDtypeStruct((m, d), out_dtype),
        compiler_params=_params(1),
        name="rmsnorm",
    )(x, w.reshape(1, d))


def _add_rmsnorm_kernel(x_ref, y_ref, w_ref, o_ref):
    x = x_ref[...] + y_ref[...].astype(F32)
    ms = jnp.mean(x * x, axis=-1, keepdims=True)
    o_ref[...] = (x * lax.rsqrt(ms + EPS) * w_ref[...]).astype(o_ref.dtype)


def _add_rmsnorm(x, y, w, out_dtype, rows=256):
    m, d = x.shape
    return pl.pallas_call(
        _add_rmsnorm_kernel,
        grid=(m // rows,),
        in_specs=[pl.BlockSpec((rows, d), lambda i: (i, 0)),
                  pl.BlockSpec((rows, d), lambda i: (i, 0)),
                  pl.BlockSpec((1, d), lambda i: (0, 0))],
        out_specs=pl.BlockSpec((rows, d), lambda i: (i, 0)),
        out_shape=jax.ShapeDtypeStruct((m, d), out_dtype),
        compiler_params=_params(1),
        name="add_rmsnorm",
    )(x, y, w.reshape(1, d))


def _mm_kernel(a_ref, w_ref, o_ref):
    acc = jnp.dot(a_ref[...], w_ref[...].astype(BF16), preferred_element_type=F32)
    o_ref[...] = acc.astype(o_ref.dtype)


def _mm_res_kernel(a_ref, w_ref, r_ref, o_ref):
    acc = jnp.dot(a_ref[...], w_ref[...].astype(BF16), preferred_element_type=F32)
    o_ref[...] = (r_ref[...] + acc).astype(o_ref.dtype)


def _matmul(a, w, out_dtype, residual=None, tm=1024, tn=512, name="matmul"):
    m, k = a.shape
    n = w.shape[1]
    tm, tn = min(tm, m), min(tn, n)
    in_specs = [pl.BlockSpec((tm, k), lambda i, j: (i, 0)),
                pl.BlockSpec((k, tn), lambda i, j: (0, j))]
    args = [a, w]
    body = _mm_kernel
    if residual is not None:
        in_specs.append(pl.BlockSpec((tm, tn), lambda i, j: (i, j)))
        args.append(residual)
        body = _mm_res_kernel
    return pl.pallas_call(
        body,
        grid=(m // tm, n // tn),
        in_specs=in_specs,
        out_specs=pl.BlockSpec((tm, tn), lambda i, j: (i, j)),
        out_shape=jax.ShapeDtypeStruct((m, n), out_dtype),
        compiler_params=_params(2),
        name=name,
    )(*args)


def _attn_kernel(q_ref, k_ref, v_ref, base_ref, o_ref, bias_ref, *, qb, scale):
    seq = q_ref.shape[0]
    width = LEFT + qb

    @pl.when(pl.program_id(1) == 0)
    def _():
        toeplitz = pltpu.roll(jnp.broadcast_to(base_ref[...], (qb, base_ref.shape[-1])),
                              0, 1, stride=1, stride_axis=0)
        q_chunk = lax.broadcasted_iota(jnp.int32, (qb, width), 0) // CHUNK
        c_chunk = lax.broadcasted_iota(jnp.int32, (qb, width), 1) // CHUNK
        in_band = (c_chunk >= q_chunk) & (c_chunk <= q_chunk + LEFT_CHUNKS)
        bias_ref[...] = jnp.where(in_band, toeplitz[:, :width], NEG_INF)

    for i in range(seq // qb):
        q0 = i * qb
        k0 = max(0, q0 - LEFT)
        kw = q0 + qb - k0
        c0 = k0 - (q0 - LEFT)
        s = lax.dot_general(q_ref[q0:q0 + qb, :], k_ref[k0:k0 + kw, :], _NT,
                            preferred_element_type=F32)
        s = s * scale + bias_ref[:, c0:c0 + kw]
        m = jnp.max(s, axis=-1, keepdims=True)
        p = jnp.exp(s - m)
        l = jnp.sum(p, axis=-1, keepdims=True)
        o = jnp.dot(p.astype(BF16), v_ref[k0:k0 + kw, :], preferred_element_type=F32)
        o_ref[q0:q0 + qb, :] = (o / l).astype(o_ref.dtype)


def _attn_bias_base(rel_bias, qb):
    w = pl.next_power_of_2(LEFT + 2 * qb)
    j = jnp.arange(w)
    j = jnp.where(j < LEFT + qb, j, j - w)
    idx = jnp.clip(LEFT - j, -MAX_REL_DIST, MAX_REL_DIST) + MAX_REL_DIST
    return rel_bias[:, None, idx].astype(F32)


def _chunk_attention(proj, rel_bias, n_heads, qb=256):
    b, s, _ = proj.shape
    dh = ATTN_HEAD_DIM
    base = _attn_bias_base(rel_bias, qb)
    kern = functools.partial(_attn_kernel, qb=qb, scale=dh ** -0.5)
    return pl.pallas_call(
        kern,
        grid=(n_heads, b),
        in_specs=[pl.BlockSpec((None, s, dh), lambda h, bi: (bi, 0, h)),
                  pl.BlockSpec((None, s, dh), lambda h, bi: (bi, 0, n_heads + h)),
                  pl.BlockSpec((None, s, dh), lambda h, bi: (bi, 0, 2 * n_heads + h)),
                  pl.BlockSpec((None, 1, base.shape[-1]), lambda h, bi: (h, 0, 0))],
        out_specs=pl.BlockSpec((None, s, dh), lambda h, bi: (bi, 0, h)),
        out_shape=jax.ShapeDtypeStruct((b, s, n_heads * dh), BF16),
        scratch_shapes=[pltpu.VMEM((qb, LEFT + qb), F32)],
        compiler_params=_params(2),
        name="chunk_attention",
    )(proj, proj, proj, base)


def _ret_kernel(q_ref, k_ref, v_ref, g_ref, cos_ref, sin_ref, dec_ref, qd_ref, kd_ref, bd_ref,
                gnw_ref, o_ref, state_ref, *, rb, scale):
    seq = q_ref.shape[0]
    state_ref[...] = jnp.zeros_like(state_ref)
    half = q_ref.shape[1] // 2

    def body(n, carry):
        r0 = pl.multiple_of(n * rb, rb)
        rows = pl.ds(r0, rb)
        cos = cos_ref[rows, :]
        sin = sin_ref[rows, :]
        q = q_ref[rows, :].astype(F32)
        k = k_ref[rows, :].astype(F32)
        q = q * cos + pltpu.roll(q, half, 1) * sin
        k = (k * cos + pltpu.roll(k, half, 1) * sin) * scale
        v = v_ref[rows, :]
        a = lax.dot_general(q.astype(BF16), k.astype(BF16), _NT, preferred_element_type=F32)
        a = a * dec_ref[...]
        st = state_ref[...]
        y = jnp.dot(a.astype(BF16), v, preferred_element_type=F32)
        y = y + jnp.dot((q * qd_ref[...]).astype(BF16), st.astype(BF16),
                        preferred_element_type=F32)
        kd = (k * kd_ref[...]).astype(BF16)
        kv = lax.dot_general(kd, v, _TN, preferred_element_type=F32)
        state_ref[...] = st * bd_ref[...] + kv
        mu = jnp.mean(y, axis=-1, keepdims=True)
        yc = y - mu
        var = jnp.mean(yc * yc, axis=-1, keepdims=True)
        yn = yc * lax.rsqrt(var + EPS) * gnw_ref[...]
        g = g_ref[rows, :].astype(F32)
        o_ref[rows, :] = (g * (1.0 / (1.0 + jnp.exp(-g))) * yn).astype(o_ref.dtype)
        return carry

    lax.fori_loop(0, seq // rb, body, 0)


def _retention(proj, gn_w, n_heads, col0, rb=256):
    b, s, _ = proj.shape
    dk, dv = RET_QK_DIM, RET_V_DIM
    qk0 = col0 // dk
    v0 = (col0 + 2 * n_heads * dk) // dv
    inv_freq = 1.0 / (ROPE_BASE ** (jnp.arange(0, dk, 2, dtype=F32) / dk))
    ang = jnp.arange(s, dtype=F32)[:, None] * inv_freq[None, :]
    cos = jnp.concatenate([jnp.cos(ang), jnp.cos(ang)], axis=-1)
    sin = jnp.concatenate([-jnp.sin(ang), jnp.sin(ang)], axis=-1)
    log_gamma = jnp.log1p(-jnp.power(2.0, -5.0 - jnp.arange(n_heads, dtype=F32)))
    pos = jnp.arange(rb, dtype=F32)
    chunk_of = jnp.arange(rb) // CHUNK
    causal = (chunk_of[None, :] <= chunk_of[:, None]).astype(F32)
    dec = jnp.exp(log_gamma[:, None, None] * jnp.abs(pos[:, None] - pos[None, :])) * causal[None]
    qd = jnp.exp(log_gamma[:, None] * (pos + 1.0))[:, :, None]
    kd = jnp.exp(log_gamma[:, None] * (rb - 1.0 - pos))[:, :, None]
    bd = jnp.exp(log_gamma * rb)[:, None, None]
    kern = functools.partial(_ret_kernel, rb=rb, scale=dk ** -0.5)
    return pl.pallas_call(
        kern,
        grid=(b, n_heads),
        in_specs=[pl.BlockSpec((None, s, dk), lambda bi, h: (bi, 0, qk0 + h)),
                  pl.BlockSpec((None, s, dk), lambda bi, h: (bi, 0, qk0 + n_heads + h)),
                  pl.BlockSpec((None, s, dv), lambda bi, h: (bi, 0, v0 + h)),
                  pl.BlockSpec((None, s, dv), lambda bi, h: (bi, 0, v0 + n_heads + h)),
                  pl.BlockSpec((s, dk), lambda bi, h: (0, 0)),
                  pl.BlockSpec((s, dk), lambda bi, h: (0, 0)),
                  pl.BlockSpec((None, rb, rb), lambda bi, h: (h, 0, 0)),
                  pl.BlockSpec((None, rb, 1), lambda bi, h: (h, 0, 0)),
                  pl.BlockSpec((None, rb, 1), lambda bi, h: (h, 0, 0)),
                  pl.BlockSpec((None, 1, 1), lambda bi, h: (h, 0, 0)),
                  pl.BlockSpec((1, dv), lambda bi, h: (0, h))],
        out_specs=pl.BlockSpec((None, s, dv), lambda bi, h: (bi, 0, h)),
        out_shape=jax.ShapeDtypeStruct((b, s, n_heads * dv), BF16),
        scratch_shapes=[pltpu.VMEM((dk, dv), F32)],
        compiler_params=_params(2),
        name="retention",
    )(proj, proj, proj, proj, cos, sin, dec, qd, kd, bd, gn_w.reshape(1, -1))


def _xattn_kernel(q_ref, kv_ref, o_ref, *, n_heads, scale):
    d = q_ref.shape[1]
    dh = d // n_heads
    for h in range(n_heads):
        cols = slice(h * dh, (h + 1) * dh)
        s = lax.dot_general(q_ref[:, cols], kv_ref[:, cols], _NT, preferred_element_type=F32)
        s = s * scale
        m = jnp.max(s, axis=-1, keepdims=True)
        p = jnp.exp(s - m)
        l = jnp.sum(p, axis=-1, keepdims=True)
        o = jnp.dot(p.astype(BF16), kv_ref[:, d + h * dh:d + (h + 1) * dh],
                    preferred_element_type=F32)
        o_ref[:, cols] = (o / l).astype(o_ref.dtype)


def _cross_attention(q, kv, n_heads, tq=512):
    b, s, d = q.shape
    m = kv.shape[1]
    kern = functools.partial(_xattn_kernel, n_heads=n_heads, scale=(d // n_heads) ** -0.5)
    return pl.pallas_call(
        kern,
        grid=(b, s // tq),
        in_specs=[pl.BlockSpec((None, tq, d), lambda bi, i: (bi, i, 0)),
                  pl.BlockSpec((None, m, 2 * d), lambda bi, i: (bi, 0, 0))],
        out_specs=pl.BlockSpec((None, tq, d), lambda bi, i: (bi, i, 0)),
        out_shape=jax.ShapeDtypeStruct((b, s, d), BF16),
        compiler_params=_params(2),
        name="cross_attention",
    )(q, kv)


def _top_rows(s, k):
    n_rows = s.shape[0]
    rows = lax.broadcasted_iota(jnp.int32, s.shape, 0)
    work = s
    rank = jnp.full(s.shape, NOT_RANKED, F32)
    vals = []
    for r in range(k):
        m = jnp.max(work, axis=0, keepdims=True)
        first = jnp.min(jnp.where(work == m, rows, n_rows), axis=0, keepdims=True)
        sel = rows == first
        rank = jnp.where(sel, float(r), rank)
        work = jnp.where(sel, -jnp.inf, work)
        vals.append(m)
    return vals, rank


def _route_kernel(q_ref, keys_ref, rank1_ref, e1_ref, cnt_ref, e0_ref):
    k = PEER_TOPK
    tl = q_ref.shape[0]
    for g in range(tl // LANES):
        q = q_ref[g * LANES:(g + 1) * LANES, :]
        s0 = lax.dot_general(keys_ref[0].astype(BF16), q[:, :PEER_HALF], _NT,
                             preferred_element_type=F32)
        s1 = lax.dot_general(keys_ref[1].astype(BF16), q[:, PEER_HALF:], _NT,
                             preferred_element_type=F32)
        a, rank0 = _top_rows(s0, k)
        b, rank1 = _top_rows(s1, k)
        b_all = jnp.concatenate(b, axis=0)
        ea = [jnp.exp(x - a[0]) for x in a]
        eb_all = jnp.exp(b_all - b[0])
        widths = [k] + [8] * (k - 1)
        cand = jnp.concatenate([a[ra] + b_all[:w] for ra, w in enumerate(widths)], axis=0)
        wgt = jnp.concatenate([ea[ra] * eb_all[:w] for ra, w in enumerate(widths)], axis=0)
        n_rows = cand.shape[0]
        rows = lax.broadcasted_iota(jnp.int32, cand.shape, 0)
        work = cand
        chosen = jnp.zeros(cand.shape, F32)
        for _ in range(k):
            m = jnp.max(work, axis=0, keepdims=True)
            first = jnp.min(jnp.where(work == m, rows, n_rows), axis=0, keepdims=True)
            sel = rows == first
            chosen = jnp.where(sel, 1.0, chosen)
            work = jnp.where(sel, -jnp.inf, work)
        z = jnp.sum(chosen * wgt, axis=0, keepdims=True)
        cnt = jnp.zeros(s0.shape, F32)
        off = 0
        for ra, w in enumerate(widths):
            n_ra = jnp.sum(chosen[off:off + w], axis=0, keepdims=True)
            cnt = jnp.where(rank0 == float(ra), n_ra, cnt)
            off += w
        lanes = slice(g * LANES, (g + 1) * LANES)
        rank1_ref[:, lanes] = rank1
        e1_ref[:, lanes] = jnp.exp(s1 - b[0]) / z
        cnt_ref[:, lanes] = cnt
        e0_ref[:, lanes] = jnp.exp(s0 - a[0])


def _peer_route(qp, subkeys, tl=256):
    t = qp.shape[0]
    hp, _, nk, half = subkeys.shape
    out = jax.ShapeDtypeStruct((hp, nk, t), F32)
    ospec = pl.BlockSpec((None, nk, tl), lambda i, h: (h, 0, i))
    return pl.pallas_call(
        _route_kernel,
        grid=(t // tl, hp),
        in_specs=[pl.BlockSpec((tl, 2 * half), lambda i, h: (i, h)),
                  pl.BlockSpec((None, 2, nk, half), lambda i, h: (h, 0, 0, 0))],
        out_specs=[ospec] * 4,
        out_shape=[out] * 4,
        compiler_params=_params(2),
        name="peer_route",
    )(qp, subkeys)


def _peer_kernel(x_ref, u_ref, v_ref, rank1_ref, e1_ref, cnt_ref, e0_ref, o_ref, acc_ref, xt_ref,
                 *stage_refs, sub):
    e = pl.program_id(1)
    n_heads, nk, tm = rank1_ref.shape
    te, d = u_ref.shape
    n_sub = te // sub
    rows_per_sub = sub // nk
    n_pieces = n_heads
    kc = d // n_pieces
    act_refs, wt_refs = stage_refs[:n_sub], stage_refs[n_sub:]

    @pl.when(e == 0)
    def _():
        acc_ref[...] = jnp.zeros_like(acc_ref)
        xt_ref[...] = x_ref[...].T

    def up_piece(j, p):
        part = jnp.dot(u_ref[j * sub:(j + 1) * sub, p * kc:(p + 1) * kc],
                       xt_ref[p * kc:(p + 1) * kc, :], preferred_element_type=F32)
        if p == 0:
            act_refs[j][...] = part
        else:
            act_refs[j][...] += part

    def gate_piece(j, p):
        ii, g = divmod(p, n_pieces // rows_per_sub)
        tg = tm // (n_pieces // rows_per_sub)
        lanes = slice(g * tg, (g + 1) * tg)
        row = j * rows_per_sub + ii
        gate = None
        for h in range(n_heads):
            cnt = cnt_ref[h, row:row + 1, lanes]
            e0 = e0_ref[h, row:row + 1, lanes]
            term = e0 * jnp.where(rank1_ref[h, :, lanes] < cnt, e1_ref[h, :, lanes], 0.0)
            gate = term if gate is None else gate + term
        act = act_refs[j][ii * nk:(ii + 1) * nk, lanes]
        act = 0.5 * act * (1.0 + lax.erf(act * np.float32(1.0 / np.sqrt(2.0))))
        wt_refs[j][lanes, ii * nk:(ii + 1) * nk] = (gate * act).T.astype(BF16)

    def down_piece(j, p):
        cols = slice(p * kc, (p + 1) * kc)
        acc_ref[:, cols] += jnp.dot(wt_refs[j][...], v_ref[j * sub:(j + 1) * sub, cols],
                                    preferred_element_type=F32)

    for step in range(n_sub + 2):
        for p in range(n_pieces):
            if step < n_sub:
                up_piece(step, p)
            if 0 <= step - 1 < n_sub:
                gate_piece(step - 1, p)
            if 0 <= step - 2 < n_sub:
                down_piece(step - 2, p)

    @pl.when(e == pl.num_programs(1) - 1)
    def _():
        o_ref[...] = acc_ref[...].astype(o_ref.dtype)


def _peer_experts(x, u, v, rank1, e1, cnt, e0, tm=512, te=512, sub=256):
    t, d = x.shape
    n_exp = u.shape[0]
    hp, nk, _ = rank1.shape
    rows = te // nk
    tab = pl.BlockSpec((hp, nk, tm), lambda i, e: (0, 0, i))
    row_tab = pl.BlockSpec((hp, None, rows, tm), lambda i, e: (0, e, 0, i))
    cnt = cnt.reshape(hp, nk // rows, rows, t)
    e0 = e0.reshape(hp, nk // rows, rows, t)
    return pl.pallas_call(
        functools.partial(_peer_kernel, sub=sub),
        grid=(t // tm, n_exp // te),
        in_specs=[pl.BlockSpec((tm, d), lambda i, e: (i, 0)),
                  pl.BlockSpec((te, d), lambda i, e: (e, 0)),
                  pl.BlockSpec((te, d), lambda i, e: (e, 0)),
                  tab, tab, row_tab, row_tab],
        out_specs=pl.BlockSpec((tm, d), lambda i, e: (i, 0)),
        out_shape=jax.ShapeDtypeStruct((t, d), BF16),
        scratch_shapes=([pltpu.VMEM((tm, d), F32), pltpu.VMEM((d, tm), BF16)]
                        + [pltpu.VMEM((sub, tm), F32)] * (te // sub)
                        + [pltpu.VMEM((tm, sub), BF16)] * (te // sub)),
        compiler_params=_params(2),
        name="peer_experts",
    )(x, u, v, rank1, e1, cnt, e0)


def kernel(x, mem, norm1_w, w_in, attn_rel_bias, ret_gn_w, w_out, norm2_w, mem_norm_w, xattn_wq,
           xattn_wkv, xattn_wo, norm3_w, peer_wq, peer_subkeys, peer_u, peer_v, final_norm_w):
    b, s, d = x.shape
    t = b * s
    depth = w_in.shape[0]
    n_attn_heads = attn_rel_bias.shape[1]
    attn_width = n_attn_heads * ATTN_HEAD_DIM
    n_ret_heads = ret_gn_w.shape[1] // RET_V_DIM
    h = x.reshape(t, d)
    xn = _rmsnorm(h, norm1_w[0], BF16)
    for l in range(depth):
        proj = _matmul(xn, w_in[l], BF16, name="in_proj").reshape(b, s, -1)
        a_out = _chunk_attention(proj, attn_rel_bias[l], n_attn_heads)
        r_out = _retention(proj, ret_gn_w[l], n_ret_heads, 3 * attn_width)
        mix = jnp.concatenate([a_out, r_out], axis=-1).reshape(t, -1)
        h = _matmul(mix, w_out[l], F32, residual=h, name="out_proj")

        mem_n = _rmsnorm(mem.reshape(-1, d), mem_norm_w[l], BF16)
        kv = _matmul(mem_n, xattn_wkv[l], BF16, name="mem_kv").reshape(b, -1, 2 * d)
        xq = _matmul(_rmsnorm(h, norm2_w[l], BF16), xattn_wq[l], BF16, name="xattn_q")
        xo = _cross_attention(xq.reshape(b, s, d), kv, XATTN_HEADS).reshape(t, d)
        h = _matmul(xo, xattn_wo[l], F32, residual=h, name="xattn_o")

        xn3 = _rmsnorm(h, norm3_w[l], BF16)
        qp = _matmul(xn3, peer_wq[l], BF16, name="peer_q")
        rank1, e1, cnt, e0 = _peer_route(qp, peer_subkeys[l])
        delta = _peer_experts(xn3, peer_u[l].astype(BF16), peer_v[l].astype(BF16),
                              rank1, e1, cnt, e0)
        if l + 1 < depth:
            h = h + delta.astype(F32)
            xn = _rmsnorm(h, norm1_w[l + 1], BF16)
    return _add_rmsnorm(h, delta, final_norm_w, F32).reshape(b, s, d)
```

```python
import functools

import numpy as np
import jax
import jax.numpy as jnp
from jax import lax
from jax.experimental import pallas as pl
from jax.experimental.pallas import tpu as pltpu

F32 = jnp.float32
BF16 = jnp.bfloat16

CHUNK = 64
LEFT_CHUNKS = 8
LEFT = LEFT_CHUNKS * CHUNK
ATTN_HEAD_DIM = 128
MAX_REL_DIST = 256
RET_V_DIM = 256
RET_QK_DIM = 128
ROPE_BASE = 10000.0
XATTN_HEADS = 4
PEER_HEADS = 8
PEER_KEYS = 128
PEER_HALF = 128
PEER_TOPK = 16
EPS = 1e-6
NEG_INF = -1e30
NOT_RANKED = 1e9

VMEM_LIMIT_BYTES = 56 * 1024 * 1024
LANES = 128

_NT = (((1,), (1,)), ((), ()))
_TN = (((0,), (0,)), ((), ()))


def _params(n_grid_dims, flags=None):
    return pltpu.CompilerParams(
        dimension_semantics=("arbitrary",) * n_grid_dims,
        vmem_limit_bytes=VMEM_LIMIT_BYTES,
        flags=flags)


def _rmsnorm_kernel(x_ref, w_ref, o_ref):
    x = x_ref[...]
    ms = jnp.mean(x * x, axis=-1, keepdims=True)
    o_ref[...] = (x * lax.rsqrt(ms + EPS) * w_ref[...]).astype(o_ref.dtype)


def _rmsnorm(x, w, out_dtype, rows=256):
    m, d = x.shape
    return pl.pallas_call(
        _rmsnorm_kernel,
        grid=(m // rows,),
        in_specs=[pl.BlockSpec((rows, d), lambda i: (i, 0)),
                  pl.BlockSpec((1, d), lambda i: (0, 0))],
        out_specs=pl.BlockSpec((rows, d), lambda i: (i, 0)),
        out_shape=jax.ShapeDtypeStruct((m, d), out_dtype),
        compiler_params=_params(1),
        name="rmsnorm",
    )(x, w.reshape(1, d))


def _add_rmsnorm_kernel(x_ref, y_ref, w_ref, o_ref):
    x = x_ref[...] + y_ref[...].astype(F32)
    ms = jnp.mean(x * x, axis=-1, keepdims=True)
    o_ref[...] = (x * lax.rsqrt(ms + EPS) * w_ref[...]).astype(o_ref.dtype)


def _add_rmsnorm(x, y, w, out_dtype, rows=256):
    m, d = x.shape
    return pl.pallas_call(
        _add_rmsnorm_kernel,
        grid=(m // rows,),
        in_specs=[pl.BlockSpec((rows, d), lambda i: (i, 0)),
                  pl.BlockSpec((rows, d), lambda i: (i, 0)),
                  pl.BlockSpec((1, d), lambda i: (0, 0))],
        out_specs=pl.BlockSpec((rows, d), lambda i: (i, 0)),
        out_shape=jax.ShapeDtypeStruct((m, d), out_dtype),
        compiler_params=_params(1),
        name="add_rmsnorm",
    )(x, y, w.reshape(1, d))


def _mm_kernel(a_ref, w_ref, o_ref):
    acc = jnp.dot(a_ref[...], w_ref[...].astype(BF16), preferred_element_type=F32)
    o_ref[...] = acc.astype(o_ref.dtype)


def _mm_res_kernel(a_ref, w_ref, r_ref, o_ref):
    acc = jnp.dot(a_ref[...], w_ref[...].astype(BF16), preferred_element_type=F32)
    o_ref[...] = (r_ref[...] + acc).astype(o_ref.dtype)


def _matmul(a, w, out_dtype, residual=None, tm=1024, tn=512, name="matmul"):
    m, k = a.shape
    n = w.shape[1]
    tm, tn = min(tm, m), min(tn, n)
    in_specs = [pl.BlockSpec((tm, k), lambda i, j: (i, 0)),
                pl.BlockSpec((k, tn), lambda i, j: (0, j))]
    args = [a, w]
    body = _mm_kernel
    if residual is not None:
        in_specs.append(pl.BlockSpec((tm, tn), lambda i, j: (i, j)))
        args.append(residual)
        body = _mm_res_kernel
    return pl.pallas_call(
        body,
        grid=(m // tm, n // tn),
        in_specs=in_specs,
        out_specs=pl.BlockSpec((tm, tn), lambda i, j: (i, j)),
        out_shape=jax.ShapeDtypeStruct((m, n), out_dtype),
        compiler_params=_params(2),
        name=name,
    )(*args)


def _attn_kernel(q_ref, k_ref, v_ref, base_ref, o_ref, bias_ref, *, qb, scale):
    seq = q_ref.shape[0]
    width = LEFT + qb

    @pl.when(pl.program_id(1) == 0)
    def _():
        toeplitz = pltpu.roll(jnp.broadcast_to(base_ref[...], (qb, base_ref.shape[-1])),
                              0, 1, stride=1, stride_axis=0)
        q_chunk = lax.broadcasted_iota(jnp.int32, (qb, width), 0) // CHUNK
        c_chunk = lax.broadcasted_iota(jnp.int32, (qb, width), 1) // CHUNK
        in_band = (c_chunk >= q_chunk) & (c_chunk <= q_chunk + LEFT_CHUNKS)
        bias_ref[...] = jnp.where(in_band, toeplitz[:, :width], NEG_INF)

    for i in range(seq // qb):
        q0 = i * qb
        k0 = max(0, q0 - LEFT)
        kw = q0 + qb - k0
        c0 = k0 - (q0 - LEFT)
        s = lax.dot_general(q_ref[q0:q0 + qb, :], k_ref[k0:k0 + kw, :], _NT,
                            preferred_element_type=F32)
        s = s * scale + bias_ref[:, c0:c0 + kw]
        m = jnp.max(s, axis=-1, keepdims=True)
        p = jnp.exp(s - m)
        l = jnp.sum(p, axis=-1, keepdims=True)
        o = jnp.dot(p.astype(BF16), v_ref[k0:k0 + kw, :], preferred_element_type=F32)
        o_ref[q0:q0 + qb, :] = (o / l).astype(o_ref.dtype)


def _attn_bias_base(rel_bias, qb):
    w = pl.next_power_of_2(LEFT + 2 * qb)
    j = jnp.arange(w)
    j = jnp.where(j < LEFT + qb, j, j - w)
    idx = jnp.clip(LEFT - j, -MAX_REL_DIST, MAX_REL_DIST) + MAX_REL_DIST
    return rel_bias[:, None, idx].astype(F32)


def _chunk_attention(proj, rel_bias, n_heads, qb=256):
    b, s, _ = proj.shape
    dh = ATTN_HEAD_DIM
    base = _attn_bias_base(rel_bias, qb)
    kern = functools.partial(_attn_kernel, qb=qb, scale=dh ** -0.5)
    return pl.pallas_call(
        kern,
        grid=(n_heads, b),
        in_specs=[pl.BlockSpec((None, s, dh), lambda h, bi: (bi, 0, h)),
                  pl.BlockSpec((None, s, dh), lambda h, bi: (bi, 0, n_heads + h)),
                  pl.BlockSpec((None, s, dh), lambda h, bi: (bi, 0, 2 * n_heads + h)),
                  pl.BlockSpec((None, 1, base.shape[-1]), lambda h, bi: (h, 0, 0))],
        out_specs=pl.BlockSpec((None, s, dh), lambda h, bi: (bi, 0, h)),
        out_shape=jax.ShapeDtypeStruct((b, s, n_heads * dh), BF16),
        scratch_shapes=[pltpu.VMEM((qb, LEFT + qb), F32)],
        compiler_params=_params(2),
        name="chunk_attention",
    )(proj, proj, proj, base)


def _ret_kernel(q_ref, k_ref, v_ref, g_ref, cos_ref, sin_ref, dec_ref, qd_ref, kd_ref, bd_ref,
                gnw_ref, o_ref, state_ref, *, rb, scale):
    seq = q_ref.shape[0]
    state_ref[...] = jnp.zeros_like(state_ref)
    half = q_ref.shape[1] // 2

    def body(n, carry):
        r0 = pl.multiple_of(n * rb, rb)
        rows = pl.ds(r0, rb)
        cos = cos_ref[rows, :]
        sin = sin_ref[rows, :]
        q = q_ref[rows, :].astype(F32)
        k = k_ref[rows, :].astype(F32)
        q = q * cos + pltpu.roll(q, half, 1) * sin
        k = (k * cos + pltpu.roll(k, half, 1) * sin) * scale
        v = v_ref[rows, :]
        a = lax.dot_general(q.astype(BF16), k.astype(BF16), _NT, preferred_element_type=F32)
        a = a * dec_ref[...]
        st = state_ref[...]
        y = jnp.dot(a.astype(BF16), v, preferred_element_type=F32)
        y = y + jnp.dot((q * qd_ref[...]).astype(BF16), st.astype(BF16),
                        preferred_element_type=F32)
        kd = (k * kd_ref[...]).astype(BF16)
        kv = lax.dot_general(kd, v, _TN, preferred_element_type=F32)
        state_ref[...] = st * bd_ref[...] + kv
        mu = jnp.mean(y, axis=-1, keepdims=True)
        yc = y - mu
        var = jnp.mean(yc * yc, axis=-1, keepdims=True)
        yn = yc * lax.rsqrt(var + EPS) * gnw_ref[...]
        g = g_ref[rows, :].astype(F32)
        o_ref[rows, :] = (g * (1.0 / (1.0 + jnp.exp(-g))) * yn).astype(o_ref.dtype)
        return carry

    lax.fori_loop(0, seq // rb, body, 0)


def _retention(proj, gn_w, n_heads, col0, rb=256):
    b, s, _ = proj.shape
    dk, dv = RET_QK_DIM, RET_V_DIM
    qk0 = col0 // dk
    v0 = (col0 + 2 * n_heads * dk) // dv
    inv_freq = 1.0 / (ROPE_BASE ** (jnp.arange(0, dk, 2, dtype=F32) / dk))
    ang = jnp.arange(s, dtype=F32)[:, None] * inv_freq[None, :]
    cos = jnp.concatenate([jnp.cos(ang), jnp.cos(ang)], axis=-1)
    sin = jnp.concatenate([-jnp.sin(ang), jnp.sin(ang)], axis=-1)
    log_gamma = jnp.log1p(-jnp.power(2.0, -5.0 - jnp.arange(n_heads, dtype=F32)))
    pos = jnp.arange(rb, dtype=F32)
    chunk_of = jnp.arange(rb) // CHUNK
    causal = (chunk_of[None, :] <= chunk_of[:, None]).astype(F32)
    dec = jnp.exp(log_gamma[:, None, None] * jnp.abs(pos[:, None] - pos[None, :])) * causal[None]
    qd = jnp.exp(log_gamma[:, None] * (pos + 1.0))[:, :, None]
    kd = jnp.exp(log_gamma[:, None] * (rb - 1.0 - pos))[:, :, None]
    bd = jnp.exp(log_gamma * rb)[:, None, None]
    kern = functools.partial(_ret_kernel, rb=rb, scale=dk ** -0.5)
    return pl.pallas_call(
        kern,
        grid=(b, n_heads),
        in_specs=[pl.BlockSpec((None, s, dk), lambda bi, h: (bi, 0, qk0 + h)),
                  pl.BlockSpec((None, s, dk), lambda bi, h: (bi, 0, qk0 + n_heads + h)),
                  pl.BlockSpec((None, s, dv), lambda bi, h: (bi, 0, v0 + h)),
                  pl.BlockSpec((None, s, dv), lambda bi, h: (bi, 0, v0 + n_heads + h)),
                  pl.BlockSpec((s, dk), lambda bi, h: (0, 0)),
                  pl.BlockSpec((s, dk), lambda bi, h: (0, 0)),
                  pl.BlockSpec((None, rb, rb), lambda bi, h: (h, 0, 0)),
                  pl.BlockSpec((None, rb, 1), lambda bi, h: (h, 0, 0)),
                  pl.BlockSpec((None, rb, 1), lambda bi, h: (h, 0, 0)),
                  pl.BlockSpec((None, 1, 1), lambda bi, h: (h, 0, 0)),
                  pl.BlockSpec((1, dv), lambda bi, h: (0, h))],
        out_specs=pl.BlockSpec((None, s, dv), lambda bi, h: (bi, 0, h)),
        out_shape=jax.ShapeDtypeStruct((b, s, n_heads * dv), BF16),
        scratch_shapes=[pltpu.VMEM((dk, dv), F32)],
        compiler_params=_params(2),
        name="retention",
    )(proj, proj, proj, proj, cos, sin, dec, qd, kd, bd, gn_w.reshape(1, -1))


def _xattn_kernel(q_ref, kv_ref, o_ref, *, n_heads, scale):
    d = q_ref.shape[1]
    dh = d // n_heads
    for h in range(n_heads):
        cols = slice(h * dh, (h + 1) * dh)
        s = lax.dot_general(q_ref[:, cols], kv_ref[:, cols], _NT, preferred_element_type=F32)
        s = s * scale
        m = jnp.max(s, axis=-1, keepdims=True)
        p = jnp.exp(s - m)
        l = jnp.sum(p, axis=-1, keepdims=True)
        o = jnp.dot(p.astype(BF16), kv_ref[:, d + h * dh:d + (h + 1) * dh],
                    preferred_element_type=F32)
        o_ref[:, cols] = (o / l).astype(o_ref.dtype)


def _cross_attention(q, kv, n_heads, tq=512):
    b, s, d = q.shape
    m = kv.shape[1]
    kern = functools.partial(_xattn_kernel, n_heads=n_heads, scale=(d // n_heads) ** -0.5)
    return pl.pallas_call(
        kern,
        grid=(b, s // tq),
        in_specs=[pl.BlockSpec((None, tq, d), lambda bi, i: (bi, i, 0)),
                  pl.BlockSpec((None, m, 2 * d), lambda bi, i: (bi, 0, 0))],
        out_specs=pl.BlockSpec((None, tq, d), lambda bi, i: (bi, i, 0)),
        out_shape=jax.ShapeDtypeStruct((b, s, d), BF16),
        compiler_params=_params(2),
        name="cross_attention",
    )(q, kv)


def _take_first_max(work, rows):
    m = jnp.max(work, axis=0, keepdims=True)
    first = jnp.min(jnp.where(work == m, rows, np.float32(work.shape[0])), axis=0, keepdims=True)
    return m, rows == first


def _top_rows(s, k):
    rows = lax.broadcasted_iota(jnp.int32, s.shape, 0).astype(F32)
    work = s
    rank = jnp.full(s.shape, NOT_RANKED, F32)
    vals = []
    for r in range(k):
        m, sel = _take_first_max(work, rows)
        rank = jnp.where(sel, np.float32(r), rank)
        work = jnp.where(sel, -jnp.inf, work)
        vals.append(m)
    return vals, rank


def _candidate_cells(k):
    cells = []
    for ra in range(k):
        cells += [(ra, rb) for rb in range(k // (ra + 1))]
    single = [c for c in cells if k // (c[0] + 1) == 1]
    multi = [c for c in cells if c not in single]
    pad = (-len(multi)) % 8
    return multi + [None] * pad + single


def _route_kernel(q_ref, keys_ref, rank1_ref, e1_ref, cnt_ref, e0_ref):
    k = PEER_TOPK
    tl = q_ref.shape[0]
    cells = _candidate_cells(k)
    for g in range(tl // LANES):
        q = q_ref[g * LANES:(g + 1) * LANES, :]
        s0 = lax.dot_general(keys_ref[0].astype(BF16), q[:, :PEER_HALF], _NT,
                             preferred_element_type=F32)
        s1 = lax.dot_general(keys_ref[1].astype(BF16), q[:, PEER_HALF:], _NT,
                             preferred_element_type=F32)
        a, rank0 = _top_rows(s0, k)
        b, rank1 = _top_rows(s1, k)
        a_all, b_all = jnp.concatenate(a, axis=0), jnp.concatenate(b, axis=0)
        ea_all, eb_all = jnp.exp(a_all - a[0]), jnp.exp(b_all - b[0])
        neg = jnp.full_like(a[0], -jnp.inf)
        zero = jnp.zeros_like(a[0])
        n0, n1 = k, k // 2
        mid = cells[n0 + n1:len(cells) - k // 2]
        cand = jnp.concatenate(
            [a[0] + b_all, a[1] + b_all[:n1]]
            + [neg if c is None else a_all[c[0]:c[0] + 1] + b_all[c[1]:c[1] + 1] for c in mid]
            + [a_all[k // 2:] + b[0]], axis=0)
        wgt = jnp.concatenate(
            [ea_all[0:1] * eb_all, ea_all[1:2] * eb_all[:n1]]
            + [zero if c is None else ea_all[c[0]:c[0] + 1] * eb_all[c[1]:c[1] + 1] for c in mid]
            + [ea_all[k // 2:] * eb_all[0:1]], axis=0)
        rows = lax.broadcasted_iota(jnp.int32, cand.shape, 0).astype(F32)
        work = cand
        chosen = jnp.zeros(cand.shape, F32)
        for _ in range(k):
            _, sel = _take_first_max(work, rows)
            chosen = jnp.where(sel, 1.0, chosen)
            work = jnp.where(sel, -jnp.inf, work)
        z = jnp.sum(chosen * wgt, axis=0, keepdims=True)
        cnt = jnp.zeros(s0.shape, F32)
        for ra in range(k):
            mine = [i for i, c in enumerate(cells) if c is not None and c[0] == ra]
            n_ra = jnp.sum(chosen[mine[0]:mine[-1] + 1], axis=0, keepdims=True)
            cnt = jnp.where(rank0 == np.float32(ra), n_ra, cnt)
        lanes = slice(g * LANES, (g + 1) * LANES)
        rank1_ref[:, lanes] = rank1.astype(rank1_ref.dtype)
        e1_ref[:, lanes] = (jnp.exp(s1 - b[0]) / z).astype(e1_ref.dtype)
        cnt_ref[:, lanes] = cnt
        e0_ref[:, lanes] = jnp.exp(s0 - a[0])


def _peer_route(qp, subkeys, tl=256):
    t = qp.shape[0]
    hp, _, nk, half = subkeys.shape
    out = lambda dt: jax.ShapeDtypeStruct((hp, nk, t), dt)
    ospec = pl.BlockSpec((None, nk, tl), lambda i, h: (h, 0, i))
    return pl.pallas_call(
        _route_kernel,
        grid=(t // tl, hp),
        in_specs=[pl.BlockSpec((tl, 2 * half), lambda i, h: (i, h)),
                  pl.BlockSpec((None, 2, nk, half), lambda i, h: (h, 0, 0, 0))],
        out_specs=[ospec] * 4,
        out_shape=[out(BF16), out(BF16), out(F32), out(F32)],
        compiler_params=_params(2),
        name="peer_route",
    )(qp, subkeys)


def _peer_kernel(x_ref, u_ref, v_ref, rank1_ref, e1_ref, cnt_ref, e0_ref, o_ref, acc_ref, xt_ref,
                 *stage_refs, sub):
    e = pl.program_id(1)
    n_heads, nk, tm = rank1_ref.shape
    te, d = u_ref.shape
    n_sub = te // sub
    rows_per_sub = sub // nk
    n_pieces = n_heads
    kc = d // n_pieces
    act_refs, wt_refs = stage_refs[:n_sub], stage_refs[n_sub:]

    @pl.when(e == 0)
    def _():
        acc_ref[...] = jnp.zeros_like(acc_ref)
        xt_ref[...] = x_ref[...].T

    def up_piece(j, p):
        part = jnp.dot(u_ref[j * sub:(j + 1) * sub, p * kc:(p + 1) * kc],
                       xt_ref[p * kc:(p + 1) * kc, :], preferred_element_type=F32)
        if p == 0:
            act_refs[j][...] = part
        else:
            act_refs[j][...] += part

    def gate_piece(j, p):
        ii, g = divmod(p, n_pieces // rows_per_sub)
        tg = tm // (n_pieces // rows_per_sub)
        lanes = slice(g * tg, (g + 1) * tg)
        row = j * rows_per_sub + ii
        gate = None
        for h in range(n_heads):
            cnt = cnt_ref[h, row:row + 1, lanes].astype(BF16)
            e0 = e0_ref[h, row:row + 1, lanes].astype(BF16)
            zero = jnp.zeros((), BF16)
            term = e0 * jnp.where(rank1_ref[h, :, lanes] < cnt, e1_ref[h, :, lanes], zero)
            gate = term if gate is None else gate + term
        act = act_refs[j][ii * nk:(ii + 1) * nk, lanes]
        act = 0.5 * act * (1.0 + lax.erf(act * np.float32(1.0 / np.sqrt(2.0))))
        wt_refs[j][lanes, ii * nk:(ii + 1) * nk] = (gate * act.astype(BF16)).T

    def down_piece(j, p):
        cols = slice(p * kc, (p + 1) * kc)
        acc_ref[:, cols] += jnp.dot(wt_refs[j][...], v_ref[j * sub:(j + 1) * sub, cols],
                                    preferred_element_type=F32)

    for step in range(n_sub + 2):
        for p in range(n_pieces):
            if step < n_sub:
                up_piece(step, p)
            if 0 <= step - 1 < n_sub:
                gate_piece(step - 1, p)
            if 0 <= step - 2 < n_sub:
                down_piece(step - 2, p)

    @pl.when(e == pl.num_programs(1) - 1)
    def _():
        o_ref[...] = acc_ref[...].astype(o_ref.dtype)


def _peer_experts(x, u, v, rank1, e1, cnt, e0, tm=512, te=512, sub=256):
    t, d = x.shape
    n_exp = u.shape[0]
    hp, nk, _ = rank1.shape
    rows = te // nk
    tab = pl.BlockSpec((hp, nk, tm), lambda i, e: (0, 0, i))
    row_tab = pl.BlockSpec((hp, None, rows, tm), lambda i, e: (0, e, 0, i))
    cnt = cnt.reshape(hp, nk // rows, rows, t)
    e0 = e0.reshape(hp, nk // rows, rows, t)
    return pl.pallas_call(
        functools.partial(_peer_kernel, sub=sub),
        grid=(t // tm, n_exp // te),
        in_specs=[pl.BlockSpec((tm, d), lambda i, e: (i, 0)),
                  pl.BlockSpec((te, d), lambda i, e: (e, 0)),
                  pl.BlockSpec((te, d), lambda i, e: (e, 0)),
                  tab, tab, row_tab, row_tab],
        out_specs=pl.BlockSpec((tm, d), lambda i, e: (i, 0)),
        out_shape=jax.ShapeDtypeStruct((t, d), BF16),
        scratch_shapes=([pltpu.VMEM((tm, d), F32), pltpu.VMEM((d, tm), BF16)]
                        + [pltpu.VMEM((sub, tm), F32)] * (te // sub)
                        + [pltpu.VMEM((tm, sub), BF16)] * (te // sub)),
        compiler_params=_params(2),
        name="peer_experts",
    )(x, u, v, rank1, e1, cnt, e0)


def kernel(x, mem, norm1_w, w_in, attn_rel_bias, ret_gn_w, w_out, norm2_w, mem_norm_w, xattn_wq,
           xattn_wkv, xattn_wo, norm3_w, peer_wq, peer_subkeys, peer_u, peer_v, final_norm_w):
    b, s, d = x.shape
    t = b * s
    depth = w_in.shape[0]
    n_attn_heads = attn_rel_bias.shape[1]
    attn_width = n_attn_heads * ATTN_HEAD_DIM
    n_ret_heads = ret_gn_w.shape[1] // RET_V_DIM
    h = x.reshape(t, d)
    xn = _rmsnorm(h, norm1_w[0], BF16)
    for l in range(depth):
        proj = _matmul(xn, w_in[l], BF16, name="in_proj").reshape(b, s, -1)
        a_out = _chunk_attention(proj, attn_rel_bias[l], n_attn_heads)
        r_out = _retention(proj, ret_gn_w[l], n_ret_heads, 3 * attn_width)
        mix = jnp.concatenate([a_out, r_out], axis=-1).reshape(t, -1)
        h = _matmul(mix, w_out[l], F32, residual=h, name="out_proj")

        mem_n = _rmsnorm(mem.reshape(-1, d), mem_norm_w[l], BF16)
        kv = _matmul(mem_n, xattn_wkv[l], BF16, name="mem_kv").reshape(b, -1, 2 * d)
        xq = _matmul(_rmsnorm(h, norm2_w[l], BF16), xattn_wq[l], BF16, name="xattn_q")
        xo = _cross_attention(xq.reshape(b, s, d), kv, XATTN_HEADS).reshape(t, d)
        h = _matmul(xo, xattn_wo[l], F32, residual=h, name="xattn_o")

        xn3 = _rmsnorm(h, norm3_w[l], BF16)
        qp = _matmul(xn3, peer_wq[l], BF16, name="peer_q")
        rank1, e1, cnt, e0 = _peer_route(qp, peer_subkeys[l])
        delta = _peer_experts(xn3, peer_u[l].astype(BF16), peer_v[l].astype(BF16),
                              rank1, e1, cnt, e0)
        if l + 1 < depth:
            h = h + delta.astype(F32)
            xn = _rmsnorm(h, norm1_w[l + 1], BF16)
    return _add_rmsnorm(h, delta, final_norm_w, F32).reshape(b, s, d)
```

```python
import functools

import numpy as np
import jax
import jax.numpy as jnp
from jax import lax
from jax.experimental import pallas as pl
from jax.experimental.pallas import tpu as pltpu

F32 = jnp.float32
BF16 = jnp.bfloat16

CHUNK = 64
LEFT_CHUNKS = 8
LEFT = LEFT_CHUNKS * CHUNK
ATTN_HEAD_DIM = 128
MAX_REL_DIST = 256
RET_V_DIM = 256
RET_QK_DIM = 128
ROPE_BASE = 10000.0
XATTN_HEADS = 4
PEER_HEADS = 8
PEER_KEYS = 128
PEER_HALF = 128
PEER_TOPK = 16
EPS = 1e-6
NEG_INF = -1e30
NOT_RANKED = 1e9

VMEM_LIMIT_BYTES = 56 * 1024 * 1024
LANES = 128

_NT = (((1,), (1,)), ((), ()))
_TN = (((0,), (0,)), ((), ()))


def _params(n_grid_dims, flags=None):
    return pltpu.CompilerParams(
        dimension_semantics=("arbitrary",) * n_grid_dims,
        vmem_limit_bytes=VMEM_LIMIT_BYTES,
        flags=flags)


def _rmsnorm_kernel(x_ref, w_ref, o_ref):
    x = x_ref[...]
    ms = jnp.mean(x * x, axis=-1, keepdims=True)
    o_ref[...] = (x * lax.rsqrt(ms + EPS) * w_ref[...]).astype(o_ref.dtype)


def _rmsnorm(x, w, out_dtype, rows=256):
    m, d = x.shape
    return pl.pallas_call(
        _rmsnorm_kernel,
        grid=(m // rows,),
        in_specs=[pl.BlockSpec((rows, d), lambda i: (i, 0)),
                  pl.BlockSpec((1, d), lambda i: (0, 0))],
        out_specs=pl.BlockSpec((rows, d), lambda i: (i, 0)),
        out_shape=jax.ShapeDtypeStruct((m, d), out_dtype),
        compiler_params=_params(1),
        name="rmsnorm",
    )(x, w.reshape(1, d))


def _add_rmsnorm_kernel(x_ref, y_ref, w_ref, o_ref):
    x = x_ref[...] + y_ref[...].astype(F32)
    ms = jnp.mean(x * x, axis=-1, keepdims=True)
    o_ref[...] = (x * lax.rsqrt(ms + EPS) * w_ref[...]).astype(o_ref.dtype)


def _add_rmsnorm(x, y, w, out_dtype, rows=256):
    m, d = x.shape
    return pl.pallas_call(
        _add_rmsnorm_kernel,
        grid=(m // rows,),
        in_specs=[pl.BlockSpec((rows, d), lambda i: (i, 0)),
                  pl.BlockSpec((rows, d), lambda i: (i, 0)),
                  pl.BlockSpec((1, d), lambda i: (0, 0))],
        out_specs=pl.BlockSpec((rows, d), lambda i: (i, 0)),
        out_shape=jax.ShapeDtypeStruct((m, d), out_dtype),
        compiler_params=_params(1),
        name="add_rmsnorm",
    )(x, y, w.reshape(1, d))


def _mm_kernel(a_ref, w_ref, o_ref):
    acc = jnp.dot(a_ref[...], w_ref[...].astype(BF16), preferred_element_type=F32)
    o_ref[...] = acc.astype(o_ref.dtype)


def _mm_res_kernel(a_ref, w_ref, r_ref, o_ref):
    acc = jnp.dot(a_ref[...], w_ref[...].astype(BF16), preferred_element_type=F32)
    o_ref[...] = (r_ref[...] + acc).astype(o_ref.dtype)


def _matmul(a, w, out_dtype, residual=None, tm=1024, tn=512, name="matmul"):
    m, k = a.shape
    n = w.shape[1]
    tm, tn = min(tm, m), min(tn, n)
    in_specs = [pl.BlockSpec((tm, k), lambda i, j: (i, 0)),
                pl.BlockSpec((k, tn), lambda i, j: (0, j))]
    args = [a, w]
    body = _mm_kernel
    if residual is not None:
        in_specs.append(pl.BlockSpec((tm, tn), lambda i, j: (i, j)))
        args.append(residual)
        body = _mm_res_kernel
    return pl.pallas_call(
        body,
        grid=(m // tm, n // tn),
        in_specs=in_specs,
        out_specs=pl.BlockSpec((tm, tn), lambda i, j: (i, j)),
        out_shape=jax.ShapeDtypeStruct((m, n), out_dtype),
        compiler_params=_params(2),
        name=name,
    )(*args)


def _attn_kernel(q_ref, k_ref, v_ref, base_ref, o_ref, bias_ref, *, qb, scale):
    seq = q_ref.shape[0]
    width = LEFT + qb

    @pl.when(pl.program_id(1) == 0)
    def _():
        toeplitz = pltpu.roll(jnp.broadcast_to(base_ref[...], (qb, base_ref.shape[-1])),
                              0, 1, stride=1, stride_axis=0)
        q_chunk = lax.broadcasted_iota(jnp.int32, (qb, width), 0) // CHUNK
        c_chunk = lax.broadcasted_iota(jnp.int32, (qb, width), 1) // CHUNK
        in_band = (c_chunk >= q_chunk) & (c_chunk <= q_chunk + LEFT_CHUNKS)
        bias_ref[...] = jnp.where(in_band, toeplitz[:, :width], NEG_INF)

    for i in range(seq // qb):
        q0 = i * qb
        k0 = max(0, q0 - LEFT)
        kw = q0 + qb - k0
        c0 = k0 - (q0 - LEFT)
        s = lax.dot_general(q_ref[q0:q0 + qb, :], k_ref[k0:k0 + kw, :], _NT,
                            preferred_element_type=F32)
        s = s * scale + bias_ref[:, c0:c0 + kw]
        m = jnp.max(s, axis=-1, keepdims=True)
        p = jnp.exp(s - m)
        l = jnp.sum(p, axis=-1, keepdims=True)
        o = jnp.dot(p.astype(BF16), v_ref[k0:k0 + kw, :], preferred_element_type=F32)
        o_ref[q0:q0 + qb, :] = (o / l).astype(o_ref.dtype)


def _attn_bias_base(rel_bias, qb):
    w = pl.next_power_of_2(LEFT + 2 * qb)
    j = jnp.arange(w)
    j = jnp.where(j < LEFT + qb, j, j - w)
    idx = jnp.clip(LEFT - j, -MAX_REL_DIST, MAX_REL_DIST) + MAX_REL_DIST
    return rel_bias[:, None, idx].astype(F32)


def _chunk_attention(proj, rel_bias, n_heads, qb=256):
    b, s, _ = proj.shape
    dh = ATTN_HEAD_DIM
    base = _attn_bias_base(rel_bias, qb)
    kern = functools.partial(_attn_kernel, qb=qb, scale=dh ** -0.5)
    return pl.pallas_call(
        kern,
        grid=(n_heads, b),
        in_specs=[pl.BlockSpec((None, s, dh), lambda h, bi: (bi, 0, h)),
                  pl.BlockSpec((None, s, dh), lambda h, bi: (bi, 0, n_heads + h)),
                  pl.BlockSpec((None, s, dh), lambda h, bi: (bi, 0, 2 * n_heads + h)),
                  pl.BlockSpec((None, 1, base.shape[-1]), lambda h, bi: (h, 0, 0))],
        out_specs=pl.BlockSpec((None, s, dh), lambda h, bi: (bi, 0, h)),
        out_shape=jax.ShapeDtypeStruct((b, s, n_heads * dh), BF16),
        scratch_shapes=[pltpu.VMEM((qb, LEFT + qb), F32)],
        compiler_params=_params(2),
        name="chunk_attention",
    )(proj, proj, proj, base)


def _ret_kernel(q_ref, k_ref, v_ref, g_ref, cos_ref, sin_ref, dec_ref, qd_ref, kd_ref, bd_ref,
                gnw_ref, o_ref, state_ref, *, rb, scale):
    seq = q_ref.shape[0]
    state_ref[...] = jnp.zeros_like(state_ref)
    half = q_ref.shape[1] // 2

    def body(n, carry):
        r0 = pl.multiple_of(n * rb, rb)
        rows = pl.ds(r0, rb)
        cos = cos_ref[rows, :]
        sin = sin_ref[rows, :]
        q = q_ref[rows, :].astype(F32)
        k = k_ref[rows, :].astype(F32)
        q = q * cos + pltpu.roll(q, half, 1) * sin
        k = (k * cos + pltpu.roll(k, half, 1) * sin) * scale
        v = v_ref[rows, :]
        a = lax.dot_general(q.astype(BF16), k.astype(BF16), _NT, preferred_element_type=F32)
        a = a * dec_ref[...]
        st = state_ref[...]
        y = jnp.dot(a.astype(BF16), v, preferred_element_type=F32)
        y = y + jnp.dot((q * qd_ref[...]).astype(BF16), st.astype(BF16),
                        preferred_element_type=F32)
        kd = (k * kd_ref[...]).astype(BF16)
        kv = lax.dot_general(kd, v, _TN, preferred_element_type=F32)
        state_ref[...] = st * bd_ref[...] + kv
        mu = jnp.mean(y, axis=-1, keepdims=True)
        yc = y - mu
        var = jnp.mean(yc * yc, axis=-1, keepdims=True)
        yn = yc * lax.rsqrt(var + EPS) * gnw_ref[...]
        g = g_ref[rows, :].astype(F32)
        o_ref[rows, :] = (g * (1.0 / (1.0 + jnp.exp(-g))) * yn).astype(o_ref.dtype)
        return carry

    lax.fori_loop(0, seq // rb, body, 0)


def _retention(proj, gn_w, n_heads, col0, rb=256):
    b, s, _ = proj.shape
    dk, dv = RET_QK_DIM, RET_V_DIM
    qk0 = col0 // dk
    v0 = (col0 + 2 * n_heads * dk) // dv
    inv_freq = 1.0 / (ROPE_BASE ** (jnp.arange(0, dk, 2, dtype=F32) / dk))
    ang = jnp.arange(s, dtype=F32)[:, None] * inv_freq[None, :]
    cos = jnp.concatenate([jnp.cos(ang), jnp.cos(ang)], axis=-1)
    sin = jnp.concatenate([-jnp.sin(ang), jnp.sin(ang)], axis=-1)
    log_gamma = jnp.log1p(-jnp.power(2.0, -5.0 - jnp.arange(n_heads, dtype=F32)))
    pos = jnp.arange(rb, dtype=F32)
    chunk_of = jnp.arange(rb) // CHUNK
    causal = (chunk_of[None, :] <= chunk_of[:, None]).astype(F32)
    dec = jnp.exp(log_gamma[:, None, None] * jnp.abs(pos[:, None] - pos[None, :])) * causal[None]
    qd = jnp.exp(log_gamma[:, None] * (pos + 1.0))[:, :, None]
    kd = jnp.exp(log_gamma[:, None] * (rb - 1.0 - pos))[:, :, None]
    bd = jnp.exp(log_gamma * rb)[:, None, None]
    kern = functools.partial(_ret_kernel, rb=rb, scale=dk ** -0.5)
    return pl.pallas_call(
        kern,
        grid=(b, n_heads),
        in_specs=[pl.BlockSpec((None, s, dk), lambda bi, h: (bi, 0, qk0 + h)),
                  pl.BlockSpec((None, s, dk), lambda bi, h: (bi, 0, qk0 + n_heads + h)),
                  pl.BlockSpec((None, s, dv), lambda bi, h: (bi, 0, v0 + h)),
                  pl.BlockSpec((None, s, dv), lambda bi, h: (bi, 0, v0 + n_heads + h)),
                  pl.BlockSpec((s, dk), lambda bi, h: (0, 0)),
                  pl.BlockSpec((s, dk), lambda bi, h: (0, 0)),
                  pl.BlockSpec((None, rb, rb), lambda bi, h: (h, 0, 0)),
                  pl.BlockSpec((None, rb, 1), lambda bi, h: (h, 0, 0)),
                  pl.BlockSpec((None, rb, 1), lambda bi, h: (h, 0, 0)),
                  pl.BlockSpec((None, 1, 1), lambda bi, h: (h, 0, 0)),
                  pl.BlockSpec((1, dv), lambda bi, h: (0, h))],
        out_specs=pl.BlockSpec((None, s, dv), lambda bi, h: (bi, 0, h)),
        out_shape=jax.ShapeDtypeStruct((b, s, n_heads * dv), BF16),
        scratch_shapes=[pltpu.VMEM((dk, dv), F32)],
        compiler_params=_params(2),
        name="retention",
    )(proj, proj, proj, proj, cos, sin, dec, qd, kd, bd, gn_w.reshape(1, -1))


def _xattn_kernel(q_ref, kv_ref, o_ref, *, n_heads, scale):
    d = q_ref.shape[1]
    dh = d // n_heads
    for h in range(n_heads):
        cols = slice(h * dh, (h + 1) * dh)
        s = lax.dot_general(q_ref[:, cols], kv_ref[:, cols], _NT, preferred_element_type=F32)
        s = s * scale
        m = jnp.max(s, axis=-1, keepdims=True)
        p = jnp.exp(s - m)
        l = jnp.sum(p, axis=-1, keepdims=True)
        o = jnp.dot(p.astype(BF16), kv_ref[:, d + h * dh:d + (h + 1) * dh],
                    preferred_element_type=F32)
        o_ref[:, cols] = (o / l).astype(o_ref.dtype)


def _cross_attention(q, kv, n_heads, tq=512):
    b, s, d = q.shape
    m = kv.shape[1]
    kern = functools.partial(_xattn_kernel, n_heads=n_heads, scale=(d // n_heads) ** -0.5)
    return pl.pallas_call(
        kern,
        grid=(b, s // tq),
        in_specs=[pl.BlockSpec((None, tq, d), lambda bi, i: (bi, i, 0)),
                  pl.BlockSpec((None, m, 2 * d), lambda bi, i: (bi, 0, 0))],
        out_specs=pl.BlockSpec((None, tq, d), lambda bi, i: (bi, i, 0)),
        out_shape=jax.ShapeDtypeStruct((b, s, d), BF16),
        compiler_params=_params(2),
        name="cross_attention",
    )(q, kv)


def _take_first_max(work, rows):
    m = jnp.max(work, axis=0, keepdims=True)
    first = jnp.min(jnp.where(work == m, rows, np.float32(work.shape[0])), axis=0, keepdims=True)
    return m, rows == first


def _top_rows(s, k):
    rows = lax.broadcasted_iota(jnp.int32, s.shape, 0).astype(F32)
    work = s
    rank = jnp.full(s.shape, NOT_RANKED, F32)
    vals = []
    for r in range(k):
        m, sel = _take_first_max(work, rows)
        rank = jnp.where(sel, np.float32(r), rank)
        work = jnp.where(sel, -jnp.inf, work)
        vals.append(m)
    return vals, rank


def _candidate_cells(k):
    cells = []
    for ra in range(k):
        cells += [(ra, rb) for rb in range(k // (ra + 1))]
    single = [c for c in cells if k // (c[0] + 1) == 1]
    multi = [c for c in cells if c not in single]
    pad = (-len(multi)) % 8
    return multi + [None] * pad + single


def _route_kernel(q_ref, keys_ref, rank1_ref, e1_ref, cnt_ref, e0_ref):
    k = PEER_TOPK
    tl = q_ref.shape[0]
    cells = _candidate_cells(k)
    for g in range(tl // LANES):
        q = q_ref[g * LANES:(g + 1) * LANES, :]
        s0 = lax.dot_general(keys_ref[0].astype(BF16), q[:, :PEER_HALF], _NT,
                             preferred_element_type=F32)
        s1 = lax.dot_general(keys_ref[1].astype(BF16), q[:, PEER_HALF:], _NT,
                             preferred_element_type=F32)
        a, rank0 = _top_rows(s0, k)
        b, rank1 = _top_rows(s1, k)
        a_all, b_all = jnp.concatenate(a, axis=0), jnp.concatenate(b, axis=0)
        ea_all, eb_all = jnp.exp(a_all - a[0]), jnp.exp(b_all - b[0])
        neg = jnp.full_like(a[0], -jnp.inf)
        zero = jnp.zeros_like(a[0])
        n0, n1 = k, k // 2
        mid = cells[n0 + n1:len(cells) - k // 2]
        cand = jnp.concatenate(
            [a[0] + b_all, a[1] + b_all[:n1]]
            + [neg if c is None else a_all[c[0]:c[0] + 1] + b_all[c[1]:c[1] + 1] for c in mid]
            + [a_all[k // 2:] + b[0]], axis=0)
        wgt = jnp.concatenate(
            [ea_all[0:1] * eb_all, ea_all[1:2] * eb_all[:n1]]
            + [zero if c is None else ea_all[c[0]:c[0] + 1] * eb_all[c[1]:c[1] + 1] for c in mid]
            + [ea_all[k // 2:] * eb_all[0:1]], axis=0)
        rows = lax.broadcasted_iota(jnp.int32, cand.shape, 0).astype(F32)
        work = cand
        chosen = jnp.zeros(cand.shape, F32)
        for _ in range(k):
            _, sel = _take_first_max(work, rows)
            chosen = jnp.where(sel, 1.0, chosen)
            work = jnp.where(sel, -jnp.inf, work)
        z = jnp.sum(chosen * wgt, axis=0, keepdims=True)
        cnt = jnp.zeros(s0.shape, F32)
        for ra in range(k):
            mine = [i for i, c in enumerate(cells) if c is not None and c[0] == ra]
            n_ra = jnp.sum(chosen[mine[0]:mine[-1] + 1], axis=0, keepdims=True)
            cnt = jnp.where(rank0 == np.float32(ra), n_ra, cnt)
        lanes = slice(g * LANES, (g + 1) * LANES)
        rank1_ref[:, lanes] = rank1.astype(rank1_ref.dtype)
        e1_ref[:, lanes] = (jnp.exp(s1 - b[0]) / z).astype(e1_ref.dtype)
        cnt_ref[:, lanes] = cnt
        e0_ref[:, lanes] = jnp.exp(s0 - a[0])


def _peer_route(qp, subkeys, tl=256):
    t = qp.shape[0]
    hp, _, nk, half = subkeys.shape
    out = lambda dt: jax.ShapeDtypeStruct((hp, nk, t), dt)
    ospec = pl.BlockSpec((None, nk, tl), lambda i, h: (h, 0, i))
    return pl.pallas_call(
        _route_kernel,
        grid=(t // tl, hp),
        in_specs=[pl.BlockSpec((tl, 2 * half), lambda i, h: (i, h)),
                  pl.BlockSpec((None, 2, nk, half), lambda i, h: (h, 0, 0, 0))],
        out_specs=[ospec] * 4,
        out_shape=[out(BF16), out(BF16), out(F32), out(F32)],
        compiler_params=_params(2),
        name="peer_route",
    )(qp, subkeys)


def _peer_kernel(x_ref, u_ref, v_ref, rank1_ref, e1_ref, cnt_ref, e0_ref, zero_ref, o_ref, acc_ref,
                 xt_ref, *stage_refs, sub):
    e = pl.program_id(1)
    n_heads, nk, tm = rank1_ref.shape
    te, d = u_ref.shape
    n_sub = te // sub
    rows_per_sub = sub // nk
    n_pieces = n_heads
    kc = d // n_pieces
    act_refs, wt_refs = stage_refs[:n_sub], stage_refs[n_sub:]

    @pl.when(e == 0)
    def _():
        acc_ref[...] = jnp.zeros_like(acc_ref)
        xt_ref[...] = x_ref[...].T

    def up_piece(j, p):
        part = jnp.dot(u_ref[j * sub:(j + 1) * sub, p * kc:(p + 1) * kc],
                       xt_ref[p * kc:(p + 1) * kc, :], preferred_element_type=F32)
        if p == 0:
            act_refs[j][...] = part
        else:
            act_refs[j][...] += part

    tg = tm // (n_pieces // rows_per_sub)

    def gate_piece(j, p, just_stored):
        ii, g = divmod(p, n_pieces // rows_per_sub)
        lanes = slice(g * tg, (g + 1) * tg)
        row = j * rows_per_sub + ii
        gate = None
        tie = lax.bitcast_convert_type(
            lax.bitcast_convert_type(just_stored, jnp.int32) & zero_ref[...], F32)
        for h in range(n_heads):
            cnt = (cnt_ref[h, row:row + 1, lanes] + tie).astype(BF16)
            e0 = e0_ref[h, row:row + 1, lanes].astype(BF16)
            zero = jnp.zeros((), BF16)
            term = e0 * jnp.where(rank1_ref[h, :, lanes] < cnt, e1_ref[h, :, lanes], zero)
            gate = term if gate is None else gate + term
        act = act_refs[j][ii * nk:(ii + 1) * nk, lanes]
        act = 0.5 * act * (1.0 + lax.erf(act * np.float32(1.0 / np.sqrt(2.0))))
        wt_refs[j][lanes, ii * nk:(ii + 1) * nk] = (gate * act.astype(BF16)).T

    def down_piece(j, p):
        cols = slice(p * kc, (p + 1) * kc)
        acc_ref[:, cols] += jnp.dot(wt_refs[j][...], v_ref[j * sub:(j + 1) * sub, cols],
                                    preferred_element_type=F32)

    for step in range(n_sub + 2):
        for p in range(n_pieces):
            if step < n_sub:
                up_piece(step, p)
            if 0 <= step - 2 < n_sub:
                down_piece(step - 2, p)
            if 0 <= step - 1 < n_sub:
                if step < n_sub:
                    just_stored = act_refs[step][0:1, 0:tg]
                else:
                    just_stored = acc_ref[0:1, p * kc:p * kc + tg]
                gate_piece(step - 1, p, just_stored)

    @pl.when(e == pl.num_programs(1) - 1)
    def _():
        o_ref[...] = acc_ref[...].astype(o_ref.dtype)


def _peer_experts(x, u, v, rank1, e1, cnt, e0, tm=512, te=512, sub=256):
    t, d = x.shape
    n_exp = u.shape[0]
    hp, nk, _ = rank1.shape
    rows = te // nk
    tab = pl.BlockSpec((hp, nk, tm), lambda i, e: (0, 0, i))
    row_tab = pl.BlockSpec((hp, None, rows, tm), lambda i, e: (0, e, 0, i))
    cnt = cnt.reshape(hp, nk // rows, rows, t)
    e0 = e0.reshape(hp, nk // rows, rows, t)
    tg = tm // (hp // (sub // nk))
    zero_bits = jnp.zeros((1, tg), jnp.int32)
    return pl.pallas_call(
        functools.partial(_peer_kernel, sub=sub),
        grid=(t // tm, n_exp // te),
        in_specs=[pl.BlockSpec((tm, d), lambda i, e: (i, 0)),
                  pl.BlockSpec((te, d), lambda i, e: (e, 0)),
                  pl.BlockSpec((te, d), lambda i, e: (e, 0)),
                  tab, tab, row_tab, row_tab,
                  pl.BlockSpec((1, tg), lambda i, e: (0, 0))],
        out_specs=pl.BlockSpec((tm, d), lambda i, e: (i, 0)),
        out_shape=jax.ShapeDtypeStruct((t, d), BF16),
        scratch_shapes=([pltpu.VMEM((tm, d), F32), pltpu.VMEM((d, tm), BF16)]
                        + [pltpu.VMEM((sub, tm), F32)] * (te // sub)
                        + [pltpu.VMEM((tm, sub), BF16)] * (te // sub)),
        compiler_params=_params(2),
        name="peer_experts",
    )(x, u, v, rank1, e1, cnt, e0, zero_bits)


def kernel(x, mem, norm1_w, w_in, attn_rel_bias, ret_gn_w, w_out, norm2_w, mem_norm_w, xattn_wq,
           xattn_wkv, xattn_wo, norm3_w, peer_wq, peer_subkeys, peer_u, peer_v, final_norm_w):
    b, s, d = x.shape
    t = b * s
    depth = w_in.shape[0]
    n_attn_heads = attn_rel_bias.shape[1]
    attn_width = n_attn_heads * ATTN_HEAD_DIM
    n_ret_heads = ret_gn_w.shape[1] // RET_V_DIM
    h = x.reshape(t, d)
    xn = _rmsnorm(h, norm1_w[0], BF16)
    for l in range(depth):
        proj = _matmul(xn, w_in[l], BF16, name="in_proj").reshape(b, s, -1)
        a_out = _chunk_attention(proj, attn_rel_bias[l], n_attn_heads)
        r_out = _retention(proj, ret_gn_w[l], n_ret_heads, 3 * attn_width)
        mix = jnp.concatenate([a_out, r_out], axis=-1).reshape(t, -1)
        h = _matmul(mix, w_out[l], F32, residual=h, name="out_proj")

        mem_n = _rmsnorm(mem.reshape(-1, d), mem_norm_w[l], BF16)
        kv = _matmul(mem_n, xattn_wkv[l], BF16, name="mem_kv").reshape(b, -1, 2 * d)
        xq = _matmul(_rmsnorm(h, norm2_w[l], BF16), xattn_wq[l], BF16, name="xattn_q")
        xo = _cross_attention(xq.reshape(b, s, d), kv, XATTN_HEADS).reshape(t, d)
        h = _matmul(xo, xattn_wo[l], F32, residual=h, name="xattn_o")

        xn3 = _rmsnorm(h, norm3_w[l], BF16)
        qp = _matmul(xn3, peer_wq[l], BF16, name="peer_q")
        rank1, e1, cnt, e0 = _peer_route(qp, peer_subkeys[l])
        delta = _peer_experts(xn3, peer_u[l].astype(BF16), peer_v[l].astype(BF16),
                              rank1, e1, cnt, e0)
        if l + 1 < depth:
            h = h + delta.astype(F32)
            xn = _rmsnorm(h, norm1_w[l + 1], BF16)
    return _add_rmsnorm(h, delta, final_norm_w, F32).reshape(b, s, d)
```

```python
import functools

import numpy as np
import jax
import jax.numpy as jnp
from jax import lax
from jax.experimental import pallas as pl
from jax.experimental.pallas import tpu as pltpu

F32 = jnp.float32
BF16 = jnp.bfloat16

CHUNK = 64
LEFT_CHUNKS = 8
LEFT = LEFT_CHUNKS * CHUNK
ATTN_HEAD_DIM = 128
MAX_REL_DIST = 256
RET_V_DIM = 256
RET_QK_DIM = 128
ROPE_BASE = 10000.0
XATTN_HEADS = 4
PEER_HEADS = 8
PEER_KEYS = 128
PEER_HALF = 128
PEER_TOPK = 16
EPS = 1e-6
NEG_INF = -1e30
NOT_RANKED = 1e9

VMEM_LIMIT_BYTES = 56 * 1024 * 1024
LANES = 128

_NT = (((1,), (1,)), ((), ()))
_TN = (((0,), (0,)), ((), ()))


def _params(n_grid_dims, flags=None):
    return pltpu.CompilerParams(
        dimension_semantics=("arbitrary",) * n_grid_dims,
        vmem_limit_bytes=VMEM_LIMIT_BYTES,
        flags=flags)


def _rmsnorm_kernel(x_ref, w_ref, o_ref):
    x = x_ref[...]
    ms = jnp.mean(x * x, axis=-1, keepdims=True)
    o_ref[...] = (x * lax.rsqrt(ms + EPS) * w_ref[...]).astype(o_ref.dtype)


def _rmsnorm(x, w, out_dtype, rows=256):
    m, d = x.shape
    return pl.pallas_call(
        _rmsnorm_kernel,
        grid=(m // rows,),
        in_specs=[pl.BlockSpec((rows, d), lambda i: (i, 0)),
                  pl.BlockSpec((1, d), lambda i: (0, 0))],
        out_specs=pl.BlockSpec((rows, d), lambda i: (i, 0)),
        out_shape=jax.ShapeDtypeStruct((m, d), out_dtype),
        compiler_params=_params(1),
        name="rmsnorm",
    )(x, w.reshape(1, d))


def _add_rmsnorm_kernel(x_ref, y_ref, w_ref, o_ref):
    x = x_ref[...] + y_ref[...].astype(F32)
    ms = jnp.mean(x * x, axis=-1, keepdims=True)
    o_ref[...] = (x * lax.rsqrt(ms + EPS) * w_ref[...]).astype(o_ref.dtype)


def _add_rmsnorm(x, y, w, out_dtype, rows=256):
    m, d = x.shape
    return pl.pallas_call(
        _add_rmsnorm_kernel,
        grid=(m // rows,),
        in_specs=[pl.BlockSpec((rows, d), lambda i: (i, 0)),
                  pl.BlockSpec((rows, d), lambda i: (i, 0)),
                  pl.BlockSpec((1, d), lambda i: (0, 0))],
        out_specs=pl.BlockSpec((rows, d), lambda i: (i, 0)),
        out_shape=jax.ShapeDtypeStruct((m, d), out_dtype),
        compiler_params=_params(1),
        name="add_rmsnorm",
    )(x, y, w.reshape(1, d))


def _mm_kernel(*refs, n_lhs, has_residual):
    lhs_refs, w_ref = refs[:n_lhs], refs[n_lhs]
    acc, k0 = None, 0
    for a_ref in lhs_refs:
        k1 = k0 + a_ref.shape[1]
        part = jnp.dot(a_ref[...], w_ref[k0:k1, :].astype(BF16), preferred_element_type=F32)
        acc = part if acc is None else acc + part
        k0 = k1
    if has_residual:
        acc = refs[n_lhs + 1][...] + acc
    o_ref = refs[-1]
    o_ref[...] = acc.astype(o_ref.dtype)


def _matmul(lhs, w, out_dtype, residual=None, tm=1024, tn=512, name="matmul"):
    lhs = lhs if isinstance(lhs, (tuple, list)) else (lhs,)
    m = lhs[0].shape[0]
    k, n = w.shape
    assert sum(a.shape[1] for a in lhs) == k
    tm, tn = min(tm, m), min(tn, n)
    in_specs = [pl.BlockSpec((tm, a.shape[1]), lambda i, j: (i, 0)) for a in lhs]
    in_specs.append(pl.BlockSpec((k, tn), lambda i, j: (0, j)))
    args = [*lhs, w]
    if residual is not None:
        in_specs.append(pl.BlockSpec((tm, tn), lambda i, j: (i, j)))
        args.append(residual)
    return pl.pallas_call(
        functools.partial(_mm_kernel, n_lhs=len(lhs), has_residual=residual is not None),
        grid=(m // tm, n // tn),
        in_specs=in_specs,
        out_specs=pl.BlockSpec((tm, tn), lambda i, j: (i, j)),
        out_shape=jax.ShapeDtypeStruct((m, n), out_dtype),
        compiler_params=_params(2),
        name=name,
    )(*args)


def _attn_kernel(q_ref, k_ref, v_ref, base_ref, o_ref, bias_ref, *, qb, scale):
    seq = q_ref.shape[0]
    width = LEFT + qb

    @pl.when(pl.program_id(1) == 0)
    def _():
        toeplitz = pltpu.roll(jnp.broadcast_to(base_ref[...], (qb, base_ref.shape[-1])),
                              0, 1, stride=1, stride_axis=0)
        q_chunk = lax.broadcasted_iota(jnp.int32, (qb, width), 0) // CHUNK
        c_chunk = lax.broadcasted_iota(jnp.int32, (qb, width), 1) // CHUNK
        in_band = (c_chunk >= q_chunk) & (c_chunk <= q_chunk + LEFT_CHUNKS)
        bias_ref[...] = jnp.where(in_band, toeplitz[:, :width], NEG_INF)

    for i in range(seq // qb):
        q0 = i * qb
        k0 = max(0, q0 - LEFT)
        kw = q0 + qb - k0
        c0 = k0 - (q0 - LEFT)
        s = lax.dot_general(q_ref[q0:q0 + qb, :], k_ref[k0:k0 + kw, :], _NT,
                            preferred_element_type=F32)
        s = s * scale + bias_ref[:, c0:c0 + kw]
        m = jnp.max(s, axis=-1, keepdims=True)
        p = jnp.exp(s - m)
        l = jnp.sum(p, axis=-1, keepdims=True)
        o = jnp.dot(p.astype(BF16), v_ref[k0:k0 + kw, :], preferred_element_type=F32)
        o_ref[q0:q0 + qb, :] = (o / l).astype(o_ref.dtype)


def _attn_bias_base(rel_bias, qb):
    w = pl.next_power_of_2(LEFT + 2 * qb)
    j = jnp.arange(w)
    j = jnp.where(j < LEFT + qb, j, j - w)
    idx = jnp.clip(LEFT - j, -MAX_REL_DIST, MAX_REL_DIST) + MAX_REL_DIST
    return rel_bias[:, None, idx].astype(F32)


def _chunk_attention(proj, rel_bias, n_heads, qb=256):
    b, s, _ = proj.shape
    dh = ATTN_HEAD_DIM
    base = _attn_bias_base(rel_bias, qb)
    kern = functools.partial(_attn_kernel, qb=qb, scale=dh ** -0.5)
    return pl.pallas_call(
        kern,
        grid=(n_heads, b),
        in_specs=[pl.BlockSpec((None, s, dh), lambda h, bi: (bi, 0, h)),
                  pl.BlockSpec((None, s, dh), lambda h, bi: (bi, 0, n_heads + h)),
                  pl.BlockSpec((None, s, dh), lambda h, bi: (bi, 0, 2 * n_heads + h)),
                  pl.BlockSpec((None, 1, base.shape[-1]), lambda h, bi: (h, 0, 0))],
        out_specs=pl.BlockSpec((None, s, dh), lambda h, bi: (bi, 0, h)),
        out_shape=jax.ShapeDtypeStruct((b, s, n_heads * dh), BF16),
        scratch_shapes=[pltpu.VMEM((qb, LEFT + qb), F32)],
        compiler_params=_params(2),
        name="chunk_attention",
    )(proj, proj, proj, base)


def _ret_kernel(q_ref, k_ref, v_ref, g_ref, cos_ref, sin_ref, dec_ref, qd_ref, kd_ref, bd_ref,
                gnw_ref, o_ref, state_ref, *, rb, scale):
    seq = q_ref.shape[0]
    state_ref[...] = jnp.zeros_like(state_ref)
    half = q_ref.shape[1] // 2

    def body(n, carry):
        r0 = pl.multiple_of(n * rb, rb)
        rows = pl.ds(r0, rb)
        cos = cos_ref[rows, :]
        sin = sin_ref[rows, :]
        q = q_ref[rows, :].astype(F32)
        k = k_ref[rows, :].astype(F32)
        q = q * cos + pltpu.roll(q, half, 1) * sin
        k = (k * cos + pltpu.roll(k, half, 1) * sin) * scale
        v = v_ref[rows, :]
        a = lax.dot_general(q.astype(BF16), k.astype(BF16), _NT, preferred_element_type=F32)
        a = a * dec_ref[...]
        st = state_ref[...]
        y = jnp.dot(a.astype(BF16), v, preferred_element_type=F32)
        y = y + jnp.dot((q * qd_ref[...]).astype(BF16), st.astype(BF16),
                        preferred_element_type=F32)
        kd = (k * kd_ref[...]).astype(BF16)
        kv = lax.dot_general(kd, v, _TN, preferred_element_type=F32)
        state_ref[...] = st * bd_ref[...] + kv
        mu = jnp.mean(y, axis=-1, keepdims=True)
        yc = y - mu
        var = jnp.mean(yc * yc, axis=-1, keepdims=True)
        yn = yc * lax.rsqrt(var + EPS) * gnw_ref[...]
        g = g_ref[rows, :].astype(F32)
        o_ref[rows, :] = (g * (1.0 / (1.0 + jnp.exp(-g))) * yn).astype(o_ref.dtype)
        return carry

    lax.fori_loop(0, seq // rb, body, 0)


def _retention(proj, gn_w, n_heads, col0, rb=256):
    b, s, _ = proj.shape
    dk, dv = RET_QK_DIM, RET_V_DIM
    qk0 = col0 // dk
    v0 = (col0 + 2 * n_heads * dk) // dv
    inv_freq = 1.0 / (ROPE_BASE ** (jnp.arange(0, dk, 2, dtype=F32) / dk))
    ang = jnp.arange(s, dtype=F32)[:, None] * inv_freq[None, :]
    cos = jnp.concatenate([jnp.cos(ang), jnp.cos(ang)], axis=-1)
    sin = jnp.concatenate([-jnp.sin(ang), jnp.sin(ang)], axis=-1)
    log_gamma = jnp.log1p(-jnp.power(2.0, -5.0 - jnp.arange(n_heads, dtype=F32)))
    pos = jnp.arange(rb, dtype=F32)
    chunk_of = jnp.arange(rb) // CHUNK
    causal = (chunk_of[None, :] <= chunk_of[:, None]).astype(F32)
    dec = jnp.exp(log_gamma[:, None, None] * jnp.abs(pos[:, None] - pos[None, :])) * causal[None]
    qd = jnp.exp(log_gamma[:, None] * (pos + 1.0))[:, :, None]
    kd = jnp.exp(log_gamma[:, None] * (rb - 1.0 - pos))[:, :, None]
    bd = jnp.exp(log_gamma * rb)[:, None, None]
    kern = functools.partial(_ret_kernel, rb=rb, scale=dk ** -0.5)
    return pl.pallas_call(
        kern,
        grid=(b, n_heads),
        in_specs=[pl.BlockSpec((None, s, dk), lambda bi, h: (bi, 0, qk0 + h)),
                  pl.BlockSpec((None, s, dk), lambda bi, h: (bi, 0, qk0 + n_heads + h)),
                  pl.BlockSpec((None, s, dv), lambda bi, h: (bi, 0, v0 + h)),
                  pl.BlockSpec((None, s, dv), lambda bi, h: (bi, 0, v0 + n_heads + h)),
                  pl.BlockSpec((s, dk), lambda bi, h: (0, 0)),
                  pl.BlockSpec((s, dk), lambda bi, h: (0, 0)),
                  pl.BlockSpec((None, rb, rb), lambda bi, h: (h, 0, 0)),
                  pl.BlockSpec((None, rb, 1), lambda bi, h: (h, 0, 0)),
                  pl.BlockSpec((None, rb, 1), lambda bi, h: (h, 0, 0)),
                  pl.BlockSpec((None, 1, 1), lambda bi, h: (h, 0, 0)),
                  pl.BlockSpec((1, dv), lambda bi, h: (0, h))],
        out_specs=pl.BlockSpec((None, s, dv), lambda bi, h: (bi, 0, h)),
        out_shape=jax.ShapeDtypeStruct((b, s, n_heads * dv), BF16),
        scratch_shapes=[pltpu.VMEM((dk, dv), F32)],
        compiler_params=_params(2),
        name="retention",
    )(proj, proj, proj, proj, cos, sin, dec, qd, kd, bd, gn_w.reshape(1, -1))


def _xattn_kernel(q_ref, kv_ref, o_ref, *, n_heads, scale):
    d = q_ref.shape[1]
    dh = d // n_heads
    for h in range(n_heads):
        cols = slice(h * dh, (h + 1) * dh)
        s = lax.dot_general(q_ref[:, cols], kv_ref[:, cols], _NT, preferred_element_type=F32)
        s = s * scale
        m = jnp.max(s, axis=-1, keepdims=True)
        p = jnp.exp(s - m)
        l = jnp.sum(p, axis=-1, keepdims=True)
        o = jnp.dot(p.astype(BF16), kv_ref[:, d + h * dh:d + (h + 1) * dh],
                    preferred_element_type=F32)
        o_ref[:, cols] = (o / l).astype(o_ref.dtype)


def _cross_attention(q, kv, n_heads, tq=512):
    b, s, d = q.shape
    m = kv.shape[1]
    kern = functools.partial(_xattn_kernel, n_heads=n_heads, scale=(d // n_heads) ** -0.5)
    return pl.pallas_call(
        kern,
        grid=(b, s // tq),
        in_specs=[pl.BlockSpec((None, tq, d), lambda bi, i: (bi, i, 0)),
                  pl.BlockSpec((None, m, 2 * d), lambda bi, i: (bi, 0, 0))],
        out_specs=pl.BlockSpec((None, tq, d), lambda bi, i: (bi, i, 0)),
        out_shape=jax.ShapeDtypeStruct((b, s, d), BF16),
        compiler_params=_params(2),
        name="cross_attention",
    )(q, kv)


def _take_first_max(work, rows):
    m = jnp.max(work, axis=0, keepdims=True)
    first = jnp.min(jnp.where(work == m, rows, np.float32(work.shape[0])), axis=0, keepdims=True)
    return m, rows == first


def _top_rows(s, k):
    rows = lax.broadcasted_iota(jnp.int32, s.shape, 0).astype(F32)
    work = s
    rank = jnp.full(s.shape, NOT_RANKED, F32)
    vals = []
    for r in range(k):
        m, sel = _take_first_max(work, rows)
        rank = jnp.where(sel, np.float32(r), rank)
        work = jnp.where(sel, -jnp.inf, work)
        vals.append(m)
    return vals, rank


def _candidate_cells(k):
    cells = []
    for ra in range(k):
        cells += [(ra, rb) for rb in range(k // (ra + 1))]
    single = [c for c in cells if k // (c[0] + 1) == 1]
    multi = [c for c in cells if c not in single]
    pad = (-len(multi)) % 8
    return multi + [None] * pad + single


def _route_kernel(q_ref, keys_ref, rank1_ref, e1_ref, cnt_ref, e0_ref):
    k = PEER_TOPK
    tl = q_ref.shape[0]
    cells = _candidate_cells(k)
    for g in range(tl // LANES):
        q = q_ref[g * LANES:(g + 1) * LANES, :]
        s0 = lax.dot_general(keys_ref[0].astype(BF16), q[:, :PEER_HALF], _NT,
                             preferred_element_type=F32)
        s1 = lax.dot_general(keys_ref[1].astype(BF16), q[:, PEER_HALF:], _NT,
                             preferred_element_type=F32)
        a, rank0 = _top_rows(s0, k)
        b, rank1 = _top_rows(s1, k)
        a_all, b_all = jnp.concatenate(a, axis=0), jnp.concatenate(b, axis=0)
        ea_all, eb_all = jnp.exp(a_all - a[0]), jnp.exp(b_all - b[0])
        neg = jnp.full_like(a[0], -jnp.inf)
        zero = jnp.zeros_like(a[0])
        n0, n1 = k, k // 2
        mid = cells[n0 + n1:len(cells) - k // 2]
        cand = jnp.concatenate(
            [a[0] + b_all, a[1] + b_all[:n1]]
            + [neg if c is None else a_all[c[0]:c[0] + 1] + b_all[c[1]:c[1] + 1] for c in mid]
            + [a_all[k // 2:] + b[0]], axis=0)
        wgt = jnp.concatenate(
            [ea_all[0:1] * eb_all, ea_all[1:2] * eb_all[:n1]]
            + [zero if c is None else ea_all[c[0]:c[0] + 1] * eb_all[c[1]:c[1] + 1] for c in mid]
            + [ea_all[k // 2:] * eb_all[0:1]], axis=0)
        rows = lax.broadcasted_iota(jnp.int32, cand.shape, 0).astype(F32)
        work = cand
        chosen = jnp.zeros(cand.shape, F32)
        for _ in range(k):
            _, sel = _take_first_max(work, rows)
            chosen = jnp.where(sel, 1.0, chosen)
            work = jnp.where(sel, -jnp.inf, work)
        z = jnp.sum(chosen * wgt, axis=0, keepdims=True)
        cnt = jnp.zeros(s0.shape, F32)
        for ra in range(k):
            mine = [i for i, c in enumerate(cells) if c is not None and c[0] == ra]
            n_ra = jnp.sum(chosen[mine[0]:mine[-1] + 1], axis=0, keepdims=True)
            cnt = jnp.where(rank0 == np.float32(ra), n_ra, cnt)
        lanes = slice(g * LANES, (g + 1) * LANES)
        rank1_ref[:, lanes] = rank1.astype(rank1_ref.dtype)
        e1_ref[:, lanes] = (jnp.exp(s1 - b[0]) / z).astype(e1_ref.dtype)
        cnt_ref[:, lanes] = cnt
        e0_ref[:, lanes] = jnp.exp(s0 - a[0])


def _peer_route(qp, subkeys, tl=512):
    t = qp.shape[0]
    hp, _, nk, half = subkeys.shape
    out = lambda dt: jax.ShapeDtypeStruct((hp, nk, t), dt)
    ospec = pl.BlockSpec((None, nk, tl), lambda i, h: (h, 0, i))
    return pl.pallas_call(
        _route_kernel,
        grid=(t // tl, hp),
        in_specs=[pl.BlockSpec((tl, 2 * half), lambda i, h: (i, h)),
                  pl.BlockSpec((None, 2, nk, half), lambda i, h: (h, 0, 0, 0))],
        out_specs=[ospec] * 4,
        out_shape=[out(BF16), out(BF16), out(F32), out(F32)],
        compiler_params=_params(2),
        name="peer_route",
    )(qp, subkeys)


def _peer_kernel(x_ref, u_ref, v_ref, rank1_ref, e1_ref, cnt_ref, e0_ref, o_ref, acc_ref, xt_ref,
                 *stage_refs, sub):
    e = pl.program_id(1)
    n_heads, nk, tm = rank1_ref.shape
    te, d = u_ref.shape
    n_sub = te // sub
    rows_per_sub = sub // nk
    n_pieces = n_heads
    kc = d // n_pieces
    rows_per_step = te // nk
    row0 = (e % (8 // rows_per_step)) * rows_per_step
    act_refs, wt_refs = stage_refs[:n_sub], stage_refs[n_sub:]

    @pl.when(e == 0)
    def _():
        acc_ref[...] = jnp.zeros_like(acc_ref)
        xt_ref[...] = x_ref[...].T

    def up_piece(j, p):
        part = jnp.dot(u_ref[j * sub:(j + 1) * sub, p * kc:(p + 1) * kc],
                       xt_ref[p * kc:(p + 1) * kc, :], preferred_element_type=F32)
        if p == 0:
            act_refs[j][...] = part
        else:
            act_refs[j][...] += part

    def gate_piece(j, p):
        ii, g = divmod(p, n_pieces // rows_per_sub)
        tg = tm // (n_pieces // rows_per_sub)
        lanes = slice(g * tg, (g + 1) * tg)
        row = j * rows_per_sub + ii
        gate = None
        for h in range(n_heads):
            cnt = cnt_ref[h, pl.ds(row0 + row, 1), :][:, lanes].astype(BF16)
            e0 = e0_ref[h, pl.ds(row0 + row, 1), :][:, lanes].astype(BF16)
            zero = jnp.zeros((), BF16)
            term = e0 * jnp.where(rank1_ref[h, :, lanes] < cnt, e1_ref[h, :, lanes], zero)
            gate = term if gate is None else gate + term
        act = act_refs[j][ii * nk:(ii + 1) * nk, lanes]
        act = 0.5 * act * (1.0 + lax.erf(act * np.float32(1.0 / np.sqrt(2.0))))
        wt_refs[j][lanes, ii * nk:(ii + 1) * nk] = (gate * act.astype(BF16)).T

    def down_piece(j, p):
        cols = slice(p * kc, (p + 1) * kc)
        acc_ref[:, cols] += jnp.dot(wt_refs[j][...], v_ref[j * sub:(j + 1) * sub, cols],
                                    preferred_element_type=F32)

    for step in range(n_sub + 2):
        for p in range(n_pieces):
            if step < n_sub:
                up_piece(step, p)
            if 0 <= step - 1 < n_sub:
                gate_piece(step - 1, p)
            if 0 <= step - 2 < n_sub:
                down_piece(step - 2, p)

    @pl.when(e == pl.num_programs(1) - 1)
    def _():
        o_ref[...] = acc_ref[...].astype(o_ref.dtype)


def _peer_experts(x, u, v, rank1, e1, cnt, e0, tm=512, te=512, sub=256):
    t, d = x.shape
    n_exp = u.shape[0]
    hp, nk, _ = rank1.shape
    rows = te // nk
    tab = pl.BlockSpec((hp, nk, tm), lambda i, e: (0, 0, i))
    row_tab = pl.BlockSpec((hp, 8, tm), lambda i, e: (0, (e * rows) // 8, i))
    return pl.pallas_call(
        functools.partial(_peer_kernel, sub=sub),
        grid=(t // tm, n_exp // te),
        in_specs=[pl.BlockSpec((tm, d), lambda i, e: (i, 0)),
                  pl.BlockSpec((te, d), lambda i, e: (e, 0)),
                  pl.BlockSpec((te, d), lambda i, e: (e, 0)),
                  tab, tab, row_tab, row_tab],
        out_specs=pl.BlockSpec((tm, d), lambda i, e: (i, 0)),
        out_shape=jax.ShapeDtypeStruct((t, d), BF16),
        scratch_shapes=([pltpu.VMEM((tm, d), F32), pltpu.VMEM((d, tm), BF16)]
                        + [pltpu.VMEM((sub, tm), F32)] * (te // sub)
                        + [pltpu.VMEM((tm, sub), BF16)] * (te // sub)),
        compiler_params=_params(2),
        name="peer_experts",
    )(x, u, v, rank1, e1, cnt, e0)


def kernel(x, mem, norm1_w, w_in, attn_rel_bias, ret_gn_w, w_out, norm2_w, mem_norm_w, xattn_wq,
           xattn_wkv, xattn_wo, norm3_w, peer_wq, peer_subkeys, peer_u, peer_v, final_norm_w):
    b, s, d = x.shape
    t = b * s
    depth = w_in.shape[0]
    n_attn_heads = attn_rel_bias.shape[1]
    attn_width = n_attn_heads * ATTN_HEAD_DIM
    n_ret_heads = ret_gn_w.shape[1] // RET_V_DIM
    h = x.reshape(t, d)
    xn = _rmsnorm(h, norm1_w[0], BF16)
    for l in range(depth):
        proj = _matmul(xn, w_in[l], BF16, name="in_proj").reshape(b, s, -1)
        a_out = _chunk_attention(proj, attn_rel_bias[l], n_attn_heads)
        r_out = _retention(proj, ret_gn_w[l], n_ret_heads, 3 * attn_width)
        h = _matmul((a_out.reshape(t, -1), r_out.reshape(t, -1)), w_out[l], F32, residual=h,
                    name="out_proj")

        mem_n = _rmsnorm(mem.reshape(-1, d), mem_norm_w[l], BF16)
        kv = _matmul(mem_n, xattn_wkv[l], BF16, name="mem_kv").reshape(b, -1, 2 * d)
        xq = _matmul(_rmsnorm(h, norm2_w[l], BF16), xattn_wq[l], BF16, name="xattn_q")
        xo = _cross_attention(xq.reshape(b, s, d), kv, XATTN_HEADS).reshape(t, d)
        h = _matmul(xo, xattn_wo[l], F32, residual=h, name="xattn_o")

        xn3 = _rmsnorm(h, norm3_w[l], BF16)
        qp = _matmul(xn3, peer_wq[l], BF16, name="peer_q")
        rank1, e1, cnt, e0 = _peer_route(qp, peer_subkeys[l])
        delta = _peer_experts(xn3, peer_u[l].astype(BF16), peer_v[l].astype(BF16),
                              rank1, e1, cnt, e0)
        if l + 1 < depth:
            h = h + delta.astype(F32)
            xn = _rmsnorm(h, norm1_w[l + 1], BF16)
    return _add_rmsnorm(h, delta, final_norm_w, F32).reshape(b, s, d)
```

```python
import functools

import numpy as np
import jax
import jax.numpy as jnp
from jax import lax
from jax.experimental import pallas as pl
from jax.experimental.pallas import tpu as pltpu

F32 = jnp.float32
BF16 = jnp.bfloat16

CHUNK = 64
LEFT_CHUNKS = 8
LEFT = LEFT_CHUNKS * CHUNK
ATTN_HEAD_DIM = 128
MAX_REL_DIST = 256
RET_V_DIM = 256
RET_QK_DIM = 128
ROPE_BASE = 10000.0
XATTN_HEADS = 4
PEER_HEADS = 8
PEER_KEYS = 128
PEER_HALF = 128
PEER_TOPK = 16
EPS = 1e-6
NEG_INF = -1e30
NOT_RANKED = 1e9

VMEM_LIMIT_BYTES = 56 * 1024 * 1024
LANES = 128

_NT = (((1,), (1,)), ((), ()))
_TN = (((0,), (0,)), ((), ()))


def _params(n_grid_dims, flags=None):
    return pltpu.CompilerParams(
        dimension_semantics=("arbitrary",) * n_grid_dims,
        vmem_limit_bytes=VMEM_LIMIT_BYTES,
        flags=flags)


def _rmsnorm_kernel(x_ref, w_ref, o_ref):
    x = x_ref[...]
    ms = jnp.mean(x * x, axis=-1, keepdims=True)
    o_ref[...] = (x * lax.rsqrt(ms + EPS) * w_ref[...]).astype(o_ref.dtype)


def _rmsnorm(x, w, out_dtype, rows=256):
    m, d = x.shape
    return pl.pallas_call(
        _rmsnorm_kernel,
        grid=(m // rows,),
        in_specs=[pl.BlockSpec((rows, d), lambda i: (i, 0)),
                  pl.BlockSpec((1, d), lambda i: (0, 0))],
        out_specs=pl.BlockSpec((rows, d), lambda i: (i, 0)),
        out_shape=jax.ShapeDtypeStruct((m, d), out_dtype),
        compiler_params=_params(1),
        name="rmsnorm",
    )(x, w.reshape(1, d))


def _add_rmsnorm_kernel(x_ref, y_ref, w_ref, o_ref):
    x = x_ref[...] + y_ref[...].astype(F32)
    ms = jnp.mean(x * x, axis=-1, keepdims=True)
    o_ref[...] = (x * lax.rsqrt(ms + EPS) * w_ref[...]).astype(o_ref.dtype)


def _add_rmsnorm(x, y, w, out_dtype, rows=256):
    m, d = x.shape
    return pl.pallas_call(
        _add_rmsnorm_kernel,
        grid=(m // rows,),
        in_specs=[pl.BlockSpec((rows, d), lambda i: (i, 0)),
                  pl.BlockSpec((rows, d), lambda i: (i, 0)),
                  pl.BlockSpec((1, d), lambda i: (0, 0))],
        out_specs=pl.BlockSpec((rows, d), lambda i: (i, 0)),
        out_shape=jax.ShapeDtypeStruct((m, d), out_dtype),
        compiler_params=_params(1),
        name="add_rmsnorm",
    )(x, y, w.reshape(1, d))


def _mm_kernel(*refs, n_lhs, has_residual, has_row_scale, emit_norm, n_cols):
    lhs_refs, w_ref = refs[:n_lhs], refs[n_lhs]
    pos = n_lhs + 1
    acc, k0 = None, 0
    for a_ref in lhs_refs:
        k1 = k0 + a_ref.shape[1]
        part = jnp.dot(a_ref[...], w_ref[k0:k1, :].astype(BF16), preferred_element_type=F32)
        acc = part if acc is None else acc + part
        k0 = k1
    if has_row_scale:
        acc = acc * refs[pos][...]
        pos += 1
    if has_residual:
        acc = refs[pos][...] + acc
        pos += 1
    if not emit_norm:
        o_ref = refs[pos]
        o_ref[...] = acc.astype(o_ref.dtype)
        return
    gain_ref, o_ref, scaled_ref, rstd_ref, ssq_ref = refs[pos:pos + 5]
    j = pl.program_id(1)
    o_ref[...] = acc.astype(o_ref.dtype)
    scaled_ref[...] = (acc * gain_ref[...]).astype(scaled_ref.dtype)
    row_ssq = jnp.sum(acc * acc, axis=-1, keepdims=True)

    @pl.when(j == 0)
    def _():
        ssq_ref[...] = row_ssq

    @pl.when(j > 0)
    def _():
        ssq_ref[...] += row_ssq

    @pl.when(j == pl.num_programs(1) - 1)
    def _():
        rstd_ref[...] = lax.rsqrt(ssq_ref[...] / n_cols + EPS)


def _matmul(lhs, w, out_dtype, residual=None, row_scale=None, norm_gain=None, tm=1024, tn=512,
            name="matmul"):
    lhs = lhs if isinstance(lhs, (tuple, list)) else (lhs,)
    m = lhs[0].shape[0]
    k, n = w.shape
    assert sum(a.shape[1] for a in lhs) == k
    tm, tn = min(tm, m), min(tn, n)
    in_specs = [pl.BlockSpec((tm, a.shape[1]), lambda i, j: (i, 0)) for a in lhs]
    in_specs.append(pl.BlockSpec((k, tn), lambda i, j: (0, j)))
    args = [*lhs, w]
    tile = pl.BlockSpec((tm, tn), lambda i, j: (i, j))
    per_row = pl.BlockSpec((tm, 1), lambda i, j: (i, 0))
    if row_scale is not None:
        in_specs.append(per_row)
        args.append(row_scale)
    if residual is not None:
        in_specs.append(tile)
        args.append(residual)
    out_specs, out_shape, scratch = tile, jax.ShapeDtypeStruct((m, n), out_dtype), []
    if norm_gain is not None:
        in_specs.append(pl.BlockSpec((1, tn), lambda i, j: (0, j)))
        args.append(norm_gain.reshape(1, n))
        out_specs = [tile, tile, per_row]
        out_shape = [out_shape, jax.ShapeDtypeStruct((m, n), BF16),
                     jax.ShapeDtypeStruct((m, 1), F32)]
        scratch = [pltpu.VMEM((tm, 1), F32)]
    return pl.pallas_call(
        functools.partial(_mm_kernel, n_lhs=len(lhs), has_residual=residual is not None,
                          has_row_scale=row_scale is not None, emit_norm=norm_gain is not None,
                          n_cols=n),
        grid=(m // tm, n // tn),
        in_specs=in_specs,
        out_specs=out_specs,
        out_shape=out_shape,
        scratch_shapes=scratch,
        compiler_params=_params(2),
        name=name,
    )(*args)


def _attn_kernel(q_ref, k_ref, v_ref, base_ref, o_ref, bias_ref, *, qb, scale):
    seq = q_ref.shape[0]
    width = LEFT + qb

    @pl.when(pl.program_id(1) == 0)
    def _():
        toeplitz = pltpu.roll(jnp.broadcast_to(base_ref[...], (qb, base_ref.shape[-1])),
                              0, 1, stride=1, stride_axis=0)
        q_chunk = lax.broadcasted_iota(jnp.int32, (qb, width), 0) // CHUNK
        c_chunk = lax.broadcasted_iota(jnp.int32, (qb, width), 1) // CHUNK
        in_band = (c_chunk >= q_chunk) & (c_chunk <= q_chunk + LEFT_CHUNKS)
        bias_ref[...] = jnp.where(in_band, toeplitz[:, :width], NEG_INF)

    for i in range(seq // qb):
        q0 = i * qb
        k0 = max(0, q0 - LEFT)
        kw = q0 + qb - k0
        c0 = k0 - (q0 - LEFT)
        s = lax.dot_general(q_ref[q0:q0 + qb, :], k_ref[k0:k0 + kw, :], _NT,
                            preferred_element_type=F32)
        s = s * scale + bias_ref[:, c0:c0 + kw]
        m = jnp.max(s, axis=-1, keepdims=True)
        p = jnp.exp(s - m)
        l = jnp.sum(p, axis=-1, keepdims=True)
        o = jnp.dot(p.astype(BF16), v_ref[k0:k0 + kw, :], preferred_element_type=F32)
        o_ref[q0:q0 + qb, :] = (o / l).astype(o_ref.dtype)


def _attn_bias_base(rel_bias, qb):
    w = pl.next_power_of_2(LEFT + 2 * qb)
    j = jnp.arange(w)
    j = jnp.where(j < LEFT + qb, j, j - w)
    idx = jnp.clip(LEFT - j, -MAX_REL_DIST, MAX_REL_DIST) + MAX_REL_DIST
    return rel_bias[:, None, idx].astype(F32)


def _chunk_attention(proj, rel_bias, n_heads, qb=256):
    b, s, _ = proj.shape
    dh = ATTN_HEAD_DIM
    base = _attn_bias_base(rel_bias, qb)
    kern = functools.partial(_attn_kernel, qb=qb, scale=dh ** -0.5)
    return pl.pallas_call(
        kern,
        grid=(n_heads, b),
        in_specs=[pl.BlockSpec((None, s, dh), lambda h, bi: (bi, 0, h)),
                  pl.BlockSpec((None, s, dh), lambda h, bi: (bi, 0, n_heads + h)),
                  pl.BlockSpec((None, s, dh), lambda h, bi: (bi, 0, 2 * n_heads + h)),
                  pl.BlockSpec((None, 1, base.shape[-1]), lambda h, bi: (h, 0, 0))],
        out_specs=pl.BlockSpec((None, s, dh), lambda h, bi: (bi, 0, h)),
        out_shape=jax.ShapeDtypeStruct((b, s, n_heads * dh), BF16),
        scratch_shapes=[pltpu.VMEM((qb, LEFT + qb), F32)],
        compiler_params=_params(2),
        name="chunk_attention",
    )(proj, proj, proj, base)


def _ret_kernel(q_ref, k_ref, v_ref, g_ref, cos_ref, sin_ref, dec_ref, qd_ref, kd_ref, bd_ref,
                gnw_ref, o_ref, state_ref, *, rb, scale):
    seq = q_ref.shape[0]
    state_ref[...] = jnp.zeros_like(state_ref)
    half = q_ref.shape[1] // 2

    def body(n, carry):
        r0 = pl.multiple_of(n * rb, rb)
        rows = pl.ds(r0, rb)
        cos = cos_ref[rows, :]
        sin = sin_ref[rows, :]
        q = q_ref[rows, :].astype(F32)
        k = k_ref[rows, :].astype(F32)
        q = q * cos + pltpu.roll(q, half, 1) * sin
        k = (k * cos + pltpu.roll(k, half, 1) * sin) * scale
        v = v_ref[rows, :]
        a = lax.dot_general(q.astype(BF16), k.astype(BF16), _NT, preferred_element_type=F32)
        a = a * dec_ref[...]
        st = state_ref[...]
        y = jnp.dot(a.astype(BF16), v, preferred_element_type=F32)
        y = y + jnp.dot((q * qd_ref[...]).astype(BF16), st.astype(BF16),
                        preferred_element_type=F32)
        kd = (k * kd_ref[...]).astype(BF16)
        kv = lax.dot_general(kd, v, _TN, preferred_element_type=F32)
        state_ref[...] = st * bd_ref[...] + kv
        mu = jnp.mean(y, axis=-1, keepdims=True)
        yc = y - mu
        var = jnp.mean(yc * yc, axis=-1, keepdims=True)
        yn = yc * lax.rsqrt(var + EPS) * gnw_ref[...]
        g = g_ref[rows, :].astype(F32)
        o_ref[rows, :] = (g * (1.0 / (1.0 + jnp.exp(-g))) * yn).astype(o_ref.dtype)
        return carry

    lax.fori_loop(0, seq // rb, body, 0)


def _retention(proj, gn_w, n_heads, col0, rb=256):
    b, s, _ = proj.shape
    dk, dv = RET_QK_DIM, RET_V_DIM
    qk0 = col0 // dk
    v0 = (col0 + 2 * n_heads * dk) // dv
    inv_freq = 1.0 / (ROPE_BASE ** (jnp.arange(0, dk, 2, dtype=F32) / dk))
    ang = jnp.arange(s, dtype=F32)[:, None] * inv_freq[None, :]
    cos = jnp.concatenate([jnp.cos(ang), jnp.cos(ang)], axis=-1)
    sin = jnp.concatenate([-jnp.sin(ang), jnp.sin(ang)], axis=-1)
    log_gamma = jnp.log1p(-jnp.power(2.0, -5.0 - jnp.arange(n_heads, dtype=F32)))
    pos = jnp.arange(rb, dtype=F32)
    chunk_of = jnp.arange(rb) // CHUNK
    causal = (chunk_of[None, :] <= chunk_of[:, None]).astype(F32)
    dec = jnp.exp(log_gamma[:, None, None] * jnp.abs(pos[:, None] - pos[None, :])) * causal[None]
    qd = jnp.exp(log_gamma[:, None] * (pos + 1.0))[:, :, None]
    kd = jnp.exp(log_gamma[:, None] * (rb - 1.0 - pos))[:, :, None]
    bd = jnp.exp(log_gamma * rb)[:, None, None]
    kern = functools.partial(_ret_kernel, rb=rb, scale=dk ** -0.5)
    return pl.pallas_call(
        kern,
        grid=(b, n_heads),
        in_specs=[pl.BlockSpec((None, s, dk), lambda bi, h: (bi, 0, qk0 + h)),
                  pl.BlockSpec((None, s, dk), lambda bi, h: (bi, 0, qk0 + n_heads + h)),
                  pl.BlockSpec((None, s, dv), lambda bi, h: (bi, 0, v0 + h)),
                  pl.BlockSpec((None, s, dv), lambda bi, h: (bi, 0, v0 + n_heads + h)),
                  pl.BlockSpec((s, dk), lambda bi, h: (0, 0)),
                  pl.BlockSpec((s, dk), lambda bi, h: (0, 0)),
                  pl.BlockSpec((None, rb, rb), lambda bi, h: (h, 0, 0)),
                  pl.BlockSpec((None, rb, 1), lambda bi, h: (h, 0, 0)),
                  pl.BlockSpec((None, rb, 1), lambda bi, h: (h, 0, 0)),
                  pl.BlockSpec((None, 1, 1), lambda bi, h: (h, 0, 0)),
                  pl.BlockSpec((1, dv), lambda bi, h: (0, h))],
        out_specs=pl.BlockSpec((None, s, dv), lambda bi, h: (bi, 0, h)),
        out_shape=jax.ShapeDtypeStruct((b, s, n_heads * dv), BF16),
        scratch_shapes=[pltpu.VMEM((dk, dv), F32)],
        compiler_params=_params(2),
        name="retention",
    )(proj, proj, proj, proj, cos, sin, dec, qd, kd, bd, gn_w.reshape(1, -1))


def _xattn_kernel(q_ref, kv_ref, o_ref, *, n_heads, scale):
    d = q_ref.shape[1]
    dh = d // n_heads
    for h in range(n_heads):
        cols = slice(h * dh, (h + 1) * dh)
        s = lax.dot_general(q_ref[:, cols], kv_ref[:, cols], _NT, preferred_element_type=F32)
        s = s * scale
        m = jnp.max(s, axis=-1, keepdims=True)
        p = jnp.exp(s - m)
        l = jnp.sum(p, axis=-1, keepdims=True)
        o = jnp.dot(p.astype(BF16), kv_ref[:, d + h * dh:d + (h + 1) * dh],
                    preferred_element_type=F32)
        o_ref[:, cols] = (o / l).astype(o_ref.dtype)


def _cross_attention(q, kv, n_heads, tq=512):
    b, s, d = q.shape
    m = kv.shape[1]
    kern = functools.partial(_xattn_kernel, n_heads=n_heads, scale=(d // n_heads) ** -0.5)
    return pl.pallas_call(
        kern,
        grid=(b, s // tq),
        in_specs=[pl.BlockSpec((None, tq, d), lambda bi, i: (bi, i, 0)),
                  pl.BlockSpec((None, m, 2 * d), lambda bi, i: (bi, 0, 0))],
        out_specs=pl.BlockSpec((None, tq, d), lambda bi, i: (bi, i, 0)),
        out_shape=jax.ShapeDtypeStruct((b, s, d), BF16),
        compiler_params=_params(2),
        name="cross_attention",
    )(q, kv)


def _take_first_max(work, rows):
    m = jnp.max(work, axis=0, keepdims=True)
    first = jnp.min(jnp.where(work == m, rows, np.float32(work.shape[0])), axis=0, keepdims=True)
    return m, rows == first


def _top_rows(s, k):
    rows = lax.broadcasted_iota(jnp.int32, s.shape, 0).astype(F32)
    work = s
    rank = jnp.full(s.shape, NOT_RANKED, F32)
    vals = []
    for r in range(k):
        m, sel = _take_first_max(work, rows)
        rank = jnp.where(sel, np.float32(r), rank)
        work = jnp.where(sel, -jnp.inf, work)
        vals.append(m)
    return vals, rank


def _candidate_cells(k):
    cells = []
    for ra in range(k):
        cells += [(ra, rb) for rb in range(k // (ra + 1))]
    single = [c for c in cells if k // (c[0] + 1) == 1]
    multi = [c for c in cells if c not in single]
    pad = (-len(multi)) % 8
    return multi + [None] * pad + single


def _route_kernel(q_ref, keys_ref, rank1_ref, e1_ref, cnt_ref, e0_ref):
    k = PEER_TOPK
    tl = q_ref.shape[0]
    cells = _candidate_cells(k)
    for g in range(tl // LANES):
        q = q_ref[g * LANES:(g + 1) * LANES, :]
        s0 = lax.dot_general(keys_ref[0].astype(BF16), q[:, :PEER_HALF], _NT,
                             preferred_element_type=F32)
        s1 = lax.dot_general(keys_ref[1].astype(BF16), q[:, PEER_HALF:], _NT,
                             preferred_element_type=F32)
        a, rank0 = _top_rows(s0, k)
        b, rank1 = _top_rows(s1, k)
        a_all, b_all = jnp.concatenate(a, axis=0), jnp.concatenate(b, axis=0)
        ea_all, eb_all = jnp.exp(a_all - a[0]), jnp.exp(b_all - b[0])
        neg = jnp.full_like(a[0], -jnp.inf)
        zero = jnp.zeros_like(a[0])
        n0, n1 = k, k // 2
        mid = cells[n0 + n1:len(cells) - k // 2]
        cand = jnp.concatenate(
            [a[0] + b_all, a[1] + b_all[:n1]]
            + [neg if c is None else a_all[c[0]:c[0] + 1] + b_all[c[1]:c[1] + 1] for c in mid]
            + [a_all[k // 2:] + b[0]], axis=0)
        wgt = jnp.concatenate(
            [ea_all[0:1] * eb_all, ea_all[1:2] * eb_all[:n1]]
            + [zero if c is None else ea_all[c[0]:c[0] + 1] * eb_all[c[1]:c[1] + 1] for c in mid]
            + [ea_all[k // 2:] * eb_all[0:1]], axis=0)
        rows = lax.broadcasted_iota(jnp.int32, cand.shape, 0).astype(F32)
        work = cand
        chosen = jnp.zeros(cand.shape, F32)
        for _ in range(k):
            _, sel = _take_first_max(work, rows)
            chosen = jnp.where(sel, 1.0, chosen)
            work = jnp.where(sel, -jnp.inf, work)
        z = jnp.sum(chosen * wgt, axis=0, keepdims=True)
        cnt = jnp.zeros(s0.shape, F32)
        for ra in range(k):
            mine = [i for i, c in enumerate(cells) if c is not None and c[0] == ra]
            n_ra = jnp.sum(chosen[mine[0]:mine[-1] + 1], axis=0, keepdims=True)
            cnt = jnp.where(rank0 == np.float32(ra), n_ra, cnt)
        lanes = slice(g * LANES, (g + 1) * LANES)
        rank1_ref[:, lanes] = rank1.astype(rank1_ref.dtype)
        e1_ref[:, lanes] = (jnp.exp(s1 - b[0]) / z).astype(e1_ref.dtype)
        cnt_ref[:, lanes] = cnt
        e0_ref[:, lanes] = jnp.exp(s0 - a[0])


def _peer_route(qp, subkeys, tl=512):
    t = qp.shape[0]
    hp, _, nk, half = subkeys.shape
    out = lambda dt: jax.ShapeDtypeStruct((hp, nk, t), dt)
    ospec = pl.BlockSpec((None, nk, tl), lambda i, h: (h, 0, i))
    return pl.pallas_call(
        _route_kernel,
        grid=(t // tl, hp),
        in_specs=[pl.BlockSpec((tl, 2 * half), lambda i, h: (i, h)),
                  pl.BlockSpec((None, 2, nk, half), lambda i, h: (h, 0, 0, 0))],
        out_specs=[ospec] * 4,
        out_shape=[out(BF16), out(BF16), out(F32), out(F32)],
        compiler_params=_params(2),
        name="peer_route",
    )(qp, subkeys)


def _peer_kernel(x_ref, rstd_ref, u_ref, v_ref, rank1_ref, e1_ref, cnt_ref, e0_ref, o_ref, acc_ref,
                 xt_ref, *stage_refs, sub):
    e = pl.program_id(1)
    n_heads, nk, tm = rank1_ref.shape
    te, d = u_ref.shape
    n_sub = te // sub
    rows_per_sub = sub // nk
    n_pieces = n_heads
    kc = d // n_pieces
    rows_per_step = te // nk
    row0 = (e % (8 // rows_per_step)) * rows_per_step
    act_refs, wt_refs = stage_refs[:n_sub], stage_refs[n_sub:]

    @pl.when(e == 0)
    def _():
        acc_ref[...] = jnp.zeros_like(acc_ref)
        xt_ref[...] = x_ref[...].T

    def up_piece(j, p):
        part = jnp.dot(u_ref[j * sub:(j + 1) * sub, p * kc:(p + 1) * kc],
                       xt_ref[p * kc:(p + 1) * kc, :], preferred_element_type=F32)
        if p == 0:
            act_refs[j][...] = part
        else:
            act_refs[j][...] += part

    def gate_piece(j, p):
        ii, g = divmod(p, n_pieces // rows_per_sub)
        tg = tm // (n_pieces // rows_per_sub)
        lanes = slice(g * tg, (g + 1) * tg)
        row = j * rows_per_sub + ii
        gate = None
        for h in range(n_heads):
            cnt = cnt_ref[h, pl.ds(row0 + row, 1), :][:, lanes].astype(BF16)
            e0 = e0_ref[h, pl.ds(row0 + row, 1), :][:, lanes].astype(BF16)
            zero = jnp.zeros((), BF16)
            term = e0 * jnp.where(rank1_ref[h, :, lanes] < cnt, e1_ref[h, :, lanes], zero)
            gate = term if gate is None else gate + term
        act = act_refs[j][ii * nk:(ii + 1) * nk, lanes] * rstd_ref[:, lanes]
        act = 0.5 * act * (1.0 + lax.erf(act * np.float32(1.0 / np.sqrt(2.0))))
        wt_refs[j][lanes, ii * nk:(ii + 1) * nk] = (gate * act.astype(BF16)).T

    def down_piece(j, p):
        cols = slice(p * kc, (p + 1) * kc)
        acc_ref[:, cols] += jnp.dot(wt_refs[j][...], v_ref[j * sub:(j + 1) * sub, cols],
                                    preferred_element_type=F32)

    for step in range(n_sub + 2):
        for p in range(n_pieces):
            if step < n_sub:
                up_piece(step, p)
            if 0 <= step - 1 < n_sub:
                gate_piece(step - 1, p)
            if 0 <= step - 2 < n_sub:
                down_piece(step - 2, p)

    @pl.when(e == pl.num_programs(1) - 1)
    def _():
        o_ref[...] = acc_ref[...].astype(o_ref.dtype)


def _peer_experts(x, rstd, u, v, rank1, e1, cnt, e0, tm=512, te=512, sub=256):
    t, d = x.shape
    n_exp = u.shape[0]
    hp, nk, _ = rank1.shape
    rows = te // nk
    tab = pl.BlockSpec((hp, nk, tm), lambda i, e: (0, 0, i))
    row_tab = pl.BlockSpec((hp, 8, tm), lambda i, e: (0, (e * rows) // 8, i))
    return pl.pallas_call(
        functools.partial(_peer_kernel, sub=sub),
        grid=(t // tm, n_exp // te),
        in_specs=[pl.BlockSpec((tm, d), lambda i, e: (i, 0)),
                  pl.BlockSpec((1, tm), lambda i, e: (0, i)),
                  pl.BlockSpec((te, d), lambda i, e: (e, 0)),
                  pl.BlockSpec((te, d), lambda i, e: (e, 0)),
                  tab, tab, row_tab, row_tab],
        out_specs=pl.BlockSpec((tm, d), lambda i, e: (i, 0)),
        out_shape=jax.ShapeDtypeStruct((t, d), BF16),
        scratch_shapes=([pltpu.VMEM((tm, d), F32), pltpu.VMEM((d, tm), BF16)]
                        + [pltpu.VMEM((sub, tm), F32)] * (te // sub)
                        + [pltpu.VMEM((tm, sub), BF16)] * (te // sub)),
        compiler_params=_params(2),
        name="peer_experts",
    )(x, rstd, u, v, rank1, e1, cnt, e0)


def kernel(x, mem, norm1_w, w_in, attn_rel_bias, ret_gn_w, w_out, norm2_w, mem_norm_w, xattn_wq,
           xattn_wkv, xattn_wo, norm3_w, peer_wq, peer_subkeys, peer_u, peer_v, final_norm_w):
    b, s, d = x.shape
    t = b * s
    depth = w_in.shape[0]
    n_attn_heads = attn_rel_bias.shape[1]
    attn_width = n_attn_heads * ATTN_HEAD_DIM
    n_ret_heads = ret_gn_w.shape[1] // RET_V_DIM
    h = x.reshape(t, d)
    xn = _rmsnorm(h, norm1_w[0], BF16)
    for l in range(depth):
        proj = _matmul(xn, w_in[l], BF16, name="in_proj").reshape(b, s, -1)
        a_out = _chunk_attention(proj, attn_rel_bias[l], n_attn_heads)
        r_out = _retention(proj, ret_gn_w[l], n_ret_heads, 3 * attn_width)
        h, hg, rstd = _matmul((a_out.reshape(t, -1), r_out.reshape(t, -1)), w_out[l], F32,
                              residual=h, norm_gain=norm2_w[l], name="out_proj")

        mem_n = _rmsnorm(mem.reshape(-1, d), mem_norm_w[l], BF16)
        kv = _matmul(mem_n, xattn_wkv[l], BF16, name="mem_kv").reshape(b, -1, 2 * d)
        xq = _matmul(hg, xattn_wq[l], BF16, row_scale=rstd, name="xattn_q")
        xo = _cross_attention(xq.reshape(b, s, d), kv, XATTN_HEADS).reshape(t, d)
        h, hg, rstd = _matmul(xo, xattn_wo[l], F32, residual=h, norm_gain=norm3_w[l],
                              name="xattn_o")

        qp = _matmul(hg, peer_wq[l], BF16, row_scale=rstd, name="peer_q")
        rank1, e1, cnt, e0 = _peer_route(qp, peer_subkeys[l])
        delta = _peer_experts(hg, rstd.reshape(1, t), peer_u[l].astype(BF16),
                              peer_v[l].astype(BF16), rank1, e1, cnt, e0)
        if l + 1 < depth:
            h = h + delta.astype(F32)
            xn = _rmsnorm(h, norm1_w[l + 1], BF16)
    return _add_rmsnorm(h, delta, final_norm_w, F32).reshape(b, s, d)
```

```python
import functools

import numpy as np
import jax
import jax.numpy as jnp
from jax import lax
from jax.experimental import pallas as pl
from jax.experimental.pallas import tpu as pltpu

F32 = jnp.float32
BF16 = jnp.bfloat16

CHUNK = 64
LEFT_CHUNKS = 8
LEFT = LEFT_CHUNKS * CHUNK
ATTN_HEAD_DIM = 128
MAX_REL_DIST = 256
RET_V_DIM = 256
RET_QK_DIM = 128
ROPE_BASE = 10000.0
XATTN_HEADS = 4
PEER_HEADS = 8
PEER_KEYS = 128
PEER_HALF = 128
PEER_TOPK = 16
EPS = 1e-6
NEG_INF = -1e30
NOT_RANKED = 1e9

VMEM_LIMIT_BYTES = 56 * 1024 * 1024
LANES = 128

_NT = (((1,), (1,)), ((), ()))
_TN = (((0,), (0,)), ((), ()))


def _params(n_grid_dims, flags=None):
    return pltpu.CompilerParams(
        dimension_semantics=("arbitrary",) * n_grid_dims,
        vmem_limit_bytes=VMEM_LIMIT_BYTES,
        flags=flags)


def _rmsnorm_kernel(x_ref, w_ref, o_ref):
    x = x_ref[...]
    ms = jnp.mean(x * x, axis=-1, keepdims=True)
    o_ref[...] = (x * lax.rsqrt(ms + EPS) * w_ref[...]).astype(o_ref.dtype)


def _rmsnorm(x, w, out_dtype, rows=256):
    m, d = x.shape
    return pl.pallas_call(
        _rmsnorm_kernel,
        grid=(m // rows,),
        in_specs=[pl.BlockSpec((rows, d), lambda i: (i, 0)),
                  pl.BlockSpec((1, d), lambda i: (0, 0))],
        out_specs=pl.BlockSpec((rows, d), lambda i: (i, 0)),
        out_shape=jax.ShapeDtypeStruct((m, d), out_dtype),
        compiler_params=_params(1),
        name="rmsnorm",
    )(x, w.reshape(1, d))


def _add_rmsnorm_kernel(x_ref, y_ref, w_ref, o_ref):
    x = x_ref[...] + y_ref[...].astype(F32)
    ms = jnp.mean(x * x, axis=-1, keepdims=True)
    o_ref[...] = (x * lax.rsqrt(ms + EPS) * w_ref[...]).astype(o_ref.dtype)


def _add_rmsnorm(x, y, w, out_dtype, rows=256):
    m, d = x.shape
    return pl.pallas_call(
        _add_rmsnorm_kernel,
        grid=(m // rows,),
        in_specs=[pl.BlockSpec((rows, d), lambda i: (i, 0)),
                  pl.BlockSpec((rows, d), lambda i: (i, 0)),
                  pl.BlockSpec((1, d), lambda i: (0, 0))],
        out_specs=pl.BlockSpec((rows, d), lambda i: (i, 0)),
        out_shape=jax.ShapeDtypeStruct((m, d), out_dtype),
        compiler_params=_params(1),
        name="add_rmsnorm",
    )(x, y, w.reshape(1, d))


def _mm_kernel(*refs, n_lhs, has_residual, has_row_scale, emit_norm, n_cols):
    lhs_refs, w_ref = refs[:n_lhs], refs[n_lhs]
    pos = n_lhs + 1
    acc, k0 = None, 0
    for a_ref in lhs_refs:
        k1 = k0 + a_ref.shape[1]
        part = jnp.dot(a_ref[...], w_ref[k0:k1, :].astype(BF16), preferred_element_type=F32)
        acc = part if acc is None else acc + part
        k0 = k1
    if has_row_scale:
        acc = acc * refs[pos][...]
        pos += 1
    if has_residual:
        acc = refs[pos][...] + acc
        pos += 1
    if not emit_norm:
        o_ref = refs[pos]
        o_ref[...] = acc.astype(o_ref.dtype)
        return
    gain_ref, o_ref, scaled_ref, rstd_ref, ssq_ref = refs[pos:pos + 5]
    j = pl.program_id(1)
    o_ref[...] = acc.astype(o_ref.dtype)
    scaled_ref[...] = (acc * gain_ref[...]).astype(scaled_ref.dtype)
    row_ssq = jnp.sum(acc * acc, axis=-1, keepdims=True)

    @pl.when(j == 0)
    def _():
        ssq_ref[...] = row_ssq

    @pl.when(j > 0)
    def _():
        ssq_ref[...] += row_ssq

    @pl.when(j == pl.num_programs(1) - 1)
    def _():
        rstd_ref[...] = lax.rsqrt(ssq_ref[...] / n_cols + EPS)


def _matmul(lhs, w, out_dtype, residual=None, row_scale=None, norm_gain=None, tm=1024, tn=512,
            name="matmul"):
    lhs = lhs if isinstance(lhs, (tuple, list)) else (lhs,)
    m = lhs[0].shape[0]
    k, n = w.shape
    assert sum(a.shape[1] for a in lhs) == k
    tm, tn = min(tm, m), min(tn, n)
    in_specs = [pl.BlockSpec((tm, a.shape[1]), lambda i, j: (i, 0)) for a in lhs]
    in_specs.append(pl.BlockSpec((k, tn), lambda i, j: (0, j)))
    args = [*lhs, w]
    tile = pl.BlockSpec((tm, tn), lambda i, j: (i, j))
    per_row = pl.BlockSpec((tm, 1), lambda i, j: (i, 0))
    if row_scale is not None:
        in_specs.append(per_row)
        args.append(row_scale)
    if residual is not None:
        in_specs.append(tile)
        args.append(residual)
    out_specs, out_shape, scratch = tile, jax.ShapeDtypeStruct((m, n), out_dtype), []
    if norm_gain is not None:
        in_specs.append(pl.BlockSpec((1, tn), lambda i, j: (0, j)))
        args.append(norm_gain.reshape(1, n))
        out_specs = [tile, tile, per_row]
        out_shape = [out_shape, jax.ShapeDtypeStruct((m, n), BF16),
                     jax.ShapeDtypeStruct((m, 1), F32)]
        scratch = [pltpu.VMEM((tm, 1), F32)]
    return pl.pallas_call(
        functools.partial(_mm_kernel, n_lhs=len(lhs), has_residual=residual is not None,
                          has_row_scale=row_scale is not None, emit_norm=norm_gain is not None,
                          n_cols=n),
        grid=(m // tm, n // tn),
        in_specs=in_specs,
        out_specs=out_specs,
        out_shape=out_shape,
        scratch_shapes=scratch,
        compiler_params=_params(2),
        name=name,
    )(*args)


def _attn_kernel(q_ref, k_ref, v_ref, base_ref, o_ref, bias_ref, *, qb, scale):
    seq = q_ref.shape[0]
    width = LEFT + qb

    @pl.when(pl.program_id(1) == 0)
    def _():
        toeplitz = pltpu.roll(jnp.broadcast_to(base_ref[...], (qb, base_ref.shape[-1])),
                              0, 1, stride=1, stride_axis=0)
        q_chunk = lax.broadcasted_iota(jnp.int32, (qb, width), 0) // CHUNK
        c_chunk = lax.broadcasted_iota(jnp.int32, (qb, width), 1) // CHUNK
        in_band = (c_chunk >= q_chunk) & (c_chunk <= q_chunk + LEFT_CHUNKS)
        bias_ref[...] = jnp.where(in_band, toeplitz[:, :width], NEG_INF)

    for i in range(seq // qb):
        q0 = i * qb
        k0 = max(0, q0 - LEFT)
        kw = q0 + qb - k0
        c0 = k0 - (q0 - LEFT)
        s = lax.dot_general(q_ref[q0:q0 + qb, :], k_ref[k0:k0 + kw, :], _NT,
                            preferred_element_type=F32)
        s = s * scale + bias_ref[:, c0:c0 + kw]
        m = jnp.max(s, axis=-1, keepdims=True)
        p = jnp.exp(s - m)
        l = jnp.sum(p, axis=-1, keepdims=True)
        o = jnp.dot(p.astype(BF16), v_ref[k0:k0 + kw, :], preferred_element_type=F32)
        o_ref[q0:q0 + qb, :] = (o / l).astype(o_ref.dtype)


def _attn_bias_base(rel_bias, qb):
    w = pl.next_power_of_2(LEFT + 2 * qb)
    j = jnp.arange(w)
    j = jnp.where(j < LEFT + qb, j, j - w)
    idx = jnp.clip(LEFT - j, -MAX_REL_DIST, MAX_REL_DIST) + MAX_REL_DIST
    return rel_bias[:, None, idx].astype(F32)


def _chunk_attention(proj, rel_bias, n_heads, qb=256):
    b, s, _ = proj.shape
    dh = ATTN_HEAD_DIM
    base = _attn_bias_base(rel_bias, qb)
    kern = functools.partial(_attn_kernel, qb=qb, scale=dh ** -0.5)
    return pl.pallas_call(
        kern,
        grid=(n_heads, b),
        in_specs=[pl.BlockSpec((None, s, dh), lambda h, bi: (bi, 0, h)),
                  pl.BlockSpec((None, s, dh), lambda h, bi: (bi, 0, n_heads + h)),
                  pl.BlockSpec((None, s, dh), lambda h, bi: (bi, 0, 2 * n_heads + h)),
                  pl.BlockSpec((None, 1, base.shape[-1]), lambda h, bi: (h, 0, 0))],
        out_specs=pl.BlockSpec((None, s, dh), lambda h, bi: (bi, 0, h)),
        out_shape=jax.ShapeDtypeStruct((b, s, n_heads * dh), BF16),
        scratch_shapes=[pltpu.VMEM((qb, LEFT + qb), F32)],
        compiler_params=_params(2),
        name="chunk_attention",
    )(proj, proj, proj, base)


def _ret_kernel(q_ref, k_ref, v_ref, g_ref, cos_ref, sin_ref, dec_ref, qd_ref, kd_ref, bd_ref,
                gnw_ref, o_ref, state_ref, *, rb, scale):
    seq = q_ref.shape[0]
    state_ref[...] = jnp.zeros_like(state_ref)
    half = q_ref.shape[1] // 2

    def body(n, carry):
        r0 = pl.multiple_of(n * rb, rb)
        rows = pl.ds(r0, rb)
        cos = cos_ref[rows, :]
        sin = sin_ref[rows, :]
        q = q_ref[rows, :].astype(F32)
        k = k_ref[rows, :].astype(F32)
        q = q * cos + pltpu.roll(q, half, 1) * sin
        k = (k * cos + pltpu.roll(k, half, 1) * sin) * scale
        v = v_ref[rows, :]
        a = lax.dot_general(q.astype(BF16), k.astype(BF16), _NT, preferred_element_type=F32)
        a = a * dec_ref[...]
        st = state_ref[...]
        y = jnp.dot(a.astype(BF16), v, preferred_element_type=F32)
        y = y + jnp.dot((q * qd_ref[...]).astype(BF16), st.astype(BF16),
                        preferred_element_type=F32)
        kd = (k * kd_ref[...]).astype(BF16)
        kv = lax.dot_general(kd, v, _TN, preferred_element_type=F32)
        state_ref[...] = st * bd_ref[...] + kv
        mu = jnp.mean(y, axis=-1, keepdims=True)
        yc = y - mu
        var = jnp.mean(yc * yc, axis=-1, keepdims=True)
        yn = yc * lax.rsqrt(var + EPS) * gnw_ref[...]
        g = g_ref[rows, :].astype(F32)
        o_ref[rows, :] = (g * (1.0 / (1.0 + jnp.exp(-g))) * yn).astype(o_ref.dtype)
        return carry

    lax.fori_loop(0, seq // rb, body, 0)


def _retention(proj, gn_w, n_heads, col0, rb=256):
    b, s, _ = proj.shape
    dk, dv = RET_QK_DIM, RET_V_DIM
    qk0 = col0 // dk
    v0 = (col0 + 2 * n_heads * dk) // dv
    inv_freq = 1.0 / (ROPE_BASE ** (jnp.arange(0, dk, 2, dtype=F32) / dk))
    ang = jnp.arange(s, dtype=F32)[:, None] * inv_freq[None, :]
    cos = jnp.concatenate([jnp.cos(ang), jnp.cos(ang)], axis=-1)
    sin = jnp.concatenate([-jnp.sin(ang), jnp.sin(ang)], axis=-1)
    log_gamma = jnp.log1p(-jnp.power(2.0, -5.0 - jnp.arange(n_heads, dtype=F32)))
    pos = jnp.arange(rb, dtype=F32)
    chunk_of = jnp.arange(rb) // CHUNK
    causal = (chunk_of[None, :] <= chunk_of[:, None]).astype(F32)
    dec = jnp.exp(log_gamma[:, None, None] * jnp.abs(pos[:, None] - pos[None, :])) * causal[None]
    qd = jnp.exp(log_gamma[:, None] * (pos + 1.0))[:, :, None]
    kd = jnp.exp(log_gamma[:, None] * (rb - 1.0 - pos))[:, :, None]
    bd = jnp.exp(log_gamma * rb)[:, None, None]
    kern = functools.partial(_ret_kernel, rb=rb, scale=dk ** -0.5)
    return pl.pallas_call(
        kern,
        grid=(b, n_heads),
        in_specs=[pl.BlockSpec((None, s, dk), lambda bi, h: (bi, 0, qk0 + h)),
                  pl.BlockSpec((None, s, dk), lambda bi, h: (bi, 0, qk0 + n_heads + h)),
                  pl.BlockSpec((None, s, dv), lambda bi, h: (bi, 0, v0 + h)),
                  pl.BlockSpec((None, s, dv), lambda bi, h: (bi, 0, v0 + n_heads + h)),
                  pl.BlockSpec((s, dk), lambda bi, h: (0, 0)),
                  pl.BlockSpec((s, dk), lambda bi, h: (0, 0)),
                  pl.BlockSpec((None, rb, rb), lambda bi, h: (h, 0, 0)),
                  pl.BlockSpec((None, rb, 1), lambda bi, h: (h, 0, 0)),
                  pl.BlockSpec((None, rb, 1), lambda bi, h: (h, 0, 0)),
                  pl.BlockSpec((None, 1, 1), lambda bi, h: (h, 0, 0)),
                  pl.BlockSpec((1, dv), lambda bi, h: (0, h))],
        out_specs=pl.BlockSpec((None, s, dv), lambda bi, h: (bi, 0, h)),
        out_shape=jax.ShapeDtypeStruct((b, s, n_heads * dv), BF16),
        scratch_shapes=[pltpu.VMEM((dk, dv), F32)],
        compiler_params=_params(2),
        name="retention",
    )(proj, proj, proj, proj, cos, sin, dec, qd, kd, bd, gn_w.reshape(1, -1))


def _xattn_kernel(q_ref, kv_ref, o_ref, *, n_heads, scale):
    d = q_ref.shape[1]
    dh = d // n_heads
    for h in range(n_heads):
        cols = slice(h * dh, (h + 1) * dh)
        s = lax.dot_general(q_ref[:, cols], kv_ref[:, cols], _NT, preferred_element_type=F32)
        s = s * scale
        m = jnp.max(s, axis=-1, keepdims=True)
        p = jnp.exp(s - m)
        l = jnp.sum(p, axis=-1, keepdims=True)
        o = jnp.dot(p.astype(BF16), kv_ref[:, d + h * dh:d + (h + 1) * dh],
                    preferred_element_type=F32)
        o_ref[:, cols] = (o / l).astype(o_ref.dtype)


def _cross_attention(q, kv, n_heads, tq=512):
    b, s, d = q.shape
    m = kv.shape[1]
    kern = functools.partial(_xattn_kernel, n_heads=n_heads, scale=(d // n_heads) ** -0.5)
    return pl.pallas_call(
        kern,
        grid=(b, s // tq),
        in_specs=[pl.BlockSpec((None, tq, d), lambda bi, i: (bi, i, 0)),
                  pl.BlockSpec((None, m, 2 * d), lambda bi, i: (bi, 0, 0))],
        out_specs=pl.BlockSpec((None, tq, d), lambda bi, i: (bi, i, 0)),
        out_shape=jax.ShapeDtypeStruct((b, s, d), BF16),
        compiler_params=_params(2),
        name="cross_attention",
    )(q, kv)


def _take_first_max(work, rows):
    m = jnp.max(work, axis=0, keepdims=True)
    first = jnp.min(jnp.where(work == m, rows, np.float32(work.shape[0])), axis=0, keepdims=True)
    return m, rows == first


def _top_rows(s, k):
    rows = lax.broadcasted_iota(jnp.int32, s.shape, 0).astype(F32)
    work = s
    rank = jnp.full(s.shape, NOT_RANKED, F32)
    vals = []
    for r in range(k):
        m, sel = _take_first_max(work, rows)
        rank = jnp.where(sel, np.float32(r), rank)
        work = jnp.where(sel, -jnp.inf, work)
        vals.append(m)
    return vals, rank


def _candidate_cells(k):
    cells = []
    for ra in range(k):
        cells += [(ra, rb) for rb in range(k // (ra + 1))]
    single = [c for c in cells if k // (c[0] + 1) == 1]
    multi = [c for c in cells if c not in single]
    pad = (-len(multi)) % 8
    return multi + [None] * pad + single


def _route_kernel(q_ref, keys_ref, u_ref, v_ref, rank1_ref, e1_ref, cnt_ref, e0_ref, ub_ref, vb_ref):
    ub_ref[...] = u_ref[...].astype(ub_ref.dtype)
    vb_ref[...] = v_ref[...].astype(vb_ref.dtype)
    k = PEER_TOPK
    tl = q_ref.shape[0]
    cells = _candidate_cells(k)
    for g in range(tl // LANES):
        q = q_ref[g * LANES:(g + 1) * LANES, :]
        s0 = lax.dot_general(keys_ref[0].astype(BF16), q[:, :PEER_HALF], _NT,
                             preferred_element_type=F32)
        s1 = lax.dot_general(keys_ref[1].astype(BF16), q[:, PEER_HALF:], _NT,
                             preferred_element_type=F32)
        a, rank0 = _top_rows(s0, k)
        b, rank1 = _top_rows(s1, k)
        a_all, b_all = jnp.concatenate(a, axis=0), jnp.concatenate(b, axis=0)
        ea_all, eb_all = jnp.exp(a_all - a[0]), jnp.exp(b_all - b[0])
        neg = jnp.full_like(a[0], -jnp.inf)
        zero = jnp.zeros_like(a[0])
        n0, n1 = k, k // 2
        mid = cells[n0 + n1:len(cells) - k // 2]
        cand = jnp.concatenate(
            [a[0] + b_all, a[1] + b_all[:n1]]
            + [neg if c is None else a_all[c[0]:c[0] + 1] + b_all[c[1]:c[1] + 1] for c in mid]
            + [a_all[k // 2:] + b[0]], axis=0)
        wgt = jnp.concatenate(
            [ea_all[0:1] * eb_all, ea_all[1:2] * eb_all[:n1]]
            + [zero if c is None else ea_all[c[0]:c[0] + 1] * eb_all[c[1]:c[1] + 1] for c in mid]
            + [ea_all[k // 2:] * eb_all[0:1]], axis=0)
        rows = lax.broadcasted_iota(jnp.int32, cand.shape, 0).astype(F32)
        work = cand
        chosen = jnp.zeros(cand.shape, F32)
        for _ in range(k):
            _, sel = _take_first_max(work, rows)
            chosen = jnp.where(sel, 1.0, chosen)
            work = jnp.where(sel, -jnp.inf, work)
        z = jnp.sum(chosen * wgt, axis=0, keepdims=True)
        cnt = jnp.zeros(s0.shape, F32)
        for ra in range(k):
            mine = [i for i, c in enumerate(cells) if c is not None and c[0] == ra]
            n_ra = jnp.sum(chosen[mine[0]:mine[-1] + 1], axis=0, keepdims=True)
            cnt = jnp.where(rank0 == np.float32(ra), n_ra, cnt)
        lanes = slice(g * LANES, (g + 1) * LANES)
        rank1_ref[:, lanes] = rank1.astype(rank1_ref.dtype)
        e1_ref[:, lanes] = (jnp.exp(s1 - b[0]) / z).astype(e1_ref.dtype)
        cnt_ref[:, lanes] = cnt
        e0_ref[:, lanes] = jnp.exp(s0 - a[0])


def _peer_route(qp, subkeys, u, v, tl=512):
    t = qp.shape[0]
    hp, _, nk, half = subkeys.shape
    n_exp, d = u.shape
    steps = (t // tl) * hp
    rows = n_exp // steps
    assert rows * steps == n_exp
    out = lambda dt: jax.ShapeDtypeStruct((hp, nk, t), dt)
    ospec = pl.BlockSpec((None, nk, tl), lambda i, h: (h, 0, i))
    wspec = pl.BlockSpec((rows, d), lambda i, h: (i * hp + h, 0))
    wout = jax.ShapeDtypeStruct((n_exp, d), BF16)
    return pl.pallas_call(
        _route_kernel,
        grid=(t // tl, hp),
        in_specs=[pl.BlockSpec((tl, 2 * half), lambda i, h: (i, h)),
                  pl.BlockSpec((None, 2, nk, half), lambda i, h: (h, 0, 0, 0)),
                  wspec, wspec],
        out_specs=[ospec] * 4 + [wspec] * 2,
        out_shape=[out(BF16), out(BF16), out(F32), out(F32), wout, wout],
        compiler_params=_params(2),
        name="peer_route",
    )(qp, subkeys, u, v)


def _peer_kernel(x_ref, rstd_ref, u_ref, v_ref, rank1_ref, e1_ref, cnt_ref, e0_ref, o_ref, acc_ref,
                 xt_ref, *stage_refs, sub):
    e = pl.program_id(1)
    n_heads, nk, tm = rank1_ref.shape
    te, d = u_ref.shape
    n_sub = te // sub
    rows_per_sub = sub // nk
    n_pieces = n_heads
    kc = d // n_pieces
    rows_per_step = te // nk
    row0 = (e % (8 // rows_per_step)) * rows_per_step
    act_refs, wt_refs = stage_refs[:n_sub], stage_refs[n_sub:]

    @pl.when(e == 0)
    def _():
        acc_ref[...] = jnp.zeros_like(acc_ref)
        xt_ref[...] = x_ref[...].T

    def up_piece(j, p):
        part = jnp.dot(u_ref[j * sub:(j + 1) * sub, p * kc:(p + 1) * kc],
                       xt_ref[p * kc:(p + 1) * kc, :], preferred_element_type=F32)
        if p == 0:
            act_refs[j][...] = part
        else:
            act_refs[j][...] += part

    def gate_piece(j, p):
        ii, g = divmod(p, n_pieces // rows_per_sub)
        tg = tm // (n_pieces // rows_per_sub)
        lanes = slice(g * tg, (g + 1) * tg)
        row = j * rows_per_sub + ii
        gate = None
        for h in range(n_heads):
            cnt = cnt_ref[h, pl.ds(row0 + row, 1), :][:, lanes].astype(BF16)
            e0 = e0_ref[h, pl.ds(row0 + row, 1), :][:, lanes].astype(BF16)
            zero = jnp.zeros((), BF16)
            term = e0 * jnp.where(rank1_ref[h, :, lanes] < cnt, e1_ref[h, :, lanes], zero)
            gate = term if gate is None else gate + term
        act = act_refs[j][ii * nk:(ii + 1) * nk, lanes] * rstd_ref[:, lanes]
        act = 0.5 * act * (1.0 + lax.erf(act * np.float32(1.0 / np.sqrt(2.0))))
        wt_refs[j][lanes, ii * nk:(ii + 1) * nk] = (gate * act.astype(BF16)).T

    def down_piece(j, p):
        cols = slice(p * kc, (p + 1) * kc)
        acc_ref[:, cols] += jnp.dot(wt_refs[j][...], v_ref[j * sub:(j + 1) * sub, cols],
                                    preferred_element_type=F32)

    for step in range(n_sub + 2):
        for p in range(n_pieces):
            if step < n_sub:
                up_piece(step, p)
            if 0 <= step - 1 < n_sub:
                gate_piece(step - 1, p)
            if 0 <= step - 2 < n_sub:
                down_piece(step - 2, p)

    @pl.when(e == pl.num_programs(1) - 1)
    def _():
        o_ref[...] = acc_ref[...].astype(o_ref.dtype)


def _peer_experts(x, rstd, u, v, rank1, e1, cnt, e0, tm=512, te=512, sub=256):
    t, d = x.shape
    n_exp = u.shape[0]
    hp, nk, _ = rank1.shape
    rows = te // nk
    tab = pl.BlockSpec((hp, nk, tm), lambda i, e: (0, 0, i))
    row_tab = pl.BlockSpec((hp, 8, tm), lambda i, e: (0, (e * rows) // 8, i))
    return pl.pallas_call(
        functools.partial(_peer_kernel, sub=sub),
        grid=(t // tm, n_exp // te),
        in_specs=[pl.BlockSpec((tm, d), lambda i, e: (i, 0)),
                  pl.BlockSpec((1, tm), lambda i, e: (0, i)),
                  pl.BlockSpec((te, d), lambda i, e: (e, 0)),
                  pl.BlockSpec((te, d), lambda i, e: (e, 0)),
                  tab, tab, row_tab, row_tab],
        out_specs=pl.BlockSpec((tm, d), lambda i, e: (i, 0)),
        out_shape=jax.ShapeDtypeStruct((t, d), BF16),
        scratch_shapes=([pltpu.VMEM((tm, d), F32), pltpu.VMEM((d, tm), BF16)]
                        + [pltpu.VMEM((sub, tm), F32)] * (te // sub)
                        + [pltpu.VMEM((tm, sub), BF16)] * (te // sub)),
        compiler_params=_params(2),
        name="peer_experts",
    )(x, rstd, u, v, rank1, e1, cnt, e0)


def kernel(x, mem, norm1_w, w_in, attn_rel_bias, ret_gn_w, w_out, norm2_w, mem_norm_w, xattn_wq,
           xattn_wkv, xattn_wo, norm3_w, peer_wq, peer_subkeys, peer_u, peer_v, final_norm_w):
    b, s, d = x.shape
    t = b * s
    depth = w_in.shape[0]
    n_attn_heads = attn_rel_bias.shape[1]
    attn_width = n_attn_heads * ATTN_HEAD_DIM
    n_ret_heads = ret_gn_w.shape[1] // RET_V_DIM
    h = x.reshape(t, d)
    xn = _rmsnorm(h, norm1_w[0], BF16)
    for l in range(depth):
        proj = _matmul(xn, w_in[l], BF16, name="in_proj").reshape(b, s, -1)
        a_out = _chunk_attention(proj, attn_rel_bias[l], n_attn_heads)
        r_out = _retention(proj, ret_gn_w[l], n_ret_heads, 3 * attn_width)
        h, hg, rstd = _matmul((a_out.reshape(t, -1), r_out.reshape(t, -1)), w_out[l], F32,
                              residual=h, norm_gain=norm2_w[l], name="out_proj")

        mem_n = _rmsnorm(mem.reshape(-1, d), mem_norm_w[l], BF16)
        kv = _matmul(mem_n, xattn_wkv[l], BF16, name="mem_kv").reshape(b, -1, 2 * d)
        xq = _matmul(hg, xattn_wq[l], BF16, row_scale=rstd, name="xattn_q")
        xo = _cross_attention(xq.reshape(b, s, d), kv, XATTN_HEADS).reshape(t, d)
        h, hg, rstd = _matmul(xo, xattn_wo[l], F32, residual=h, norm_gain=norm3_w[l],
                              name="xattn_o")

        qp = _matmul(hg, peer_wq[l], BF16, row_scale=rstd, name="peer_q")
        rank1, e1, cnt, e0, u_b, v_b = _peer_route(qp, peer_subkeys[l], peer_u[l], peer_v[l])
        delta = _peer_experts(hg, rstd.reshape(1, t), u_b, v_b, rank1, e1, cnt, e0)
        if l + 1 < depth:
            h = h + delta.astype(F32)
            xn = _rmsnorm(h, norm1_w[l + 1], BF16)
    return _add_rmsnorm(h, delta, final_norm_w, F32).reshape(b, s, d)
```

```python
import functools

import numpy as np
import jax
import jax.numpy as jnp
from jax import lax
from jax.experimental import pallas as pl
from jax.experimental.pallas import tpu as pltpu

F32 = jnp.float32
BF16 = jnp.bfloat16

CHUNK = 64
LEFT_CHUNKS = 8
LEFT = LEFT_CHUNKS * CHUNK
ATTN_HEAD_DIM = 128
MAX_REL_DIST = 256
RET_V_DIM = 256
RET_QK_DIM = 128
ROPE_BASE = 10000.0
XATTN_HEADS = 4
PEER_HEADS = 8
PEER_KEYS = 128
PEER_HALF = 128
PEER_TOPK = 16
EPS = 1e-6
NEG_INF = -1e30
NOT_RANKED = 1e9

VMEM_LIMIT_BYTES = 56 * 1024 * 1024
LANES = 128

_NT = (((1,), (1,)), ((), ()))
_TN = (((0,), (0,)), ((), ()))


def _params(n_grid_dims, flags=None):
    return pltpu.CompilerParams(
        dimension_semantics=("arbitrary",) * n_grid_dims,
        vmem_limit_bytes=VMEM_LIMIT_BYTES,
        flags=flags)


def _rmsnorm_kernel(x_ref, w_ref, o_ref):
    x = x_ref[...]
    ms = jnp.mean(x * x, axis=-1, keepdims=True)
    o_ref[...] = (x * lax.rsqrt(ms + EPS) * w_ref[...]).astype(o_ref.dtype)


def _rmsnorm(x, w, out_dtype, rows=256):
    m, d = x.shape
    return pl.pallas_call(
        _rmsnorm_kernel,
        grid=(m // rows,),
        in_specs=[pl.BlockSpec((rows, d), lambda i: (i, 0)),
                  pl.BlockSpec((1, d), lambda i: (0, 0))],
        out_specs=pl.BlockSpec((rows, d), lambda i: (i, 0)),
        out_shape=jax.ShapeDtypeStruct((m, d), out_dtype),
        compiler_params=_params(1),
        name="rmsnorm",
    )(x, w.reshape(1, d))


def _add_rmsnorm_kernel(x_ref, y_ref, w_ref, o_ref):
    x = x_ref[...] + y_ref[...].astype(F32)
    ms = jnp.mean(x * x, axis=-1, keepdims=True)
    o_ref[...] = (x * lax.rsqrt(ms + EPS) * w_ref[...]).astype(o_ref.dtype)


def _add_rmsnorm(x, y, w, out_dtype, rows=256):
    m, d = x.shape
    return pl.pallas_call(
        _add_rmsnorm_kernel,
        grid=(m // rows,),
        in_specs=[pl.BlockSpec((rows, d), lambda i: (i, 0)),
                  pl.BlockSpec((rows, d), lambda i: (i, 0)),
                  pl.BlockSpec((1, d), lambda i: (0, 0))],
        out_specs=pl.BlockSpec((rows, d), lambda i: (i, 0)),
        out_shape=jax.ShapeDtypeStruct((m, d), out_dtype),
        compiler_params=_params(1),
        name="add_rmsnorm",
    )(x, y, w.reshape(1, d))


def _mm_kernel(*refs, n_lhs, has_residual, has_row_scale, emit_norm, n_cols):
    lhs_refs, w_ref = refs[:n_lhs], refs[n_lhs]
    pos = n_lhs + 1
    acc, k0 = None, 0
    for a_ref in lhs_refs:
        k1 = k0 + a_ref.shape[1]
        part = jnp.dot(a_ref[...], w_ref[k0:k1, :].astype(BF16), preferred_element_type=F32)
        acc = part if acc is None else acc + part
        k0 = k1
    if has_row_scale:
        acc = acc * refs[pos][...]
        pos += 1
    if has_residual:
        acc = refs[pos][...] + acc
        pos += 1
    if not emit_norm:
        o_ref = refs[pos]
        o_ref[...] = acc.astype(o_ref.dtype)
        return
    gain_ref, o_ref, scaled_ref, rstd_ref, ssq_ref = refs[pos:pos + 5]
    j = pl.program_id(1)
    o_ref[...] = acc.astype(o_ref.dtype)
    scaled_ref[...] = (acc * gain_ref[...]).astype(scaled_ref.dtype)
    row_ssq = jnp.sum(acc * acc, axis=-1, keepdims=True)

    @pl.when(j == 0)
    def _():
        ssq_ref[...] = row_ssq

    @pl.when(j > 0)
    def _():
        ssq_ref[...] += row_ssq

    @pl.when(j == pl.num_programs(1) - 1)
    def _():
        rstd_ref[...] = lax.rsqrt(ssq_ref[...] / n_cols + EPS)


def _matmul(lhs, w, out_dtype, residual=None, row_scale=None, norm_gain=None, tm=1024, tn=512,
            name="matmul"):
    lhs = lhs if isinstance(lhs, (tuple, list)) else (lhs,)
    m = lhs[0].shape[0]
    k, n = w.shape
    assert sum(a.shape[1] for a in lhs) == k
    tm, tn = min(tm, m), min(tn, n)
    in_specs = [pl.BlockSpec((tm, a.shape[1]), lambda i, j: (i, 0)) for a in lhs]
    in_specs.append(pl.BlockSpec((k, tn), lambda i, j: (0, j)))
    args = [*lhs, w]
    tile = pl.BlockSpec((tm, tn), lambda i, j: (i, j))
    per_row = pl.BlockSpec((tm, 1), lambda i, j: (i, 0))
    if row_scale is not None:
        in_specs.append(per_row)
        args.append(row_scale)
    if residual is not None:
        in_specs.append(tile)
        args.append(residual)
    out_specs, out_shape, scratch = tile, jax.ShapeDtypeStruct((m, n), out_dtype), []
    if norm_gain is not None:
        in_specs.append(pl.BlockSpec((1, tn), lambda i, j: (0, j)))
        args.append(norm_gain.reshape(1, n))
        out_specs = [tile, tile, per_row]
        out_shape = [out_shape, jax.ShapeDtypeStruct((m, n), BF16),
                     jax.ShapeDtypeStruct((m, 1), F32)]
        scratch = [pltpu.VMEM((tm, 1), F32)]
    return pl.pallas_call(
        functools.partial(_mm_kernel, n_lhs=len(lhs), has_residual=residual is not None,
                          has_row_scale=row_scale is not None, emit_norm=norm_gain is not None,
                          n_cols=n),
        grid=(m // tm, n // tn),
        in_specs=in_specs,
        out_specs=out_specs,
        out_shape=out_shape,
        scratch_shapes=scratch,
        compiler_params=_params(2),
        name=name,
    )(*args)


def _attn_kernel(q_ref, k_ref, v_ref, base_ref, o_ref, bias_ref, *, qb, scale):
    seq = q_ref.shape[0]
    width = LEFT + qb

    @pl.when(pl.program_id(1) == 0)
    def _():
        toeplitz = pltpu.roll(jnp.broadcast_to(base_ref[...], (qb, base_ref.shape[-1])),
                              0, 1, stride=1, stride_axis=0)
        q_chunk = lax.broadcasted_iota(jnp.int32, (qb, width), 0) // CHUNK
        c_chunk = lax.broadcasted_iota(jnp.int32, (qb, width), 1) // CHUNK
        in_band = (c_chunk >= q_chunk) & (c_chunk <= q_chunk + LEFT_CHUNKS)
        bias_ref[...] = jnp.where(in_band, toeplitz[:, :width], NEG_INF)

    for i in range(seq // qb):
        q0 = i * qb
        k0 = max(0, q0 - LEFT)
        kw = q0 + qb - k0
        c0 = k0 - (q0 - LEFT)
        s = lax.dot_general(q_ref[q0:q0 + qb, :], k_ref[k0:k0 + kw, :], _NT,
                            preferred_element_type=F32)
        s = s * scale + bias_ref[:, c0:c0 + kw]
        m = jnp.max(s, axis=-1, keepdims=True)
        p = jnp.exp(s - m)
        l = jnp.sum(p, axis=-1, keepdims=True)
        o = jnp.dot(p.astype(BF16), v_ref[k0:k0 + kw, :], preferred_element_type=F32)
        o_ref[q0:q0 + qb, :] = (o / l).astype(o_ref.dtype)


def _attn_bias_base(rel_bias, qb):
    w = pl.next_power_of_2(LEFT + 2 * qb)
    j = jnp.arange(w)
    j = jnp.where(j < LEFT + qb, j, j - w)
    idx = jnp.clip(LEFT - j, -MAX_REL_DIST, MAX_REL_DIST) + MAX_REL_DIST
    return rel_bias[:, None, idx].astype(F32)


def _chunk_attention(proj, rel_bias, n_heads, qb=256):
    b, s, _ = proj.shape
    dh = ATTN_HEAD_DIM
    base = _attn_bias_base(rel_bias, qb)
    kern = functools.partial(_attn_kernel, qb=qb, scale=dh ** -0.5)
    return pl.pallas_call(
        kern,
        grid=(n_heads, b),
        in_specs=[pl.BlockSpec((None, s, dh), lambda h, bi: (bi, 0, h)),
                  pl.BlockSpec((None, s, dh), lambda h, bi: (bi, 0, n_heads + h)),
                  pl.BlockSpec((None, s, dh), lambda h, bi: (bi, 0, 2 * n_heads + h)),
                  pl.BlockSpec((None, 1, base.shape[-1]), lambda h, bi: (h, 0, 0))],
        out_specs=pl.BlockSpec((None, s, dh), lambda h, bi: (bi, 0, h)),
        out_shape=jax.ShapeDtypeStruct((b, s, n_heads * dh), BF16),
        scratch_shapes=[pltpu.VMEM((qb, LEFT + qb), F32)],
        compiler_params=_params(2),
        name="chunk_attention",
    )(proj, proj, proj, base)


def _ret_kernel(q_ref, k_ref, v_ref, g_ref, cos_ref, sin_ref, dec_ref, qd_ref, kd_ref, bd_ref,
                gnw_ref, o_ref, state_ref, *, rb, scale):
    seq = q_ref.shape[0]
    state_ref[...] = jnp.zeros_like(state_ref)
    half = q_ref.shape[1] // 2

    def body(n, carry):
        r0 = pl.multiple_of(n * rb, rb)
        rows = pl.ds(r0, rb)
        cos = cos_ref[rows, :]
        sin = sin_ref[rows, :]
        q = q_ref[rows, :].astype(F32)
        k = k_ref[rows, :].astype(F32)
        q = q * cos + pltpu.roll(q, half, 1) * sin
        k = (k * cos + pltpu.roll(k, half, 1) * sin) * scale
        v = v_ref[rows, :]
        a = lax.dot_general(q.astype(BF16), k.astype(BF16), _NT, preferred_element_type=F32)
        a = a * dec_ref[...]
        st = state_ref[...]
        y = jnp.dot(a.astype(BF16), v, preferred_element_type=F32)
        y = y + jnp.dot((q * qd_ref[...]).astype(BF16), st.astype(BF16),
                        preferred_element_type=F32)
        kd = (k * kd_ref[...]).astype(BF16)
        kv = lax.dot_general(kd, v, _TN, preferred_element_type=F32)
        state_ref[...] = st * bd_ref[...] + kv
        mu = jnp.mean(y, axis=-1, keepdims=True)
        yc = y - mu
        var = jnp.mean(yc * yc, axis=-1, keepdims=True)
        yn = yc * lax.rsqrt(var + EPS) * gnw_ref[...]
        g = g_ref[rows, :].astype(F32)
        o_ref[rows, :] = (g * (1.0 / (1.0 + jnp.exp(-g))) * yn).astype(o_ref.dtype)
        return carry

    lax.fori_loop(0, seq // rb, body, 0, unroll=True)


def _retention(proj, gn_w, n_heads, col0, rb=256):
    b, s, _ = proj.shape
    dk, dv = RET_QK_DIM, RET_V_DIM
    qk0 = col0 // dk
    v0 = (col0 + 2 * n_heads * dk) // dv
    inv_freq = 1.0 / (ROPE_BASE ** (jnp.arange(0, dk, 2, dtype=F32) / dk))
    ang = jnp.arange(s, dtype=F32)[:, None] * inv_freq[None, :]
    cos = jnp.concatenate([jnp.cos(ang), jnp.cos(ang)], axis=-1)
    sin = jnp.concatenate([-jnp.sin(ang), jnp.sin(ang)], axis=-1)
    log_gamma = jnp.log1p(-jnp.power(2.0, -5.0 - jnp.arange(n_heads, dtype=F32)))
    pos = jnp.arange(rb, dtype=F32)
    chunk_of = jnp.arange(rb) // CHUNK
    causal = (chunk_of[None, :] <= chunk_of[:, None]).astype(F32)
    dec = jnp.exp(log_gamma[:, None, None] * jnp.abs(pos[:, None] - pos[None, :])) * causal[None]
    qd = jnp.exp(log_gamma[:, None] * (pos + 1.0))[:, :, None]
    kd = jnp.exp(log_gamma[:, None] * (rb - 1.0 - pos))[:, :, None]
    bd = jnp.exp(log_gamma * rb)[:, None, None]
    kern = functools.partial(_ret_kernel, rb=rb, scale=dk ** -0.5)
    return pl.pallas_call(
        kern,
        grid=(b, n_heads),
        in_specs=[pl.BlockSpec((None, s, dk), lambda bi, h: (bi, 0, qk0 + h)),
                  pl.BlockSpec((None, s, dk), lambda bi, h: (bi, 0, qk0 + n_heads + h)),
                  pl.BlockSpec((None, s, dv), lambda bi, h: (bi, 0, v0 + h)),
                  pl.BlockSpec((None, s, dv), lambda bi, h: (bi, 0, v0 + n_heads + h)),
                  pl.BlockSpec((s, dk), lambda bi, h: (0, 0)),
                  pl.BlockSpec((s, dk), lambda bi, h: (0, 0)),
                  pl.BlockSpec((None, rb, rb), lambda bi, h: (h, 0, 0)),
                  pl.BlockSpec((None, rb, 1), lambda bi, h: (h, 0, 0)),
                  pl.BlockSpec((None, rb, 1), lambda bi, h: (h, 0, 0)),
                  pl.BlockSpec((None, 1, 1), lambda bi, h: (h, 0, 0)),
                  pl.BlockSpec((1, dv), lambda bi, h: (0, h))],
        out_specs=pl.BlockSpec((None, s, dv), lambda bi, h: (bi, 0, h)),
        out_shape=jax.ShapeDtypeStruct((b, s, n_heads * dv), BF16),
        scratch_shapes=[pltpu.VMEM((dk, dv), F32)],
        compiler_params=_params(2),
        name="retention",
    )(proj, proj, proj, proj, cos, sin, dec, qd, kd, bd, gn_w.reshape(1, -1))


def _xattn_kernel(q_ref, kv_ref, o_ref, *, n_heads, scale):
    d = q_ref.shape[1]
    dh = d // n_heads
    for h in range(n_heads):
        cols = slice(h * dh, (h + 1) * dh)
        s = lax.dot_general(q_ref[:, cols], kv_ref[:, cols], _NT, preferred_element_type=F32)
        s = s * scale
        m = jnp.max(s, axis=-1, keepdims=True)
        p = jnp.exp(s - m)
        l = jnp.sum(p, axis=-1, keepdims=True)
        o = jnp.dot(p.astype(BF16), kv_ref[:, d + h * dh:d + (h + 1) * dh],
                    preferred_element_type=F32)
        o_ref[:, cols] = (o / l).astype(o_ref.dtype)


def _cross_attention(q, kv, n_heads, tq=512):
    b, s, d = q.shape
    m = kv.shape[1]
    kern = functools.partial(_xattn_kernel, n_heads=n_heads, scale=(d // n_heads) ** -0.5)
    return pl.pallas_call(
        kern,
        grid=(b, s // tq),
        in_specs=[pl.BlockSpec((None, tq, d), lambda bi, i: (bi, i, 0)),
                  pl.BlockSpec((None, m, 2 * d), lambda bi, i: (bi, 0, 0))],
        out_specs=pl.BlockSpec((None, tq, d), lambda bi, i: (bi, i, 0)),
        out_shape=jax.ShapeDtypeStruct((b, s, d), BF16),
        compiler_params=_params(2),
        name="cross_attention",
    )(q, kv)


def _take_first_max(work, rows):
    m = jnp.max(work, axis=0, keepdims=True)
    first = jnp.min(jnp.where(work == m, rows, np.float32(work.shape[0])), axis=0, keepdims=True)
    return m, rows == first


def _top_rows(s, k):
    rows = lax.broadcasted_iota(jnp.int32, s.shape, 0).astype(F32)
    work = s
    rank = jnp.full(s.shape, NOT_RANKED, F32)
    vals = []
    for r in range(k):
        m, sel = _take_first_max(work, rows)
        rank = jnp.where(sel, np.float32(r), rank)
        work = jnp.where(sel, -jnp.inf, work)
        vals.append(m)
    return vals, rank


def _candidate_cells(k):
    cells = []
    for ra in range(k):
        cells += [(ra, rb) for rb in range(k // (ra + 1))]
    single = [c for c in cells if k // (c[0] + 1) == 1]
    multi = [c for c in cells if c not in single]
    pad = (-len(multi)) % 8
    return multi + [None] * pad + single


def _route_kernel(q_ref, keys_ref, u_ref, v_ref, rank1_ref, e1_ref, cnt_ref, e0_ref, ub_ref, vb_ref):
    ub_ref[...] = u_ref[...].astype(ub_ref.dtype)
    vb_ref[...] = v_ref[...].astype(vb_ref.dtype)
    k = PEER_TOPK
    tl = q_ref.shape[0]
    cells = _candidate_cells(k)
    for g in range(tl // LANES):
        q = q_ref[g * LANES:(g + 1) * LANES, :]
        s0 = lax.dot_general(keys_ref[0].astype(BF16), q[:, :PEER_HALF], _NT,
                             preferred_element_type=F32)
        s1 = lax.dot_general(keys_ref[1].astype(BF16), q[:, PEER_HALF:], _NT,
                             preferred_element_type=F32)
        a, rank0 = _top_rows(s0, k)
        b, rank1 = _top_rows(s1, k)
        a_all, b_all = jnp.concatenate(a, axis=0), jnp.concatenate(b, axis=0)
        ea_all, eb_all = jnp.exp(a_all - a[0]), jnp.exp(b_all - b[0])
        neg = jnp.full_like(a[0], -jnp.inf)
        zero = jnp.zeros_like(a[0])
        n0, n1 = k, k // 2
        mid = cells[n0 + n1:len(cells) - k // 2]
        cand = jnp.concatenate(
            [a[0] + b_all, a[1] + b_all[:n1]]
            + [neg if c is None else a_all[c[0]:c[0] + 1] + b_all[c[1]:c[1] + 1] for c in mid]
            + [a_all[k // 2:] + b[0]], axis=0)
        wgt = jnp.concatenate(
            [ea_all[0:1] * eb_all, ea_all[1:2] * eb_all[:n1]]
            + [zero if c is None else ea_all[c[0]:c[0] + 1] * eb_all[c[1]:c[1] + 1] for c in mid]
            + [ea_all[k // 2:] * eb_all[0:1]], axis=0)
        rows = lax.broadcasted_iota(jnp.int32, cand.shape, 0).astype(F32)
        work = cand
        chosen = jnp.zeros(cand.shape, F32)
        for _ in range(k):
            _, sel = _take_first_max(work, rows)
            chosen = jnp.where(sel, 1.0, chosen)
            work = jnp.where(sel, -jnp.inf, work)
        z = jnp.sum(chosen * wgt, axis=0, keepdims=True)
        cnt = jnp.zeros(s0.shape, F32)
        for ra in range(k):
            mine = [i for i, c in enumerate(cells) if c is not None and c[0] == ra]
            n_ra = jnp.sum(chosen[mine[0]:mine[-1] + 1], axis=0, keepdims=True)
            cnt = jnp.where(rank0 == np.float32(ra), n_ra, cnt)
        lanes = slice(g * LANES, (g + 1) * LANES)
        rank1_ref[:, lanes] = rank1.astype(rank1_ref.dtype)
        e1_ref[:, lanes] = (jnp.exp(s1 - b[0]) / z).astype(e1_ref.dtype)
        cnt_ref[:, lanes] = cnt
        e0_ref[:, lanes] = jnp.exp(s0 - a[0])


def _peer_route(qp, subkeys, u, v, tl=512):
    t = qp.shape[0]
    hp, _, nk, half = subkeys.shape
    n_exp, d = u.shape
    steps = (t // tl) * hp
    rows = n_exp // steps
    assert rows * steps == n_exp
    out = lambda dt: jax.ShapeDtypeStruct((hp, nk, t), dt)
    ospec = pl.BlockSpec((None, nk, tl), lambda i, h: (h, 0, i))
    wspec = pl.BlockSpec((rows, d), lambda i, h: (i * hp + h, 0))
    wout = jax.ShapeDtypeStruct((n_exp, d), BF16)
    return pl.pallas_call(
        _route_kernel,
        grid=(t // tl, hp),
        in_specs=[pl.BlockSpec((tl, 2 * half), lambda i, h: (i, h)),
                  pl.BlockSpec((None, 2, nk, half), lambda i, h: (h, 0, 0, 0)),
                  wspec, wspec],
        out_specs=[ospec] * 4 + [wspec] * 2,
        out_shape=[out(BF16), out(BF16), out(F32), out(F32), wout, wout],
        compiler_params=_params(2),
        name="peer_route",
    )(qp, subkeys, u, v)


def _peer_kernel(x_ref, rstd_ref, u_ref, v_ref, rank1_ref, e1_ref, cnt_ref, e0_ref, zero_ref, o_ref,
                 acc_ref, xt_ref, *stage_refs, sub):
    e = pl.program_id(1)
    n_heads, nk, tm = rank1_ref.shape
    te, d = u_ref.shape
    n_sub = te // sub
    rows_per_sub = sub // nk
    n_pieces = n_heads
    kc = d // n_pieces
    rows_per_step = te // nk
    row0 = (e % (8 // rows_per_step)) * rows_per_step
    act_refs, wt_refs = stage_refs[:n_sub], stage_refs[n_sub:]

    @pl.when(e == 0)
    def _():
        acc_ref[...] = jnp.zeros_like(acc_ref)
        xt_ref[...] = x_ref[...].T

    def up_piece(j, p):
        part = jnp.dot(u_ref[j * sub:(j + 1) * sub, p * kc:(p + 1) * kc],
                       xt_ref[p * kc:(p + 1) * kc, :], preferred_element_type=F32)
        if p == 0:
            act_refs[j][...] = part
        else:
            act_refs[j][...] += part

    tg = tm // (n_pieces // rows_per_sub)

    def gate_piece(j, p, just_stored=None):
        ii, g = divmod(p, n_pieces // rows_per_sub)
        lanes = slice(g * tg, (g + 1) * tg)
        row = j * rows_per_sub + ii
        gate = None
        tie = 0.0
        if just_stored is not None:
            tie = lax.bitcast_convert_type(
                lax.bitcast_convert_type(just_stored, jnp.int32) & zero_ref[...], F32)
        for h in range(n_heads):
            cnt = (cnt_ref[h, pl.ds(row0 + row, 1), :][:, lanes] + tie).astype(BF16)
            e0 = e0_ref[h, pl.ds(row0 + row, 1), :][:, lanes].astype(BF16)
            zero = jnp.zeros((), BF16)
            term = e0 * jnp.where(rank1_ref[h, :, lanes] < cnt, e1_ref[h, :, lanes], zero)
            gate = term if gate is None else gate + term
        act = act_refs[j][ii * nk:(ii + 1) * nk, lanes] * rstd_ref[:, lanes]
        act = 0.5 * act * (1.0 + lax.erf(act * np.float32(1.0 / np.sqrt(2.0))))
        wt_refs[j][lanes, ii * nk:(ii + 1) * nk] = (gate * act.astype(BF16)).T

    def down_piece(j, p):
        cols = slice(p * kc, (p + 1) * kc)
        acc_ref[:, cols] += jnp.dot(wt_refs[j][...], v_ref[j * sub:(j + 1) * sub, cols],
                                    preferred_element_type=F32)

    for step in range(n_sub + 2):
        for p in range(n_pieces):
            if step < n_sub:
                up_piece(step, p)
            if 0 <= step - 2 < n_sub:
                down_piece(step - 2, p)
            if 0 <= step - 1 < n_sub:
                if step < n_sub:
                    gate_piece(step - 1, p)
                else:
                    gate_piece(step - 1, p, acc_ref[0:1, p * kc:p * kc + tg])

    @pl.when(e == pl.num_programs(1) - 1)
    def _():
        o_ref[...] = acc_ref[...].astype(o_ref.dtype)


def _peer_experts(x, rstd, u, v, rank1, e1, cnt, e0, tm=512, te=512, sub=256):
    t, d = x.shape
    n_exp = u.shape[0]
    hp, nk, _ = rank1.shape
    rows = te // nk
    tab = pl.BlockSpec((hp, nk, tm), lambda i, e: (0, 0, i))
    row_tab = pl.BlockSpec((hp, 8, tm), lambda i, e: (0, (e * rows) // 8, i))
    tg = tm // (hp // (sub // nk))
    zero_bits = jnp.zeros((1, tg), jnp.int32)
    return pl.pallas_call(
        functools.partial(_peer_kernel, sub=sub),
        grid=(t // tm, n_exp // te),
        in_specs=[pl.BlockSpec((tm, d), lambda i, e: (i, 0)),
                  pl.BlockSpec((1, tm), lambda i, e: (0, i)),
                  pl.BlockSpec((te, d), lambda i, e: (e, 0)),
                  pl.BlockSpec((te, d), lambda i, e: (e, 0)),
                  tab, tab, row_tab, row_tab,
                  pl.BlockSpec((1, tg), lambda i, e: (0, 0))],
        out_specs=pl.BlockSpec((tm, d), lambda i, e: (i, 0)),
        out_shape=jax.ShapeDtypeStruct((t, d), BF16),
        scratch_shapes=([pltpu.VMEM((tm, d), F32), pltpu.VMEM((d, tm), BF16)]
                        + [pltpu.VMEM((sub, tm), F32)] * (te // sub)
                        + [pltpu.VMEM((tm, sub), BF16)] * (te // sub)),
        compiler_params=_params(2),
        name="peer_experts",
    )(x, rstd, u, v, rank1, e1, cnt, e0, zero_bits)


def kernel(x, mem, norm1_w, w_in, attn_rel_bias, ret_gn_w, w_out, norm2_w, mem_norm_w, xattn_wq,
           xattn_wkv, xattn_wo, norm3_w, peer_wq, peer_subkeys, peer_u, peer_v, final_norm_w):
    b, s, d = x.shape
    t = b * s
    depth = w_in.shape[0]
    n_attn_heads = attn_rel_bias.shape[1]
    attn_width = n_attn_heads * ATTN_HEAD_DIM
    n_ret_heads = ret_gn_w.shape[1] // RET_V_DIM
    h = x.reshape(t, d)
    xn = _rmsnorm(h, norm1_w[0], BF16)
    for l in range(depth):
        proj = _matmul(xn, w_in[l], BF16, name="in_proj").reshape(b, s, -1)
        a_out = _chunk_attention(proj, attn_rel_bias[l], n_attn_heads)
        r_out = _retention(proj, ret_gn_w[l], n_ret_heads, 3 * attn_width)
        h, hg, rstd = _matmul((a_out.reshape(t, -1), r_out.reshape(t, -1)), w_out[l], F32,
                              residual=h, norm_gain=norm2_w[l], name="out_proj")

        mem_n = _rmsnorm(mem.reshape(-1, d), mem_norm_w[l], BF16)
        kv = _matmul(mem_n, xattn_wkv[l], BF16, name="mem_kv").reshape(b, -1, 2 * d)
        xq = _matmul(hg, xattn_wq[l], BF16, row_scale=rstd, name="xattn_q")
        xo = _cross_attention(xq.reshape(b, s, d), kv, XATTN_HEADS).reshape(t, d)
        h, hg, rstd = _matmul(xo, xattn_wo[l], F32, residual=h, norm_gain=norm3_w[l],
                              name="xattn_o")

        qp = _matmul(hg, peer_wq[l], BF16, row_scale=rstd, name="peer_q")
        rank1, e1, cnt, e0, u_b, v_b = _peer_route(qp, peer_subkeys[l], peer_u[l], peer_v[l])
        delta = _peer_experts(hg, rstd.reshape(1, t), u_b, v_b, rank1, e1, cnt, e0)
        if l + 1 < depth:
            h = h + delta.astype(F32)
            xn = _rmsnorm(h, norm1_w[l + 1], BF16)
    return _add_rmsnorm(h, delta, final_norm_w, F32).reshape(b, s, d)
```

```python
import functools

import numpy as np
import jax
import jax.numpy as jnp
from jax import lax
from jax.experimental import pallas as pl
from jax.experimental.pallas import tpu as pltpu

F32 = jnp.float32
BF16 = jnp.bfloat16

CHUNK = 64
LEFT_CHUNKS = 8
LEFT = LEFT_CHUNKS * CHUNK
ATTN_HEAD_DIM = 128
MAX_REL_DIST = 256
RET_V_DIM = 256
RET_QK_DIM = 128
ROPE_BASE = 10000.0
XATTN_HEADS = 4
PEER_HEADS = 8
PEER_KEYS = 128
PEER_HALF = 128
PEER_TOPK = 16
EPS = 1e-6
NEG_INF = -1e30
NOT_RANKED = 1e9

VMEM_LIMIT_BYTES = 56 * 1024 * 1024
LANES = 128

_NT = (((1,), (1,)), ((), ()))
_TN = (((0,), (0,)), ((), ()))


def _params(n_grid_dims, flags=None):
    return pltpu.CompilerParams(
        dimension_semantics=("arbitrary",) * n_grid_dims,
        vmem_limit_bytes=VMEM_LIMIT_BYTES,
        flags=flags)


def _rmsnorm_kernel(x_ref, w_ref, o_ref):
    x = x_ref[...]
    ms = jnp.mean(x * x, axis=-1, keepdims=True)
    o_ref[...] = (x * lax.rsqrt(ms + EPS) * w_ref[...]).astype(o_ref.dtype)


def _rmsnorm(x, w, out_dtype, rows=256):
    m, d = x.shape
    return pl.pallas_call(
        _rmsnorm_kernel,
        grid=(m // rows,),
        in_specs=[pl.BlockSpec((rows, d), lambda i: (i, 0)),
                  pl.BlockSpec((1, d), lambda i: (0, 0))],
        out_specs=pl.BlockSpec((rows, d), lambda i: (i, 0)),
        out_shape=jax.ShapeDtypeStruct((m, d), out_dtype),
        compiler_params=_params(1),
        name="rmsnorm",
    )(x, w.reshape(1, d))


def _add_rmsnorm_kernel(x_ref, y_ref, w_ref, o_ref):
    x = x_ref[...] + y_ref[...].astype(F32)
    ms = jnp.mean(x * x, axis=-1, keepdims=True)
    o_ref[...] = (x * lax.rsqrt(ms + EPS) * w_ref[...]).astype(o_ref.dtype)


def _add_rmsnorm(x, y, w, out_dtype, rows=256):
    m, d = x.shape
    return pl.pallas_call(
        _add_rmsnorm_kernel,
        grid=(m // rows,),
        in_specs=[pl.BlockSpec((rows, d), lambda i: (i, 0)),
                  pl.BlockSpec((rows, d), lambda i: (i, 0)),
                  pl.BlockSpec((1, d), lambda i: (0, 0))],
        out_specs=pl.BlockSpec((rows, d), lambda i: (i, 0)),
        out_shape=jax.ShapeDtypeStruct((m, d), out_dtype),
        compiler_params=_params(1),
        name="add_rmsnorm",
    )(x, y, w.reshape(1, d))


def _mm_kernel(*refs, n_lhs, has_residual, has_row_scale, emit_norm, n_cols):
    lhs_refs, w_ref = refs[:n_lhs], refs[n_lhs]
    pos = n_lhs + 1
    acc, k0 = None, 0
    for a_ref in lhs_refs:
        k1 = k0 + a_ref.shape[1]
        part = jnp.dot(a_ref[...], w_ref[k0:k1, :].astype(BF16), preferred_element_type=F32)
        acc = part if acc is None else acc + part
        k0 = k1
    if has_row_scale:
        acc = acc * refs[pos][...]
        pos += 1
    if has_residual:
        acc = refs[pos][...] + acc
        pos += 1
    if not emit_norm:
        o_ref = refs[pos]
        o_ref[...] = acc.astype(o_ref.dtype)
        return
    gain_ref, o_ref, scaled_ref, rstd_ref, ssq_ref = refs[pos:pos + 5]
    j = pl.program_id(1)
    o_ref[...] = acc.astype(o_ref.dtype)
    scaled_ref[...] = (acc * gain_ref[...]).astype(scaled_ref.dtype)
    row_ssq = jnp.sum(acc * acc, axis=-1, keepdims=True)

    @pl.when(j == 0)
    def _():
        ssq_ref[...] = row_ssq

    @pl.when(j > 0)
    def _():
        ssq_ref[...] += row_ssq

    @pl.when(j == pl.num_programs(1) - 1)
    def _():
        rstd_ref[...] = lax.rsqrt(ssq_ref[...] / n_cols + EPS)


def _matmul(lhs, w, out_dtype, residual=None, row_scale=None, norm_gain=None, tm=1024, tn=512,
            name="matmul"):
    lhs = lhs if isinstance(lhs, (tuple, list)) else (lhs,)
    m = lhs[0].shape[0]
    k, n = w.shape
    assert sum(a.shape[1] for a in lhs) == k
    tm, tn = min(tm, m), min(tn, n)
    in_specs = [pl.BlockSpec((tm, a.shape[1]), lambda i, j: (i, 0)) for a in lhs]
    in_specs.append(pl.BlockSpec((k, tn), lambda i, j: (0, j)))
    args = [*lhs, w]
    tile = pl.BlockSpec((tm, tn), lambda i, j: (i, j))
    per_row = pl.BlockSpec((tm, 1), lambda i, j: (i, 0))
    if row_scale is not None:
        in_specs.append(per_row)
        args.append(row_scale)
    if residual is not None:
        in_specs.append(tile)
        args.append(residual)
    out_specs, out_shape, scratch = tile, jax.ShapeDtypeStruct((m, n), out_dtype), []
    if norm_gain is not None:
        in_specs.append(pl.BlockSpec((1, tn), lambda i, j: (0, j)))
        args.append(norm_gain.reshape(1, n))
        out_specs = [tile, tile, per_row]
        out_shape = [out_shape, jax.ShapeDtypeStruct((m, n), BF16),
                     jax.ShapeDtypeStruct((m, 1), F32)]
        scratch = [pltpu.VMEM((tm, 1), F32)]
    return pl.pallas_call(
        functools.partial(_mm_kernel, n_lhs=len(lhs), has_residual=residual is not None,
                          has_row_scale=row_scale is not None, emit_norm=norm_gain is not None,
                          n_cols=n),
        grid=(m // tm, n // tn),
        in_specs=in_specs,
        out_specs=out_specs,
        out_shape=out_shape,
        scratch_shapes=scratch,
        compiler_params=_params(2),
        name=name,
    )(*args)


def _attn_kernel(q_ref, k_ref, v_ref, base_ref, o_ref, bias_ref, *, qb, scale):
    seq = q_ref.shape[0]
    width = LEFT + qb

    @pl.when(pl.program_id(1) == 0)
    def _():
        toeplitz = pltpu.roll(jnp.broadcast_to(base_ref[...], (qb, base_ref.shape[-1])),
                              0, 1, stride=1, stride_axis=0)
        q_chunk = lax.broadcasted_iota(jnp.int32, (qb, width), 0) // CHUNK
        c_chunk = lax.broadcasted_iota(jnp.int32, (qb, width), 1) // CHUNK
        in_band = (c_chunk >= q_chunk) & (c_chunk <= q_chunk + LEFT_CHUNKS)
        bias_ref[...] = jnp.where(in_band, toeplitz[:, :width], NEG_INF)

    for i in range(seq // qb):
        q0 = i * qb
        k0 = max(0, q0 - LEFT)
        kw = q0 + qb - k0
        c0 = k0 - (q0 - LEFT)
        s = lax.dot_general(q_ref[q0:q0 + qb, :], k_ref[k0:k0 + kw, :], _NT,
                            preferred_element_type=F32)
        s = s * scale + bias_ref[:, c0:c0 + kw]
        m = jnp.max(s, axis=-1, keepdims=True)
        p = jnp.exp(s - m)
        l = jnp.sum(p, axis=-1, keepdims=True)
        o = jnp.dot(p.astype(BF16), v_ref[k0:k0 + kw, :], preferred_element_type=F32)
        o_ref[q0:q0 + qb, :] = (o / l).astype(o_ref.dtype)


def _attn_bias_base(rel_bias, qb):
    w = pl.next_power_of_2(LEFT + 2 * qb)
    j = jnp.arange(w)
    j = jnp.where(j < LEFT + qb, j, j - w)
    idx = jnp.clip(LEFT - j, -MAX_REL_DIST, MAX_REL_DIST) + MAX_REL_DIST
    return rel_bias[:, None, idx].astype(F32)


def _chunk_attention(proj, rel_bias, n_heads, qb=256):
    b, s, _ = proj.shape
    dh = ATTN_HEAD_DIM
    base = _attn_bias_base(rel_bias, qb)
    kern = functools.partial(_attn_kernel, qb=qb, scale=dh ** -0.5)
    return pl.pallas_call(
        kern,
        grid=(n_heads, b),
        in_specs=[pl.BlockSpec((None, s, dh), lambda h, bi: (bi, 0, h)),
                  pl.BlockSpec((None, s, dh), lambda h, bi: (bi, 0, n_heads + h)),
                  pl.BlockSpec((None, s, dh), lambda h, bi: (bi, 0, 2 * n_heads + h)),
                  pl.BlockSpec((None, 1, base.shape[-1]), lambda h, bi: (h, 0, 0))],
        out_specs=pl.BlockSpec((None, s, dh), lambda h, bi: (bi, 0, h)),
        out_shape=jax.ShapeDtypeStruct((b, s, n_heads * dh), BF16),
        scratch_shapes=[pltpu.VMEM((qb, LEFT + qb), F32)],
        compiler_params=_params(2),
        name="chunk_attention",
    )(proj, proj, proj, base)


def _ret_kernel(q_ref, k_ref, v_ref, g_ref, cos_ref, sin_ref, dec_ref, qd_ref, kd_ref, bd_ref,
                gnw_ref, o_ref, state_ref, *, rb, scale):
    seq = q_ref.shape[0]
    state_ref[...] = jnp.zeros_like(state_ref)
    half = q_ref.shape[1] // 2

    def body(n, carry):
        r0 = pl.multiple_of(n * rb, rb)
        rows = pl.ds(r0, rb)
        cos = cos_ref[rows, :]
        sin = sin_ref[rows, :]
        q = q_ref[rows, :].astype(F32)
        k = k_ref[rows, :].astype(F32)
        q = q * cos + pltpu.roll(q, half, 1) * sin
        k = (k * cos + pltpu.roll(k, half, 1) * sin) * scale
        v = v_ref[rows, :]
        a = lax.dot_general(q.astype(BF16), k.astype(BF16), _NT, preferred_element_type=F32)
        a = a * dec_ref[...]
        st = state_ref[...]
        y = jnp.dot(a.astype(BF16), v, preferred_element_type=F32)
        y = y + jnp.dot((q * qd_ref[...]).astype(BF16), st.astype(BF16),
                        preferred_element_type=F32)
        kd = (k * kd_ref[...]).astype(BF16)
        kv = lax.dot_general(kd, v, _TN, preferred_element_type=F32)
        state_ref[...] = st * bd_ref[...] + kv
        mu = jnp.mean(y, axis=-1, keepdims=True)
        yc = y - mu
        var = jnp.mean(yc * yc, axis=-1, keepdims=True)
        yn = yc * lax.rsqrt(var + EPS) * gnw_ref[...]
        g = g_ref[rows, :].astype(F32)
        o_ref[rows, :] = (g * (1.0 / (1.0 + jnp.exp(-g))) * yn).astype(o_ref.dtype)
        return carry

    lax.fori_loop(0, seq // rb, body, 0)


def _retention(proj, gn_w, n_heads, col0, rb=256):
    b, s, _ = proj.shape
    dk, dv = RET_QK_DIM, RET_V_DIM
    qk0 = col0 // dk
    v0 = (col0 + 2 * n_heads * dk) // dv
    inv_freq = 1.0 / (ROPE_BASE ** (jnp.arange(0, dk, 2, dtype=F32) / dk))
    ang = jnp.arange(s, dtype=F32)[:, None] * inv_freq[None, :]
    cos = jnp.concatenate([jnp.cos(ang), jnp.cos(ang)], axis=-1)
    sin = jnp.concatenate([-jnp.sin(ang), jnp.sin(ang)], axis=-1)
    log_gamma = jnp.log1p(-jnp.power(2.0, -5.0 - jnp.arange(n_heads, dtype=F32)))
    pos = jnp.arange(rb, dtype=F32)
    chunk_of = jnp.arange(rb) // CHUNK
    causal = (chunk_of[None, :] <= chunk_of[:, None]).astype(F32)
    dec = jnp.exp(log_gamma[:, None, None] * jnp.abs(pos[:, None] - pos[None, :])) * causal[None]
    qd = jnp.exp(log_gamma[:, None] * (pos + 1.0))[:, :, None]
    kd = jnp.exp(log_gamma[:, None] * (rb - 1.0 - pos))[:, :, None]
    bd = jnp.exp(log_gamma * rb)[:, None, None]
    kern = functools.partial(_ret_kernel, rb=rb, scale=dk ** -0.5)
    return pl.pallas_call(
        kern,
        grid=(b, n_heads),
        in_specs=[pl.BlockSpec((None, s, dk), lambda bi, h: (bi, 0, qk0 + h)),
                  pl.BlockSpec((None, s, dk), lambda bi, h: (bi, 0, qk0 + n_heads + h)),
                  pl.BlockSpec((None, s, dv), lambda bi, h: (bi, 0, v0 + h)),
                  pl.BlockSpec((None, s, dv), lambda bi, h: (bi, 0, v0 + n_heads + h)),
                  pl.BlockSpec((s, dk), lambda bi, h: (0, 0)),
                  pl.BlockSpec((s, dk), lambda bi, h: (0, 0)),
                  pl.BlockSpec((None, rb, rb), lambda bi, h: (h, 0, 0)),
                  pl.BlockSpec((None, rb, 1), lambda bi, h: (h, 0, 0)),
                  pl.BlockSpec((None, rb, 1), lambda bi, h: (h, 0, 0)),
                  pl.BlockSpec((None, 1, 1), lambda bi, h: (h, 0, 0)),
                  pl.BlockSpec((1, dv), lambda bi, h: (0, h))],
        out_specs=pl.BlockSpec((None, s, dv), lambda bi, h: (bi, 0, h)),
        out_shape=jax.ShapeDtypeStruct((b, s, n_heads * dv), BF16),
        scratch_shapes=[pltpu.VMEM((dk, dv), F32)],
        compiler_params=_params(2),
        name="retention",
    )(proj, proj, proj, proj, cos, sin, dec, qd, kd, bd, gn_w.reshape(1, -1))


def _xattn_kernel(q_ref, kv_ref, o_ref, *, n_heads, scale):
    d = q_ref.shape[1]
    dh = d // n_heads
    for h in range(n_heads):
        cols = slice(h * dh, (h + 1) * dh)
        s = lax.dot_general(q_ref[:, cols], kv_ref[:, cols], _NT, preferred_element_type=F32)
        s = s * scale
        m = jnp.max(s, axis=-1, keepdims=True)
        p = jnp.exp(s - m)
        l = jnp.sum(p, axis=-1, keepdims=True)
        o = jnp.dot(p.astype(BF16), kv_ref[:, d + h * dh:d + (h + 1) * dh],
                    preferred_element_type=F32)
        o_ref[:, cols] = (o / l).astype(o_ref.dtype)


def _cross_attention(q, kv, n_heads, tq=512):
    b, s, d = q.shape
    m = kv.shape[1]
    kern = functools.partial(_xattn_kernel, n_heads=n_heads, scale=(d // n_heads) ** -0.5)
    return pl.pallas_call(
        kern,
        grid=(b, s // tq),
        in_specs=[pl.BlockSpec((None, tq, d), lambda bi, i: (bi, i, 0)),
                  pl.BlockSpec((None, m, 2 * d), lambda bi, i: (bi, 0, 0))],
        out_specs=pl.BlockSpec((None, tq, d), lambda bi, i: (bi, i, 0)),
        out_shape=jax.ShapeDtypeStruct((b, s, d), BF16),
        compiler_params=_params(2),
        name="cross_attention",
    )(q, kv)


def _take_max(work, rows, exact):
    m = jnp.max(work, axis=0, keepdims=True)
    hit = work == m
    if not exact:
        return m, hit
    first = jnp.min(jnp.where(hit, rows, np.float32(work.shape[0])), axis=0, keepdims=True)
    return m, rows == first


def _top_rows(s, k, exact):
    rows = lax.broadcasted_iota(jnp.int32, s.shape, 0).astype(F32)
    work = s
    rank = jnp.full(s.shape, NOT_RANKED, F32)
    vals = []
    for r in range(k):
        m, sel = _take_max(work, rows, exact)
        rank = jnp.where(sel, np.float32(r), rank)
        work = jnp.where(sel, -jnp.inf, work)
        vals.append(m)
    taken = jnp.sum(jnp.where(rank < NOT_RANKED, 1.0, 0.0), axis=0, keepdims=True)
    return vals, rank, taken


def _candidate_cells(k):
    cells = []
    for ra in range(k):
        cells += [(ra, rb) for rb in range(k // (ra + 1))]
    single = [c for c in cells if k // (c[0] + 1) == 1]
    multi = [c for c in cells if c not in single]
    pad = (-len(multi)) % 8
    return multi + [None] * pad + single


def _route_kernel(q_ref, keys_ref, u_ref, v_ref, rank1_ref, e1_ref, cnt_ref, e0_ref, ub_ref, vb_ref):
    ub_ref[...] = u_ref[...].astype(ub_ref.dtype)
    vb_ref[...] = v_ref[...].astype(vb_ref.dtype)
    k = PEER_TOPK
    tl = q_ref.shape[0]
    cells = _candidate_cells(k)

    def route_group(g, exact):
        q = q_ref[g * LANES:(g + 1) * LANES, :]
        s0 = lax.dot_general(keys_ref[0].astype(BF16), q[:, :PEER_HALF], _NT,
                             preferred_element_type=F32)
        s1 = lax.dot_general(keys_ref[1].astype(BF16), q[:, PEER_HALF:], _NT,
                             preferred_element_type=F32)
        a, rank0, taken0 = _top_rows(s0, k, exact)
        b, rank1, taken1 = _top_rows(s1, k, exact)
        a_all, b_all = jnp.concatenate(a, axis=0), jnp.concatenate(b, axis=0)
        ea_all, eb_all = jnp.exp(a_all - a[0]), jnp.exp(b_all - b[0])
        neg = jnp.full_like(a[0], -jnp.inf)
        zero = jnp.zeros_like(a[0])
        n0, n1 = k, k // 2
        mid = cells[n0 + n1:len(cells) - k // 2]
        cand = jnp.concatenate(
            [a[0] + b_all, a[1] + b_all[:n1]]
            + [neg if c is None else a_all[c[0]:c[0] + 1] + b_all[c[1]:c[1] + 1] for c in mid]
            + [a_all[k // 2:] + b[0]], axis=0)
        wgt = jnp.concatenate(
            [ea_all[0:1] * eb_all, ea_all[1:2] * eb_all[:n1]]
            + [zero if c is None else ea_all[c[0]:c[0] + 1] * eb_all[c[1]:c[1] + 1] for c in mid]
            + [ea_all[k // 2:] * eb_all[0:1]], axis=0)
        rows = lax.broadcasted_iota(jnp.int32, cand.shape, 0).astype(F32)
        work = cand
        chosen = jnp.zeros(cand.shape, F32)
        for _ in range(k):
            _, sel = _take_max(work, rows, exact)
            chosen = jnp.where(sel, 1.0, chosen)
            work = jnp.where(sel, -jnp.inf, work)
        z = jnp.sum(chosen * wgt, axis=0, keepdims=True)
        cnt = jnp.zeros(s0.shape, F32)
        for ra in range(k):
            mine = [i for i, c in enumerate(cells) if c is not None and c[0] == ra]
            n_ra = jnp.sum(chosen[mine[0]:mine[-1] + 1], axis=0, keepdims=True)
            cnt = jnp.where(rank0 == np.float32(ra), n_ra, cnt)
        lanes = slice(g * LANES, (g + 1) * LANES)
        rank1_ref[:, lanes] = rank1.astype(rank1_ref.dtype)
        e1_ref[:, lanes] = (jnp.exp(s1 - b[0]) / z).astype(e1_ref.dtype)
        cnt_ref[:, lanes] = cnt
        e0_ref[:, lanes] = jnp.exp(s0 - a[0])
        taken2 = jnp.sum(chosen, axis=0, keepdims=True)
        return (jnp.abs(taken0 - k) + jnp.abs(taken1 - k) + jnp.abs(taken2 - k))

    groups = range(tl // LANES)
    tie = sum(route_group(g, exact=False) for g in groups)

    @pl.when(jnp.max(tie) > 0.0)
    def _():
        for g in groups:
            route_group(g, exact=True)


def _peer_route(qp, subkeys, u, v, tl=512):
    t = qp.shape[0]
    hp, _, nk, half = subkeys.shape
    n_exp, d = u.shape
    steps = (t // tl) * hp
    rows = n_exp // steps
    assert rows * steps == n_exp
    out = lambda dt: jax.ShapeDtypeStruct((hp, nk, t), dt)
    ospec = pl.BlockSpec((None, nk, tl), lambda i, h: (h, 0, i))
    wspec = pl.BlockSpec((rows, d), lambda i, h: (i * hp + h, 0))
    wout = jax.ShapeDtypeStruct((n_exp, d), BF16)
    return pl.pallas_call(
        _route_kernel,
        grid=(t // tl, hp),
        in_specs=[pl.BlockSpec((tl, 2 * half), lambda i, h: (i, h)),
                  pl.BlockSpec((None, 2, nk, half), lambda i, h: (h, 0, 0, 0)),
                  wspec, wspec],
        out_specs=[ospec] * 4 + [wspec] * 2,
        out_shape=[out(BF16), out(BF16), out(F32), out(F32), wout, wout],
        compiler_params=_params(2),
        name="peer_route",
    )(qp, subkeys, u, v)


def _peer_kernel(x_ref, rstd_ref, u_ref, v_ref, rank1_ref, e1_ref, cnt_ref, e0_ref, o_ref, acc_ref,
                 xt_ref, *stage_refs, sub):
    e = pl.program_id(1)
    n_heads, nk, tm = rank1_ref.shape
    te, d = u_ref.shape
    n_sub = te // sub
    rows_per_sub = sub // nk
    n_pieces = n_heads
    kc = d // n_pieces
    rows_per_step = te // nk
    row0 = (e % (8 // rows_per_step)) * rows_per_step
    act_refs, wt_refs = stage_refs[:n_sub], stage_refs[n_sub:]

    @pl.when(e == 0)
    def _():
        acc_ref[...] = jnp.zeros_like(acc_ref)
        xt_ref[...] = x_ref[...].T

    def up_piece(j, p):
        part = jnp.dot(u_ref[j * sub:(j + 1) * sub, p * kc:(p + 1) * kc],
                       xt_ref[p * kc:(p + 1) * kc, :], preferred_element_type=F32)
        if p == 0:
            act_refs[j][...] = part
        else:
            act_refs[j][...] += part

    def gate_piece(j, p):
        ii, g = divmod(p, n_pieces // rows_per_sub)
        tg = tm // (n_pieces // rows_per_sub)
        lanes = slice(g * tg, (g + 1) * tg)
        row = j * rows_per_sub + ii
        gate = None
        for h in range(n_heads):
            cnt = cnt_ref[h, pl.ds(row0 + row, 1), :][:, lanes].astype(BF16)
            e0 = e0_ref[h, pl.ds(row0 + row, 1), :][:, lanes].astype(BF16)
            zero = jnp.zeros((), BF16)
            term = e0 * jnp.where(rank1_ref[h, :, lanes] < cnt, e1_ref[h, :, lanes], zero)
            gate = term if gate is None else gate + term
        act = act_refs[j][ii * nk:(ii + 1) * nk, lanes] * rstd_ref[:, lanes]
        act = 0.5 * act * (1.0 + lax.erf(act * np.float32(1.0 / np.sqrt(2.0))))
        wt_refs[j][lanes, ii * nk:(ii + 1) * nk] = (gate * act.astype(BF16)).T

    def down_piece(j, p):
        cols = slice(p * kc, (p + 1) * kc)
        acc_ref[:, cols] += jnp.dot(wt_refs[j][...], v_ref[j * sub:(j + 1) * sub, cols],
                                    preferred_element_type=F32)

    for step in range(n_sub + 2):
        for p in range(n_pieces):
            if step < n_sub:
                up_piece(step, p)
            if 0 <= step - 1 < n_sub:
                gate_piece(step - 1, p)
            if 0 <= step - 2 < n_sub:
                down_piece(step - 2, p)

    @pl.when(e == pl.num_programs(1) - 1)
    def _():
        o_ref[...] = acc_ref[...].astype(o_ref.dtype)


def _peer_experts(x, rstd, u, v, rank1, e1, cnt, e0, tm=512, te=512, sub=256):
    t, d = x.shape
    n_exp = u.shape[0]
    hp, nk, _ = rank1.shape
    rows = te // nk
    tab = pl.BlockSpec((hp, nk, tm), lambda i, e: (0, 0, i))
    row_tab = pl.BlockSpec((hp, 8, tm), lambda i, e: (0, (e * rows) // 8, i))
    return pl.pallas_call(
        functools.partial(_peer_kernel, sub=sub),
        grid=(t // tm, n_exp // te),
        in_specs=[pl.BlockSpec((tm, d), lambda i, e: (i, 0)),
                  pl.BlockSpec((1, tm), lambda i, e: (0, i)),
                  pl.BlockSpec((te, d), lambda i, e: (e, 0)),
                  pl.BlockSpec((te, d), lambda i, e: (e, 0)),
                  tab, tab, row_tab, row_tab],
        out_specs=pl.BlockSpec((tm, d), lambda i, e: (i, 0)),
        out_shape=jax.ShapeDtypeStruct((t, d), BF16),
        scratch_shapes=([pltpu.VMEM((tm, d), F32), pltpu.VMEM((d, tm), BF16)]
                        + [pltpu.VMEM((sub, tm), F32)] * (te // sub)
                        + [pltpu.VMEM((tm, sub), BF16)] * (te // sub)),
        compiler_params=_params(2),
        name="peer_experts",
    )(x, rstd, u, v, rank1, e1, cnt, e0)


def kernel(x, mem, norm1_w, w_in, attn_rel_bias, ret_gn_w, w_out, norm2_w, mem_norm_w, xattn_wq,
           xattn_wkv, xattn_wo, norm3_w, peer_wq, peer_subkeys, peer_u, peer_v, final_norm_w):
    b, s, d = x.shape
    t = b * s
    depth = w_in.shape[0]
    n_attn_heads = attn_rel_bias.shape[1]
    attn_width = n_attn_heads * ATTN_HEAD_DIM
    n_ret_heads = ret_gn_w.shape[1] // RET_V_DIM
    h = x.reshape(t, d)
    xn = _rmsnorm(h, norm1_w[0], BF16)
    for l in range(depth):
        proj = _matmul(xn, w_in[l], BF16, tn=768, name="in_proj").reshape(b, s, -1)
        a_out = _chunk_attention(proj, attn_rel_bias[l], n_attn_heads)
        r_out = _retention(proj, ret_gn_w[l], n_ret_heads, 3 * attn_width)
        h, hg, rstd = _matmul((a_out.reshape(t, -1), r_out.reshape(t, -1)), w_out[l], F32,
                              residual=h, norm_gain=norm2_w[l], name="out_proj")

        mem_n = _rmsnorm(mem.reshape(-1, d), mem_norm_w[l], BF16)
        kv = _matmul(mem_n, xattn_wkv[l], BF16, name="mem_kv").reshape(b, -1, 2 * d)
        xq = _matmul(hg, xattn_wq[l], BF16, row_scale=rstd, name="xattn_q")
        xo = _cross_attention(xq.reshape(b, s, d), kv, XATTN_HEADS).reshape(t, d)
        h, hg, rstd = _matmul(xo, xattn_wo[l], F32, residual=h, norm_gain=norm3_w[l],
                              name="xattn_o")

        qp = _matmul(hg, peer_wq[l], BF16, row_scale=rstd, name="peer_q")
        rank1, e1, cnt, e0, u_b, v_b = _peer_route(qp, peer_subkeys[l], peer_u[l], peer_v[l])
        delta = _peer_experts(hg, rstd.reshape(1, t), u_b, v_b, rank1, e1, cnt, e0)
        if l + 1 < depth:
            h = h + delta.astype(F32)
            xn = _rmsnorm(h, norm1_w[l + 1], BF16)
    return _add_rmsnorm(h, delta, final_norm_w, F32).reshape(b, s, d)
```

```python
import functools

import numpy as np
import jax
import jax.numpy as jnp
from jax import lax
from jax.experimental import pallas as pl
from jax.experimental.pallas import tpu as pltpu

F32 = jnp.float32
BF16 = jnp.bfloat16

CHUNK = 64
LEFT_CHUNKS = 8
LEFT = LEFT_CHUNKS * CHUNK
ATTN_HEAD_DIM = 128
MAX_REL_DIST = 256
RET_V_DIM = 256
RET_QK_DIM = 128
ROPE_BASE = 10000.0
XATTN_HEADS = 4
PEER_HEADS = 8
PEER_KEYS = 128
PEER_HALF = 128
PEER_TOPK = 16
EPS = 1e-6
NEG_INF = -1e30
NOT_RANKED = 1e9

VMEM_LIMIT_BYTES = 56 * 1024 * 1024
LANES = 128

_NT = (((1,), (1,)), ((), ()))
_TN = (((0,), (0,)), ((), ()))


def _params(n_grid_dims, flags=None):
    return pltpu.CompilerParams(
        dimension_semantics=("arbitrary",) * n_grid_dims,
        vmem_limit_bytes=VMEM_LIMIT_BYTES,
        flags=flags)


def _rmsnorm_kernel(x_ref, w_ref, o_ref):
    x = x_ref[...]
    ms = jnp.mean(x * x, axis=-1, keepdims=True)
    o_ref[...] = (x * lax.rsqrt(ms + EPS) * w_ref[...]).astype(o_ref.dtype)


def _rmsnorm(x, w, out_dtype, rows=256):
    m, d = x.shape
    return pl.pallas_call(
        _rmsnorm_kernel,
        grid=(m // rows,),
        in_specs=[pl.BlockSpec((rows, d), lambda i: (i, 0)),
                  pl.BlockSpec((1, d), lambda i: (0, 0))],
        out_specs=pl.BlockSpec((rows, d), lambda i: (i, 0)),
        out_shape=jax.ShapeDtypeStruct((m, d), out_dtype),
        compiler_params=_params(1),
        name="rmsnorm",
    )(x, w.reshape(1, d))


def _add_rmsnorm_kernel(x_ref, y_ref, w_ref, o_ref):
    x = x_ref[...] + y_ref[...].astype(F32)
    ms = jnp.mean(x * x, axis=-1, keepdims=True)
    o_ref[...] = (x * lax.rsqrt(ms + EPS) * w_ref[...]).astype(o_ref.dtype)


def _add_rmsnorm(x, y, w, out_dtype, rows=256):
    m, d = x.shape
    return pl.pallas_call(
        _add_rmsnorm_kernel,
        grid=(m // rows,),
        in_specs=[pl.BlockSpec((rows, d), lambda i: (i, 0)),
                  pl.BlockSpec((rows, d), lambda i: (i, 0)),
                  pl.BlockSpec((1, d), lambda i: (0, 0))],
        out_specs=pl.BlockSpec((rows, d), lambda i: (i, 0)),
        out_shape=jax.ShapeDtypeStruct((m, d), out_dtype),
        compiler_params=_params(1),
        name="add_rmsnorm",
    )(x, y, w.reshape(1, d))


def _mm_kernel(*refs, n_lhs, has_residual, has_row_scale, emit_norm, n_cols):
    lhs_refs, w_ref = refs[:n_lhs], refs[n_lhs]
    pos = n_lhs + 1
    acc, k0 = None, 0
    for a_ref in lhs_refs:
        k1 = k0 + a_ref.shape[1]
        part = jnp.dot(a_ref[...], w_ref[k0:k1, :].astype(BF16), preferred_element_type=F32)
        acc = part if acc is None else acc + part
        k0 = k1
    if has_row_scale:
        acc = acc * refs[pos][...]
        pos += 1
    if has_residual:
        acc = refs[pos][...] + acc
        pos += 1
    if not emit_norm:
        o_ref = refs[pos]
        o_ref[...] = acc.astype(o_ref.dtype)
        return
    gain_ref, o_ref, scaled_ref, rstd_ref, ssq_ref = refs[pos:pos + 5]
    j = pl.program_id(1)
    o_ref[...] = acc.astype(o_ref.dtype)
    scaled_ref[...] = (acc * gain_ref[...]).astype(scaled_ref.dtype)
    row_ssq = jnp.sum(acc * acc, axis=-1, keepdims=True)

    @pl.when(j == 0)
    def _():
        ssq_ref[...] = row_ssq

    @pl.when(j > 0)
    def _():
        ssq_ref[...] += row_ssq

    @pl.when(j == pl.num_programs(1) - 1)
    def _():
        rstd_ref[...] = lax.rsqrt(ssq_ref[...] / n_cols + EPS)


def _matmul(lhs, w, out_dtype, residual=None, row_scale=None, norm_gain=None, tm=1024, tn=512,
            name="matmul"):
    lhs = lhs if isinstance(lhs, (tuple, list)) else (lhs,)
    m = lhs[0].shape[0]
    k, n = w.shape
    assert sum(a.shape[1] for a in lhs) == k
    tm, tn = min(tm, m), min(tn, n)
    in_specs = [pl.BlockSpec((tm, a.shape[1]), lambda i, j: (i, 0)) for a in lhs]
    in_specs.append(pl.BlockSpec((k, tn), lambda i, j: (0, j)))
    args = [*lhs, w]
    tile = pl.BlockSpec((tm, tn), lambda i, j: (i, j))
    per_row = pl.BlockSpec((tm, 1), lambda i, j: (i, 0))
    if row_scale is not None:
        in_specs.append(per_row)
        args.append(row_scale)
    if residual is not None:
        in_specs.append(tile)
        args.append(residual)
    out_specs, out_shape, scratch = tile, jax.ShapeDtypeStruct((m, n), out_dtype), []
    if norm_gain is not None:
        in_specs.append(pl.BlockSpec((1, tn), lambda i, j: (0, j)))
        args.append(norm_gain.reshape(1, n))
        out_specs = [tile, tile, per_row]
        out_shape = [out_shape, jax.ShapeDtypeStruct((m, n), BF16),
                     jax.ShapeDtypeStruct((m, 1), F32)]
        scratch = [pltpu.VMEM((tm, 1), F32)]
    return pl.pallas_call(
        functools.partial(_mm_kernel, n_lhs=len(lhs), has_residual=residual is not None,
                          has_row_scale=row_scale is not None, emit_norm=norm_gain is not None,
                          n_cols=n),
        grid=(m // tm, n // tn),
        in_specs=in_specs,
        out_specs=out_specs,
        out_shape=out_shape,
        scratch_shapes=scratch,
        compiler_params=_params(2),
        name=name,
    )(*args)


def _attn_kernel(q_ref, k_ref, v_ref, base_ref, o_ref, bias_ref, *, qb, scale):
    seq = q_ref.shape[0]
    width = LEFT + qb

    @pl.when(pl.program_id(1) == 0)
    def _():
        toeplitz = pltpu.roll(jnp.broadcast_to(base_ref[...], (qb, base_ref.shape[-1])),
                              0, 1, stride=1, stride_axis=0)
        q_chunk = lax.broadcasted_iota(jnp.int32, (qb, width), 0) // CHUNK
        c_chunk = lax.broadcasted_iota(jnp.int32, (qb, width), 1) // CHUNK
        in_band = (c_chunk >= q_chunk) & (c_chunk <= q_chunk + LEFT_CHUNKS)
        bias_ref[...] = jnp.where(in_band, toeplitz[:, :width], NEG_INF)

    for i in range(seq // qb):
        q0 = i * qb
        k0 = max(0, q0 - LEFT)
        kw = q0 + qb - k0
        c0 = k0 - (q0 - LEFT)
        s = lax.dot_general(q_ref[q0:q0 + qb, :], k_ref[k0:k0 + kw, :], _NT,
                            preferred_element_type=F32)
        s = s * scale + bias_ref[:, c0:c0 + kw]
        m = jnp.max(s, axis=-1, keepdims=True)
        p = jnp.exp(s - m)
        l = jnp.sum(p, axis=-1, keepdims=True)
        o = jnp.dot(p.astype(BF16), v_ref[k0:k0 + kw, :], preferred_element_type=F32)
        o_ref[q0:q0 + qb, :] = (o / l).astype(o_ref.dtype)


def _attn_bias_base(rel_bias, qb):
    w = pl.next_power_of_2(LEFT + 2 * qb)
    j = jnp.arange(w)
    j = jnp.where(j < LEFT + qb, j, j - w)
    idx = jnp.clip(LEFT - j, -MAX_REL_DIST, MAX_REL_DIST) + MAX_REL_DIST
    return rel_bias[:, None, idx].astype(F32)


def _chunk_attention(proj, rel_bias, n_heads, qb=256):
    b, s, _ = proj.shape
    dh = ATTN_HEAD_DIM
    base = _attn_bias_base(rel_bias, qb)
    kern = functools.partial(_attn_kernel, qb=qb, scale=dh ** -0.5)
    return pl.pallas_call(
        kern,
        grid=(n_heads, b),
        in_specs=[pl.BlockSpec((None, s, dh), lambda h, bi: (bi, 0, h)),
                  pl.BlockSpec((None, s, dh), lambda h, bi: (bi, 0, n_heads + h)),
                  pl.BlockSpec((None, s, dh), lambda h, bi: (bi, 0, 2 * n_heads + h)),
                  pl.BlockSpec((None, 1, base.shape[-1]), lambda h, bi: (h, 0, 0))],
        out_specs=pl.BlockSpec((None, s, dh), lambda h, bi: (bi, 0, h)),
        out_shape=jax.ShapeDtypeStruct((b, s, n_heads * dh), BF16),
        scratch_shapes=[pltpu.VMEM((qb, LEFT + qb), F32)],
        compiler_params=_params(2),
        name="chunk_attention",
    )(proj, proj, proj, base)


def _ret_kernel(q_ref, k_ref, v_ref, g_ref, cos_ref, sin_ref, dec_ref, qd_ref, kd_ref, bd_ref,
                gnw_ref, o_ref, state_ref, *, rb, scale):
    seq = q_ref.shape[0]
    state_ref[...] = jnp.zeros_like(state_ref)
    half = q_ref.shape[1] // 2

    def body(n, carry):
        r0 = pl.multiple_of(n * rb, rb)
        rows = pl.ds(r0, rb)
        cos = cos_ref[rows, :]
        sin = sin_ref[rows, :]
        q = q_ref[rows, :].astype(F32)
        k = k_ref[rows, :].astype(F32)
        q = q * cos + pltpu.roll(q, half, 1) * sin
        k = (k * cos + pltpu.roll(k, half, 1) * sin) * scale
        v = v_ref[rows, :]
        a = lax.dot_general(q.astype(BF16), k.astype(BF16), _NT, preferred_element_type=F32)
        a = a * dec_ref[...]
        st = state_ref[...]
        y = jnp.dot(a.astype(BF16), v, preferred_element_type=F32)
        y = y + jnp.dot((q * qd_ref[...]).astype(BF16), st.astype(BF16),
                        preferred_element_type=F32)
        kd = (k * kd_ref[...]).astype(BF16)
        kv = lax.dot_general(kd, v, _TN, preferred_element_type=F32)
        state_ref[...] = st * bd_ref[...] + kv
        mu = jnp.mean(y, axis=-1, keepdims=True)
        yc = y - mu
        var = jnp.mean(yc * yc, axis=-1, keepdims=True)
        yn = yc * lax.rsqrt(var + EPS) * gnw_ref[...]
        g = g_ref[rows, :].astype(F32)
        o_ref[rows, :] = (g * (1.0 / (1.0 + jnp.exp(-g))) * yn).astype(o_ref.dtype)
        return carry

    lax.fori_loop(0, seq // rb, body, 0)


def _retention(proj, gn_w, n_heads, col0, rb=256):
    b, s, _ = proj.shape
    dk, dv = RET_QK_DIM, RET_V_DIM
    qk0 = col0 // dk
    v0 = (col0 + 2 * n_heads * dk) // dv
    inv_freq = 1.0 / (ROPE_BASE ** (jnp.arange(0, dk, 2, dtype=F32) / dk))
    ang = jnp.arange(s, dtype=F32)[:, None] * inv_freq[None, :]
    cos = jnp.concatenate([jnp.cos(ang), jnp.cos(ang)], axis=-1)
    sin = jnp.concatenate([-jnp.sin(ang), jnp.sin(ang)], axis=-1)
    log_gamma = jnp.log1p(-jnp.power(2.0, -5.0 - jnp.arange(n_heads, dtype=F32)))
    pos = jnp.arange(rb, dtype=F32)
    chunk_of = jnp.arange(rb) // CHUNK
    causal = (chunk_of[None, :] <= chunk_of[:, None]).astype(F32)
    dec = jnp.exp(log_gamma[:, None, None] * jnp.abs(pos[:, None] - pos[None, :])) * causal[None]
    qd = jnp.exp(log_gamma[:, None] * (pos + 1.0))[:, :, None]
    kd = jnp.exp(log_gamma[:, None] * (rb - 1.0 - pos))[:, :, None]
    bd = jnp.exp(log_gamma * rb)[:, None, None]
    kern = functools.partial(_ret_kernel, rb=rb, scale=dk ** -0.5)
    return pl.pallas_call(
        kern,
        grid=(b, n_heads),
        in_specs=[pl.BlockSpec((None, s, dk), lambda bi, h: (bi, 0, qk0 + h)),
                  pl.BlockSpec((None, s, dk), lambda bi, h: (bi, 0, qk0 + n_heads + h)),
                  pl.BlockSpec((None, s, dv), lambda bi, h: (bi, 0, v0 + h)),
                  pl.BlockSpec((None, s, dv), lambda bi, h: (bi, 0, v0 + n_heads + h)),
                  pl.BlockSpec((s, dk), lambda bi, h: (0, 0)),
                  pl.BlockSpec((s, dk), lambda bi, h: (0, 0)),
                  pl.BlockSpec((None, rb, rb), lambda bi, h: (h, 0, 0)),
                  pl.BlockSpec((None, rb, 1), lambda bi, h: (h, 0, 0)),
                  pl.BlockSpec((None, rb, 1), lambda bi, h: (h, 0, 0)),
                  pl.BlockSpec((None, 1, 1), lambda bi, h: (h, 0, 0)),
                  pl.BlockSpec((1, dv), lambda bi, h: (0, h))],
        out_specs=pl.BlockSpec((None, s, dv), lambda bi, h: (bi, 0, h)),
        out_shape=jax.ShapeDtypeStruct((b, s, n_heads * dv), BF16),
        scratch_shapes=[pltpu.VMEM((dk, dv), F32)],
        compiler_params=_params(2),
        name="retention",
    )(proj, proj, proj, proj, cos, sin, dec, qd, kd, bd, gn_w.reshape(1, -1))


def _xattn_kernel(q_ref, kv_ref, o_ref, *, n_heads, scale):
    d = q_ref.shape[1]
    dh = d // n_heads
    for h in range(n_heads):
        cols = slice(h * dh, (h + 1) * dh)
        s = lax.dot_general(q_ref[:, cols], kv_ref[:, cols], _NT, preferred_element_type=F32)
        s = s * scale
        m = jnp.max(s, axis=-1, keepdims=True)
        p = jnp.exp(s - m)
        l = jnp.sum(p, axis=-1, keepdims=True)
        o = jnp.dot(p.astype(BF16), kv_ref[:, d + h * dh:d + (h + 1) * dh],
                    preferred_element_type=F32)
        o_ref[:, cols] = (o / l).astype(o_ref.dtype)


def _cross_attention(q, kv, n_heads, tq=512):
    b, s, d = q.shape
    m = kv.shape[1]
    kern = functools.partial(_xattn_kernel, n_heads=n_heads, scale=(d // n_heads) ** -0.5)
    return pl.pallas_call(
        kern,
        grid=(b, s // tq),
        in_specs=[pl.BlockSpec((None, tq, d), lambda bi, i: (bi, i, 0)),
                  pl.BlockSpec((None, m, 2 * d), lambda bi, i: (bi, 0, 0))],
        out_specs=pl.BlockSpec((None, tq, d), lambda bi, i: (bi, i, 0)),
        out_shape=jax.ShapeDtypeStruct((b, s, d), BF16),
        compiler_params=_params(2),
        name="cross_attention",
    )(q, kv)


def _take_max(work, rows, exact):
    m = jnp.max(work, axis=0, keepdims=True)
    hit = work == m
    if not exact:
        return m, hit
    first = jnp.min(jnp.where(hit, rows, np.float32(work.shape[0])), axis=0, keepdims=True)
    return m, rows == first


def _top_rows(s, k, exact):
    rows = lax.broadcasted_iota(jnp.int32, s.shape, 0).astype(F32)
    work = s
    rank = jnp.full(s.shape, NOT_RANKED, F32)
    vals = []
    for r in range(k):
        m, sel = _take_max(work, rows, exact)
        rank = jnp.where(sel, np.float32(r), rank)
        work = jnp.where(sel, -jnp.inf, work)
        vals.append(m)
    taken = jnp.sum(jnp.where(rank < NOT_RANKED, 1.0, 0.0), axis=0, keepdims=True)
    return vals, rank, taken


def _candidate_cells(k):
    cells = []
    for ra in range(k):
        cells += [(ra, rb) for rb in range(k // (ra + 1))]
    single = [c for c in cells if k // (c[0] + 1) == 1]
    multi = [c for c in cells if c not in single]
    pad = (-len(multi)) % 8
    return multi + [None] * pad + single


def _route_kernel(q_ref, keys_ref, u_ref, v_ref, rank1_ref, e1_ref, cnt_ref, e0_ref, ub_ref, vb_ref):
    ub_ref[...] = u_ref[...].astype(ub_ref.dtype)
    vb_ref[...] = v_ref[...].astype(vb_ref.dtype)
    k = PEER_TOPK
    tl = q_ref.shape[0]
    cells = _candidate_cells(k)

    def route_group(g, exact):
        q = q_ref[g * LANES:(g + 1) * LANES, :]
        s0 = lax.dot_general(keys_ref[0].astype(BF16), q[:, :PEER_HALF], _NT,
                             preferred_element_type=F32)
        s1 = lax.dot_general(keys_ref[1].astype(BF16), q[:, PEER_HALF:], _NT,
                             preferred_element_type=F32)
        a, rank0, taken0 = _top_rows(s0, k, exact)
        b, rank1, taken1 = _top_rows(s1, k, exact)
        a_all, b_all = jnp.concatenate(a, axis=0), jnp.concatenate(b, axis=0)
        ea_all, eb_all = jnp.exp(a_all - a[0]), jnp.exp(b_all - b[0])
        neg = jnp.full_like(a[0], -jnp.inf)
        zero = jnp.zeros_like(a[0])
        n0, n1 = k, k // 2
        mid = cells[n0 + n1:len(cells) - k // 2]
        cand = jnp.concatenate(
            [a[0] + b_all, a[1] + b_all[:n1]]
            + [neg if c is None else a_all[c[0]:c[0] + 1] + b_all[c[1]:c[1] + 1] for c in mid]
            + [a_all[k // 2:] + b[0]], axis=0)
        wgt = jnp.concatenate(
            [ea_all[0:1] * eb_all, ea_all[1:2] * eb_all[:n1]]
            + [zero if c is None else ea_all[c[0]:c[0] + 1] * eb_all[c[1]:c[1] + 1] for c in mid]
            + [ea_all[k // 2:] * eb_all[0:1]], axis=0)
        rows = lax.broadcasted_iota(jnp.int32, cand.shape, 0).astype(F32)
        work = cand
        chosen = jnp.zeros(cand.shape, F32)
        for _ in range(k):
            _, sel = _take_max(work, rows, exact)
            chosen = jnp.where(sel, 1.0, chosen)
            work = jnp.where(sel, -jnp.inf, work)
        z = jnp.sum(chosen * wgt, axis=0, keepdims=True)
        cnt = jnp.zeros(s0.shape, F32)
        for ra in range(k):
            mine = [i for i, c in enumerate(cells) if c is not None and c[0] == ra]
            n_ra = jnp.sum(chosen[mine[0]:mine[-1] + 1], axis=0, keepdims=True)
            cnt = jnp.where(rank0 == np.float32(ra), n_ra, cnt)
        lanes = slice(g * LANES, (g + 1) * LANES)
        rank1_ref[:, lanes] = rank1.astype(rank1_ref.dtype)
        e1_ref[:, lanes] = (jnp.exp(s1 - b[0]) / z).astype(e1_ref.dtype)
        cnt_ref[:, lanes] = cnt
        e0_ref[:, lanes] = jnp.exp(s0 - a[0])
        taken2 = jnp.sum(chosen, axis=0, keepdims=True)
        return (jnp.abs(taken0 - k) + jnp.abs(taken1 - k) + jnp.abs(taken2 - k))

    groups = range(tl // LANES)
    tie = sum(route_group(g, exact=False) for g in groups)

    @pl.when(jnp.max(tie) > 0.0)
    def _():
        for g in groups:
            route_group(g, exact=True)


def _peer_route(qp, subkeys, u, v, tl=512):
    t = qp.shape[0]
    hp, _, nk, half = subkeys.shape
    n_exp, d = u.shape
    steps = (t // tl) * hp
    rows = n_exp // steps
    assert rows * steps == n_exp
    out = lambda dt: jax.ShapeDtypeStruct((hp, nk, t), dt)
    ospec = pl.BlockSpec((None, nk, tl), lambda i, h: (h, 0, i))
    wspec = pl.BlockSpec((rows, d), lambda i, h: (i * hp + h, 0))
    wout = jax.ShapeDtypeStruct((n_exp, d), BF16)
    return pl.pallas_call(
        _route_kernel,
        grid=(t // tl, hp),
        in_specs=[pl.BlockSpec((tl, 2 * half), lambda i, h: (i, h)),
                  pl.BlockSpec((None, 2, nk, half), lambda i, h: (h, 0, 0, 0)),
                  wspec, wspec],
        out_specs=[ospec] * 4 + [wspec] * 2,
        out_shape=[out(BF16), out(BF16), out(F32), out(F32), wout, wout],
        compiler_params=_params(2),
        name="peer_route",
    )(qp, subkeys, u, v)


def _peer_kernel(x_ref, rstd_ref, u_ref, v_ref, rank1_ref, e1_ref, cnt_ref, e0_ref, o_ref, acc_ref,
                 xt_ref, *stage_refs, sub, n_pieces):
    e = pl.program_id(1)
    n_heads, nk, tm = rank1_ref.shape
    te, d = u_ref.shape
    n_sub = te // sub
    rows_per_sub = sub // nk
    kc = d // n_pieces
    rows_per_step = te // nk
    row0 = (e % (8 // rows_per_step)) * rows_per_step
    act_refs, wt_refs = stage_refs[:n_sub], stage_refs[n_sub:]

    @pl.when(e == 0)
    def _():
        acc_ref[...] = jnp.zeros_like(acc_ref)
        xt_ref[...] = x_ref[...].T

    def up_piece(j, p):
        part = jnp.dot(u_ref[j * sub:(j + 1) * sub, p * kc:(p + 1) * kc],
                       xt_ref[p * kc:(p + 1) * kc, :], preferred_element_type=F32)
        if p == 0:
            act_refs[j][...] = part
        else:
            act_refs[j][...] += part

    def gate_piece(j, p):
        ii, g = divmod(p, n_pieces // rows_per_sub)
        tg = tm // (n_pieces // rows_per_sub)
        lanes = slice(g * tg, (g + 1) * tg)
        row = j * rows_per_sub + ii
        gate = None
        for h in range(n_heads):
            cnt = cnt_ref[h, pl.ds(row0 + row, 1), :][:, lanes].astype(BF16)
            e0 = e0_ref[h, pl.ds(row0 + row, 1), :][:, lanes].astype(BF16)
            zero = jnp.zeros((), BF16)
            term = e0 * jnp.where(rank1_ref[h, :, lanes] < cnt, e1_ref[h, :, lanes], zero)
            gate = term if gate is None else gate + term
        act = act_refs[j][ii * nk:(ii + 1) * nk, lanes] * rstd_ref[:, lanes]
        act = 0.5 * act * (1.0 + lax.erf(act * np.float32(1.0 / np.sqrt(2.0))))
        wt_refs[j][lanes, ii * nk:(ii + 1) * nk] = (gate * act.astype(BF16)).T

    def down_piece(j, p):
        cols = slice(p * kc, (p + 1) * kc)
        acc_ref[:, cols] += jnp.dot(wt_refs[j][...], v_ref[j * sub:(j + 1) * sub, cols],
                                    preferred_element_type=F32)

    for step in range(n_sub + 2):
        for p in range(n_pieces):
            if step < n_sub:
                up_piece(step, p)
            if 0 <= step - 1 < n_sub:
                gate_piece(step - 1, p)
            if 0 <= step - 2 < n_sub:
                down_piece(step - 2, p)

    @pl.when(e == pl.num_programs(1) - 1)
    def _():
        o_ref[...] = acc_ref[...].astype(o_ref.dtype)


def _peer_experts(x, rstd, u, v, rank1, e1, cnt, e0, tm=512, te=512, sub=256, n_pieces=4):
    t, d = x.shape
    n_exp = u.shape[0]
    hp, nk, _ = rank1.shape
    rows = te // nk
    tab = pl.BlockSpec((hp, nk, tm), lambda i, e: (0, 0, i))
    row_tab = pl.BlockSpec((hp, 8, tm), lambda i, e: (0, (e * rows) // 8, i))
    return pl.pallas_call(
        functools.partial(_peer_kernel, sub=sub, n_pieces=n_pieces),
        grid=(t // tm, n_exp // te),
        in_specs=[pl.BlockSpec((tm, d), lambda i, e: (i, 0)),
                  pl.BlockSpec((1, tm), lambda i, e: (0, i)),
                  pl.BlockSpec((te, d), lambda i, e: (e, 0)),
                  pl.BlockSpec((te, d), lambda i, e: (e, 0)),
                  tab, tab, row_tab, row_tab],
        out_specs=pl.BlockSpec((tm, d), lambda i, e: (i, 0)),
        out_shape=jax.ShapeDtypeStruct((t, d), BF16),
        scratch_shapes=([pltpu.VMEM((tm, d), F32), pltpu.VMEM((d, tm), BF16)]
                        + [pltpu.VMEM((sub, tm), F32)] * (te // sub)
                        + [pltpu.VMEM((tm, sub), BF16)] * (te // sub)),
        compiler_params=_params(2),
        name="peer_experts",
    )(x, rstd, u, v, rank1, e1, cnt, e0)


def kernel(x, mem, norm1_w, w_in, attn_rel_bias, ret_gn_w, w_out, norm2_w, mem_norm_w, xattn_wq,
           xattn_wkv, xattn_wo, norm3_w, peer_wq, peer_subkeys, peer_u, peer_v, final_norm_w):
    b, s, d = x.shape
    t = b * s
    depth = w_in.shape[0]
    n_attn_heads = attn_rel_bias.shape[1]
    attn_width = n_attn_heads * ATTN_HEAD_DIM
    n_ret_heads = ret_gn_w.shape[1] // RET_V_DIM
    h = x.reshape(t, d)
    xn = _rmsnorm(h, norm1_w[0], BF16)
    for l in range(depth):
        proj = _matmul(xn, w_in[l], BF16, tn=768, name="in_proj").reshape(b, s, -1)
        a_out = _chunk_attention(proj, attn_rel_bias[l], n_attn_heads)
        r_out = _retention(proj, ret_gn_w[l], n_ret_heads, 3 * attn_width)
        h, hg, rstd = _matmul((a_out.reshape(t, -1), r_out.reshape(t, -1)), w_out[l], F32,
                              residual=h, norm_gain=norm2_w[l], name="out_proj")

        mem_n = _rmsnorm(mem.reshape(-1, d), mem_norm_w[l], BF16)
        kv = _matmul(mem_n, xattn_wkv[l], BF16, name="mem_kv").reshape(b, -1, 2 * d)
        xq = _matmul(hg, xattn_wq[l], BF16, row_scale=rstd, name="xattn_q")
        xo = _cross_attention(xq.reshape(b, s, d), kv, XATTN_HEADS).reshape(t, d)
        h, hg, rstd = _matmul(xo, xattn_wo[l], F32, residual=h, norm_gain=norm3_w[l],
                              name="xattn_o")

        qp = _matmul(hg, peer_wq[l], BF16, row_scale=rstd, name="peer_q")
        rank1, e1, cnt, e0, u_b, v_b = _peer_route(qp, peer_subkeys[l], peer_u[l], peer_v[l])
        delta = _peer_experts(hg, rstd.reshape(1, t), u_b, v_b, rank1, e1, cnt, e0)
        if l + 1 < depth:
            h = h + delta.astype(F32)
            xn = _rmsnorm(h, norm1_w[l + 1], BF16)
    return _add_rmsnorm(h, delta, final_norm_w, F32).reshape(b, s, d)
```

```python
import functools

import numpy as np
import jax
import jax.numpy as jnp
from jax import lax
from jax.experimental import pallas as pl
from jax.experimental.pallas import tpu as pltpu

F32 = jnp.float32
BF16 = jnp.bfloat16

CHUNK = 64
LEFT_CHUNKS = 8
LEFT = LEFT_CHUNKS * CHUNK
ATTN_HEAD_DIM = 128
MAX_REL_DIST = 256
RET_V_DIM = 256
RET_QK_DIM = 128
ROPE_BASE = 10000.0
XATTN_HEADS = 4
PEER_HEADS = 8
PEER_KEYS = 128
PEER_HALF = 128
PEER_TOPK = 16
EPS = 1e-6
NEG_INF = -1e30
NOT_RANKED = 1e9

VMEM_LIMIT_BYTES = 56 * 1024 * 1024
LANES = 128

_NT = (((1,), (1,)), ((), ()))
_TN = (((0,), (0,)), ((), ()))


def _params(n_grid_dims, flags=None):
    return pltpu.CompilerParams(
        dimension_semantics=("arbitrary",) * n_grid_dims,
        vmem_limit_bytes=VMEM_LIMIT_BYTES,
        flags=flags)


def _rmsnorm_kernel(x_ref, w_ref, o_ref):
    x = x_ref[...]
    ms = jnp.mean(x * x, axis=-1, keepdims=True)
    o_ref[...] = (x * lax.rsqrt(ms + EPS) * w_ref[...]).astype(o_ref.dtype)


def _rmsnorm(x, w, out_dtype, rows=256):
    m, d = x.shape
    return pl.pallas_call(
        _rmsnorm_kernel,
        grid=(m // rows,),
        in_specs=[pl.BlockSpec((rows, d), lambda i: (i, 0)),
                  pl.BlockSpec((1, d), lambda i: (0, 0))],
        out_specs=pl.BlockSpec((rows, d), lambda i: (i, 0)),
        out_shape=jax.ShapeDtypeStruct((m, d), out_dtype),
        compiler_params=_params(1),
        name="rmsnorm",
    )(x, w.reshape(1, d))


def _add_rmsnorm_kernel(x_ref, y_ref, w_ref, o_ref):
    x = x_ref[...] + y_ref[...].astype(F32)
    ms = jnp.mean(x * x, axis=-1, keepdims=True)
    o_ref[...] = (x * lax.rsqrt(ms + EPS) * w_ref[...]).astype(o_ref.dtype)


def _add_rmsnorm(x, y, w, out_dtype, rows=256):
    m, d = x.shape
    return pl.pallas_call(
        _add_rmsnorm_kernel,
        grid=(m // rows,),
        in_specs=[pl.BlockSpec((rows, d), lambda i: (i, 0)),
                  pl.BlockSpec((rows, d), lambda i: (i, 0)),
                  pl.BlockSpec((1, d), lambda i: (0, 0))],
        out_specs=pl.BlockSpec((rows, d), lambda i: (i, 0)),
        out_shape=jax.ShapeDtypeStruct((m, d), out_dtype),
        compiler_params=_params(1),
        name="add_rmsnorm",
    )(x, y, w.reshape(1, d))


def _mm_kernel(*refs, n_lhs, has_residual, has_row_scale, emit_norm, n_cols):
    lhs_refs, w_ref = refs[:n_lhs], refs[n_lhs]
    pos = n_lhs + 1
    acc, k0 = None, 0
    for a_ref in lhs_refs:
        k1 = k0 + a_ref.shape[1]
        part = jnp.dot(a_ref[...], w_ref[k0:k1, :].astype(BF16), preferred_element_type=F32)
        acc = part if acc is None else acc + part
        k0 = k1
    if has_row_scale:
        acc = acc * refs[pos][...]
        pos += 1
    if has_residual:
        acc = refs[pos][...] + acc
        pos += 1
    if not emit_norm:
        o_ref = refs[pos]
        o_ref[...] = acc.astype(o_ref.dtype)
        return
    gain_ref, o_ref, scaled_ref, rstd_ref, ssq_ref = refs[pos:pos + 5]
    j = pl.program_id(1)
    o_ref[...] = acc.astype(o_ref.dtype)
    scaled_ref[...] = (acc * gain_ref[...]).astype(scaled_ref.dtype)
    row_ssq = jnp.sum(acc * acc, axis=-1, keepdims=True)

    @pl.when(j == 0)
    def _():
        ssq_ref[...] = row_ssq

    @pl.when(j > 0)
    def _():
        ssq_ref[...] += row_ssq

    @pl.when(j == pl.num_programs(1) - 1)
    def _():
        rstd_ref[...] = lax.rsqrt(ssq_ref[...] / n_cols + EPS)


def _matmul(lhs, w, out_dtype, residual=None, row_scale=None, norm_gain=None, tm=1024, tn=512,
            name="matmul"):
    lhs = lhs if isinstance(lhs, (tuple, list)) else (lhs,)
    m = lhs[0].shape[0]
    k, n = w.shape
    assert sum(a.shape[1] for a in lhs) == k
    tm, tn = min(tm, m), min(tn, n)
    in_specs = [pl.BlockSpec((tm, a.shape[1]), lambda i, j: (i, 0)) for a in lhs]
    in_specs.append(pl.BlockSpec((k, tn), lambda i, j: (0, j)))
    args = [*lhs, w]
    tile = pl.BlockSpec((tm, tn), lambda i, j: (i, j))
    per_row = pl.BlockSpec((tm, 1), lambda i, j: (i, 0))
    if row_scale is not None:
        in_specs.append(per_row)
        args.append(row_scale)
    if residual is not None:
        in_specs.append(tile)
        args.append(residual)
    out_specs, out_shape, scratch = tile, jax.ShapeDtypeStruct((m, n), out_dtype), []
    if norm_gain is not None:
        in_specs.append(pl.BlockSpec((1, tn), lambda i, j: (0, j)))
        args.append(norm_gain.reshape(1, n))
        out_specs = [tile, tile, per_row]
        out_shape = [out_shape, jax.ShapeDtypeStruct((m, n), BF16),
                     jax.ShapeDtypeStruct((m, 1), F32)]
        scratch = [pltpu.VMEM((tm, 1), F32)]
    return pl.pallas_call(
        functools.partial(_mm_kernel, n_lhs=len(lhs), has_residual=residual is not None,
                          has_row_scale=row_scale is not None, emit_norm=norm_gain is not None,
                          n_cols=n),
        grid=(m // tm, n // tn),
        in_specs=in_specs,
        out_specs=out_specs,
        out_shape=out_shape,
        scratch_shapes=scratch,
        compiler_params=_params(2),
        name=name,
    )(*args)


def _attn_kernel(q_ref, k_ref, v_ref, base_ref, o_ref, bias_ref, *, qb, scale):
    seq = q_ref.shape[0]
    width = LEFT + qb

    @pl.when(pl.program_id(1) == 0)
    def _():
        toeplitz = pltpu.roll(jnp.broadcast_to(base_ref[...], (qb, base_ref.shape[-1])),
                              0, 1, stride=1, stride_axis=0)
        q_chunk = lax.broadcasted_iota(jnp.int32, (qb, width), 0) // CHUNK
        c_chunk = lax.broadcasted_iota(jnp.int32, (qb, width), 1) // CHUNK
        in_band = (c_chunk >= q_chunk) & (c_chunk <= q_chunk + LEFT_CHUNKS)
        bias_ref[...] = jnp.where(in_band, toeplitz[:, :width], NEG_INF)

    for i in range(seq // qb):
        q0 = i * qb
        k0 = max(0, q0 - LEFT)
        kw = q0 + qb - k0
        c0 = k0 - (q0 - LEFT)
        s = lax.dot_general(q_ref[q0:q0 + qb, :], k_ref[k0:k0 + kw, :], _NT,
                            preferred_element_type=F32)
        s = s * scale + bias_ref[:, c0:c0 + kw]
        m = jnp.max(s, axis=-1, keepdims=True)
        p = jnp.exp(s - m)
        l = jnp.sum(p, axis=-1, keepdims=True)
        o = jnp.dot(p.astype(BF16), v_ref[k0:k0 + kw, :], preferred_element_type=F32)
        o_ref[q0:q0 + qb, :] = (o / l).astype(o_ref.dtype)


def _attn_bias_base(rel_bias, qb):
    w = pl.next_power_of_2(LEFT + 2 * qb)
    j = jnp.arange(w)
    j = jnp.where(j < LEFT + qb, j, j - w)
    idx = jnp.clip(LEFT - j, -MAX_REL_DIST, MAX_REL_DIST) + MAX_REL_DIST
    return rel_bias[:, None, idx].astype(F32)


def _chunk_attention(proj, rel_bias, n_heads, qb=256):
    b, s, _ = proj.shape
    dh = ATTN_HEAD_DIM
    base = _attn_bias_base(rel_bias, qb)
    kern = functools.partial(_attn_kernel, qb=qb, scale=dh ** -0.5)
    return pl.pallas_call(
        kern,
        grid=(n_heads, b),
        in_specs=[pl.BlockSpec((None, s, dh), lambda h, bi: (bi, 0, h)),
                  pl.BlockSpec((None, s, dh), lambda h, bi: (bi, 0, n_heads + h)),
                  pl.BlockSpec((None, s, dh), lambda h, bi: (bi, 0, 2 * n_heads + h)),
                  pl.BlockSpec((None, 1, base.shape[-1]), lambda h, bi: (h, 0, 0))],
        out_specs=pl.BlockSpec((None, s, dh), lambda h, bi: (bi, 0, h)),
        out_shape=jax.ShapeDtypeStruct((b, s, n_heads * dh), BF16),
        scratch_shapes=[pltpu.VMEM((qb, LEFT + qb), F32)],
        compiler_params=_params(2),
        name="chunk_attention",
    )(proj, proj, proj, base)


def _ret_kernel(q_ref, k_ref, v_ref, g_ref, cos_ref, sin_ref, dec_ref, qd_ref, kd_ref, bd_ref,
                gnw_ref, o_ref, state_ref, *, rb, scale):
    seq = q_ref.shape[0]
    state_ref[...] = jnp.zeros_like(state_ref)
    half = q_ref.shape[1] // 2

    def body(n, carry):
        r0 = pl.multiple_of(n * rb, rb)
        rows = pl.ds(r0, rb)
        cos = cos_ref[rows, :]
        sin = sin_ref[rows, :]
        q = q_ref[rows, :].astype(F32)
        k = k_ref[rows, :].astype(F32)
        q = q * cos + pltpu.roll(q, half, 1) * sin
        k = (k * cos + pltpu.roll(k, half, 1) * sin) * scale
        v = v_ref[rows, :]
        a = lax.dot_general(q.astype(BF16), k.astype(BF16), _NT, preferred_element_type=F32)
        a = a * dec_ref[...]
        st = state_ref[...]
        y = jnp.dot(a.astype(BF16), v, preferred_element_type=F32)
        y = y + jnp.dot((q * qd_ref[...]).astype(BF16), st.astype(BF16),
                        preferred_element_type=F32)
        kd = (k * kd_ref[...]).astype(BF16)
        kv = lax.dot_general(kd, v, _TN, preferred_element_type=F32)
        state_ref[...] = st * bd_ref[...] + kv
        mu = jnp.mean(y, axis=-1, keepdims=True)
        yc = y - mu
        var = jnp.mean(yc * yc, axis=-1, keepdims=True)
        yn = yc * lax.rsqrt(var + EPS) * gnw_ref[...]
        g = g_ref[rows, :].astype(F32)
        o_ref[rows, :] = (g * (1.0 / (1.0 + jnp.exp(-g))) * yn).astype(o_ref.dtype)
        return carry

    lax.fori_loop(0, seq // rb, body, 0)


def _retention(proj, gn_w, n_heads, col0, rb=256):
    b, s, _ = proj.shape
    dk, dv = RET_QK_DIM, RET_V_DIM
    qk0 = col0 // dk
    v0 = (col0 + 2 * n_heads * dk) // dv
    inv_freq = 1.0 / (ROPE_BASE ** (jnp.arange(0, dk, 2, dtype=F32) / dk))
    ang = jnp.arange(s, dtype=F32)[:, None] * inv_freq[None, :]
    cos = jnp.concatenate([jnp.cos(ang), jnp.cos(ang)], axis=-1)
    sin = jnp.concatenate([-jnp.sin(ang), jnp.sin(ang)], axis=-1)
    log_gamma = jnp.log1p(-jnp.power(2.0, -5.0 - jnp.arange(n_heads, dtype=F32)))
    pos = jnp.arange(rb, dtype=F32)
    chunk_of = jnp.arange(rb) // CHUNK
    causal = (chunk_of[None, :] <= chunk_of[:, None]).astype(F32)
    dec = jnp.exp(log_gamma[:, None, None] * jnp.abs(pos[:, None] - pos[None, :])) * causal[None]
    qd = jnp.exp(log_gamma[:, None] * (pos + 1.0))[:, :, None]
    kd = jnp.exp(log_gamma[:, None] * (rb - 1.0 - pos))[:, :, None]
    bd = jnp.exp(log_gamma * rb)[:, None, None]
    kern = functools.partial(_ret_kernel, rb=rb, scale=dk ** -0.5)
    return pl.pallas_call(
        kern,
        grid=(b, n_heads),
        in_specs=[pl.BlockSpec((None, s, dk), lambda bi, h: (bi, 0, qk0 + h)),
                  pl.BlockSpec((None, s, dk), lambda bi, h: (bi, 0, qk0 + n_heads + h)),
                  pl.BlockSpec((None, s, dv), lambda bi, h: (bi, 0, v0 + h)),
                  pl.BlockSpec((None, s, dv), lambda bi, h: (bi, 0, v0 + n_heads + h)),
                  pl.BlockSpec((s, dk), lambda bi, h: (0, 0)),
                  pl.BlockSpec((s, dk), lambda bi, h: (0, 0)),
                  pl.BlockSpec((None, rb, rb), lambda bi, h: (h, 0, 0)),
                  pl.BlockSpec((None, rb, 1), lambda bi, h: (h, 0, 0)),
                  pl.BlockSpec((None, rb, 1), lambda bi, h: (h, 0, 0)),
                  pl.BlockSpec((None, 1, 1), lambda bi, h: (h, 0, 0)),
                  pl.BlockSpec((1, dv), lambda bi, h: (0, h))],
        out_specs=pl.BlockSpec((None, s, dv), lambda bi, h: (bi, 0, h)),
        out_shape=jax.ShapeDtypeStruct((b, s, n_heads * dv), BF16),
        scratch_shapes=[pltpu.VMEM((dk, dv), F32)],
        compiler_params=_params(2),
        name="retention",
    )(proj, proj, proj, proj, cos, sin, dec, qd, kd, bd, gn_w.reshape(1, -1))


def _xattn_kernel(q_ref, kv_ref, o_ref, *, n_heads, scale):
    d = q_ref.shape[1]
    dh = d // n_heads
    for h in range(n_heads):
        cols = slice(h * dh, (h + 1) * dh)
        s = lax.dot_general(q_ref[:, cols], kv_ref[:, cols], _NT, preferred_element_type=F32)
        s = s * scale
        m = jnp.max(s, axis=-1, keepdims=True)
        p = jnp.exp(s - m)
        l = jnp.sum(p, axis=-1, keepdims=True)
        o = jnp.dot(p.astype(BF16), kv_ref[:, d + h * dh:d + (h + 1) * dh],
                    preferred_element_type=F32)
        o_ref[:, cols] = (o / l).astype(o_ref.dtype)


def _cross_attention(q, kv, n_heads, tq=512):
    b, s, d = q.shape
    m = kv.shape[1]
    kern = functools.partial(_xattn_kernel, n_heads=n_heads, scale=(d // n_heads) ** -0.5)
    return pl.pallas_call(
        kern,
        grid=(b, s // tq),
        in_specs=[pl.BlockSpec((None, tq, d), lambda bi, i: (bi, i, 0)),
                  pl.BlockSpec((None, m, 2 * d), lambda bi, i: (bi, 0, 0))],
        out_specs=pl.BlockSpec((None, tq, d), lambda bi, i: (bi, i, 0)),
        out_shape=jax.ShapeDtypeStruct((b, s, d), BF16),
        compiler_params=_params(2),
        name="cross_attention",
    )(q, kv)


def _take_max(work, rows, exact):
    m = jnp.max(work, axis=0, keepdims=True)
    hit = work == m
    if not exact:
        return m, hit
    first = jnp.min(jnp.where(hit, rows, np.float32(work.shape[0])), axis=0, keepdims=True)
    return m, rows == first


def _top_rows(s, k, exact):
    rows = lax.broadcasted_iota(jnp.int32, s.shape, 0).astype(F32)
    work = s
    rank = jnp.full(s.shape, NOT_RANKED, F32)
    vals = []
    for r in range(k):
        m, sel = _take_max(work, rows, exact)
        rank = jnp.where(sel, np.float32(r), rank)
        work = jnp.where(sel, -jnp.inf, work)
        vals.append(m)
    taken = jnp.sum(jnp.where(rank < NOT_RANKED, 1.0, 0.0), axis=0, keepdims=True)
    return vals, rank, taken


def _candidate_cells(k):
    cells = []
    for ra in range(k):
        cells += [(ra, rb) for rb in range(k // (ra + 1))]
    single = [c for c in cells if k // (c[0] + 1) == 1]
    multi = [c for c in cells if c not in single]
    pad = (-len(multi)) % 8
    return multi + [None] * pad + single


def _route_kernel(q_ref, keys_ref, u_ref, v_ref, rank1_ref, e1_ref, cnt_ref, e0_ref, ub_ref, vb_ref):
    ub_ref[...] = u_ref[...].astype(ub_ref.dtype)
    vb_ref[...] = v_ref[...].astype(vb_ref.dtype)
    k = PEER_TOPK
    tl = q_ref.shape[0]
    cells = _candidate_cells(k)

    def route_group(g, exact):
        q = q_ref[g * LANES:(g + 1) * LANES, :]
        s0 = lax.dot_general(keys_ref[0].astype(BF16), q[:, :PEER_HALF], _NT,
                             preferred_element_type=F32)
        s1 = lax.dot_general(keys_ref[1].astype(BF16), q[:, PEER_HALF:], _NT,
                             preferred_element_type=F32)
        a, rank0, taken0 = _top_rows(s0, k, exact)
        b, rank1, taken1 = _top_rows(s1, k, exact)
        a_all, b_all = jnp.concatenate(a, axis=0), jnp.concatenate(b, axis=0)
        ea_all, eb_all = jnp.exp(a_all - a[0]), jnp.exp(b_all - b[0])
        neg = jnp.full_like(a[0], -jnp.inf)
        zero = jnp.zeros_like(a[0])
        n0, n1 = k, k // 2
        mid = cells[n0 + n1:len(cells) - k // 2]
        cand = jnp.concatenate(
            [a[0] + b_all, a[1] + b_all[:n1]]
            + [neg if c is None else a_all[c[0]:c[0] + 1] + b_all[c[1]:c[1] + 1] for c in mid]
            + [a_all[k // 2:] + b[0]], axis=0)
        wgt = jnp.concatenate(
            [ea_all[0:1] * eb_all, ea_all[1:2] * eb_all[:n1]]
            + [zero if c is None else ea_all[c[0]:c[0] + 1] * eb_all[c[1]:c[1] + 1] for c in mid]
            + [ea_all[k // 2:] * eb_all[0:1]], axis=0)
        rows = lax.broadcasted_iota(jnp.int32, cand.shape, 0).astype(F32)
        work = cand
        chosen = jnp.zeros(cand.shape, F32)
        for _ in range(k):
            _, sel = _take_max(work, rows, exact)
            chosen = jnp.where(sel, 1.0, chosen)
            work = jnp.where(sel, -jnp.inf, work)
        z = jnp.sum(chosen * wgt, axis=0, keepdims=True)
        cnt = jnp.zeros(s0.shape, F32)
        for ra in range(k):
            mine = [i for i, c in enumerate(cells) if c is not None and c[0] == ra]
            n_ra = jnp.sum(chosen[mine[0]:mine[-1] + 1], axis=0, keepdims=True)
            cnt = jnp.where(rank0 == np.float32(ra), n_ra, cnt)
        lanes = slice(g * LANES, (g + 1) * LANES)
        rank1_ref[:, lanes] = rank1.astype(rank1_ref.dtype)
        e1_ref[:, lanes] = (jnp.exp(s1 - b[0]) / z).astype(e1_ref.dtype)
        cnt_ref[:, lanes] = cnt
        e0_ref[:, lanes] = jnp.exp(s0 - a[0])
        taken2 = jnp.sum(chosen, axis=0, keepdims=True)
        return (jnp.abs(taken0 - k) + jnp.abs(taken1 - k) + jnp.abs(taken2 - k))

    groups = range(tl // LANES)
    tie = sum(route_group(g, exact=False) for g in groups)

    @pl.when(jnp.max(tie) > 0.0)
    def _():
        for g in groups:
            route_group(g, exact=True)


def _peer_route(qp, subkeys, u, v, tl=512):
    t = qp.shape[0]
    hp, _, nk, half = subkeys.shape
    n_exp, d = u.shape
    steps = (t // tl) * hp
    rows = n_exp // steps
    assert rows * steps == n_exp
    out = lambda dt: jax.ShapeDtypeStruct((hp, nk, t), dt)
    ospec = pl.BlockSpec((None, nk, tl), lambda i, h: (h, 0, i))
    wspec = pl.BlockSpec((rows, d), lambda i, h: (i * hp + h, 0))
    wout = jax.ShapeDtypeStruct((n_exp, d), BF16)
    return pl.pallas_call(
        _route_kernel,
        grid=(t // tl, hp),
        in_specs=[pl.BlockSpec((tl, 2 * half), lambda i, h: (i, h)),
                  pl.BlockSpec((None, 2, nk, half), lambda i, h: (h, 0, 0, 0)),
                  wspec, wspec],
        out_specs=[ospec] * 4 + [wspec] * 2,
        out_shape=[out(BF16), out(BF16), out(F32), out(F32), wout, wout],
        compiler_params=_params(2),
        name="peer_route",
    )(qp, subkeys, u, v)


def _peer_kernel(x_ref, rstd_ref, u_ref, v_ref, rank1_ref, e1_ref, cnt_ref, e0_ref, o_ref, acc_ref,
                 xt_ref, *stage_refs, sub, n_pieces):
    e = pl.program_id(1)
    n_heads, nk, tm = rank1_ref.shape
    te, d = u_ref.shape
    n_sub = te // sub
    rows_per_sub = sub // nk
    kc = d // n_pieces
    rows_per_step = te // nk
    row0 = (e % (8 // rows_per_step)) * rows_per_step
    act_refs, wt_refs = stage_refs[:n_sub], stage_refs[n_sub:]

    @pl.when(e == 0)
    def _():
        acc_ref[...] = jnp.zeros_like(acc_ref)
        xt_ref[...] = x_ref[...].T

    def up_piece(j, p):
        part = jnp.dot(u_ref[j * sub:(j + 1) * sub, p * kc:(p + 1) * kc],
                       xt_ref[p * kc:(p + 1) * kc, :], preferred_element_type=F32)
        if p == 0:
            act_refs[j][...] = part
        else:
            act_refs[j][...] += part

    def gate_piece(j, p):
        ii, g = divmod(p, n_pieces // rows_per_sub)
        tg = tm // (n_pieces // rows_per_sub)
        lanes = slice(g * tg, (g + 1) * tg)
        row = j * rows_per_sub + ii
        gate = None
        for h in range(n_heads):
            cnt = cnt_ref[h, pl.ds(row0 + row, 1), :][:, lanes].astype(BF16)
            e0 = e0_ref[h, pl.ds(row0 + row, 1), :][:, lanes].astype(BF16)
            zero = jnp.zeros((), BF16)
            term = e0 * jnp.where(rank1_ref[h, :, lanes] < cnt, e1_ref[h, :, lanes], zero)
            gate = term if gate is None else gate + term
        act = act_refs[j][ii * nk:(ii + 1) * nk, lanes] * rstd_ref[:, lanes]
        act = 0.5 * act * (1.0 + lax.erf(act * np.float32(1.0 / np.sqrt(2.0))))
        wt_refs[j][lanes, ii * nk:(ii + 1) * nk] = (gate * act.astype(BF16)).T

    def down_piece(j, p):
        cols = slice(p * kc, (p + 1) * kc)
        acc_ref[:, cols] += jnp.dot(wt_refs[j][...], v_ref[j * sub:(j + 1) * sub, cols],
                                    preferred_element_type=F32)

    for step in range(n_sub + 2):
        for p in range(n_pieces):
            if step < n_sub:
                up_piece(step, p)
            if 0 <= step - 1 < n_sub:
                gate_piece(step - 1, p)
            if 0 <= step - 2 < n_sub:
                down_piece(step - 2, p)

    @pl.when(e == pl.num_programs(1) - 1)
    def _():
        o_ref[...] = acc_ref[...].astype(o_ref.dtype)


def _peer_experts(x, rstd, u, v, rank1, e1, cnt, e0, tm=512, te=512, sub=256, n_pieces=2):
    t, d = x.shape
    n_exp = u.shape[0]
    hp, nk, _ = rank1.shape
    rows = te // nk
    tab = pl.BlockSpec((hp, nk, tm), lambda i, e: (0, 0, i))
    row_tab = pl.BlockSpec((hp, 8, tm), lambda i, e: (0, (e * rows) // 8, i))
    return pl.pallas_call(
        functools.partial(_peer_kernel, sub=sub, n_pieces=n_pieces),
        grid=(t // tm, n_exp // te),
        in_specs=[pl.BlockSpec((tm, d), lambda i, e: (i, 0)),
                  pl.BlockSpec((1, tm), lambda i, e: (0, i)),
                  pl.BlockSpec((te, d), lambda i, e: (e, 0)),
                  pl.BlockSpec((te, d), lambda i, e: (e, 0)),
                  tab, tab, row_tab, row_tab],
        out_specs=pl.BlockSpec((tm, d), lambda i, e: (i, 0)),
        out_shape=jax.ShapeDtypeStruct((t, d), BF16),
        scratch_shapes=([pltpu.VMEM((tm, d), F32), pltpu.VMEM((d, tm), BF16)]
                        + [pltpu.VMEM((sub, tm), F32)] * (te // sub)
                        + [pltpu.VMEM((tm, sub), BF16)] * (te // sub)),
        compiler_params=_params(2),
        name="peer_experts",
    )(x, rstd, u, v, rank1, e1, cnt, e0)


def kernel(x, mem, norm1_w, w_in, attn_rel_bias, ret_gn_w, w_out, norm2_w, mem_norm_w, xattn_wq,
           xattn_wkv, xattn_wo, norm3_w, peer_wq, peer_subkeys, peer_u, peer_v, final_norm_w):
    b, s, d = x.shape
    t = b * s
    depth = w_in.shape[0]
    n_attn_heads = attn_rel_bias.shape[1]
    attn_width = n_attn_heads * ATTN_HEAD_DIM
    n_ret_heads = ret_gn_w.shape[1] // RET_V_DIM
    h = x.reshape(t, d)
    xn = _rmsnorm(h, norm1_w[0], BF16)
    for l in range(depth):
        proj = _matmul(xn, w_in[l], BF16, tn=768, name="in_proj").reshape(b, s, -1)
        a_out = _chunk_attention(proj, attn_rel_bias[l], n_attn_heads)
        r_out = _retention(proj, ret_gn_w[l], n_ret_heads, 3 * attn_width)
        h, hg, rstd = _matmul((a_out.reshape(t, -1), r_out.reshape(t, -1)), w_out[l], F32,
                              residual=h, norm_gain=norm2_w[l], name="out_proj")

        mem_n = _rmsnorm(mem.reshape(-1, d), mem_norm_w[l], BF16)
        kv = _matmul(mem_n, xattn_wkv[l], BF16, name="mem_kv").reshape(b, -1, 2 * d)
        xq = _matmul(hg, xattn_wq[l], BF16, row_scale=rstd, name="xattn_q")
        xo = _cross_attention(xq.reshape(b, s, d), kv, XATTN_HEADS).reshape(t, d)
        h, hg, rstd = _matmul(xo, xattn_wo[l], F32, residual=h, norm_gain=norm3_w[l],
                              name="xattn_o")

        qp = _matmul(hg, peer_wq[l], BF16, row_scale=rstd, name="peer_q")
        rank1, e1, cnt, e0, u_b, v_b = _peer_route(qp, peer_subkeys[l], peer_u[l], peer_v[l])
        delta = _peer_experts(hg, rstd.reshape(1, t), u_b, v_b, rank1, e1, cnt, e0)
        if l + 1 < depth:
            h = h + delta.astype(F32)
            xn = _rmsnorm(h, norm1_w[l + 1], BF16)
    return _add_rmsnorm(h, delta, final_norm_w, F32).reshape(b, s, d)
```

```python
import functools

import numpy as np
import jax
import jax.numpy as jnp
from jax import lax
from jax.experimental import pallas as pl
from jax.experimental.pallas import tpu as pltpu

F32 = jnp.float32
BF16 = jnp.bfloat16

CHUNK = 64
LEFT_CHUNKS = 8
LEFT = LEFT_CHUNKS * CHUNK
ATTN_HEAD_DIM = 128
MAX_REL_DIST = 256
RET_V_DIM = 256
RET_QK_DIM = 128
ROPE_BASE = 10000.0
XATTN_HEADS = 4
PEER_HEADS = 8
PEER_KEYS = 128
PEER_HALF = 128
PEER_TOPK = 16
EPS = 1e-6
NEG_INF = -1e30
NOT_RANKED = 1e9

VMEM_LIMIT_BYTES = 56 * 1024 * 1024
LANES = 128

_NT = (((1,), (1,)), ((), ()))
_TN = (((0,), (0,)), ((), ()))


def _params(n_grid_dims, flags=None):
    return pltpu.CompilerParams(
        dimension_semantics=("arbitrary",) * n_grid_dims,
        vmem_limit_bytes=VMEM_LIMIT_BYTES,
        flags=flags)


def _rmsnorm_kernel(x_ref, w_ref, o_ref):
    x = x_ref[...]
    ms = jnp.mean(x * x, axis=-1, keepdims=True)
    o_ref[...] = (x * lax.rsqrt(ms + EPS) * w_ref[...]).astype(o_ref.dtype)


def _rmsnorm(x, w, out_dtype, rows=256):
    m, d = x.shape
    return pl.pallas_call(
        _rmsnorm_kernel,
        grid=(m // rows,),
        in_specs=[pl.BlockSpec((rows, d), lambda i: (i, 0)),
                  pl.BlockSpec((1, d), lambda i: (0, 0))],
        out_specs=pl.BlockSpec((rows, d), lambda i: (i, 0)),
        out_shape=jax.ShapeDtypeStruct((m, d), out_dtype),
        compiler_params=_params(1),
        name="rmsnorm",
    )(x, w.reshape(1, d))


def _add_rmsnorm_kernel(x_ref, y_ref, w_ref, o_ref):
    x = x_ref[...] + y_ref[...].astype(F32)
    ms = jnp.mean(x * x, axis=-1, keepdims=True)
    o_ref[...] = (x * lax.rsqrt(ms + EPS) * w_ref[...]).astype(o_ref.dtype)


def _add_rmsnorm(x, y, w, out_dtype, rows=256):
    m, d = x.shape
    return pl.pallas_call(
        _add_rmsnorm_kernel,
        grid=(m // rows,),
        in_specs=[pl.BlockSpec((rows, d), lambda i: (i, 0)),
                  pl.BlockSpec((rows, d), lambda i: (i, 0)),
                  pl.BlockSpec((1, d), lambda i: (0, 0))],
        out_specs=pl.BlockSpec((rows, d), lambda i: (i, 0)),
        out_shape=jax.ShapeDtypeStruct((m, d), out_dtype),
        compiler_params=_params(1),
        name="add_rmsnorm",
    )(x, y, w.reshape(1, d))


def _mm_kernel(*refs, n_lhs, has_residual, has_row_scale, emit_norm, n_cols):
    lhs_refs, w_ref = refs[:n_lhs], refs[n_lhs]
    pos = n_lhs + 1
    acc, k0 = None, 0
    for a_ref in lhs_refs:
        k1 = k0 + a_ref.shape[1]
        part = jnp.dot(a_ref[...], w_ref[k0:k1, :].astype(BF16), preferred_element_type=F32)
        acc = part if acc is None else acc + part
        k0 = k1
    if has_row_scale:
        acc = acc * refs[pos][...]
        pos += 1
    if has_residual:
        acc = refs[pos][...] + acc
        pos += 1
    if not emit_norm:
        o_ref = refs[pos]
        o_ref[...] = acc.astype(o_ref.dtype)
        return
    gain_ref, o_ref, scaled_ref, rstd_ref, ssq_ref = refs[pos:pos + 5]
    j = pl.program_id(1)
    o_ref[...] = acc.astype(o_ref.dtype)
    scaled_ref[...] = (acc * gain_ref[...]).astype(scaled_ref.dtype)
    row_ssq = jnp.sum(acc * acc, axis=-1, keepdims=True)

    @pl.when(j == 0)
    def _():
        ssq_ref[...] = row_ssq

    @pl.when(j > 0)
    def _():
        ssq_ref[...] += row_ssq

    @pl.when(j == pl.num_programs(1) - 1)
    def _():
        rstd_ref[...] = lax.rsqrt(ssq_ref[...] / n_cols + EPS)


def _matmul(lhs, w, out_dtype, residual=None, row_scale=None, norm_gain=None, tm=1024, tn=512,
            name="matmul"):
    lhs = lhs if isinstance(lhs, (tuple, list)) else (lhs,)
    m = lhs[0].shape[0]
    k, n = w.shape[-2:]
    assert sum(a.shape[1] for a in lhs) == k
    tm, tn = min(tm, m), min(tn, n)
    in_specs = [pl.BlockSpec((tm, a.shape[1]), lambda i, j: (i, 0)) for a in lhs]
    if w.ndim == 2:
        in_specs.append(pl.BlockSpec((k, tn), lambda i, j: (0, j)))
    else:
        tiles_per_group = m // w.shape[0] // tm
        in_specs.append(pl.BlockSpec((None, k, tn), lambda i, j: (i // tiles_per_group, 0, j)))
    args = [*lhs, w]
    tile = pl.BlockSpec((tm, tn), lambda i, j: (i, j))
    per_row = pl.BlockSpec((tm, 1), lambda i, j: (i, 0))
    if row_scale is not None:
        in_specs.append(per_row)
        args.append(row_scale)
    if residual is not None:
        in_specs.append(tile)
        args.append(residual)
    out_specs, out_shape, scratch = tile, jax.ShapeDtypeStruct((m, n), out_dtype), []
    if norm_gain is not None:
        in_specs.append(pl.BlockSpec((1, tn), lambda i, j: (0, j)))
        args.append(norm_gain.reshape(1, n))
        out_specs = [tile, tile, per_row]
        out_shape = [out_shape, jax.ShapeDtypeStruct((m, n), BF16),
                     jax.ShapeDtypeStruct((m, 1), F32)]
        scratch = [pltpu.VMEM((tm, 1), F32)]
    return pl.pallas_call(
        functools.partial(_mm_kernel, n_lhs=len(lhs), has_residual=residual is not None,
                          has_row_scale=row_scale is not None, emit_norm=norm_gain is not None,
                          n_cols=n),
        grid=(m // tm, n // tn),
        in_specs=in_specs,
        out_specs=out_specs,
        out_shape=out_shape,
        scratch_shapes=scratch,
        compiler_params=_params(2),
        name=name,
    )(*args)


def _attn_kernel(q_ref, k_ref, v_ref, base_ref, o_ref, bias_ref, *, qb, scale):
    seq = q_ref.shape[0]
    width = LEFT + qb

    @pl.when(pl.program_id(1) == 0)
    def _():
        toeplitz = pltpu.roll(jnp.broadcast_to(base_ref[...], (qb, base_ref.shape[-1])),
                              0, 1, stride=1, stride_axis=0)
        q_chunk = lax.broadcasted_iota(jnp.int32, (qb, width), 0) // CHUNK
        c_chunk = lax.broadcasted_iota(jnp.int32, (qb, width), 1) // CHUNK
        in_band = (c_chunk >= q_chunk) & (c_chunk <= q_chunk + LEFT_CHUNKS)
        bias_ref[...] = jnp.where(in_band, toeplitz[:, :width], NEG_INF)

    for i in range(seq // qb):
        q0 = i * qb
        k0 = max(0, q0 - LEFT)
        kw = q0 + qb - k0
        c0 = k0 - (q0 - LEFT)
        s = lax.dot_general(q_ref[q0:q0 + qb, :], k_ref[k0:k0 + kw, :], _NT,
                            preferred_element_type=F32)
        s = s * scale + bias_ref[:, c0:c0 + kw]
        m = jnp.max(s, axis=-1, keepdims=True)
        p = jnp.exp(s - m)
        l = jnp.sum(p, axis=-1, keepdims=True)
        o = jnp.dot(p.astype(BF16), v_ref[k0:k0 + kw, :], preferred_element_type=F32)
        o_ref[q0:q0 + qb, :] = (o / l).astype(o_ref.dtype)


def _attn_bias_base(rel_bias, qb):
    w = pl.next_power_of_2(LEFT + 2 * qb)
    j = jnp.arange(w)
    j = jnp.where(j < LEFT + qb, j, j - w)
    idx = jnp.clip(LEFT - j, -MAX_REL_DIST, MAX_REL_DIST) + MAX_REL_DIST
    return rel_bias[:, None, idx].astype(F32)


def _chunk_attention(proj, rel_bias, n_heads, qb=256):
    b, s, _ = proj.shape
    dh = ATTN_HEAD_DIM
    base = _attn_bias_base(rel_bias, qb)
    kern = functools.partial(_attn_kernel, qb=qb, scale=dh ** -0.5)
    return pl.pallas_call(
        kern,
        grid=(n_heads, b),
        in_specs=[pl.BlockSpec((None, s, dh), lambda h, bi: (bi, 0, h)),
                  pl.BlockSpec((None, s, dh), lambda h, bi: (bi, 0, n_heads + h)),
                  pl.BlockSpec((None, s, dh), lambda h, bi: (bi, 0, 2 * n_heads + h)),
                  pl.BlockSpec((None, 1, base.shape[-1]), lambda h, bi: (h, 0, 0))],
        out_specs=pl.BlockSpec((None, s, dh), lambda h, bi: (bi, 0, h)),
        out_shape=jax.ShapeDtypeStruct((b, s, n_heads * dh), BF16),
        scratch_shapes=[pltpu.VMEM((qb, LEFT + qb), F32)],
        compiler_params=_params(2),
        name="chunk_attention",
    )(proj, proj, proj, base)


def _ret_kernel(q_ref, k_ref, v_ref, g_ref, cos_ref, sin_ref, dec_ref, qd_ref, kd_ref, bd_ref,
                gnw_ref, o_ref, state_ref, *, rb, scale):
    seq = q_ref.shape[0]
    state_ref[...] = jnp.zeros_like(state_ref)
    half = q_ref.shape[1] // 2

    def body(n, carry):
        r0 = pl.multiple_of(n * rb, rb)
        rows = pl.ds(r0, rb)
        cos = cos_ref[rows, :]
        sin = sin_ref[rows, :]
        q = q_ref[rows, :].astype(F32)
        k = k_ref[rows, :].astype(F32)
        q = q * cos + pltpu.roll(q, half, 1) * sin
        k = (k * cos + pltpu.roll(k, half, 1) * sin) * scale
        v = v_ref[rows, :]
        a = lax.dot_general(q.astype(BF16), k.astype(BF16), _NT, preferred_element_type=F32)
        a = a * dec_ref[...]
        st = state_ref[...]
        y = jnp.dot(a.astype(BF16), v, preferred_element_type=F32)
        y = y + jnp.dot((q * qd_ref[...]).astype(BF16), st.astype(BF16),
                        preferred_element_type=F32)
        kd = (k * kd_ref[...]).astype(BF16)
        kv = lax.dot_general(kd, v, _TN, preferred_element_type=F32)
        state_ref[...] = st * bd_ref[...] + kv
        mu = jnp.mean(y, axis=-1, keepdims=True)
        yc = y - mu
        var = jnp.mean(yc * yc, axis=-1, keepdims=True)
        yn = yc * lax.rsqrt(var + EPS) * gnw_ref[...]
        g = g_ref[rows, :].astype(F32)
        o_ref[rows, :] = (g * (1.0 / (1.0 + jnp.exp(-g))) * yn).astype(o_ref.dtype)
        return carry

    lax.fori_loop(0, seq // rb, body, 0)


def _retention(proj, gn_w, n_heads, col0, rb=256):
    b, s, _ = proj.shape
    dk, dv = RET_QK_DIM, RET_V_DIM
    qk0 = col0 // dk
    v0 = (col0 + 2 * n_heads * dk) // dv
    inv_freq = 1.0 / (ROPE_BASE ** (jnp.arange(0, dk, 2, dtype=F32) / dk))
    ang = jnp.arange(s, dtype=F32)[:, None] * inv_freq[None, :]
    cos = jnp.concatenate([jnp.cos(ang), jnp.cos(ang)], axis=-1)
    sin = jnp.concatenate([-jnp.sin(ang), jnp.sin(ang)], axis=-1)
    log_gamma = jnp.log1p(-jnp.power(2.0, -5.0 - jnp.arange(n_heads, dtype=F32)))
    pos = jnp.arange(rb, dtype=F32)
    chunk_of = jnp.arange(rb) // CHUNK
    causal = (chunk_of[None, :] <= chunk_of[:, None]).astype(F32)
    dec = jnp.exp(log_gamma[:, None, None] * jnp.abs(pos[:, None] - pos[None, :])) * causal[None]
    qd = jnp.exp(log_gamma[:, None] * (pos + 1.0))[:, :, None]
    kd = jnp.exp(log_gamma[:, None] * (rb - 1.0 - pos))[:, :, None]
    bd = jnp.exp(log_gamma * rb)[:, None, None]
    kern = functools.partial(_ret_kernel, rb=rb, scale=dk ** -0.5)
    return pl.pallas_call(
        kern,
        grid=(b, n_heads),
        in_specs=[pl.BlockSpec((None, s, dk), lambda bi, h: (bi, 0, qk0 + h)),
                  pl.BlockSpec((None, s, dk), lambda bi, h: (bi, 0, qk0 + n_heads + h)),
                  pl.BlockSpec((None, s, dv), lambda bi, h: (bi, 0, v0 + h)),
                  pl.BlockSpec((None, s, dv), lambda bi, h: (bi, 0, v0 + n_heads + h)),
                  pl.BlockSpec((s, dk), lambda bi, h: (0, 0)),
                  pl.BlockSpec((s, dk), lambda bi, h: (0, 0)),
                  pl.BlockSpec((None, rb, rb), lambda bi, h: (h, 0, 0)),
                  pl.BlockSpec((None, rb, 1), lambda bi, h: (h, 0, 0)),
                  pl.BlockSpec((None, rb, 1), lambda bi, h: (h, 0, 0)),
                  pl.BlockSpec((None, 1, 1), lambda bi, h: (h, 0, 0)),
                  pl.BlockSpec((1, dv), lambda bi, h: (0, h))],
        out_specs=pl.BlockSpec((None, s, dv), lambda bi, h: (bi, 0, h)),
        out_shape=jax.ShapeDtypeStruct((b, s, n_heads * dv), BF16),
        scratch_shapes=[pltpu.VMEM((dk, dv), F32)],
        compiler_params=_params(2),
        name="retention",
    )(proj, proj, proj, proj, cos, sin, dec, qd, kd, bd, gn_w.reshape(1, -1))


def _fold_keys_kernel(wq_ref, k_ref, o_ref, wq16_ref):
    @pl.when(pl.program_id(1) == 0)
    def _():
        wq16_ref[...] = wq_ref[...].astype(BF16)

    o_ref[...] = lax.dot_general(wq16_ref[...], k_ref[...], _NT,
                                 preferred_element_type=F32).astype(o_ref.dtype)


def _fold_values_kernel(v_ref, wo_ref, o_ref, wo16_ref):
    @pl.when(pl.program_id(1) == 0)
    def _():
        wo16_ref[...] = wo_ref[...].astype(BF16)

    o_ref[...] = jnp.dot(v_ref[...], wo16_ref[...],
                         preferred_element_type=F32).astype(o_ref.dtype)


def _fold_memory(kv, wq, wo, n_heads):
    b, m, d2 = kv.shape
    d = d2 // 2
    dh = d // n_heads
    folded_k = pl.pallas_call(
        _fold_keys_kernel,
        grid=(n_heads, b),
        in_specs=[pl.BlockSpec((d, dh), lambda h, bi: (0, h)),
                  pl.BlockSpec((None, m, dh), lambda h, bi: (bi, 0, h))],
        out_specs=pl.BlockSpec((None, d, m), lambda h, bi: (bi, 0, h)),
        out_shape=jax.ShapeDtypeStruct((b, d, n_heads * m), BF16),
        scratch_shapes=[pltpu.VMEM((d, dh), BF16)],
        compiler_params=_params(2),
        name="fold_keys",
    )(wq, kv)
    folded_v = pl.pallas_call(
        _fold_values_kernel,
        grid=(n_heads, b),
        in_specs=[pl.BlockSpec((None, m, dh), lambda h, bi: (bi, 0, n_heads + h)),
                  pl.BlockSpec((dh, d), lambda h, bi: (h, 0))],
        out_specs=pl.BlockSpec((None, m, d), lambda h, bi: (bi, h, 0)),
        out_shape=jax.ShapeDtypeStruct((b, n_heads * m, d), BF16),
        scratch_shapes=[pltpu.VMEM((dh, d), BF16)],
        compiler_params=_params(2),
        name="fold_values",
    )(kv, wo)
    return folded_k, folded_v


def _mem_probs_kernel(x_ref, rstd_ref, fk_ref, o_ref, *, n_heads, scale):
    m = fk_ref.shape[1] // n_heads
    s = jnp.dot(x_ref[...], fk_ref[...], preferred_element_type=F32)
    s = s * (rstd_ref[...] * scale)
    for h in range(n_heads):
        sh = s[:, h * m:(h + 1) * m]
        mx = jnp.max(sh, axis=-1, keepdims=True)
        p = jnp.exp(sh - mx)
        o_ref[:, h * m:(h + 1) * m] = (p / jnp.sum(p, axis=-1, keepdims=True)).astype(o_ref.dtype)


def _mem_probs(x, rstd, folded_k, n_heads, tq=1024):
    b, s, d = x.shape
    hm = folded_k.shape[2]
    kern = functools.partial(_mem_probs_kernel, n_heads=n_heads, scale=(d // n_heads) ** -0.5)
    return pl.pallas_call(
        kern,
        grid=(b, s // tq),
        in_specs=[pl.BlockSpec((None, tq, d), lambda bi, i: (bi, i, 0)),
                  pl.BlockSpec((None, tq, 1), lambda bi, i: (bi, i, 0)),
                  pl.BlockSpec((None, d, hm), lambda bi, i: (bi, 0, 0))],
        out_specs=pl.BlockSpec((None, tq, hm), lambda bi, i: (bi, i, 0)),
        out_shape=jax.ShapeDtypeStruct((b, s, hm), BF16),
        compiler_params=_params(2),
        name="mem_probs",
    )(x, rstd, folded_k)


def _take_max(work, rows, exact):
    m = jnp.max(work, axis=0, keepdims=True)
    hit = work == m
    if not exact:
        return m, hit
    first = jnp.min(jnp.where(hit, rows, np.float32(work.shape[0])), axis=0, keepdims=True)
    return m, rows == first


def _top_rows(s, k, exact):
    rows = lax.broadcasted_iota(jnp.int32, s.shape, 0).astype(F32)
    work = s
    rank = jnp.full(s.shape, NOT_RANKED, F32)
    vals = []
    for r in range(k):
        m, sel = _take_max(work, rows, exact)
        rank = jnp.where(sel, np.float32(r), rank)
        work = jnp.where(sel, -jnp.inf, work)
        vals.append(m)
    taken = jnp.sum(jnp.where(rank < NOT_RANKED, 1.0, 0.0), axis=0, keepdims=True)
    return vals, rank, taken


def _candidate_cells(k):
    cells = []
    for ra in range(k):
        cells += [(ra, rb) for rb in range(k // (ra + 1))]
    single = [c for c in cells if k // (c[0] + 1) == 1]
    multi = [c for c in cells if c not in single]
    pad = (-len(multi)) % 8
    return multi + [None] * pad + single


def _route_kernel(q_ref, keys_ref, u_ref, v_ref, rank1_ref, e1_ref, cnt_ref, e0_ref, ub_ref, vb_ref):
    ub_ref[...] = u_ref[...].astype(ub_ref.dtype)
    vb_ref[...] = v_ref[...].astype(vb_ref.dtype)
    k = PEER_TOPK
    tl = q_ref.shape[0]
    cells = _candidate_cells(k)

    def route_group(g, exact):
        q = q_ref[g * LANES:(g + 1) * LANES, :]
        s0 = lax.dot_general(keys_ref[0].astype(BF16), q[:, :PEER_HALF], _NT,
                             preferred_element_type=F32)
        s1 = lax.dot_general(keys_ref[1].astype(BF16), q[:, PEER_HALF:], _NT,
                             preferred_element_type=F32)
        a, rank0, taken0 = _top_rows(s0, k, exact)
        b, rank1, taken1 = _top_rows(s1, k, exact)
        a_all, b_all = jnp.concatenate(a, axis=0), jnp.concatenate(b, axis=0)
        ea_all, eb_all = jnp.exp(a_all - a[0]), jnp.exp(b_all - b[0])
        neg = jnp.full_like(a[0], -jnp.inf)
        zero = jnp.zeros_like(a[0])
        n0, n1 = k, k // 2
        mid = cells[n0 + n1:len(cells) - k // 2]
        cand = jnp.concatenate(
            [a[0] + b_all, a[1] + b_all[:n1]]
            + [neg if c is None else a_all[c[0]:c[0] + 1] + b_all[c[1]:c[1] + 1] for c in mid]
            + [a_all[k // 2:] + b[0]], axis=0)
        wgt = jnp.concatenate(
            [ea_all[0:1] * eb_all, ea_all[1:2] * eb_all[:n1]]
            + [zero if c is None else ea_all[c[0]:c[0] + 1] * eb_all[c[1]:c[1] + 1] for c in mid]
            + [ea_all[k // 2:] * eb_all[0:1]], axis=0)
        rows = lax.broadcasted_iota(jnp.int32, cand.shape, 0).astype(F32)
        work = cand
        chosen = jnp.zeros(cand.shape, F32)
        for _ in range(k):
            _, sel = _take_max(work, rows, exact)
            chosen = jnp.where(sel, 1.0, chosen)
            work = jnp.where(sel, -jnp.inf, work)
        z = jnp.sum(chosen * wgt, axis=0, keepdims=True)
        cnt = jnp.zeros(s0.shape, F32)
        for ra in range(k):
            mine = [i for i, c in enumerate(cells) if c is not None and c[0] == ra]
            n_ra = jnp.sum(chosen[mine[0]:mine[-1] + 1], axis=0, keepdims=True)
            cnt = jnp.where(rank0 == np.float32(ra), n_ra, cnt)
        lanes = slice(g * LANES, (g + 1) * LANES)
        rank1_ref[:, lanes] = rank1.astype(rank1_ref.dtype)
        e1_ref[:, lanes] = (jnp.exp(s1 - b[0]) / z).astype(e1_ref.dtype)
        cnt_ref[:, lanes] = cnt
        e0_ref[:, lanes] = jnp.exp(s0 - a[0])
        taken2 = jnp.sum(chosen, axis=0, keepdims=True)
        return (jnp.abs(taken0 - k) + jnp.abs(taken1 - k) + jnp.abs(taken2 - k))

    groups = range(tl // LANES)
    tie = sum(route_group(g, exact=False) for g in groups)

    @pl.when(jnp.max(tie) > 0.0)
    def _():
        for g in groups:
            route_group(g, exact=True)


def _peer_route(qp, subkeys, u, v, tl=512):
    t = qp.shape[0]
    hp, _, nk, half = subkeys.shape
    n_exp, d = u.shape
    steps = (t // tl) * hp
    rows = n_exp // steps
    assert rows * steps == n_exp
    out = lambda dt: jax.ShapeDtypeStruct((hp, nk, t), dt)
    ospec = pl.BlockSpec((None, nk, tl), lambda i, h: (h, 0, i))
    wspec = pl.BlockSpec((rows, d), lambda i, h: (i * hp + h, 0))
    wout = jax.ShapeDtypeStruct((n_exp, d), BF16)
    return pl.pallas_call(
        _route_kernel,
        grid=(t // tl, hp),
        in_specs=[pl.BlockSpec((tl, 2 * half), lambda i, h: (i, h)),
                  pl.BlockSpec((None, 2, nk, half), lambda i, h: (h, 0, 0, 0)),
                  wspec, wspec],
        out_specs=[ospec] * 4 + [wspec] * 2,
        out_shape=[out(BF16), out(BF16), out(F32), out(F32), wout, wout],
        compiler_params=_params(2),
        name="peer_route",
    )(qp, subkeys, u, v)


def _peer_kernel(x_ref, rstd_ref, u_ref, v_ref, rank1_ref, e1_ref, cnt_ref, e0_ref, o_ref, acc_ref,
                 xt_ref, *stage_refs, sub, n_pieces):
    e = pl.program_id(1)
    n_heads, nk, tm = rank1_ref.shape
    te, d = u_ref.shape
    n_sub = te // sub
    rows_per_sub = sub // nk
    kc = d // n_pieces
    rows_per_step = te // nk
    row0 = (e % (8 // rows_per_step)) * rows_per_step
    act_refs, wt_refs = stage_refs[:n_sub], stage_refs[n_sub:]

    @pl.when(e == 0)
    def _():
        acc_ref[...] = jnp.zeros_like(acc_ref)
        xt_ref[...] = x_ref[...].T

    def up_piece(j, p):
        part = jnp.dot(u_ref[j * sub:(j + 1) * sub, p * kc:(p + 1) * kc],
                       xt_ref[p * kc:(p + 1) * kc, :], preferred_element_type=F32)
        if p == 0:
            act_refs[j][...] = part
        else:
            act_refs[j][...] += part

    def gate_piece(j, p):
        ii, g = divmod(p, n_pieces // rows_per_sub)
        tg = tm // (n_pieces // rows_per_sub)
        lanes = slice(g * tg, (g + 1) * tg)
        row = j * rows_per_sub + ii
        gate = None
        for h in range(n_heads):
            cnt = cnt_ref[h, pl.ds(row0 + row, 1), :][:, lanes].astype(BF16)
            e0 = e0_ref[h, pl.ds(row0 + row, 1), :][:, lanes].astype(BF16)
            zero = jnp.zeros((), BF16)
            term = e0 * jnp.where(rank1_ref[h, :, lanes] < cnt, e1_ref[h, :, lanes], zero)
            gate = term if gate is None else gate + term
        act = act_refs[j][ii * nk:(ii + 1) * nk, lanes] * rstd_ref[:, lanes]
        act = 0.5 * act * (1.0 + lax.erf(act * np.float32(1.0 / np.sqrt(2.0))))
        wt_refs[j][lanes, ii * nk:(ii + 1) * nk] = (gate * act.astype(BF16)).T

    def down_piece(j, p):
        cols = slice(p * kc, (p + 1) * kc)
        acc_ref[:, cols] += jnp.dot(wt_refs[j][...], v_ref[j * sub:(j + 1) * sub, cols],
                                    preferred_element_type=F32)

    for step in range(n_sub + 2):
        for p in range(n_pieces):
            if step < n_sub:
                up_piece(step, p)
            if 0 <= step - 1 < n_sub:
                gate_piece(step - 1, p)
            if 0 <= step - 2 < n_sub:
                down_piece(step - 2, p)

    @pl.when(e == pl.num_programs(1) - 1)
    def _():
        o_ref[...] = acc_ref[...].astype(o_ref.dtype)


def _peer_experts(x, rstd, u, v, rank1, e1, cnt, e0, tm=512, te=512, sub=256, n_pieces=4):
    t, d = x.shape
    n_exp = u.shape[0]
    hp, nk, _ = rank1.shape
    rows = te // nk
    tab = pl.BlockSpec((hp, nk, tm), lambda i, e: (0, 0, i))
    row_tab = pl.BlockSpec((hp, 8, tm), lambda i, e: (0, (e * rows) // 8, i))
    return pl.pallas_call(
        functools.partial(_peer_kernel, sub=sub, n_pieces=n_pieces),
        grid=(t // tm, n_exp // te),
        in_specs=[pl.BlockSpec((tm, d), lambda i, e: (i, 0)),
                  pl.BlockSpec((1, tm), lambda i, e: (0, i)),
                  pl.BlockSpec((te, d), lambda i, e: (e, 0)),
                  pl.BlockSpec((te, d), lambda i, e: (e, 0)),
                  tab, tab, row_tab, row_tab],
        out_specs=pl.BlockSpec((tm, d), lambda i, e: (i, 0)),
        out_shape=jax.ShapeDtypeStruct((t, d), BF16),
        scratch_shapes=([pltpu.VMEM((tm, d), F32), pltpu.VMEM((d, tm), BF16)]
                        + [pltpu.VMEM((sub, tm), F32)] * (te // sub)
                        + [pltpu.VMEM((tm, sub), BF16)] * (te // sub)),
        compiler_params=_params(2),
        name="peer_experts",
    )(x, rstd, u, v, rank1, e1, cnt, e0)


def kernel(x, mem, norm1_w, w_in, attn_rel_bias, ret_gn_w, w_out, norm2_w, mem_norm_w, xattn_wq,
           xattn_wkv, xattn_wo, norm3_w, peer_wq, peer_subkeys, peer_u, peer_v, final_norm_w):
    b, s, d = x.shape
    t = b * s
    depth = w_in.shape[0]
    n_attn_heads = attn_rel_bias.shape[1]
    attn_width = n_attn_heads * ATTN_HEAD_DIM
    n_ret_heads = ret_gn_w.shape[1] // RET_V_DIM
    h = x.reshape(t, d)
    xn = _rmsnorm(h, norm1_w[0], BF16)
    for l in range(depth):
        proj = _matmul(xn, w_in[l], BF16, tn=768, name="in_proj").reshape(b, s, -1)
        a_out = _chunk_attention(proj, attn_rel_bias[l], n_attn_heads)
        r_out = _retention(proj, ret_gn_w[l], n_ret_heads, 3 * attn_width)
        h, hg, rstd = _matmul((a_out.reshape(t, -1), r_out.reshape(t, -1)), w_out[l], F32,
                              residual=h, norm_gain=norm2_w[l], name="out_proj")

        mem_n = _rmsnorm(mem.reshape(-1, d), mem_norm_w[l], BF16)
        kv = _matmul(mem_n, xattn_wkv[l], BF16, name="mem_kv").reshape(b, -1, 2 * d)
        folded_k, folded_v = _fold_memory(kv, xattn_wq[l], xattn_wo[l], XATTN_HEADS)
        probs = _mem_probs(hg.reshape(b, s, d), rstd.reshape(b, s, 1), folded_k, XATTN_HEADS)
        h, hg, rstd = _matmul(probs.reshape(t, -1), folded_v, F32, residual=h,
                              norm_gain=norm3_w[l], tn=1024, name="xattn_o")

        qp = _matmul(hg, peer_wq[l], BF16, row_scale=rstd, name="peer_q")
        rank1, e1, cnt, e0, u_b, v_b = _peer_route(qp, peer_subkeys[l], peer_u[l], peer_v[l])
        delta = _peer_experts(hg, rstd.reshape(1, t), u_b, v_b, rank1, e1, cnt, e0)
        if l + 1 < depth:
            h = h + delta.astype(F32)
            xn = _rmsnorm(h, norm1_w[l + 1], BF16)
    return _add_rmsnorm(h, delta, final_norm_w, F32).reshape(b, s, d)
```

```python
import functools

import numpy as np
import jax
import jax.numpy as jnp
from jax import lax
from jax.experimental import pallas as pl
from jax.experimental.pallas import tpu as pltpu

F32 = jnp.float32
BF16 = jnp.bfloat16

CHUNK = 64
LEFT_CHUNKS = 8
LEFT = LEFT_CHUNKS * CHUNK
ATTN_HEAD_DIM = 128
MAX_REL_DIST = 256
RET_V_DIM = 256
RET_QK_DIM = 128
ROPE_BASE = 10000.0
XATTN_HEADS = 4
PEER_HEADS = 8
PEER_KEYS = 128
PEER_HALF = 128
PEER_TOPK = 16
EPS = 1e-6
NEG_INF = -1e30
NOT_RANKED = 1e9

VMEM_LIMIT_BYTES = 56 * 1024 * 1024
LANES = 128

_NT = (((1,), (1,)), ((), ()))
_TN = (((0,), (0,)), ((), ()))


def _params(n_grid_dims, flags=None):
    return pltpu.CompilerParams(
        dimension_semantics=("arbitrary",) * n_grid_dims,
        vmem_limit_bytes=VMEM_LIMIT_BYTES,
        flags=flags)


def _rmsnorm_kernel(x_ref, w_ref, o_ref):
    x = x_ref[...]
    ms = jnp.mean(x * x, axis=-1, keepdims=True)
    o_ref[...] = (x * lax.rsqrt(ms + EPS) * w_ref[...]).astype(o_ref.dtype)


def _rmsnorm(x, w, out_dtype, rows=256):
    m, d = x.shape
    return pl.pallas_call(
        _rmsnorm_kernel,
        grid=(m // rows,),
        in_specs=[pl.BlockSpec((rows, d), lambda i: (i, 0)),
                  pl.BlockSpec((1, d), lambda i: (0, 0))],
        out_specs=pl.BlockSpec((rows, d), lambda i: (i, 0)),
        out_shape=jax.ShapeDtypeStruct((m, d), out_dtype),
        compiler_params=_params(1),
        name="rmsnorm",
    )(x, w.reshape(1, d))


def _add_rmsnorm_kernel(x_ref, y_ref, w_ref, o_ref):
    x = x_ref[...] + y_ref[...].astype(F32)
    ms = jnp.mean(x * x, axis=-1, keepdims=True)
    o_ref[...] = (x * lax.rsqrt(ms + EPS) * w_ref[...]).astype(o_ref.dtype)


def _add_rmsnorm(x, y, w, out_dtype, rows=256):
    m, d = x.shape
    return pl.pallas_call(
        _add_rmsnorm_kernel,
        grid=(m // rows,),
        in_specs=[pl.BlockSpec((rows, d), lambda i: (i, 0)),
                  pl.BlockSpec((rows, d), lambda i: (i, 0)),
                  pl.BlockSpec((1, d), lambda i: (0, 0))],
        out_specs=pl.BlockSpec((rows, d), lambda i: (i, 0)),
        out_shape=jax.ShapeDtypeStruct((m, d), out_dtype),
        compiler_params=_params(1),
        name="add_rmsnorm",
    )(x, y, w.reshape(1, d))


def _mm_kernel(*refs, n_lhs, has_residual, has_row_scale, emit_norm, n_cols):
    lhs_refs, w_ref = refs[:n_lhs], refs[n_lhs]
    pos = n_lhs + 1
    acc, k0 = None, 0
    for a_ref in lhs_refs:
        k1 = k0 + a_ref.shape[1]
        part = jnp.dot(a_ref[...], w_ref[k0:k1, :].astype(BF16), preferred_element_type=F32)
        acc = part if acc is None else acc + part
        k0 = k1
    if has_row_scale:
        acc = acc * refs[pos][...]
        pos += 1
    if has_residual:
        acc = refs[pos][...] + acc
        pos += 1
    if not emit_norm:
        o_ref = refs[pos]
        o_ref[...] = acc.astype(o_ref.dtype)
        return
    gain_ref, o_ref, scaled_ref, rstd_ref, ssq_ref = refs[pos:pos + 5]
    j = pl.program_id(1)
    o_ref[...] = acc.astype(o_ref.dtype)
    scaled_ref[...] = (acc * gain_ref[...]).astype(scaled_ref.dtype)
    row_ssq = jnp.sum(acc * acc, axis=-1, keepdims=True)

    @pl.when(j == 0)
    def _():
        ssq_ref[...] = row_ssq

    @pl.when(j > 0)
    def _():
        ssq_ref[...] += row_ssq

    @pl.when(j == pl.num_programs(1) - 1)
    def _():
        rstd_ref[...] = lax.rsqrt(ssq_ref[...] / n_cols + EPS)


def _matmul(lhs, w, out_dtype, residual=None, row_scale=None, norm_gain=None, tm=1024, tn=512,
            name="matmul"):
    lhs = lhs if isinstance(lhs, (tuple, list)) else (lhs,)
    m = lhs[0].shape[0]
    k, n = w.shape[-2:]
    assert sum(a.shape[1] for a in lhs) == k
    tm, tn = min(tm, m), min(tn, n)
    in_specs = [pl.BlockSpec((tm, a.shape[1]), lambda i, j: (i, 0)) for a in lhs]
    if w.ndim == 2:
        in_specs.append(pl.BlockSpec((k, tn), lambda i, j: (0, j)))
    else:
        tiles_per_group = m // w.shape[0] // tm
        in_specs.append(pl.BlockSpec((None, k, tn), lambda i, j: (i // tiles_per_group, 0, j)))
    args = [*lhs, w]
    tile = pl.BlockSpec((tm, tn), lambda i, j: (i, j))
    per_row = pl.BlockSpec((tm, 1), lambda i, j: (i, 0))
    if row_scale is not None:
        in_specs.append(per_row)
        args.append(row_scale)
    if residual is not None:
        in_specs.append(tile)
        args.append(residual)
    out_specs, out_shape, scratch = tile, jax.ShapeDtypeStruct((m, n), out_dtype), []
    if norm_gain is not None:
        in_specs.append(pl.BlockSpec((1, tn), lambda i, j: (0, j)))
        args.append(norm_gain.reshape(1, n))
        out_specs = [tile, tile, per_row]
        out_shape = [out_shape, jax.ShapeDtypeStruct((m, n), BF16),
                     jax.ShapeDtypeStruct((m, 1), F32)]
        scratch = [pltpu.VMEM((tm, 1), F32)]
    return pl.pallas_call(
        functools.partial(_mm_kernel, n_lhs=len(lhs), has_residual=residual is not None,
                          has_row_scale=row_scale is not None, emit_norm=norm_gain is not None,
                          n_cols=n),
        grid=(m // tm, n // tn),
        in_specs=in_specs,
        out_specs=out_specs,
        out_shape=out_shape,
        scratch_shapes=scratch,
        compiler_params=_params(2),
        name=name,
    )(*args)


def _attn_kernel(q_ref, k_ref, v_ref, base_ref, o_ref, bias_ref, *, qb, scale):
    seq = q_ref.shape[0]
    width = LEFT + qb

    @pl.when(pl.program_id(1) == 0)
    def _():
        toeplitz = pltpu.roll(jnp.broadcast_to(base_ref[...], (qb, base_ref.shape[-1])),
                              0, 1, stride=1, stride_axis=0)
        q_chunk = lax.broadcasted_iota(jnp.int32, (qb, width), 0) // CHUNK
        c_chunk = lax.broadcasted_iota(jnp.int32, (qb, width), 1) // CHUNK
        in_band = (c_chunk >= q_chunk) & (c_chunk <= q_chunk + LEFT_CHUNKS)
        bias_ref[...] = jnp.where(in_band, toeplitz[:, :width], NEG_INF)

    for i in range(seq // qb):
        q0 = i * qb
        k0 = max(0, q0 - LEFT)
        kw = q0 + qb - k0
        c0 = k0 - (q0 - LEFT)
        s = lax.dot_general(q_ref[q0:q0 + qb, :], k_ref[k0:k0 + kw, :], _NT,
                            preferred_element_type=F32)
        s = s * scale + bias_ref[:, c0:c0 + kw]
        m = jnp.max(s, axis=-1, keepdims=True)
        p = jnp.exp(s - m)
        l = jnp.sum(p, axis=-1, keepdims=True)
        o = jnp.dot(p.astype(BF16), v_ref[k0:k0 + kw, :], preferred_element_type=F32)
        o_ref[q0:q0 + qb, :] = (o / l).astype(o_ref.dtype)


def _attn_bias_base(rel_bias, qb):
    w = pl.next_power_of_2(LEFT + 2 * qb)
    j = jnp.arange(w)
    j = jnp.where(j < LEFT + qb, j, j - w)
    idx = jnp.clip(LEFT - j, -MAX_REL_DIST, MAX_REL_DIST) + MAX_REL_DIST
    return rel_bias[:, None, idx].astype(F32)


def _chunk_attention(proj, rel_bias, n_heads, qb=256):
    b, s, _ = proj.shape
    dh = ATTN_HEAD_DIM
    base = _attn_bias_base(rel_bias, qb)
    kern = functools.partial(_attn_kernel, qb=qb, scale=dh ** -0.5)
    return pl.pallas_call(
        kern,
        grid=(n_heads, b),
        in_specs=[pl.BlockSpec((None, s, dh), lambda h, bi: (bi, 0, h)),
                  pl.BlockSpec((None, s, dh), lambda h, bi: (bi, 0, n_heads + h)),
                  pl.BlockSpec((None, s, dh), lambda h, bi: (bi, 0, 2 * n_heads + h)),
                  pl.BlockSpec((None, 1, base.shape[-1]), lambda h, bi: (h, 0, 0))],
        out_specs=pl.BlockSpec((None, s, dh), lambda h, bi: (bi, 0, h)),
        out_shape=jax.ShapeDtypeStruct((b, s, n_heads * dh), BF16),
        scratch_shapes=[pltpu.VMEM((qb, LEFT + qb), F32)],
        compiler_params=_params(2),
        name="chunk_attention",
    )(proj, proj, proj, base)


def _ret_kernel(q_ref, k_ref, v_ref, g_ref, cos_ref, sin_ref, dec_ref, qd_ref, kd_ref, bd_ref,
                gnw_ref, o_ref, state_ref, *, rb, scale):
    seq = q_ref.shape[0]
    state_ref[...] = jnp.zeros_like(state_ref)
    half = q_ref.shape[1] // 2

    def body(n, carry):
        r0 = pl.multiple_of(n * rb, rb)
        rows = pl.ds(r0, rb)
        cos = cos_ref[rows, :]
        sin = sin_ref[rows, :]
        q = q_ref[rows, :].astype(F32)
        k = k_ref[rows, :].astype(F32)
        q = q * cos + pltpu.roll(q, half, 1) * sin
        k = (k * cos + pltpu.roll(k, half, 1) * sin) * scale
        v = v_ref[rows, :]
        a = lax.dot_general(q.astype(BF16), k.astype(BF16), _NT, preferred_element_type=F32)
        a = a * dec_ref[...]
        st = state_ref[...]
        y = jnp.dot(a.astype(BF16), v, preferred_element_type=F32)
        y = y + jnp.dot((q * qd_ref[...]).astype(BF16), st.astype(BF16),
                        preferred_element_type=F32)
        kd = (k * kd_ref[...]).astype(BF16)
        kv = lax.dot_general(kd, v, _TN, preferred_element_type=F32)
        state_ref[...] = st * bd_ref[...] + kv
        mu = jnp.mean(y, axis=-1, keepdims=True)
        yc = y - mu
        var = jnp.mean(yc * yc, axis=-1, keepdims=True)
        yn = yc * lax.rsqrt(var + EPS) * gnw_ref[...]
        g = g_ref[rows, :].astype(F32)
        o_ref[rows, :] = (g * (1.0 / (1.0 + jnp.exp(-g))) * yn).astype(o_ref.dtype)
        return carry

    lax.fori_loop(0, seq // rb, body, 0)


def _retention(proj, gn_w, n_heads, col0, rb=512):
    b, s, _ = proj.shape
    dk, dv = RET_QK_DIM, RET_V_DIM
    qk0 = col0 // dk
    v0 = (col0 + 2 * n_heads * dk) // dv
    inv_freq = 1.0 / (ROPE_BASE ** (jnp.arange(0, dk, 2, dtype=F32) / dk))
    ang = jnp.arange(s, dtype=F32)[:, None] * inv_freq[None, :]
    cos = jnp.concatenate([jnp.cos(ang), jnp.cos(ang)], axis=-1)
    sin = jnp.concatenate([-jnp.sin(ang), jnp.sin(ang)], axis=-1)
    log_gamma = jnp.log1p(-jnp.power(2.0, -5.0 - jnp.arange(n_heads, dtype=F32)))
    pos = jnp.arange(rb, dtype=F32)
    chunk_of = jnp.arange(rb) // CHUNK
    causal = (chunk_of[None, :] <= chunk_of[:, None]).astype(F32)
    dec = jnp.exp(log_gamma[:, None, None] * jnp.abs(pos[:, None] - pos[None, :])) * causal[None]
    qd = jnp.exp(log_gamma[:, None] * (pos + 1.0))[:, :, None]
    kd = jnp.exp(log_gamma[:, None] * (rb - 1.0 - pos))[:, :, None]
    bd = jnp.exp(log_gamma * rb)[:, None, None]
    kern = functools.partial(_ret_kernel, rb=rb, scale=dk ** -0.5)
    return pl.pallas_call(
        kern,
        grid=(b, n_heads),
        in_specs=[pl.BlockSpec((None, s, dk), lambda bi, h: (bi, 0, qk0 + h)),
                  pl.BlockSpec((None, s, dk), lambda bi, h: (bi, 0, qk0 + n_heads + h)),
                  pl.BlockSpec((None, s, dv), lambda bi, h: (bi, 0, v0 + h)),
                  pl.BlockSpec((None, s, dv), lambda bi, h: (bi, 0, v0 + n_heads + h)),
                  pl.BlockSpec((s, dk), lambda bi, h: (0, 0)),
                  pl.BlockSpec((s, dk), lambda bi, h: (0, 0)),
                  pl.BlockSpec((None, rb, rb), lambda bi, h: (h, 0, 0)),
                  pl.BlockSpec((None, rb, 1), lambda bi, h: (h, 0, 0)),
                  pl.BlockSpec((None, rb, 1), lambda bi, h: (h, 0, 0)),
                  pl.BlockSpec((None, 1, 1), lambda bi, h: (h, 0, 0)),
                  pl.BlockSpec((1, dv), lambda bi, h: (0, h))],
        out_specs=pl.BlockSpec((None, s, dv), lambda bi, h: (bi, 0, h)),
        out_shape=jax.ShapeDtypeStruct((b, s, n_heads * dv), BF16),
        scratch_shapes=[pltpu.VMEM((dk, dv), F32)],
        compiler_params=_params(2),
        name="retention",
    )(proj, proj, proj, proj, cos, sin, dec, qd, kd, bd, gn_w.reshape(1, -1))


def _fold_keys_kernel(wq_ref, k_ref, o_ref, wq16_ref):
    @pl.when(pl.program_id(1) == 0)
    def _():
        wq16_ref[...] = wq_ref[...].astype(BF16)

    o_ref[...] = lax.dot_general(wq16_ref[...], k_ref[...], _NT,
                                 preferred_element_type=F32).astype(o_ref.dtype)


def _fold_values_kernel(v_ref, wo_ref, o_ref, wo16_ref):
    @pl.when(pl.program_id(1) == 0)
    def _():
        wo16_ref[...] = wo_ref[...].astype(BF16)

    o_ref[...] = jnp.dot(v_ref[...], wo16_ref[...],
                         preferred_element_type=F32).astype(o_ref.dtype)


def _fold_memory(kv, wq, wo, n_heads):
    b, m, d2 = kv.shape
    d = d2 // 2
    dh = d // n_heads
    folded_k = pl.pallas_call(
        _fold_keys_kernel,
        grid=(n_heads, b),
        in_specs=[pl.BlockSpec((d, dh), lambda h, bi: (0, h)),
                  pl.BlockSpec((None, m, dh), lambda h, bi: (bi, 0, h))],
        out_specs=pl.BlockSpec((None, d, m), lambda h, bi: (bi, 0, h)),
        out_shape=jax.ShapeDtypeStruct((b, d, n_heads * m), BF16),
        scratch_shapes=[pltpu.VMEM((d, dh), BF16)],
        compiler_params=_params(2),
        name="fold_keys",
    )(wq, kv)
    folded_v = pl.pallas_call(
        _fold_values_kernel,
        grid=(n_heads, b),
        in_specs=[pl.BlockSpec((None, m, dh), lambda h, bi: (bi, 0, n_heads + h)),
                  pl.BlockSpec((dh, d), lambda h, bi: (h, 0))],
        out_specs=pl.BlockSpec((None, m, d), lambda h, bi: (bi, h, 0)),
        out_shape=jax.ShapeDtypeStruct((b, n_heads * m, d), BF16),
        scratch_shapes=[pltpu.VMEM((dh, d), BF16)],
        compiler_params=_params(2),
        name="fold_values",
    )(kv, wo)
    return folded_k, folded_v


def _mem_probs_kernel(x_ref, rstd_ref, fk_ref, o_ref, *, n_heads, scale):
    m = fk_ref.shape[1] // n_heads
    s = jnp.dot(x_ref[...], fk_ref[...], preferred_element_type=F32)
    s = s * (rstd_ref[...] * scale)
    for h in range(n_heads):
        sh = s[:, h * m:(h + 1) * m]
        mx = jnp.max(sh, axis=-1, keepdims=True)
        p = jnp.exp(sh - mx)
        o_ref[:, h * m:(h + 1) * m] = (p / jnp.sum(p, axis=-1, keepdims=True)).astype(o_ref.dtype)


def _mem_probs(x, rstd, folded_k, n_heads, tq=1024):
    b, s, d = x.shape
    hm = folded_k.shape[2]
    kern = functools.partial(_mem_probs_kernel, n_heads=n_heads, scale=(d // n_heads) ** -0.5)
    return pl.pallas_call(
        kern,
        grid=(b, s // tq),
        in_specs=[pl.BlockSpec((None, tq, d), lambda bi, i: (bi, i, 0)),
                  pl.BlockSpec((None, tq, 1), lambda bi, i: (bi, i, 0)),
                  pl.BlockSpec((None, d, hm), lambda bi, i: (bi, 0, 0))],
        out_specs=pl.BlockSpec((None, tq, hm), lambda bi, i: (bi, i, 0)),
        out_shape=jax.ShapeDtypeStruct((b, s, hm), BF16),
        compiler_params=_params(2),
        name="mem_probs",
    )(x, rstd, folded_k)


def _take_max(work, rows, exact):
    m = jnp.max(work, axis=0, keepdims=True)
    hit = work == m
    if not exact:
        return m, hit
    first = jnp.min(jnp.where(hit, rows, np.float32(work.shape[0])), axis=0, keepdims=True)
    return m, rows == first


def _top_rows(s, k, exact):
    rows = lax.broadcasted_iota(jnp.int32, s.shape, 0).astype(F32)
    work = s
    rank = jnp.full(s.shape, NOT_RANKED, F32)
    vals = []
    for r in range(k):
        m, sel = _take_max(work, rows, exact)
        rank = jnp.where(sel, np.float32(r), rank)
        work = jnp.where(sel, -jnp.inf, work)
        vals.append(m)
    taken = jnp.sum(jnp.where(rank < NOT_RANKED, 1.0, 0.0), axis=0, keepdims=True)
    return vals, rank, taken


def _candidate_cells(k):
    cells = []
    for ra in range(k):
        cells += [(ra, rb) for rb in range(k // (ra + 1))]
    single = [c for c in cells if k // (c[0] + 1) == 1]
    multi = [c for c in cells if c not in single]
    pad = (-len(multi)) % 8
    return multi + [None] * pad + single


def _route_kernel(q_ref, keys_ref, u_ref, v_ref, rank1_ref, e1_ref, cnt_ref, e0_ref, ub_ref, vb_ref):
    ub_ref[...] = u_ref[...].astype(ub_ref.dtype)
    vb_ref[...] = v_ref[...].astype(vb_ref.dtype)
    k = PEER_TOPK
    tl = q_ref.shape[0]
    cells = _candidate_cells(k)

    def route_group(g, exact):
        q = q_ref[g * LANES:(g + 1) * LANES, :]
        s0 = lax.dot_general(keys_ref[0].astype(BF16), q[:, :PEER_HALF], _NT,
                             preferred_element_type=F32)
        s1 = lax.dot_general(keys_ref[1].astype(BF16), q[:, PEER_HALF:], _NT,
                             preferred_element_type=F32)
        a, rank0, taken0 = _top_rows(s0, k, exact)
        b, rank1, taken1 = _top_rows(s1, k, exact)
        a_all, b_all = jnp.concatenate(a, axis=0), jnp.concatenate(b, axis=0)
        ea_all, eb_all = jnp.exp(a_all - a[0]), jnp.exp(b_all - b[0])
        neg = jnp.full_like(a[0], -jnp.inf)
        zero = jnp.zeros_like(a[0])
        n0, n1 = k, k // 2
        mid = cells[n0 + n1:len(cells) - k // 2]
        cand = jnp.concatenate(
            [a[0] + b_all, a[1] + b_all[:n1]]
            + [neg if c is None else a_all[c[0]:c[0] + 1] + b_all[c[1]:c[1] + 1] for c in mid]
            + [a_all[k // 2:] + b[0]], axis=0)
        wgt = jnp.concatenate(
            [ea_all[0:1] * eb_all, ea_all[1:2] * eb_all[:n1]]
            + [zero if c is None else ea_all[c[0]:c[0] + 1] * eb_all[c[1]:c[1] + 1] for c in mid]
            + [ea_all[k // 2:] * eb_all[0:1]], axis=0)
        rows = lax.broadcasted_iota(jnp.int32, cand.shape, 0).astype(F32)
        work = cand
        chosen = jnp.zeros(cand.shape, F32)
        for _ in range(k):
            _, sel = _take_max(work, rows, exact)
            chosen = jnp.where(sel, 1.0, chosen)
            work = jnp.where(sel, -jnp.inf, work)
        z = jnp.sum(chosen * wgt, axis=0, keepdims=True)
        cnt = jnp.zeros(s0.shape, F32)
        for ra in range(k):
            mine = [i for i, c in enumerate(cells) if c is not None and c[0] == ra]
            n_ra = jnp.sum(chosen[mine[0]:mine[-1] + 1], axis=0, keepdims=True)
            cnt = jnp.where(rank0 == np.float32(ra), n_ra, cnt)
        lanes = slice(g * LANES, (g + 1) * LANES)
        rank1_ref[:, lanes] = rank1.astype(rank1_ref.dtype)
        e1_ref[:, lanes] = (jnp.exp(s1 - b[0]) / z).astype(e1_ref.dtype)
        cnt_ref[:, lanes] = cnt
        e0_ref[:, lanes] = jnp.exp(s0 - a[0])
        taken2 = jnp.sum(chosen, axis=0, keepdims=True)
        return (jnp.abs(taken0 - k) + jnp.abs(taken1 - k) + jnp.abs(taken2 - k))

    groups = range(tl // LANES)
    tie = sum(route_group(g, exact=False) for g in groups)

    @pl.when(jnp.max(tie) > 0.0)
    def _():
        for g in groups:
            route_group(g, exact=True)


def _peer_route(qp, subkeys, u, v, tl=512):
    t = qp.shape[0]
    hp, _, nk, half = subkeys.shape
    n_exp, d = u.shape
    steps = (t // tl) * hp
    rows = n_exp // steps
    assert rows * steps == n_exp
    out = lambda dt: jax.ShapeDtypeStruct((hp, nk, t), dt)
    ospec = pl.BlockSpec((None, nk, tl), lambda i, h: (h, 0, i))
    wspec = pl.BlockSpec((rows, d), lambda i, h: (i * hp + h, 0))
    wout = jax.ShapeDtypeStruct((n_exp, d), BF16)
    return pl.pallas_call(
        _route_kernel,
        grid=(t // tl, hp),
        in_specs=[pl.BlockSpec((tl, 2 * half), lambda i, h: (i, h)),
                  pl.BlockSpec((None, 2, nk, half), lambda i, h: (h, 0, 0, 0)),
                  wspec, wspec],
        out_specs=[ospec] * 4 + [wspec] * 2,
        out_shape=[out(BF16), out(BF16), out(F32), out(F32), wout, wout],
        compiler_params=_params(2),
        name="peer_route",
    )(qp, subkeys, u, v)


def _peer_kernel(x_ref, rstd_ref, u_ref, v_ref, rank1_ref, e1_ref, cnt_ref, e0_ref, o_ref, acc_ref,
                 xt_ref, *stage_refs, sub, n_pieces):
    e = pl.program_id(1)
    n_heads, nk, tm = rank1_ref.shape
    te, d = u_ref.shape
    n_sub = te // sub
    rows_per_sub = sub // nk
    kc = d // n_pieces
    rows_per_step = te // nk
    row0 = (e % (8 // rows_per_step)) * rows_per_step
    act_refs, wt_refs = stage_refs[:n_sub], stage_refs[n_sub:]

    @pl.when(e == 0)
    def _():
        acc_ref[...] = jnp.zeros_like(acc_ref)
        xt_ref[...] = x_ref[...].T

    def up_piece(j, p):
        part = jnp.dot(u_ref[j * sub:(j + 1) * sub, p * kc:(p + 1) * kc],
                       xt_ref[p * kc:(p + 1) * kc, :], preferred_element_type=F32)
        if p == 0:
            act_refs[j][...] = part
        else:
            act_refs[j][...] += part

    def gate_piece(j, p):
        ii, g = divmod(p, n_pieces // rows_per_sub)
        tg = tm // (n_pieces // rows_per_sub)
        lanes = slice(g * tg, (g + 1) * tg)
        row = j * rows_per_sub + ii
        gate = None
        for h in range(n_heads):
            cnt = cnt_ref[h, pl.ds(row0 + row, 1), :][:, lanes].astype(BF16)
            e0 = e0_ref[h, pl.ds(row0 + row, 1), :][:, lanes].astype(BF16)
            zero = jnp.zeros((), BF16)
            term = e0 * jnp.where(rank1_ref[h, :, lanes] < cnt, e1_ref[h, :, lanes], zero)
            gate = term if gate is None else gate + term
        act = act_refs[j][ii * nk:(ii + 1) * nk, lanes] * rstd_ref[:, lanes]
        act = 0.5 * act * (1.0 + lax.erf(act * np.float32(1.0 / np.sqrt(2.0))))
        wt_refs[j][lanes, ii * nk:(ii + 1) * nk] = (gate * act.astype(BF16)).T

    def down_piece(j, p):
        cols = slice(p * kc, (p + 1) * kc)
        acc_ref[:, cols] += jnp.dot(wt_refs[j][...], v_ref[j * sub:(j + 1) * sub, cols],
                                    preferred_element_type=F32)

    for step in range(n_sub + 2):
        for p in range(n_pieces):
            if step < n_sub:
                up_piece(step, p)
            if 0 <= step - 1 < n_sub:
                gate_piece(step - 1, p)
            if 0 <= step - 2 < n_sub:
                down_piece(step - 2, p)

    @pl.when(e == pl.num_programs(1) - 1)
    def _():
        o_ref[...] = acc_ref[...].astype(o_ref.dtype)


def _peer_experts(x, rstd, u, v, rank1, e1, cnt, e0, tm=512, te=512, sub=256, n_pieces=4):
    t, d = x.shape
    n_exp = u.shape[0]
    hp, nk, _ = rank1.shape
    rows = te // nk
    tab = pl.BlockSpec((hp, nk, tm), lambda i, e: (0, 0, i))
    row_tab = pl.BlockSpec((hp, 8, tm), lambda i, e: (0, (e * rows) // 8, i))
    return pl.pallas_call(
        functools.partial(_peer_kernel, sub=sub, n_pieces=n_pieces),
        grid=(t // tm, n_exp // te),
        in_specs=[pl.BlockSpec((tm, d), lambda i, e: (i, 0)),
                  pl.BlockSpec((1, tm), lambda i, e: (0, i)),
                  pl.BlockSpec((te, d), lambda i, e: (e, 0)),
                  pl.BlockSpec((te, d), lambda i, e: (e, 0)),
                  tab, tab, row_tab, row_tab],
        out_specs=pl.BlockSpec((tm, d), lambda i, e: (i, 0)),
        out_shape=jax.ShapeDtypeStruct((t, d), BF16),
        scratch_shapes=([pltpu.VMEM((tm, d), F32), pltpu.VMEM((d, tm), BF16)]
                        + [pltpu.VMEM((sub, tm), F32)] * (te // sub)
                        + [pltpu.VMEM((tm, sub), BF16)] * (te // sub)),
        compiler_params=_params(2),
        name="peer_experts",
    )(x, rstd, u, v, rank1, e1, cnt, e0)


def kernel(x, mem, norm1_w, w_in, attn_rel_bias, ret_gn_w, w_out, norm2_w, mem_norm_w, xattn_wq,
           xattn_wkv, xattn_wo, norm3_w, peer_wq, peer_subkeys, peer_u, peer_v, final_norm_w):
    b, s, d = x.shape
    t = b * s
    depth = w_in.shape[0]
    n_attn_heads = attn_rel_bias.shape[1]
    attn_width = n_attn_heads * ATTN_HEAD_DIM
    n_ret_heads = ret_gn_w.shape[1] // RET_V_DIM
    h = x.reshape(t, d)
    xn = _rmsnorm(h, norm1_w[0], BF16)
    for l in range(depth):
        proj = _matmul(xn, w_in[l], BF16, tn=768, name="in_proj").reshape(b, s, -1)
        a_out = _chunk_attention(proj, attn_rel_bias[l], n_attn_heads)
        r_out = _retention(proj, ret_gn_w[l], n_ret_heads, 3 * attn_width)
        h, hg, rstd = _matmul((a_out.reshape(t, -1), r_out.reshape(t, -1)), w_out[l], F32,
                              residual=h, norm_gain=norm2_w[l], name="out_proj")

        mem_n = _rmsnorm(mem.reshape(-1, d), mem_norm_w[l], BF16)
        kv = _matmul(mem_n, xattn_wkv[l], BF16, name="mem_kv").reshape(b, -1, 2 * d)
        folded_k, folded_v = _fold_memory(kv, xattn_wq[l], xattn_wo[l], XATTN_HEADS)
        probs = _mem_probs(hg.reshape(b, s, d), rstd.reshape(b, s, 1), folded_k, XATTN_HEADS)
        h, hg, rstd = _matmul(probs.reshape(t, -1), folded_v, F32, residual=h,
                              norm_gain=norm3_w[l], tn=1024, name="xattn_o")

        qp = _matmul(hg, peer_wq[l], BF16, row_scale=rstd, name="peer_q")
        rank1, e1, cnt, e0, u_b, v_b = _peer_route(qp, peer_subkeys[l], peer_u[l], peer_v[l])
        delta = _peer_experts(hg, rstd.reshape(1, t), u_b, v_b, rank1, e1, cnt, e0)
        if l + 1 < depth:
            h = h + delta.astype(F32)
            xn = _rmsnorm(h, norm1_w[l + 1], BF16)
    return _add_rmsnorm(h, delta, final_norm_w, F32).reshape(b, s, d)
```

```python
import functools

import numpy as np
import jax
import jax.numpy as jnp
from jax import lax
from jax.experimental import pallas as pl
from jax.experimental.pallas import tpu as pltpu

F32 = jnp.float32
BF16 = jnp.bfloat16

CHUNK = 64
LEFT_CHUNKS = 8
LEFT = LEFT_CHUNKS * CHUNK
ATTN_HEAD_DIM = 128
MAX_REL_DIST = 256
RET_V_DIM = 256
RET_QK_DIM = 128
ROPE_BASE = 10000.0
XATTN_HEADS = 4
PEER_HALF = 128
PEER_TOPK = 16
EPS = 1e-6
NEG_INF = -1e30
NOT_RANKED = 1e9

VMEM_LIMIT_BYTES = 56 * 1024 * 1024
LANES = 128
SUBLANES = 8

_NT = (((1,), (1,)), ((), ()))
_TN = (((0,), (0,)), ((), ()))


def _params(n_grid_dims):
    return pltpu.CompilerParams(
        dimension_semantics=("arbitrary",) * n_grid_dims,
        vmem_limit_bytes=VMEM_LIMIT_BYTES)


def _rmsnorm_kernel(x_ref, w_ref, o_ref):
    x = x_ref[...]
    ms = jnp.mean(x * x, axis=-1, keepdims=True)
    o_ref[...] = (x * lax.rsqrt(ms + EPS) * w_ref[...]).astype(o_ref.dtype)


def _rmsnorm(x, w, out_dtype, rows=256):
    m, d = x.shape
    return pl.pallas_call(
        _rmsnorm_kernel,
        grid=(m // rows,),
        in_specs=[pl.BlockSpec((rows, d), lambda i: (i, 0)),
                  pl.BlockSpec((1, d), lambda i: (0, 0))],
        out_specs=pl.BlockSpec((rows, d), lambda i: (i, 0)),
        out_shape=jax.ShapeDtypeStruct((m, d), out_dtype),
        compiler_params=_params(1),
        name="rmsnorm",
    )(x, w.reshape(1, d))


def _add_rmsnorm_kernel(x_ref, y_ref, w_ref, o_ref):
    x = x_ref[...] + y_ref[...].astype(F32)
    ms = jnp.mean(x * x, axis=-1, keepdims=True)
    o_ref[...] = (x * lax.rsqrt(ms + EPS) * w_ref[...]).astype(o_ref.dtype)


def _add_rmsnorm(x, y, w, out_dtype, rows=256):
    m, d = x.shape
    return pl.pallas_call(
        _add_rmsnorm_kernel,
        grid=(m // rows,),
        in_specs=[pl.BlockSpec((rows, d), lambda i: (i, 0)),
                  pl.BlockSpec((rows, d), lambda i: (i, 0)),
                  pl.BlockSpec((1, d), lambda i: (0, 0))],
        out_specs=pl.BlockSpec((rows, d), lambda i: (i, 0)),
        out_shape=jax.ShapeDtypeStruct((m, d), out_dtype),
        compiler_params=_params(1),
        name="add_rmsnorm",
    )(x, y, w.reshape(1, d))


def _mm_kernel(*refs, n_lhs, has_residual, has_row_scale, emit_norm, n_cols):
    lhs_refs, w_ref = refs[:n_lhs], refs[n_lhs]
    pos = n_lhs + 1
    acc, k0 = None, 0
    for a_ref in lhs_refs:
        k1 = k0 + a_ref.shape[1]
        w_blk = w_ref[k0:k1, :]
        if w_blk.dtype != BF16:
            w_blk = w_blk.astype(BF16)
        part = jnp.dot(a_ref[...], w_blk, preferred_element_type=F32)
        acc = part if acc is None else acc + part
        k0 = k1
    if has_row_scale:
        acc = acc * refs[pos][...]
        pos += 1
    if has_residual:
        acc = refs[pos][...] + acc
        pos += 1
    if not emit_norm:
        o_ref = refs[pos]
        o_ref[...] = acc.astype(o_ref.dtype)
        return
    gain_ref, o_ref, scaled_ref, rstd_ref, ssq_ref = refs[pos:pos + 5]
    j = pl.program_id(1)
    o_ref[...] = acc.astype(o_ref.dtype)
    scaled_ref[...] = (acc * gain_ref[...]).astype(scaled_ref.dtype)
    row_ssq = jnp.sum(acc * acc, axis=-1, keepdims=True)

    @pl.when(j == 0)
    def _():
        ssq_ref[...] = row_ssq

    @pl.when(j > 0)
    def _():
        ssq_ref[...] += row_ssq

    @pl.when(j == pl.num_programs(1) - 1)
    def _():
        rstd_ref[...] = lax.rsqrt(ssq_ref[...] / n_cols + EPS)


def _matmul(lhs, w, out_dtype, residual=None, row_scale=None, norm_gain=None, tm=1024, tn=512,
            name="matmul"):
    lhs = lhs if isinstance(lhs, (tuple, list)) else (lhs,)
    m = lhs[0].shape[0]
    k, n = w.shape[-2:]
    assert sum(a.shape[1] for a in lhs) == k
    tm, tn = min(tm, m), min(tn, n)
    in_specs = [pl.BlockSpec((tm, a.shape[1]), lambda i, j: (i, 0)) for a in lhs]
    if w.ndim == 2:
        in_specs.append(pl.BlockSpec((k, tn), lambda i, j: (0, j)))
    else:
        tiles_per_group = m // w.shape[0] // tm
        in_specs.append(pl.BlockSpec((None, k, tn), lambda i, j: (i // tiles_per_group, 0, j)))
    args = [*lhs, w]
    tile = pl.BlockSpec((tm, tn), lambda i, j: (i, j))
    per_row = pl.BlockSpec((tm, 1), lambda i, j: (i, 0))
    if row_scale is not None:
        in_specs.append(per_row)
        args.append(row_scale)
    if residual is not None:
        in_specs.append(tile)
        args.append(residual)
    out_specs, out_shape, scratch = tile, jax.ShapeDtypeStruct((m, n), out_dtype), []
    if norm_gain is not None:
        in_specs.append(pl.BlockSpec((1, tn), lambda i, j: (0, j)))
        args.append(norm_gain.reshape(1, n))
        out_specs = [tile, tile, per_row]
        out_shape = [out_shape, jax.ShapeDtypeStruct((m, n), BF16),
                     jax.ShapeDtypeStruct((m, 1), F32)]
        scratch = [pltpu.VMEM((tm, 1), F32)]
    return pl.pallas_call(
        functools.partial(_mm_kernel, n_lhs=len(lhs), has_residual=residual is not None,
                          has_row_scale=row_scale is not None, emit_norm=norm_gain is not None,
                          n_cols=n),
        grid=(m // tm, n // tn),
        in_specs=in_specs,
        out_specs=out_specs,
        out_shape=out_shape,
        scratch_shapes=scratch,
        compiler_params=_params(2),
        name=name,
    )(*args)


def _attn_kernel(q_ref, k_ref, v_ref, base_ref, o_ref, bias_ref, *, qb, scale):
    seq = q_ref.shape[0]
    width = LEFT + qb

    @pl.when(pl.program_id(1) == 0)
    def _():
        toeplitz = pltpu.roll(jnp.broadcast_to(base_ref[...], (qb, base_ref.shape[-1])),
                              0, 1, stride=1, stride_axis=0)
        q_chunk = lax.broadcasted_iota(jnp.int32, (qb, width), 0) // CHUNK
        c_chunk = lax.broadcasted_iota(jnp.int32, (qb, width), 1) // CHUNK
        in_band = (c_chunk >= q_chunk) & (c_chunk <= q_chunk + LEFT_CHUNKS)
        bias_ref[...] = jnp.where(in_band, toeplitz[:, :width], NEG_INF)

    for i in range(seq // qb):
        q0 = i * qb
        k0 = max(0, q0 - LEFT)
        kw = q0 + qb - k0
        c0 = k0 - (q0 - LEFT)
        s = lax.dot_general(q_ref[q0:q0 + qb, :], k_ref[k0:k0 + kw, :], _NT,
                            preferred_element_type=F32)
        s = s * scale + bias_ref[:, c0:c0 + kw]
        m = jnp.max(s, axis=-1, keepdims=True)
        p = jnp.exp(s - m)
        l = jnp.sum(p, axis=-1, keepdims=True)
        o = jnp.dot(p.astype(BF16), v_ref[k0:k0 + kw, :], preferred_element_type=F32)
        o_ref[q0:q0 + qb, :] = (o / l).astype(o_ref.dtype)


def _attn_bias_base(rel_bias, qb):
    w = pl.next_power_of_2(LEFT + 2 * qb)
    j = jnp.arange(w)
    j = jnp.where(j < LEFT + qb, j, j - w)
    idx = jnp.clip(LEFT - j, -MAX_REL_DIST, MAX_REL_DIST) + MAX_REL_DIST
    return rel_bias[:, None, idx].astype(F32)


def _chunk_attention(proj, rel_bias, n_heads, qb=256):
    b, s, _ = proj.shape
    dh = ATTN_HEAD_DIM
    base = _attn_bias_base(rel_bias, qb)
    kern = functools.partial(_attn_kernel, qb=qb, scale=dh ** -0.5)
    return pl.pallas_call(
        kern,
        grid=(n_heads, b),
        in_specs=[pl.BlockSpec((None, s, dh), lambda h, bi: (bi, 0, h)),
                  pl.BlockSpec((None, s, dh), lambda h, bi: (bi, 0, n_heads + h)),
                  pl.BlockSpec((None, s, dh), lambda h, bi: (bi, 0, 2 * n_heads + h)),
                  pl.BlockSpec((None, 1, base.shape[-1]), lambda h, bi: (h, 0, 0))],
        out_specs=pl.BlockSpec((None, s, dh), lambda h, bi: (bi, 0, h)),
        out_shape=jax.ShapeDtypeStruct((b, s, n_heads * dh), BF16),
        scratch_shapes=[pltpu.VMEM((qb, LEFT + qb), F32)],
        compiler_params=_params(2),
        name="chunk_attention",
    )(proj, proj, proj, base)


def _ret_kernel(q_ref, k_ref, v_ref, g_ref, cos_ref, sin_ref, dec_ref, qd_ref, kd_ref, bd_ref,
                gnw_ref, o_ref, state_ref, *, rb, scale):
    seq = q_ref.shape[0]
    state_ref[...] = jnp.zeros_like(state_ref)
    half = q_ref.shape[1] // 2

    def body(n, carry):
        r0 = pl.multiple_of(n * rb, rb)
        rows = pl.ds(r0, rb)
        cos = cos_ref[rows, :]
        sin = sin_ref[rows, :]
        q = q_ref[rows, :].astype(F32)
        k = k_ref[rows, :].astype(F32)
        q = q * cos + pltpu.roll(q, half, 1) * sin
        k = (k * cos + pltpu.roll(k, half, 1) * sin) * scale
        v = v_ref[rows, :]
        a = lax.dot_general(q.astype(BF16), k.astype(BF16), _NT, preferred_element_type=F32)
        a = a * dec_ref[...]
        st = state_ref[...]
        y = jnp.dot(a.astype(BF16), v, preferred_element_type=F32)
        y = y + jnp.dot((q * qd_ref[...]).astype(BF16), st.astype(BF16),
                        preferred_element_type=F32)
        kd = (k * kd_ref[...]).astype(BF16)
        kv = lax.dot_general(kd, v, _TN, preferred_element_type=F32)
        state_ref[...] = st * bd_ref[...] + kv
        mu = jnp.mean(y, axis=-1, keepdims=True)
        yc = y - mu
        var = jnp.mean(yc * yc, axis=-1, keepdims=True)
        yn = yc * lax.rsqrt(var + EPS) * gnw_ref[...]
        g = g_ref[rows, :].astype(F32)
        o_ref[rows, :] = (g * (1.0 / (1.0 + jnp.exp(-g))) * yn).astype(o_ref.dtype)
        return carry

    lax.fori_loop(0, seq // rb, body, 0)


def _retention(proj, gn_w, n_heads, col0, rb=512):
    b, s, _ = proj.shape
    dk, dv = RET_QK_DIM, RET_V_DIM
    qk0 = col0 // dk
    v0 = (col0 + 2 * n_heads * dk) // dv
    inv_freq = 1.0 / (ROPE_BASE ** (jnp.arange(0, dk, 2, dtype=F32) / dk))
    ang = jnp.arange(s, dtype=F32)[:, None] * inv_freq[None, :]
    cos = jnp.concatenate([jnp.cos(ang), jnp.cos(ang)], axis=-1)
    sin = jnp.concatenate([-jnp.sin(ang), jnp.sin(ang)], axis=-1)
    log_gamma = jnp.log1p(-jnp.power(2.0, -5.0 - jnp.arange(n_heads, dtype=F32)))
    pos = jnp.arange(rb, dtype=F32)
    chunk_of = jnp.arange(rb) // CHUNK
    causal = (chunk_of[None, :] <= chunk_of[:, None]).astype(F32)
    dec = jnp.exp(log_gamma[:, None, None] * jnp.abs(pos[:, None] - pos[None, :])) * causal[None]
    qd = jnp.exp(log_gamma[:, None] * (pos + 1.0))[:, :, None]
    kd = jnp.exp(log_gamma[:, None] * (rb - 1.0 - pos))[:, :, None]
    bd = jnp.exp(log_gamma * rb)[:, None, None]
    kern = functools.partial(_ret_kernel, rb=rb, scale=dk ** -0.5)
    return pl.pallas_call(
        kern,
        grid=(b, n_heads),
        in_specs=[pl.BlockSpec((None, s, dk), lambda bi, h: (bi, 0, qk0 + h)),
                  pl.BlockSpec((None, s, dk), lambda bi, h: (bi, 0, qk0 + n_heads + h)),
                  pl.BlockSpec((None, s, dv), lambda bi, h: (bi, 0, v0 + h)),
                  pl.BlockSpec((None, s, dv), lambda bi, h: (bi, 0, v0 + n_heads + h)),
                  pl.BlockSpec((s, dk), lambda bi, h: (0, 0)),
                  pl.BlockSpec((s, dk), lambda bi, h: (0, 0)),
                  pl.BlockSpec((None, rb, rb), lambda bi, h: (h, 0, 0)),
                  pl.BlockSpec((None, rb, 1), lambda bi, h: (h, 0, 0)),
                  pl.BlockSpec((None, rb, 1), lambda bi, h: (h, 0, 0)),
                  pl.BlockSpec((None, 1, 1), lambda bi, h: (h, 0, 0)),
                  pl.BlockSpec((1, dv), lambda bi, h: (0, h))],
        out_specs=pl.BlockSpec((None, s, dv), lambda bi, h: (bi, 0, h)),
        out_shape=jax.ShapeDtypeStruct((b, s, n_heads * dv), BF16),
        scratch_shapes=[pltpu.VMEM((dk, dv), F32)],
        compiler_params=_params(2),
        name="retention",
    )(proj, proj, proj, proj, cos, sin, dec, qd, kd, bd, gn_w.reshape(1, -1))


def _fold_keys_kernel(wq_ref, k_ref, o_ref, wq16_ref):
    @pl.when(pl.program_id(1) == 0)
    def _():
        wq16_ref[...] = wq_ref[...].astype(BF16)

    o_ref[...] = lax.dot_general(wq16_ref[...], k_ref[...], _NT,
                                 preferred_element_type=F32).astype(o_ref.dtype)


def _fold_values_kernel(v_ref, wo_ref, o_ref, wo16_ref):
    @pl.when(pl.program_id(1) == 0)
    def _():
        wo16_ref[...] = wo_ref[...].astype(BF16)

    o_ref[...] = jnp.dot(v_ref[...], wo16_ref[...],
                         preferred_element_type=F32).astype(o_ref.dtype)


def _fold_memory(kv, wq, wo, n_heads):
    b, m, d2 = kv.shape
    d = d2 // 2
    dh = d // n_heads
    folded_k = pl.pallas_call(
        _fold_keys_kernel,
        grid=(n_heads, b),
        in_specs=[pl.BlockSpec((d, dh), lambda h, bi: (0, h)),
                  pl.BlockSpec((None, m, dh), lambda h, bi: (bi, 0, h))],
        out_specs=pl.BlockSpec((None, d, m), lambda h, bi: (bi, 0, h)),
        out_shape=jax.ShapeDtypeStruct((b, d, n_heads * m), BF16),
        scratch_shapes=[pltpu.VMEM((d, dh), BF16)],
        compiler_params=_params(2),
        name="fold_keys",
    )(wq, kv)
    folded_v = pl.pallas_call(
        _fold_values_kernel,
        grid=(n_heads, b),
        in_specs=[pl.BlockSpec((None, m, dh), lambda h, bi: (bi, 0, n_heads + h)),
                  pl.BlockSpec((dh, d), lambda h, bi: (h, 0))],
        out_specs=pl.BlockSpec((None, m, d), lambda h, bi: (bi, h, 0)),
        out_shape=jax.ShapeDtypeStruct((b, n_heads * m, d), BF16),
        scratch_shapes=[pltpu.VMEM((dh, d), BF16)],
        compiler_params=_params(2),
        name="fold_values",
    )(kv, wo)
    return folded_k, folded_v


def _mem_probs_kernel(x_ref, rstd_ref, fk_ref, o_ref, *, n_heads, scale):
    m = fk_ref.shape[1] // n_heads
    s = jnp.dot(x_ref[...], fk_ref[...], preferred_element_type=F32)
    s = s * (rstd_ref[...] * scale)
    for h in range(n_heads):
        sh = s[:, h * m:(h + 1) * m]
        mx = jnp.max(sh, axis=-1, keepdims=True)
        p = jnp.exp(sh - mx)
        o_ref[:, h * m:(h + 1) * m] = (p / jnp.sum(p, axis=-1, keepdims=True)).astype(o_ref.dtype)


def _mem_probs(x, rstd, folded_k, n_heads, tq=1024):
    b, s, d = x.shape
    hm = folded_k.shape[2]
    kern = functools.partial(_mem_probs_kernel, n_heads=n_heads, scale=(d // n_heads) ** -0.5)
    return pl.pallas_call(
        kern,
        grid=(b, s // tq),
        in_specs=[pl.BlockSpec((None, tq, d), lambda bi, i: (bi, i, 0)),
                  pl.BlockSpec((None, tq, 1), lambda bi, i: (bi, i, 0)),
                  pl.BlockSpec((None, d, hm), lambda bi, i: (bi, 0, 0))],
        out_specs=pl.BlockSpec((None, tq, hm), lambda bi, i: (bi, i, 0)),
        out_shape=jax.ShapeDtypeStruct((b, s, hm), BF16),
        compiler_params=_params(2),
        name="mem_probs",
    )(x, rstd, folded_k)


def _take_max(work, rows, exact):
    m = jnp.max(work, axis=0, keepdims=True)
    hit = work == m
    if not exact:
        return m, hit
    first = jnp.min(jnp.where(hit, rows, np.float32(work.shape[0])), axis=0, keepdims=True)
    return m, rows == first


def _top_rows(s, k, exact):
    rows = lax.broadcasted_iota(jnp.int32, s.shape, 0).astype(F32)
    work = s
    rank = jnp.full(s.shape, NOT_RANKED, F32)
    vals = []
    for r in range(k):
        m, sel = _take_max(work, rows, exact)
        rank = jnp.where(sel, np.float32(r), rank)
        work = jnp.where(sel, -jnp.inf, work)
        vals.append(m)
    taken = jnp.sum(jnp.where(rank < NOT_RANKED, 1.0, 0.0), axis=0, keepdims=True)
    return vals, rank, taken


def _candidate_cells(k):
    cells = []
    for ra in range(k):
        cells += [(ra, rb) for rb in range(k // (ra + 1))]
    single = [c for c in cells if k // (c[0] + 1) == 1]
    multi = [c for c in cells if c not in single]
    pad = (-len(multi)) % 8
    return multi + [None] * pad + single


def _route_kernel(q_ref, keys_ref, u_ref, v_ref, rank1_ref, e1_ref, cnt_ref, e0_ref, ub_ref, vb_ref):
    ub_ref[...] = u_ref[...].astype(ub_ref.dtype)
    vb_ref[...] = v_ref[...].astype(vb_ref.dtype)
    k = PEER_TOPK
    tl = q_ref.shape[0]
    cells = _candidate_cells(k)

    def route_group(g, exact):
        q = q_ref[g * LANES:(g + 1) * LANES, :]
        s0 = lax.dot_general(keys_ref[0].astype(BF16), q[:, :PEER_HALF], _NT,
                             preferred_element_type=F32)
        s1 = lax.dot_general(keys_ref[1].astype(BF16), q[:, PEER_HALF:], _NT,
                             preferred_element_type=F32)
        a, rank0, taken0 = _top_rows(s0, k, exact)
        b, rank1, taken1 = _top_rows(s1, k, exact)
        a_all, b_all = jnp.concatenate(a, axis=0), jnp.concatenate(b, axis=0)
        ea_all, eb_all = jnp.exp(a_all - a[0]), jnp.exp(b_all - b[0])
        neg = jnp.full_like(a[0], -jnp.inf)
        zero = jnp.zeros_like(a[0])
        n0, n1 = k, k // 2
        mid = cells[n0 + n1:len(cells) - k // 2]
        cand = jnp.concatenate(
            [a[0] + b_all, a[1] + b_all[:n1]]
            + [neg if c is None else a_all[c[0]:c[0] + 1] + b_all[c[1]:c[1] + 1] for c in mid]
            + [a_all[k // 2:] + b[0]], axis=0)
        wgt = jnp.concatenate(
            [ea_all[0:1] * eb_all, ea_all[1:2] * eb_all[:n1]]
            + [zero if c is None else ea_all[c[0]:c[0] + 1] * eb_all[c[1]:c[1] + 1] for c in mid]
            + [ea_all[k // 2:] * eb_all[0:1]], axis=0)
        rows = lax.broadcasted_iota(jnp.int32, cand.shape, 0).astype(F32)
        work = cand
        chosen = jnp.zeros(cand.shape, F32)
        for _ in range(k):
            _, sel = _take_max(work, rows, exact)
            chosen = jnp.where(sel, 1.0, chosen)
            work = jnp.where(sel, -jnp.inf, work)
        z = jnp.sum(chosen * wgt, axis=0, keepdims=True)
        cnt = jnp.zeros(s0.shape, F32)
        for ra in range(k):
            mine = [i for i, c in enumerate(cells) if c is not None and c[0] == ra]
            n_ra = jnp.sum(chosen[mine[0]:mine[-1] + 1], axis=0, keepdims=True)
            cnt = jnp.where(rank0 == np.float32(ra), n_ra, cnt)
        lanes = slice(g * LANES, (g + 1) * LANES)
        rank1_ref[:, lanes] = rank1.astype(rank1_ref.dtype)
        e1_ref[:, lanes] = (jnp.exp(s1 - b[0]) / z).astype(e1_ref.dtype)
        cnt_ref[:, lanes] = cnt
        e0_ref[:, lanes] = jnp.exp(s0 - a[0])
        taken2 = jnp.sum(chosen, axis=0, keepdims=True)
        return (jnp.abs(taken0 - k) + jnp.abs(taken1 - k) + jnp.abs(taken2 - k))

    groups = range(tl // LANES)
    tie = sum(route_group(g, exact=False) for g in groups)

    @pl.when(jnp.max(tie) > 0.0)
    def _():
        for g in groups:
            route_group(g, exact=True)


def _peer_route(qp, subkeys, u, v, tl=512):
    t = qp.shape[0]
    hp, _, nk, half = subkeys.shape
    n_exp, d = u.shape
    steps = (t // tl) * hp
    rows = n_exp // steps
    assert rows * steps == n_exp
    out = lambda dt: jax.ShapeDtypeStruct((hp, nk, t), dt)
    ospec = pl.BlockSpec((None, nk, tl), lambda i, h: (h, 0, i))
    wspec = pl.BlockSpec((rows, d), lambda i, h: (i * hp + h, 0))
    wout = jax.ShapeDtypeStruct((n_exp, d), BF16)
    return pl.pallas_call(
        _route_kernel,
        grid=(t // tl, hp),
        in_specs=[pl.BlockSpec((tl, 2 * half), lambda i, h: (i, h)),
                  pl.BlockSpec((None, 2, nk, half), lambda i, h: (h, 0, 0, 0)),
                  wspec, wspec],
        out_specs=[ospec] * 4 + [wspec] * 2,
        out_shape=[out(BF16), out(BF16), out(F32), out(F32), wout, wout],
        compiler_params=_params(2),
        name="peer_route",
    )(qp, subkeys, u, v)


def _peer_kernel(x_ref, rstd_ref, u_ref, v_ref, rank1_ref, e1_ref, cnt_ref, e0_ref, o_ref, acc_ref,
                 xt_ref, *stage_refs, sub, n_pieces):
    e = pl.program_id(1)
    n_heads, nk, tm = rank1_ref.shape
    te, d = u_ref.shape
    n_sub = te // sub
    rows_per_sub = sub // nk
    kc = d // n_pieces
    rows_per_step = te // nk
    row0 = (e % (SUBLANES // rows_per_step)) * rows_per_step
    act_refs, wt_ref = stage_refs[:n_sub], stage_refs[n_sub]

    @pl.when(e == 0)
    def _():
        acc_ref[...] = jnp.zeros_like(acc_ref)
        xt_ref[...] = x_ref[...].T

    def up_piece(j, p):
        part = jnp.dot(u_ref[j * sub:(j + 1) * sub, p * kc:(p + 1) * kc],
                       xt_ref[p * kc:(p + 1) * kc, :], preferred_element_type=F32)
        if p == 0:
            act_refs[j][...] = part
        else:
            act_refs[j][...] += part

    def gate_piece(j, p):
        ii, g = divmod(p, n_pieces // rows_per_sub)
        tg = tm // (n_pieces // rows_per_sub)
        lanes = slice(g * tg, (g + 1) * tg)
        row = j * rows_per_sub + ii
        gate = None
        for h in range(n_heads):
            cnt = cnt_ref[h, pl.ds(row0 + row, 1), :][:, lanes].astype(BF16)
            e0 = e0_ref[h, pl.ds(row0 + row, 1), :][:, lanes].astype(BF16)
            zero = jnp.zeros((), BF16)
            term = e0 * jnp.where(rank1_ref[h, :, lanes] < cnt, e1_ref[h, :, lanes], zero)
            gate = term if gate is None else gate + term
        act = act_refs[j][ii * nk:(ii + 1) * nk, lanes] * rstd_ref[:, lanes]
        act = 0.5 * act * (1.0 + lax.erf(act * np.float32(1.0 / np.sqrt(2.0))))
        col0 = j * sub + ii * nk
        wt_ref[lanes, col0:col0 + nk] = (gate * act.astype(BF16)).T

    def down_piece(p):
        cols = slice(p * kc, (p + 1) * kc)
        acc_ref[:, cols] += jnp.dot(wt_ref[...], v_ref[:, cols], preferred_element_type=F32)

    for step in range(n_sub + 1):
        for p in range(n_pieces):
            if step < n_sub:
                up_piece(step, p)
            if step >= 1:
                gate_piece(step - 1, p)
    for p in range(n_pieces):
        down_piece(p)

    @pl.when(e == pl.num_programs(1) - 1)
    def _():
        o_ref[...] = acc_ref[...].astype(o_ref.dtype)


def _peer_experts(x, rstd, u, v, rank1, e1, cnt, e0, tm=512, te=512, sub=256, n_pieces=4):
    t, d = x.shape
    n_exp = u.shape[0]
    hp, nk, _ = rank1.shape
    rows = te // nk
    tab = pl.BlockSpec((hp, nk, tm), lambda i, e: (0, 0, i))
    assert SUBLANES % rows == 0
    row_tab = pl.BlockSpec((hp, SUBLANES, tm), lambda i, e: (0, (e * rows) // SUBLANES, i))
    return pl.pallas_call(
        functools.partial(_peer_kernel, sub=sub, n_pieces=n_pieces),
        grid=(t // tm, n_exp // te),
        in_specs=[pl.BlockSpec((tm, d), lambda i, e: (i, 0)),
                  pl.BlockSpec((1, tm), lambda i, e: (0, i)),
                  pl.BlockSpec((te, d), lambda i, e: (e, 0)),
                  pl.BlockSpec((te, d), lambda i, e: (e, 0)),
                  tab, tab, row_tab, row_tab],
        out_specs=pl.BlockSpec((tm, d), lambda i, e: (i, 0)),
        out_shape=jax.ShapeDtypeStruct((t, d), BF16),
        scratch_shapes=([pltpu.VMEM((tm, d), F32), pltpu.VMEM((d, tm), BF16)]
                        + [pltpu.VMEM((sub, tm), F32)] * (te // sub)
                        + [pltpu.VMEM((tm, te), BF16)]),
        compiler_params=_params(2),
        name="peer_experts",
    )(x, rstd, u, v, rank1, e1, cnt, e0)


def kernel(x, mem, norm1_w, w_in, attn_rel_bias, ret_gn_w, w_out, norm2_w, mem_norm_w, xattn_wq,
           xattn_wkv, xattn_wo, norm3_w, peer_wq, peer_subkeys, peer_u, peer_v, final_norm_w):
    b, s, d = x.shape
    t = b * s
    depth = w_in.shape[0]
    n_attn_heads = attn_rel_bias.shape[1]
    attn_width = n_attn_heads * ATTN_HEAD_DIM
    n_ret_heads = ret_gn_w.shape[1] // RET_V_DIM
    h = x.reshape(t, d)
    xn = _rmsnorm(h, norm1_w[0], BF16)
    for l in range(depth):
        proj = _matmul(xn, w_in[l], BF16, tn=768, name="in_proj").reshape(b, s, -1)
        a_out = _chunk_attention(proj, attn_rel_bias[l], n_attn_heads)
        r_out = _retention(proj, ret_gn_w[l], n_ret_heads, 3 * attn_width)
        h, hg, rstd = _matmul((a_out.reshape(t, -1), r_out.reshape(t, -1)), w_out[l], F32,
                              residual=h, norm_gain=norm2_w[l], name="out_proj")

        mem_n = _rmsnorm(mem.reshape(-1, d), mem_norm_w[l], BF16)
        kv = _matmul(mem_n, xattn_wkv[l], BF16, name="mem_kv").reshape(b, -1, 2 * d)
        folded_k, folded_v = _fold_memory(kv, xattn_wq[l], xattn_wo[l], XATTN_HEADS)
        probs = _mem_probs(hg.reshape(b, s, d), rstd.reshape(b, s, 1), folded_k, XATTN_HEADS)
        h, hg, rstd = _matmul(probs.reshape(t, -1), folded_v, F32, residual=h,
                              norm_gain=norm3_w[l], tn=1024, name="xattn_o")

        qp = _matmul(hg, peer_wq[l], BF16, row_scale=rstd, name="peer_q")
        rank1, e1, cnt, e0, u_b, v_b = _peer_route(qp, peer_subkeys[l], peer_u[l], peer_v[l])
        delta = _peer_experts(hg, rstd.reshape(1, t), u_b, v_b, rank1, e1, cnt, e0)
        if l + 1 < depth:
            h = h + delta.astype(F32)
            xn = _rmsnorm(h, norm1_w[l + 1], BF16)
    return _add_rmsnorm(h, delta, final_norm_w, F32).reshape(b, s, d)
```

```python
import functools

import numpy as np
import jax
import jax.numpy as jnp
from jax import lax
from jax.experimental import pallas as pl
from jax.experimental.pallas import tpu as pltpu

F32 = jnp.float32
BF16 = jnp.bfloat16

CHUNK = 64
LEFT_CHUNKS = 8
LEFT = LEFT_CHUNKS * CHUNK
ATTN_HEAD_DIM = 128
MAX_REL_DIST = 256
RET_V_DIM = 256
RET_QK_DIM = 128
ROPE_BASE = 10000.0
XATTN_HEADS = 4
PEER_HALF = 128
PEER_TOPK = 16
EPS = 1e-6
NEG_INF = -1e30
NOT_RANKED = 1e9

VMEM_LIMIT_BYTES = 56 * 1024 * 1024
LANES = 128
SUBLANES = 8

_NT = (((1,), (1,)), ((), ()))
_TN = (((0,), (0,)), ((), ()))


def _params(n_grid_dims):
    return pltpu.CompilerParams(
        dimension_semantics=("arbitrary",) * n_grid_dims,
        vmem_limit_bytes=VMEM_LIMIT_BYTES)


def _rmsnorm_kernel(x_ref, w_ref, o_ref):
    x = x_ref[...]
    ms = jnp.mean(x * x, axis=-1, keepdims=True)
    o_ref[...] = (x * lax.rsqrt(ms + EPS) * w_ref[...]).astype(o_ref.dtype)


def _rmsnorm(x, w, out_dtype, rows=256):
    m, d = x.shape
    return pl.pallas_call(
        _rmsnorm_kernel,
        grid=(m // rows,),
        in_specs=[pl.BlockSpec((rows, d), lambda i: (i, 0)),
                  pl.BlockSpec((1, d), lambda i: (0, 0))],
        out_specs=pl.BlockSpec((rows, d), lambda i: (i, 0)),
        out_shape=jax.ShapeDtypeStruct((m, d), out_dtype),
        compiler_params=_params(1),
        name="rmsnorm",
    )(x, w.reshape(1, d))


def _add_rmsnorm_kernel(x_ref, y_ref, w_ref, o_ref):
    x = x_ref[...] + y_ref[...].astype(F32)
    ms = jnp.mean(x * x, axis=-1, keepdims=True)
    o_ref[...] = (x * lax.rsqrt(ms + EPS) * w_ref[...]).astype(o_ref.dtype)


def _add_rmsnorm(x, y, w, out_dtype, rows=256):
    m, d = x.shape
    return pl.pallas_call(
        _add_rmsnorm_kernel,
        grid=(m // rows,),
        in_specs=[pl.BlockSpec((rows, d), lambda i: (i, 0)),
                  pl.BlockSpec((rows, d), lambda i: (i, 0)),
                  pl.BlockSpec((1, d), lambda i: (0, 0))],
        out_specs=pl.BlockSpec((rows, d), lambda i: (i, 0)),
        out_shape=jax.ShapeDtypeStruct((m, d), out_dtype),
        compiler_params=_params(1),
        name="add_rmsnorm",
    )(x, y, w.reshape(1, d))


def _mm_kernel(*refs, n_lhs, has_residual, has_row_scale, emit_norm, n_cols):
    lhs_refs, w_ref = refs[:n_lhs], refs[n_lhs]
    pos = n_lhs + 1
    acc, k0 = None, 0
    for a_ref in lhs_refs:
        k1 = k0 + a_ref.shape[1]
        w_blk = w_ref[k0:k1, :]
        if w_blk.dtype != BF16:
            w_blk = w_blk.astype(BF16)
        part = jnp.dot(a_ref[...], w_blk, preferred_element_type=F32)
        acc = part if acc is None else acc + part
        k0 = k1
    if has_row_scale:
        acc = acc * refs[pos][...]
        pos += 1
    if has_residual:
        acc = refs[pos][...] + acc
        pos += 1
    if not emit_norm:
        o_ref = refs[pos]
        o_ref[...] = acc.astype(o_ref.dtype)
        return
    gain_ref, o_ref, scaled_ref, rstd_ref, ssq_ref = refs[pos:pos + 5]
    j = pl.program_id(1)
    o_ref[...] = acc.astype(o_ref.dtype)
    scaled_ref[...] = (acc * gain_ref[...]).astype(scaled_ref.dtype)
    row_ssq = jnp.sum(acc * acc, axis=-1, keepdims=True)

    @pl.when(j == 0)
    def _():
        ssq_ref[...] = row_ssq

    @pl.when(j > 0)
    def _():
        ssq_ref[...] += row_ssq

    @pl.when(j == pl.num_programs(1) - 1)
    def _():
        rstd_ref[...] = lax.rsqrt(ssq_ref[...] / n_cols + EPS)


def _matmul(lhs, w, out_dtype, residual=None, row_scale=None, norm_gain=None, tm=1024, tn=512,
            name="matmul"):
    lhs = lhs if isinstance(lhs, (tuple, list)) else (lhs,)
    m = lhs[0].shape[0]
    k, n = w.shape[-2:]
    assert sum(a.shape[1] for a in lhs) == k
    tm, tn = min(tm, m), min(tn, n)
    in_specs = [pl.BlockSpec((tm, a.shape[1]), lambda i, j: (i, 0)) for a in lhs]
    if w.ndim == 2:
        in_specs.append(pl.BlockSpec((k, tn), lambda i, j: (0, j)))
    else:
        tiles_per_group = m // w.shape[0] // tm
        in_specs.append(pl.BlockSpec((None, k, tn), lambda i, j: (i // tiles_per_group, 0, j)))
    args = [*lhs, w]
    tile = pl.BlockSpec((tm, tn), lambda i, j: (i, j))
    per_row = pl.BlockSpec((tm, 1), lambda i, j: (i, 0))
    if row_scale is not None:
        in_specs.append(per_row)
        args.append(row_scale)
    if residual is not None:
        in_specs.append(tile)
        args.append(residual)
    out_specs, out_shape, scratch = tile, jax.ShapeDtypeStruct((m, n), out_dtype), []
    if norm_gain is not None:
        in_specs.append(pl.BlockSpec((1, tn), lambda i, j: (0, j)))
        args.append(norm_gain.reshape(1, n))
        out_specs = [tile, tile, per_row]
        out_shape = [out_shape, jax.ShapeDtypeStruct((m, n), BF16),
                     jax.ShapeDtypeStruct((m, 1), F32)]
        scratch = [pltpu.VMEM((tm, 1), F32)]
    return pl.pallas_call(
        functools.partial(_mm_kernel, n_lhs=len(lhs), has_residual=residual is not None,
                          has_row_scale=row_scale is not None, emit_norm=norm_gain is not None,
                          n_cols=n),
        grid=(m // tm, n // tn),
        in_specs=in_specs,
        out_specs=out_specs,
        out_shape=out_shape,
        scratch_shapes=scratch,
        compiler_params=_params(2),
        name=name,
    )(*args)


def _attn_kernel(q_ref, k_ref, v_ref, base_ref, o_ref, bias_ref, *, qb, scale):
    seq = q_ref.shape[0]
    width = LEFT + qb

    @pl.when(pl.program_id(1) == 0)
    def _():
        toeplitz = pltpu.roll(jnp.broadcast_to(base_ref[...], (qb, base_ref.shape[-1])),
                              0, 1, stride=1, stride_axis=0)
        q_chunk = lax.broadcasted_iota(jnp.int32, (qb, width), 0) // CHUNK
        c_chunk = lax.broadcasted_iota(jnp.int32, (qb, width), 1) // CHUNK
        in_band = (c_chunk >= q_chunk) & (c_chunk <= q_chunk + LEFT_CHUNKS)
        bias_ref[...] = jnp.where(in_band, toeplitz[:, :width], NEG_INF)

    for i in range(seq // qb):
        q0 = i * qb
        k0 = max(0, q0 - LEFT)
        kw = q0 + qb - k0
        c0 = k0 - (q0 - LEFT)
        s = lax.dot_general(q_ref[q0:q0 + qb, :], k_ref[k0:k0 + kw, :], _NT,
                            preferred_element_type=F32)
        s = s * scale + bias_ref[:, c0:c0 + kw]
        m = jnp.max(s, axis=-1, keepdims=True)
        p = jnp.exp(s - m)
        l = jnp.sum(p, axis=-1, keepdims=True)
        o = jnp.dot(p.astype(BF16), v_ref[k0:k0 + kw, :], preferred_element_type=F32)
        o_ref[q0:q0 + qb, :] = (o / l).astype(o_ref.dtype)


def _attn_bias_base(rel_bias, qb):
    w = pl.next_power_of_2(LEFT + 2 * qb)
    j = jnp.arange(w)
    j = jnp.where(j < LEFT + qb, j, j - w)
    idx = jnp.clip(LEFT - j, -MAX_REL_DIST, MAX_REL_DIST) + MAX_REL_DIST
    return rel_bias[:, None, idx].astype(F32)


def _chunk_attention(proj, rel_bias, n_heads, qb=256):
    b, s, _ = proj.shape
    dh = ATTN_HEAD_DIM
    base = _attn_bias_base(rel_bias, qb)
    kern = functools.partial(_attn_kernel, qb=qb, scale=dh ** -0.5)
    return pl.pallas_call(
        kern,
        grid=(n_heads, b),
        in_specs=[pl.BlockSpec((None, s, dh), lambda h, bi: (bi, 0, h)),
                  pl.BlockSpec((None, s, dh), lambda h, bi: (bi, 0, n_heads + h)),
                  pl.BlockSpec((None, s, dh), lambda h, bi: (bi, 0, 2 * n_heads + h)),
                  pl.BlockSpec((None, 1, base.shape[-1]), lambda h, bi: (h, 0, 0))],
        out_specs=pl.BlockSpec((None, s, dh), lambda h, bi: (bi, 0, h)),
        out_shape=jax.ShapeDtypeStruct((b, s, n_heads * dh), BF16),
        scratch_shapes=[pltpu.VMEM((qb, LEFT + qb), F32)],
        compiler_params=_params(2),
        name="chunk_attention",
    )(proj, proj, proj, base)


def _ret_kernel(q_ref, k_ref, v_ref, g_ref, cos_ref, sin_ref, dec_ref, qd_ref, kd_ref, bd_ref,
                gnw_ref, o_ref, state_ref, *, rb, scale):
    seq = q_ref.shape[0]
    state_ref[...] = jnp.zeros_like(state_ref)
    half = q_ref.shape[1] // 2

    def body(n, carry):
        r0 = pl.multiple_of(n * rb, rb)
        rows = pl.ds(r0, rb)
        cos = cos_ref[rows, :]
        sin = sin_ref[rows, :]
        q = q_ref[rows, :].astype(F32)
        k = k_ref[rows, :].astype(F32)
        q = q * cos + pltpu.roll(q, half, 1) * sin
        k = (k * cos + pltpu.roll(k, half, 1) * sin) * scale
        v = v_ref[rows, :]
        a = lax.dot_general(q.astype(BF16), k.astype(BF16), _NT, preferred_element_type=F32)
        a = a * dec_ref[...]
        st = state_ref[...]
        y = jnp.dot(a.astype(BF16), v, preferred_element_type=F32)
        y = y + jnp.dot((q * qd_ref[...]).astype(BF16), st.astype(BF16),
                        preferred_element_type=F32)
        kd = (k * kd_ref[...]).astype(BF16)
        kv = lax.dot_general(kd, v, _TN, preferred_element_type=F32)
        state_ref[...] = st * bd_ref[...] + kv
        mu = jnp.mean(y, axis=-1, keepdims=True)
        yc = y - mu
        var = jnp.mean(yc * yc, axis=-1, keepdims=True)
        yn = yc * lax.rsqrt(var + EPS) * gnw_ref[...]
        g = g_ref[rows, :].astype(F32)
        o_ref[rows, :] = (g * (1.0 / (1.0 + jnp.exp(-g))) * yn).astype(o_ref.dtype)
        return carry

    lax.fori_loop(0, seq // rb, body, 0)


def _retention(proj, gn_w, n_heads, col0, rb=512):
    b, s, _ = proj.shape
    dk, dv = RET_QK_DIM, RET_V_DIM
    qk0 = col0 // dk
    v0 = (col0 + 2 * n_heads * dk) // dv
    inv_freq = 1.0 / (ROPE_BASE ** (jnp.arange(0, dk, 2, dtype=F32) / dk))
    ang = jnp.arange(s, dtype=F32)[:, None] * inv_freq[None, :]
    cos = jnp.concatenate([jnp.cos(ang), jnp.cos(ang)], axis=-1)
    sin = jnp.concatenate([-jnp.sin(ang), jnp.sin(ang)], axis=-1)
    log_gamma = jnp.log1p(-jnp.power(2.0, -5.0 - jnp.arange(n_heads, dtype=F32)))
    pos = jnp.arange(rb, dtype=F32)
    chunk_of = jnp.arange(rb) // CHUNK
    causal = (chunk_of[None, :] <= chunk_of[:, None]).astype(F32)
    dec = jnp.exp(log_gamma[:, None, None] * jnp.abs(pos[:, None] - pos[None, :])) * causal[None]
    qd = jnp.exp(log_gamma[:, None] * (pos + 1.0))[:, :, None]
    kd = jnp.exp(log_gamma[:, None] * (rb - 1.0 - pos))[:, :, None]
    bd = jnp.exp(log_gamma * rb)[:, None, None]
    kern = functools.partial(_ret_kernel, rb=rb, scale=dk ** -0.5)
    return pl.pallas_call(
        kern,
        grid=(b, n_heads),
        in_specs=[pl.BlockSpec((None, s, dk), lambda bi, h: (bi, 0, qk0 + h)),
                  pl.BlockSpec((None, s, dk), lambda bi, h: (bi, 0, qk0 + n_heads + h)),
                  pl.BlockSpec((None, s, dv), lambda bi, h: (bi, 0, v0 + h)),
                  pl.BlockSpec((None, s, dv), lambda bi, h: (bi, 0, v0 + n_heads + h)),
                  pl.BlockSpec((s, dk), lambda bi, h: (0, 0)),
                  pl.BlockSpec((s, dk), lambda bi, h: (0, 0)),
                  pl.BlockSpec((None, rb, rb), lambda bi, h: (h, 0, 0)),
                  pl.BlockSpec((None, rb, 1), lambda bi, h: (h, 0, 0)),
                  pl.BlockSpec((None, rb, 1), lambda bi, h: (h, 0, 0)),
                  pl.BlockSpec((None, 1, 1), lambda bi, h: (h, 0, 0)),
                  pl.BlockSpec((1, dv), lambda bi, h: (0, h))],
        out_specs=pl.BlockSpec((None, s, dv), lambda bi, h: (bi, 0, h)),
        out_shape=jax.ShapeDtypeStruct((b, s, n_heads * dv), BF16),
        scratch_shapes=[pltpu.VMEM((dk, dv), F32)],
        compiler_params=_params(2),
        name="retention",
    )(proj, proj, proj, proj, cos, sin, dec, qd, kd, bd, gn_w.reshape(1, -1))


def _fold_keys_kernel(wq_ref, k_ref, o_ref, wq16_ref):
    @pl.when(pl.program_id(1) == 0)
    def _():
        wq16_ref[...] = wq_ref[...].astype(BF16)

    o_ref[...] = lax.dot_general(wq16_ref[...], k_ref[...], _NT,
                                 preferred_element_type=F32).astype(o_ref.dtype)


def _fold_values_kernel(v_ref, wo_ref, o_ref, wo16_ref):
    @pl.when(pl.program_id(1) == 0)
    def _():
        wo16_ref[...] = wo_ref[...].astype(BF16)

    o_ref[...] = jnp.dot(v_ref[...], wo16_ref[...],
                         preferred_element_type=F32).astype(o_ref.dtype)


def _fold_memory(kv, wq, wo, n_heads):
    b, m, d2 = kv.shape
    d = d2 // 2
    dh = d // n_heads
    folded_k = pl.pallas_call(
        _fold_keys_kernel,
        grid=(n_heads, b),
        in_specs=[pl.BlockSpec((d, dh), lambda h, bi: (0, h)),
                  pl.BlockSpec((None, m, dh), lambda h, bi: (bi, 0, h))],
        out_specs=pl.BlockSpec((None, d, m), lambda h, bi: (bi, 0, h)),
        out_shape=jax.ShapeDtypeStruct((b, d, n_heads * m), BF16),
        scratch_shapes=[pltpu.VMEM((d, dh), BF16)],
        compiler_params=_params(2),
        name="fold_keys",
    )(wq, kv)
    folded_v = pl.pallas_call(
        _fold_values_kernel,
        grid=(n_heads, b),
        in_specs=[pl.BlockSpec((None, m, dh), lambda h, bi: (bi, 0, n_heads + h)),
                  pl.BlockSpec((dh, d), lambda h, bi: (h, 0))],
        out_specs=pl.BlockSpec((None, m, d), lambda h, bi: (bi, h, 0)),
        out_shape=jax.ShapeDtypeStruct((b, n_heads * m, d), BF16),
        scratch_shapes=[pltpu.VMEM((dh, d), BF16)],
        compiler_params=_params(2),
        name="fold_values",
    )(kv, wo)
    return folded_k, folded_v


def _mem_probs_kernel(x_ref, rstd_ref, fk_ref, o_ref, *, n_heads, scale):
    m = fk_ref.shape[1] // n_heads
    s = jnp.dot(x_ref[...], fk_ref[...], preferred_element_type=F32)
    s = s * (rstd_ref[...] * scale)
    for h in range(n_heads):
        sh = s[:, h * m:(h + 1) * m]
        mx = jnp.max(sh, axis=-1, keepdims=True)
        p = jnp.exp(sh - mx)
        o_ref[:, h * m:(h + 1) * m] = (p / jnp.sum(p, axis=-1, keepdims=True)).astype(o_ref.dtype)


def _mem_probs(x, rstd, folded_k, n_heads, tq=1024):
    b, s, d = x.shape
    hm = folded_k.shape[2]
    kern = functools.partial(_mem_probs_kernel, n_heads=n_heads, scale=(d // n_heads) ** -0.5)
    return pl.pallas_call(
        kern,
        grid=(b, s // tq),
        in_specs=[pl.BlockSpec((None, tq, d), lambda bi, i: (bi, i, 0)),
                  pl.BlockSpec((None, tq, 1), lambda bi, i: (bi, i, 0)),
                  pl.BlockSpec((None, d, hm), lambda bi, i: (bi, 0, 0))],
        out_specs=pl.BlockSpec((None, tq, hm), lambda bi, i: (bi, i, 0)),
        out_shape=jax.ShapeDtypeStruct((b, s, hm), BF16),
        compiler_params=_params(2),
        name="mem_probs",
    )(x, rstd, folded_k)


def _take_max(work, rows, exact):
    m = jnp.max(work, axis=0, keepdims=True)
    hit = work == m
    if not exact:
        return m, hit
    first = jnp.min(jnp.where(hit, rows, np.float32(work.shape[0])), axis=0, keepdims=True)
    return m, rows == first


def _top_rows(s, k, exact):
    rows = lax.broadcasted_iota(jnp.int32, s.shape, 0).astype(F32)
    work = s
    rank = jnp.full(s.shape, NOT_RANKED, F32)
    vals = []
    for r in range(k):
        m, sel = _take_max(work, rows, exact)
        rank = jnp.where(sel, np.float32(r), rank)
        work = jnp.where(sel, -jnp.inf, work)
        vals.append(m)
    taken = jnp.sum(jnp.where(rank < NOT_RANKED, 1.0, 0.0), axis=0, keepdims=True)
    return vals, rank, taken


def _candidate_cells(k):
    cells = []
    for ra in range(k):
        cells += [(ra, rb) for rb in range(k // (ra + 1))]
    single = [c for c in cells if k // (c[0] + 1) == 1]
    multi = [c for c in cells if c not in single]
    pad = (-len(multi)) % 8
    return multi + [None] * pad + single


def _route_kernel(q_ref, keys_ref, u_ref, v_ref, rank1_ref, e1_ref, cnt_ref, e0_ref, ub_ref, vb_ref):
    ub_ref[...] = u_ref[...].astype(ub_ref.dtype)
    vb_ref[...] = v_ref[...].astype(vb_ref.dtype)
    k = PEER_TOPK
    tl = q_ref.shape[0]
    cells = _candidate_cells(k)

    def route_group(g, exact):
        q = q_ref[g * LANES:(g + 1) * LANES, :]
        s0 = lax.dot_general(keys_ref[0].astype(BF16), q[:, :PEER_HALF], _NT,
                             preferred_element_type=F32)
        s1 = lax.dot_general(keys_ref[1].astype(BF16), q[:, PEER_HALF:], _NT,
                             preferred_element_type=F32)
        a, rank0, taken0 = _top_rows(s0, k, exact)
        b, rank1, taken1 = _top_rows(s1, k, exact)
        a_all, b_all = jnp.concatenate(a, axis=0), jnp.concatenate(b, axis=0)
        ea_all, eb_all = jnp.exp(a_all - a[0]), jnp.exp(b_all - b[0])
        neg = jnp.full_like(a[0], -jnp.inf)
        zero = jnp.zeros_like(a[0])
        n0, n1 = k, k // 2
        mid = cells[n0 + n1:len(cells) - k // 2]
        cand = jnp.concatenate(
            [a[0] + b_all, a[1] + b_all[:n1]]
            + [neg if c is None else a_all[c[0]:c[0] + 1] + b_all[c[1]:c[1] + 1] for c in mid]
            + [a_all[k // 2:] + b[0]], axis=0)
        wgt = jnp.concatenate(
            [ea_all[0:1] * eb_all, ea_all[1:2] * eb_all[:n1]]
            + [zero if c is None else ea_all[c[0]:c[0] + 1] * eb_all[c[1]:c[1] + 1] for c in mid]
            + [ea_all[k // 2:] * eb_all[0:1]], axis=0)
        rows = lax.broadcasted_iota(jnp.int32, cand.shape, 0).astype(F32)
        work = cand
        chosen = jnp.zeros(cand.shape, F32)
        for _ in range(k):
            _, sel = _take_max(work, rows, exact)
            chosen = jnp.where(sel, 1.0, chosen)
            work = jnp.where(sel, -jnp.inf, work)
        z = jnp.sum(chosen * wgt, axis=0, keepdims=True)
        cnt = jnp.zeros(s0.shape, F32)
        for ra in range(k):
            mine = [i for i, c in enumerate(cells) if c is not None and c[0] == ra]
            n_ra = jnp.sum(chosen[mine[0]:mine[-1] + 1], axis=0, keepdims=True)
            cnt = jnp.where(rank0 == np.float32(ra), n_ra, cnt)
        lanes = slice(g * LANES, (g + 1) * LANES)
        rank1_ref[:, lanes] = rank1.astype(rank1_ref.dtype)
        e1_ref[:, lanes] = (jnp.exp(s1 - b[0]) / z).astype(e1_ref.dtype)
        cnt_ref[:, lanes] = cnt
        e0_ref[:, lanes] = jnp.exp(s0 - a[0])
        taken2 = jnp.sum(chosen, axis=0, keepdims=True)
        return (jnp.abs(taken0 - k) + jnp.abs(taken1 - k) + jnp.abs(taken2 - k))

    groups = range(tl // LANES)
    tie = sum(route_group(g, exact=False) for g in groups)

    @pl.when(jnp.max(tie) > 0.0)
    def _():
        for g in groups:
            route_group(g, exact=True)


def _peer_route(qp, subkeys, u, v, tl=512):
    t = qp.shape[0]
    hp, _, nk, half = subkeys.shape
    n_exp, d = u.shape
    steps = (t // tl) * hp
    rows = n_exp // steps
    assert rows * steps == n_exp
    out = lambda dt: jax.ShapeDtypeStruct((hp, nk, t), dt)
    ospec = pl.BlockSpec((None, nk, tl), lambda i, h: (h, 0, i))
    wspec = pl.BlockSpec((rows, d), lambda i, h: (i * hp + h, 0))
    wout = jax.ShapeDtypeStruct((n_exp, d), BF16)
    return pl.pallas_call(
        _route_kernel,
        grid=(t // tl, hp),
        in_specs=[pl.BlockSpec((tl, 2 * half), lambda i, h: (i, h)),
                  pl.BlockSpec((None, 2, nk, half), lambda i, h: (h, 0, 0, 0)),
                  wspec, wspec],
        out_specs=[ospec] * 4 + [wspec] * 2,
        out_shape=[out(BF16), out(BF16), out(F32), out(F32), wout, wout],
        compiler_params=_params(2),
        name="peer_route",
    )(qp, subkeys, u, v)


def _peer_kernel(x_ref, rstd_ref, u_ref, v_ref, rank1_ref, e1_ref, cnt_ref, e0_ref, o_ref, acc_ref,
                 xt_ref, *stage_refs, sub, n_pieces):
    e = pl.program_id(1)
    n_heads, nk, tm = rank1_ref.shape
    te, d = u_ref.shape
    n_sub = te // sub
    rows_per_sub = sub // nk
    kc = d // n_pieces
    rows_per_step = te // nk
    row0 = (e % (SUBLANES // rows_per_step)) * rows_per_step
    act_refs, wt_refs = stage_refs[:n_sub], stage_refs[n_sub:]

    @pl.when(e == 0)
    def _():
        acc_ref[...] = jnp.zeros_like(acc_ref)
        xt_ref[...] = x_ref[...].T

    def up_piece(j, p):
        part = jnp.dot(u_ref[j * sub:(j + 1) * sub, p * kc:(p + 1) * kc],
                       xt_ref[p * kc:(p + 1) * kc, :], preferred_element_type=F32)
        if p == 0:
            act_refs[j][...] = part
        else:
            act_refs[j][...] += part

    def gate_piece(j, p):
        ii, g = divmod(p, n_pieces // rows_per_sub)
        tg = tm // (n_pieces // rows_per_sub)
        lanes = slice(g * tg, (g + 1) * tg)
        row = j * rows_per_sub + ii
        gate = None
        for h in range(n_heads):
            cnt = cnt_ref[h, pl.ds(row0 + row, 1), :][:, lanes].astype(BF16)
            e0 = e0_ref[h, pl.ds(row0 + row, 1), :][:, lanes].astype(BF16)
            zero = jnp.zeros((), BF16)
            term = e0 * jnp.where(rank1_ref[h, :, lanes] < cnt, e1_ref[h, :, lanes], zero)
            gate = term if gate is None else gate + term
        act = act_refs[j][ii * nk:(ii + 1) * nk, lanes] * rstd_ref[:, lanes]
        act = 0.5 * act * (1.0 + lax.erf(act * np.float32(1.0 / np.sqrt(2.0))))
        wt_refs[j][lanes, ii * nk:(ii + 1) * nk] = (gate * act.astype(BF16)).T

    def down_piece(j, p):
        cols = slice(p * kc, (p + 1) * kc)
        acc_ref[:, cols] += jnp.dot(wt_refs[j][...], v_ref[j * sub:(j + 1) * sub, cols],
                                    preferred_element_type=F32)

    for step in range(n_sub + 2):
        for p in range(n_pieces):
            if step < n_sub:
                up_piece(step, p)
            if 0 <= step - 1 < n_sub:
                gate_piece(step - 1, p)
            if 0 <= step - 2 < n_sub:
                down_piece(step - 2, p)

    @pl.when(e == pl.num_programs(1) - 1)
    def _():
        o_ref[...] = acc_ref[...].astype(o_ref.dtype)


def _peer_experts(x, rstd, u, v, rank1, e1, cnt, e0, tm=512, te=512, sub=256, n_pieces=4):
    t, d = x.shape
    n_exp = u.shape[0]
    hp, nk, _ = rank1.shape
    rows = te // nk
    tab = pl.BlockSpec((hp, nk, tm), lambda i, e: (0, 0, i))
    assert SUBLANES % rows == 0
    row_tab = pl.BlockSpec((hp, SUBLANES, tm), lambda i, e: (0, (e * rows) // SUBLANES, i))
    return pl.pallas_call(
        functools.partial(_peer_kernel, sub=sub, n_pieces=n_pieces),
        grid=(t // tm, n_exp // te),
        in_specs=[pl.BlockSpec((tm, d), lambda i, e: (i, 0)),
                  pl.BlockSpec((1, tm), lambda i, e: (0, i)),
                  pl.BlockSpec((te, d), lambda i, e: (e, 0)),
                  pl.BlockSpec((te, d), lambda i, e: (e, 0)),
                  tab, tab, row_tab, row_tab],
        out_specs=pl.BlockSpec((tm, d), lambda i, e: (i, 0)),
        out_shape=jax.ShapeDtypeStruct((t, d), BF16),
        scratch_shapes=([pltpu.VMEM((tm, d), F32), pltpu.VMEM((d, tm), BF16)]
                        + [pltpu.VMEM((sub, tm), F32)] * (te // sub)
                        + [pltpu.VMEM((tm, sub), BF16)] * (te // sub)),
        compiler_params=_params(2),
        name="peer_experts",
    )(x, rstd, u, v, rank1, e1, cnt, e0)


def kernel(x, mem, norm1_w, w_in, attn_rel_bias, ret_gn_w, w_out, norm2_w, mem_norm_w, xattn_wq,
           xattn_wkv, xattn_wo, norm3_w, peer_wq, peer_subkeys, peer_u, peer_v, final_norm_w):
    b, s, d = x.shape
    t = b * s
    depth = w_in.shape[0]
    n_attn_heads = attn_rel_bias.shape[1]
    attn_width = n_attn_heads * ATTN_HEAD_DIM
    n_ret_heads = ret_gn_w.shape[1] // RET_V_DIM
    h = x.reshape(t, d)
    xn = _rmsnorm(h, norm1_w[0], BF16)
    for l in range(depth):
        proj = _matmul(xn, w_in[l], BF16, tn=768, name="in_proj").reshape(b, s, -1)
        a_out = _chunk_attention(proj, attn_rel_bias[l], n_attn_heads)
        r_out = _retention(proj, ret_gn_w[l], n_ret_heads, 3 * attn_width)
        h, hg, rstd = _matmul((a_out.reshape(t, -1), r_out.reshape(t, -1)), w_out[l], F32,
                              residual=h, norm_gain=norm2_w[l], name="out_proj")

        mem_n = _rmsnorm(mem.reshape(-1, d), mem_norm_w[l], BF16)
        kv = _matmul(mem_n, xattn_wkv[l], BF16, name="mem_kv").reshape(b, -1, 2 * d)
        folded_k, folded_v = _fold_memory(kv, xattn_wq[l], xattn_wo[l], XATTN_HEADS)
        probs = _mem_probs(hg.reshape(b, s, d), rstd.reshape(b, s, 1), folded_k, XATTN_HEADS)
        h, hg, rstd = _matmul(probs.reshape(t, -1), folded_v, F32, residual=h,
                              norm_gain=norm3_w[l], tn=1024, name="xattn_o")

        qp = _matmul(hg, peer_wq[l], BF16, row_scale=rstd, name="peer_q")
        rank1, e1, cnt, e0, u_b, v_b = _peer_route(qp, peer_subkeys[l], peer_u[l], peer_v[l])
        delta = _peer_experts(hg, rstd.reshape(1, t), u_b, v_b, rank1, e1, cnt, e0)
        if l + 1 < depth:
            h = h + delta.astype(F32)
            xn = _rmsnorm(h, norm1_w[l + 1], BF16)
    return _add_rmsnorm(h, delta, final_norm_w, F32).reshape(b, s, d)
```

```python
import functools

import numpy as np
import jax
import jax.numpy as jnp
from jax import lax
from jax.experimental import pallas as pl
from jax.experimental.pallas import tpu as pltpu

F32 = jnp.float32
BF16 = jnp.bfloat16

CHUNK = 64
LEFT_CHUNKS = 8
LEFT = LEFT_CHUNKS * CHUNK
ATTN_HEAD_DIM = 128
MAX_REL_DIST = 256
RET_V_DIM = 256
RET_QK_DIM = 128
ROPE_BASE = 10000.0
XATTN_HEADS = 4
PEER_HALF = 128
PEER_TOPK = 16
EPS = 1e-6
NEG_INF = -1e30
NOT_RANKED = 1e9

VMEM_LIMIT_BYTES = 56 * 1024 * 1024
PEER_VMEM_LIMIT_BYTES = 62 * 1024 * 1024
LANES = 128
SUBLANES = 8

_NT = (((1,), (1,)), ((), ()))
_TN = (((0,), (0,)), ((), ()))


def _params(n_grid_dims, vmem_limit_bytes=VMEM_LIMIT_BYTES):
    return pltpu.CompilerParams(
        dimension_semantics=("arbitrary",) * n_grid_dims,
        vmem_limit_bytes=vmem_limit_bytes)


def _rmsnorm_kernel(x_ref, w_ref, o_ref):
    x = x_ref[...]
    ms = jnp.mean(x * x, axis=-1, keepdims=True)
    o_ref[...] = (x * lax.rsqrt(ms + EPS) * w_ref[...]).astype(o_ref.dtype)


def _rmsnorm(x, w, out_dtype, rows=256):
    m, d = x.shape
    return pl.pallas_call(
        _rmsnorm_kernel,
        grid=(m // rows,),
        in_specs=[pl.BlockSpec((rows, d), lambda i: (i, 0)),
                  pl.BlockSpec((1, d), lambda i: (0, 0))],
        out_specs=pl.BlockSpec((rows, d), lambda i: (i, 0)),
        out_shape=jax.ShapeDtypeStruct((m, d), out_dtype),
        compiler_params=_params(1),
        name="rmsnorm",
    )(x, w.reshape(1, d))


def _add_rmsnorm_kernel(x_ref, y_ref, w_ref, o_ref):
    x = x_ref[...] + y_ref[...].astype(F32)
    ms = jnp.mean(x * x, axis=-1, keepdims=True)
    o_ref[...] = (x * lax.rsqrt(ms + EPS) * w_ref[...]).astype(o_ref.dtype)


def _add_rmsnorm(x, y, w, out_dtype, rows=256):
    m, d = x.shape
    return pl.pallas_call(
        _add_rmsnorm_kernel,
        grid=(m // rows,),
        in_specs=[pl.BlockSpec((rows, d), lambda i: (i, 0)),
                  pl.BlockSpec((rows, d), lambda i: (i, 0)),
                  pl.BlockSpec((1, d), lambda i: (0, 0))],
        out_specs=pl.BlockSpec((rows, d), lambda i: (i, 0)),
        out_shape=jax.ShapeDtypeStruct((m, d), out_dtype),
        compiler_params=_params(1),
        name="add_rmsnorm",
    )(x, y, w.reshape(1, d))


def _mm_kernel(*refs, n_lhs, has_residual, has_row_scale, emit_norm, n_cols):
    lhs_refs, w_ref = refs[:n_lhs], refs[n_lhs]
    pos = n_lhs + 1
    acc, k0 = None, 0
    for a_ref in lhs_refs:
        k1 = k0 + a_ref.shape[1]
        w_blk = w_ref[k0:k1, :]
        if w_blk.dtype != BF16:
            w_blk = w_blk.astype(BF16)
        part = jnp.dot(a_ref[...], w_blk, preferred_element_type=F32)
        acc = part if acc is None else acc + part
        k0 = k1
    if has_row_scale:
        acc = acc * refs[pos][...]
        pos += 1
    if has_residual:
        acc = refs[pos][...] + acc
        pos += 1
    if not emit_norm:
        o_ref = refs[pos]
        o_ref[...] = acc.astype(o_ref.dtype)
        return
    gain_ref, o_ref, scaled_ref, rstd_ref, ssq_ref = refs[pos:pos + 5]
    j = pl.program_id(1)
    o_ref[...] = acc.astype(o_ref.dtype)
    scaled_ref[...] = (acc * gain_ref[...]).astype(scaled_ref.dtype)
    row_ssq = jnp.sum(acc * acc, axis=-1, keepdims=True)

    @pl.when(j == 0)
    def _():
        ssq_ref[...] = row_ssq

    @pl.when(j > 0)
    def _():
        ssq_ref[...] += row_ssq

    @pl.when(j == pl.num_programs(1) - 1)
    def _():
        rstd_ref[...] = lax.rsqrt(ssq_ref[...] / n_cols + EPS)


def _matmul(lhs, w, out_dtype, residual=None, row_scale=None, norm_gain=None, tm=1024, tn=512,
            name="matmul"):
    lhs = lhs if isinstance(lhs, (tuple, list)) else (lhs,)
    m = lhs[0].shape[0]
    k, n = w.shape[-2:]
    assert sum(a.shape[1] for a in lhs) == k
    tm, tn = min(tm, m), min(tn, n)
    in_specs = [pl.BlockSpec((tm, a.shape[1]), lambda i, j: (i, 0)) for a in lhs]
    if w.ndim == 2:
        in_specs.append(pl.BlockSpec((k, tn), lambda i, j: (0, j)))
    else:
        tiles_per_group = m // w.shape[0] // tm
        in_specs.append(pl.BlockSpec((None, k, tn), lambda i, j: (i // tiles_per_group, 0, j)))
    args = [*lhs, w]
    tile = pl.BlockSpec((tm, tn), lambda i, j: (i, j))
    per_row = pl.BlockSpec((tm, 1), lambda i, j: (i, 0))
    if row_scale is not None:
        in_specs.append(per_row)
        args.append(row_scale)
    if residual is not None:
        in_specs.append(tile)
        args.append(residual)
    out_specs, out_shape, scratch = tile, jax.ShapeDtypeStruct((m, n), out_dtype), []
    if norm_gain is not None:
        in_specs.append(pl.BlockSpec((1, tn), lambda i, j: (0, j)))
        args.append(norm_gain.reshape(1, n))
        out_specs = [tile, tile, per_row]
        out_shape = [out_shape, jax.ShapeDtypeStruct((m, n), BF16),
                     jax.ShapeDtypeStruct((m, 1), F32)]
        scratch = [pltpu.VMEM((tm, 1), F32)]
    return pl.pallas_call(
        functools.partial(_mm_kernel, n_lhs=len(lhs), has_residual=residual is not None,
                          has_row_scale=row_scale is not None, emit_norm=norm_gain is not None,
                          n_cols=n),
        grid=(m // tm, n // tn),
        in_specs=in_specs,
        out_specs=out_specs,
        out_shape=out_shape,
        scratch_shapes=scratch,
        compiler_params=_params(2),
        name=name,
    )(*args)


def _attn_kernel(q_ref, k_ref, v_ref, base_ref, o_ref, bias_ref, *, qb, scale):
    seq = q_ref.shape[0]
    width = LEFT + qb

    @pl.when(pl.program_id(1) == 0)
    def _():
        toeplitz = pltpu.roll(jnp.broadcast_to(base_ref[...], (qb, base_ref.shape[-1])),
                              0, 1, stride=1, stride_axis=0)
        q_chunk = lax.broadcasted_iota(jnp.int32, (qb, width), 0) // CHUNK
        c_chunk = lax.broadcasted_iota(jnp.int32, (qb, width), 1) // CHUNK
        in_band = (c_chunk >= q_chunk) & (c_chunk <= q_chunk + LEFT_CHUNKS)
        bias_ref[...] = jnp.where(in_band, toeplitz[:, :width], NEG_INF)

    for i in range(seq // qb):
        q0 = i * qb
        k0 = max(0, q0 - LEFT)
        kw = q0 + qb - k0
        c0 = k0 - (q0 - LEFT)
        s = lax.dot_general(q_ref[q0:q0 + qb, :], k_ref[k0:k0 + kw, :], _NT,
                            preferred_element_type=F32)
        s = s * scale + bias_ref[:, c0:c0 + kw]
        m = jnp.max(s, axis=-1, keepdims=True)
        p = jnp.exp(s - m)
        l = jnp.sum(p, axis=-1, keepdims=True)
        o = jnp.dot(p.astype(BF16), v_ref[k0:k0 + kw, :], preferred_element_type=F32)
        o_ref[q0:q0 + qb, :] = (o / l).astype(o_ref.dtype)


def _attn_bias_base(rel_bias, qb):
    w = pl.next_power_of_2(LEFT + 2 * qb)
    j = jnp.arange(w)
    j = jnp.where(j < LEFT + qb, j, j - w)
    idx = jnp.clip(LEFT - j, -MAX_REL_DIST, MAX_REL_DIST) + MAX_REL_DIST
    return rel_bias[:, None, idx].astype(F32)


def _chunk_attention(proj, rel_bias, n_heads, qb=256):
    b, s, _ = proj.shape
    dh = ATTN_HEAD_DIM
    base = _attn_bias_base(rel_bias, qb)
    kern = functools.partial(_attn_kernel, qb=qb, scale=dh ** -0.5)
    return pl.pallas_call(
        kern,
        grid=(n_heads, b),
        in_specs=[pl.BlockSpec((None, s, dh), lambda h, bi: (bi, 0, h)),
                  pl.BlockSpec((None, s, dh), lambda h, bi: (bi, 0, n_heads + h)),
                  pl.BlockSpec((None, s, dh), lambda h, bi: (bi, 0, 2 * n_heads + h)),
                  pl.BlockSpec((None, 1, base.shape[-1]), lambda h, bi: (h, 0, 0))],
        out_specs=pl.BlockSpec((None, s, dh), lambda h, bi: (bi, 0, h)),
        out_shape=jax.ShapeDtypeStruct((b, s, n_heads * dh), BF16),
        scratch_shapes=[pltpu.VMEM((qb, LEFT + qb), F32)],
        compiler_params=_params(2),
        name="chunk_attention",
    )(proj, proj, proj, base)


def _ret_kernel(q_ref, k_ref, v_ref, g_ref, cos_ref, sin_ref, dec_ref, qd_ref, kd_ref, bd_ref,
                gnw_ref, o_ref, state_ref, *, rb, scale):
    seq = q_ref.shape[0]
    state_ref[...] = jnp.zeros_like(state_ref)
    half = q_ref.shape[1] // 2

    def body(n, carry):
        r0 = pl.multiple_of(n * rb, rb)
        rows = pl.ds(r0, rb)
        cos = cos_ref[rows, :]
        sin = sin_ref[rows, :]
        q = q_ref[rows, :].astype(F32)
        k = k_ref[rows, :].astype(F32)
        q = q * cos + pltpu.roll(q, half, 1) * sin
        k = (k * cos + pltpu.roll(k, half, 1) * sin) * scale
        v = v_ref[rows, :]
        a = lax.dot_general(q.astype(BF16), k.astype(BF16), _NT, preferred_element_type=F32)
        a = a * dec_ref[...]
        st = state_ref[...]
        y = jnp.dot(a.astype(BF16), v, preferred_element_type=F32)
        y = y + jnp.dot((q * qd_ref[...]).astype(BF16), st.astype(BF16),
                        preferred_element_type=F32)
        kd = (k * kd_ref[...]).astype(BF16)
        kv = lax.dot_general(kd, v, _TN, preferred_element_type=F32)
        state_ref[...] = st * bd_ref[...] + kv
        mu = jnp.mean(y, axis=-1, keepdims=True)
        yc = y - mu
        var = jnp.mean(yc * yc, axis=-1, keepdims=True)
        yn = yc * lax.rsqrt(var + EPS) * gnw_ref[...]
        g = g_ref[rows, :].astype(F32)
        o_ref[rows, :] = (g * (1.0 / (1.0 + jnp.exp(-g))) * yn).astype(o_ref.dtype)
        return carry

    lax.fori_loop(0, seq // rb, body, 0)


def _retention(proj, gn_w, n_heads, col0, rb=512):
    b, s, _ = proj.shape
    dk, dv = RET_QK_DIM, RET_V_DIM
    qk0 = col0 // dk
    v0 = (col0 + 2 * n_heads * dk) // dv
    inv_freq = 1.0 / (ROPE_BASE ** (jnp.arange(0, dk, 2, dtype=F32) / dk))
    ang = jnp.arange(s, dtype=F32)[:, None] * inv_freq[None, :]
    cos = jnp.concatenate([jnp.cos(ang), jnp.cos(ang)], axis=-1)
    sin = jnp.concatenate([-jnp.sin(ang), jnp.sin(ang)], axis=-1)
    log_gamma = jnp.log1p(-jnp.power(2.0, -5.0 - jnp.arange(n_heads, dtype=F32)))
    pos = jnp.arange(rb, dtype=F32)
    chunk_of = jnp.arange(rb) // CHUNK
    causal = (chunk_of[None, :] <= chunk_of[:, None]).astype(F32)
    dec = jnp.exp(log_gamma[:, None, None] * jnp.abs(pos[:, None] - pos[None, :])) * causal[None]
    qd = jnp.exp(log_gamma[:, None] * (pos + 1.0))[:, :, None]
    kd = jnp.exp(log_gamma[:, None] * (rb - 1.0 - pos))[:, :, None]
    bd = jnp.exp(log_gamma * rb)[:, None, None]
    kern = functools.partial(_ret_kernel, rb=rb, scale=dk ** -0.5)
    return pl.pallas_call(
        kern,
        grid=(b, n_heads),
        in_specs=[pl.BlockSpec((None, s, dk), lambda bi, h: (bi, 0, qk0 + h)),
                  pl.BlockSpec((None, s, dk), lambda bi, h: (bi, 0, qk0 + n_heads + h)),
                  pl.BlockSpec((None, s, dv), lambda bi, h: (bi, 0, v0 + h)),
                  pl.BlockSpec((None, s, dv), lambda bi, h: (bi, 0, v0 + n_heads + h)),
                  pl.BlockSpec((s, dk), lambda bi, h: (0, 0)),
                  pl.BlockSpec((s, dk), lambda bi, h: (0, 0)),
                  pl.BlockSpec((None, rb, rb), lambda bi, h: (h, 0, 0)),
                  pl.BlockSpec((None, rb, 1), lambda bi, h: (h, 0, 0)),
                  pl.BlockSpec((None, rb, 1), lambda bi, h: (h, 0, 0)),
                  pl.BlockSpec((None, 1, 1), lambda bi, h: (h, 0, 0)),
                  pl.BlockSpec((1, dv), lambda bi, h: (0, h))],
        out_specs=pl.BlockSpec((None, s, dv), lambda bi, h: (bi, 0, h)),
        out_shape=jax.ShapeDtypeStruct((b, s, n_heads * dv), BF16),
        scratch_shapes=[pltpu.VMEM((dk, dv), F32)],
        compiler_params=_params(2),
        name="retention",
    )(proj, proj, proj, proj, cos, sin, dec, qd, kd, bd, gn_w.reshape(1, -1))


def _fold_keys_kernel(wq_ref, k_ref, o_ref, wq16_ref):
    @pl.when(pl.program_id(1) == 0)
    def _():
        wq16_ref[...] = wq_ref[...].astype(BF16)

    o_ref[...] = lax.dot_general(wq16_ref[...], k_ref[...], _NT,
                                 preferred_element_type=F32).astype(o_ref.dtype)


def _fold_values_kernel(v_ref, wo_ref, o_ref, wo16_ref):
    @pl.when(pl.program_id(1) == 0)
    def _():
        wo16_ref[...] = wo_ref[...].astype(BF16)

    o_ref[...] = jnp.dot(v_ref[...], wo16_ref[...],
                         preferred_element_type=F32).astype(o_ref.dtype)


def _fold_memory(kv, wq, wo, n_heads):
    b, m, d2 = kv.shape
    d = d2 // 2
    dh = d // n_heads
    folded_k = pl.pallas_call(
        _fold_keys_kernel,
        grid=(n_heads, b),
        in_specs=[pl.BlockSpec((d, dh), lambda h, bi: (0, h)),
                  pl.BlockSpec((None, m, dh), lambda h, bi: (bi, 0, h))],
        out_specs=pl.BlockSpec((None, d, m), lambda h, bi: (bi, 0, h)),
        out_shape=jax.ShapeDtypeStruct((b, d, n_heads * m), BF16),
        scratch_shapes=[pltpu.VMEM((d, dh), BF16)],
        compiler_params=_params(2),
        name="fold_keys",
    )(wq, kv)
    folded_v = pl.pallas_call(
        _fold_values_kernel,
        grid=(n_heads, b),
        in_specs=[pl.BlockSpec((None, m, dh), lambda h, bi: (bi, 0, n_heads + h)),
                  pl.BlockSpec((dh, d), lambda h, bi: (h, 0))],
        out_specs=pl.BlockSpec((None, m, d), lambda h, bi: (bi, h, 0)),
        out_shape=jax.ShapeDtypeStruct((b, n_heads * m, d), BF16),
        scratch_shapes=[pltpu.VMEM((dh, d), BF16)],
        compiler_params=_params(2),
        name="fold_values",
    )(kv, wo)
    return folded_k, folded_v


def _mem_probs_kernel(x_ref, rstd_ref, fk_ref, o_ref, *, n_heads, scale):
    m = fk_ref.shape[1] // n_heads
    s = jnp.dot(x_ref[...], fk_ref[...], preferred_element_type=F32)
    s = s * (rstd_ref[...] * scale)
    for h in range(n_heads):
        sh = s[:, h * m:(h + 1) * m]
        mx = jnp.max(sh, axis=-1, keepdims=True)
        p = jnp.exp(sh - mx)
        o_ref[:, h * m:(h + 1) * m] = (p / jnp.sum(p, axis=-1, keepdims=True)).astype(o_ref.dtype)


def _mem_probs(x, rstd, folded_k, n_heads, tq=1024):
    b, s, d = x.shape
    hm = folded_k.shape[2]
    kern = functools.partial(_mem_probs_kernel, n_heads=n_heads, scale=(d // n_heads) ** -0.5)
    return pl.pallas_call(
        kern,
        grid=(b, s // tq),
        in_specs=[pl.BlockSpec((None, tq, d), lambda bi, i: (bi, i, 0)),
                  pl.BlockSpec((None, tq, 1), lambda bi, i: (bi, i, 0)),
                  pl.BlockSpec((None, d, hm), lambda bi, i: (bi, 0, 0))],
        out_specs=pl.BlockSpec((None, tq, hm), lambda bi, i: (bi, i, 0)),
        out_shape=jax.ShapeDtypeStruct((b, s, hm), BF16),
        compiler_params=_params(2),
        name="mem_probs",
    )(x, rstd, folded_k)


def _take_max(work, rows, exact):
    m = jnp.max(work, axis=0, keepdims=True)
    hit = work == m
    if not exact:
        return m, hit
    first = jnp.min(jnp.where(hit, rows, np.float32(work.shape[0])), axis=0, keepdims=True)
    return m, rows == first


def _top_rows(s, k, exact):
    rows = lax.broadcasted_iota(jnp.int32, s.shape, 0).astype(F32)
    work = s
    rank = jnp.full(s.shape, NOT_RANKED, F32)
    vals = []
    for r in range(k):
        m, sel = _take_max(work, rows, exact)
        rank = jnp.where(sel, np.float32(r), rank)
        work = jnp.where(sel, -jnp.inf, work)
        vals.append(m)
    taken = jnp.sum(jnp.where(rank < NOT_RANKED, 1.0, 0.0), axis=0, keepdims=True)
    return vals, rank, taken


def _candidate_cells(k):
    cells = []
    for ra in range(k):
        cells += [(ra, rb) for rb in range(k // (ra + 1))]
    single = [c for c in cells if k // (c[0] + 1) == 1]
    multi = [c for c in cells if c not in single]
    pad = (-len(multi)) % 8
    return multi + [None] * pad + single


def _route_kernel(q_ref, keys_ref, u_ref, v_ref, rank1_ref, e1_ref, cnt_ref, e0_ref, ub_ref, vb_ref):
    ub_ref[...] = u_ref[...].astype(ub_ref.dtype)
    vb_ref[...] = v_ref[...].astype(vb_ref.dtype)
    k = PEER_TOPK
    tl = q_ref.shape[0]
    cells = _candidate_cells(k)

    def route_group(g, exact):
        q = q_ref[g * LANES:(g + 1) * LANES, :]
        s0 = lax.dot_general(keys_ref[0].astype(BF16), q[:, :PEER_HALF], _NT,
                             preferred_element_type=F32)
        s1 = lax.dot_general(keys_ref[1].astype(BF16), q[:, PEER_HALF:], _NT,
                             preferred_element_type=F32)
        a, rank0, taken0 = _top_rows(s0, k, exact)
        b, rank1, taken1 = _top_rows(s1, k, exact)
        a_all, b_all = jnp.concatenate(a, axis=0), jnp.concatenate(b, axis=0)
        ea_all, eb_all = jnp.exp(a_all - a[0]), jnp.exp(b_all - b[0])
        neg = jnp.full_like(a[0], -jnp.inf)
        zero = jnp.zeros_like(a[0])
        n0, n1 = k, k // 2
        mid = cells[n0 + n1:len(cells) - k // 2]
        cand = jnp.concatenate(
            [a[0] + b_all, a[1] + b_all[:n1]]
            + [neg if c is None else a_all[c[0]:c[0] + 1] + b_all[c[1]:c[1] + 1] for c in mid]
            + [a_all[k // 2:] + b[0]], axis=0)
        wgt = jnp.concatenate(
            [ea_all[0:1] * eb_all, ea_all[1:2] * eb_all[:n1]]
            + [zero if c is None else ea_all[c[0]:c[0] + 1] * eb_all[c[1]:c[1] + 1] for c in mid]
            + [ea_all[k // 2:] * eb_all[0:1]], axis=0)
        rows = lax.broadcasted_iota(jnp.int32, cand.shape, 0).astype(F32)
        work = cand
        chosen = jnp.zeros(cand.shape, F32)
        for _ in range(k):
            _, sel = _take_max(work, rows, exact)
            chosen = jnp.where(sel, 1.0, chosen)
            work = jnp.where(sel, -jnp.inf, work)
        z = jnp.sum(chosen * wgt, axis=0, keepdims=True)
        cnt = jnp.zeros(s0.shape, F32)
        for ra in range(k):
            mine = [i for i, c in enumerate(cells) if c is not None and c[0] == ra]
            n_ra = jnp.sum(chosen[mine[0]:mine[-1] + 1], axis=0, keepdims=True)
            cnt = jnp.where(rank0 == np.float32(ra), n_ra, cnt)
        lanes = slice(g * LANES, (g + 1) * LANES)
        rank1_ref[:, lanes] = rank1.astype(rank1_ref.dtype)
        e1_ref[:, lanes] = (jnp.exp(s1 - b[0]) / z).astype(e1_ref.dtype)
        cnt_ref[:, lanes] = cnt
        e0_ref[:, lanes] = jnp.exp(s0 - a[0])
        taken2 = jnp.sum(chosen, axis=0, keepdims=True)
        return (jnp.abs(taken0 - k) + jnp.abs(taken1 - k) + jnp.abs(taken2 - k))

    groups = range(tl // LANES)
    tie = sum(route_group(g, exact=False) for g in groups)

    @pl.when(jnp.max(tie) > 0.0)
    def _():
        for g in groups:
            route_group(g, exact=True)


def _peer_route(qp, subkeys, u, v, tl=512):
    t = qp.shape[0]
    hp, _, nk, half = subkeys.shape
    n_exp, d = u.shape
    steps = (t // tl) * hp
    rows = n_exp // steps
    assert rows * steps == n_exp
    out = lambda dt: jax.ShapeDtypeStruct((hp, nk, t), dt)
    ospec = pl.BlockSpec((None, nk, tl), lambda i, h: (h, 0, i))
    wspec = pl.BlockSpec((rows, d), lambda i, h: (i * hp + h, 0))
    wout = jax.ShapeDtypeStruct((n_exp, d), BF16)
    return pl.pallas_call(
        _route_kernel,
        grid=(t // tl, hp),
        in_specs=[pl.BlockSpec((tl, 2 * half), lambda i, h: (i, h)),
                  pl.BlockSpec((None, 2, nk, half), lambda i, h: (h, 0, 0, 0)),
                  wspec, wspec],
        out_specs=[ospec] * 4 + [wspec] * 2,
        out_shape=[out(BF16), out(BF16), out(F32), out(F32), wout, wout],
        compiler_params=_params(2),
        name="peer_route",
    )(qp, subkeys, u, v)


def _peer_kernel(x_ref, rstd_ref, u_ref, v_ref, rank1_ref, e1_ref, cnt_ref, e0_ref, o_ref, acc_ref,
                 xt_ref, *stage_refs, sub, n_pieces):
    e = pl.program_id(1)
    n_heads, nk, tm = rank1_ref.shape
    te, d = u_ref.shape
    n_sub = te // sub
    rows_per_sub = sub // nk
    kc = d // n_pieces
    rows_per_step = te // nk
    row0 = (e % (SUBLANES // rows_per_step)) * rows_per_step
    act_refs, wt_refs = stage_refs[:n_sub], stage_refs[n_sub:]

    @pl.when(e == 0)
    def _():
        acc_ref[...] = jnp.zeros_like(acc_ref)
        xt_ref[...] = x_ref[...].T

    def up_piece(j, p):
        part = jnp.dot(u_ref[j * sub:(j + 1) * sub, p * kc:(p + 1) * kc],
                       xt_ref[p * kc:(p + 1) * kc, :], preferred_element_type=F32)
        if p == 0:
            act_refs[j][...] = part
        else:
            act_refs[j][...] += part

    def gate_piece(j, p):
        ii, g = divmod(p, n_pieces // rows_per_sub)
        tg = tm // (n_pieces // rows_per_sub)
        lanes = slice(g * tg, (g + 1) * tg)
        row = j * rows_per_sub + ii
        gate = None
        for h in range(n_heads):
            cnt = cnt_ref[h, pl.ds(row0 + row, 1), :][:, lanes].astype(BF16)
            e0 = e0_ref[h, pl.ds(row0 + row, 1), :][:, lanes].astype(BF16)
            zero = jnp.zeros((), BF16)
            term = e0 * jnp.where(rank1_ref[h, :, lanes] < cnt, e1_ref[h, :, lanes], zero)
            gate = term if gate is None else gate + term
        act = act_refs[j][ii * nk:(ii + 1) * nk, lanes] * rstd_ref[:, lanes]
        act = 0.5 * act * (1.0 + lax.erf(act * np.float32(1.0 / np.sqrt(2.0))))
        wt_refs[j][lanes, ii * nk:(ii + 1) * nk] = (gate * act.astype(BF16)).T

    def down_piece(j, p):
        cols = slice(p * kc, (p + 1) * kc)
        acc_ref[:, cols] += jnp.dot(wt_refs[j][...], v_ref[j * sub:(j + 1) * sub, cols],
                                    preferred_element_type=F32)

    for step in range(n_sub + 2):
        for p in range(n_pieces):
            if step < n_sub:
                up_piece(step, p)
            if 0 <= step - 1 < n_sub:
                gate_piece(step - 1, p)
            if 0 <= step - 2 < n_sub:
                down_piece(step - 2, p)

    @pl.when(e == pl.num_programs(1) - 1)
    def _():
        o_ref[...] = acc_ref[...].astype(o_ref.dtype)


def _peer_experts(x, rstd, u, v, rank1, e1, cnt, e0, tm=512, te=1024, sub=256, n_pieces=4):
    t, d = x.shape
    n_exp = u.shape[0]
    hp, nk, _ = rank1.shape
    rows = te // nk
    tab = pl.BlockSpec((hp, nk, tm), lambda i, e: (0, 0, i))
    assert SUBLANES % rows == 0
    row_tab = pl.BlockSpec((hp, SUBLANES, tm), lambda i, e: (0, (e * rows) // SUBLANES, i))
    return pl.pallas_call(
        functools.partial(_peer_kernel, sub=sub, n_pieces=n_pieces),
        grid=(t // tm, n_exp // te),
        in_specs=[pl.BlockSpec((tm, d), lambda i, e: (i, 0), pipeline_mode=pl.Buffered(1)),
                  pl.BlockSpec((1, tm), lambda i, e: (0, i)),
                  pl.BlockSpec((te, d), lambda i, e: (e, 0)),
                  pl.BlockSpec((te, d), lambda i, e: (e, 0)),
                  tab, tab, row_tab, row_tab],
        out_specs=pl.BlockSpec((tm, d), lambda i, e: (i, 0), pipeline_mode=pl.Buffered(1)),
        out_shape=jax.ShapeDtypeStruct((t, d), BF16),
        scratch_shapes=([pltpu.VMEM((tm, d), F32), pltpu.VMEM((d, tm), BF16)]
                        + [pltpu.VMEM((sub, tm), F32)] * (te // sub)
                        + [pltpu.VMEM((tm, sub), BF16)] * (te // sub)),
        compiler_params=_params(2, PEER_VMEM_LIMIT_BYTES),
        name="peer_experts",
    )(x, rstd, u, v, rank1, e1, cnt, e0)


def kernel(x, mem, norm1_w, w_in, attn_rel_bias, ret_gn_w, w_out, norm2_w, mem_norm_w, xattn_wq,
           xattn_wkv, xattn_wo, norm3_w, peer_wq, peer_subkeys, peer_u, peer_v, final_norm_w):
    b, s, d = x.shape
    t = b * s
    depth = w_in.shape[0]
    n_attn_heads = attn_rel_bias.shape[1]
    attn_width = n_attn_heads * ATTN_HEAD_DIM
    n_ret_heads = ret_gn_w.shape[1] // RET_V_DIM
    h = x.reshape(t, d)
    xn = _rmsnorm(h, norm1_w[0], BF16)
    for l in range(depth):
        proj = _matmul(xn, w_in[l], BF16, tn=768, name="in_proj").reshape(b, s, -1)
        a_out = _chunk_attention(proj, attn_rel_bias[l], n_attn_heads)
        r_out = _retention(proj, ret_gn_w[l], n_ret_heads, 3 * attn_width)
        h, hg, rstd = _matmul((a_out.reshape(t, -1), r_out.reshape(t, -1)), w_out[l], F32,
                              residual=h, norm_gain=norm2_w[l], name="out_proj")

        mem_n = _rmsnorm(mem.reshape(-1, d), mem_norm_w[l], BF16)
        kv = _matmul(mem_n, xattn_wkv[l], BF16, name="mem_kv").reshape(b, -1, 2 * d)
        folded_k, folded_v = _fold_memory(kv, xattn_wq[l], xattn_wo[l], XATTN_HEADS)
        probs = _mem_probs(hg.reshape(b, s, d), rstd.reshape(b, s, 1), folded_k, XATTN_HEADS)
        h, hg, rstd = _matmul(probs.reshape(t, -1), folded_v, F32, residual=h,
                              norm_gain=norm3_w[l], tn=1024, name="xattn_o")

        qp = _matmul(hg, peer_wq[l], BF16, row_scale=rstd, name="peer_q")
        rank1, e1, cnt, e0, u_b, v_b = _peer_route(qp, peer_subkeys[l], peer_u[l], peer_v[l])
        delta = _peer_experts(hg, rstd.reshape(1, t), u_b, v_b, rank1, e1, cnt, e0)
        if l + 1 < depth:
            h = h + delta.astype(F32)
            xn = _rmsnorm(h, norm1_w[l + 1], BF16)
    return _add_rmsnorm(h, delta, final_norm_w, F32).reshape(b, s, d)
```

```python
import functools

import numpy as np
import jax
import jax.numpy as jnp
from jax import lax
from jax.experimental import pallas as pl
from jax.experimental.pallas import tpu as pltpu

F32 = jnp.float32
BF16 = jnp.bfloat16

CHUNK = 64
LEFT_CHUNKS = 8
LEFT = LEFT_CHUNKS * CHUNK
ATTN_HEAD_DIM = 128
MAX_REL_DIST = 256
RET_V_DIM = 256
RET_QK_DIM = 128
ROPE_BASE = 10000.0
XATTN_HEADS = 4
PEER_HALF = 128
PEER_TOPK = 16
EPS = 1e-6
NEG_INF = -1e30
NOT_RANKED = 1e9

VMEM_LIMIT_BYTES = 56 * 1024 * 1024
LANES = 128
SUBLANES = 8

_NT = (((1,), (1,)), ((), ()))
_TN = (((0,), (0,)), ((), ()))


def _params(n_grid_dims):
    return pltpu.CompilerParams(
        dimension_semantics=("arbitrary",) * n_grid_dims,
        vmem_limit_bytes=VMEM_LIMIT_BYTES)


def _rmsnorm_kernel(x_ref, w_ref, o_ref):
    x = x_ref[...]
    ms = jnp.mean(x * x, axis=-1, keepdims=True)
    o_ref[...] = (x * lax.rsqrt(ms + EPS) * w_ref[...]).astype(o_ref.dtype)


def _rmsnorm(x, w, out_dtype, rows=512):
    m, d = x.shape
    return pl.pallas_call(
        _rmsnorm_kernel,
        grid=(m // rows,),
        in_specs=[pl.BlockSpec((rows, d), lambda i: (i, 0)),
                  pl.BlockSpec((1, d), lambda i: (0, 0))],
        out_specs=pl.BlockSpec((rows, d), lambda i: (i, 0)),
        out_shape=jax.ShapeDtypeStruct((m, d), out_dtype),
        compiler_params=_params(1),
        name="rmsnorm",
    )(x, w.reshape(1, d))


def _add_rmsnorm_kernel(x_ref, y_ref, w_ref, o_ref):
    x = x_ref[...] + y_ref[...].astype(F32)
    ms = jnp.mean(x * x, axis=-1, keepdims=True)
    o_ref[...] = (x * lax.rsqrt(ms + EPS) * w_ref[...]).astype(o_ref.dtype)


def _add_rmsnorm(x, y, w, out_dtype, rows=512):
    m, d = x.shape
    return pl.pallas_call(
        _add_rmsnorm_kernel,
        grid=(m // rows,),
        in_specs=[pl.BlockSpec((rows, d), lambda i: (i, 0)),
                  pl.BlockSpec((rows, d), lambda i: (i, 0)),
                  pl.BlockSpec((1, d), lambda i: (0, 0))],
        out_specs=pl.BlockSpec((rows, d), lambda i: (i, 0)),
        out_shape=jax.ShapeDtypeStruct((m, d), out_dtype),
        compiler_params=_params(1),
        name="add_rmsnorm",
    )(x, y, w.reshape(1, d))


def _mm_kernel(*refs, n_lhs, has_residual, has_row_scale, emit_norm, n_cols):
    lhs_refs, w_ref = refs[:n_lhs], refs[n_lhs]
    pos = n_lhs + 1
    acc, k0 = None, 0
    for a_ref in lhs_refs:
        k1 = k0 + a_ref.shape[1]
        w_blk = w_ref[k0:k1, :]
        if w_blk.dtype != BF16:
            w_blk = w_blk.astype(BF16)
        part = jnp.dot(a_ref[...], w_blk, preferred_element_type=F32)
        acc = part if acc is None else acc + part
        k0 = k1
    if has_row_scale:
        acc = acc * refs[pos][...]
        pos += 1
    if has_residual:
        acc = refs[pos][...] + acc
        pos += 1
    if not emit_norm:
        o_ref = refs[pos]
        o_ref[...] = acc.astype(o_ref.dtype)
        return
    gain_ref, o_ref, scaled_ref, rstd_ref, ssq_ref = refs[pos:pos + 5]
    j = pl.program_id(1)
    o_ref[...] = acc.astype(o_ref.dtype)
    scaled_ref[...] = (acc * gain_ref[...]).astype(scaled_ref.dtype)
    row_ssq = jnp.sum(acc * acc, axis=-1, keepdims=True)

    @pl.when(j == 0)
    def _():
        ssq_ref[...] = row_ssq

    @pl.when(j > 0)
    def _():
        ssq_ref[...] += row_ssq

    @pl.when(j == pl.num_programs(1) - 1)
    def _():
        rstd_ref[...] = lax.rsqrt(ssq_ref[...] / n_cols + EPS)


def _matmul(lhs, w, out_dtype, residual=None, row_scale=None, norm_gain=None, tm=1024, tn=512,
            name="matmul"):
    lhs = lhs if isinstance(lhs, (tuple, list)) else (lhs,)
    m = lhs[0].shape[0]
    k, n = w.shape[-2:]
    assert sum(a.shape[1] for a in lhs) == k
    tm, tn = min(tm, m), min(tn, n)
    in_specs = [pl.BlockSpec((tm, a.shape[1]), lambda i, j: (i, 0)) for a in lhs]
    if w.ndim == 2:
        in_specs.append(pl.BlockSpec((k, tn), lambda i, j: (0, j)))
    else:
        tiles_per_group = m // w.shape[0] // tm
        in_specs.append(pl.BlockSpec((None, k, tn), lambda i, j: (i // tiles_per_group, 0, j)))
    args = [*lhs, w]
    tile = pl.BlockSpec((tm, tn), lambda i, j: (i, j))
    per_row = pl.BlockSpec((tm, 1), lambda i, j: (i, 0))
    if row_scale is not None:
        in_specs.append(per_row)
        args.append(row_scale)
    if residual is not None:
        in_specs.append(tile)
        args.append(residual)
    out_specs, out_shape, scratch = tile, jax.ShapeDtypeStruct((m, n), out_dtype), []
    if norm_gain is not None:
        in_specs.append(pl.BlockSpec((1, tn), lambda i, j: (0, j)))
        args.append(norm_gain.reshape(1, n))
        out_specs = [tile, tile, per_row]
        out_shape = [out_shape, jax.ShapeDtypeStruct((m, n), BF16),
                     jax.ShapeDtypeStruct((m, 1), F32)]
        scratch = [pltpu.VMEM((tm, 1), F32)]
    return pl.pallas_call(
        functools.partial(_mm_kernel, n_lhs=len(lhs), has_residual=residual is not None,
                          has_row_scale=row_scale is not None, emit_norm=norm_gain is not None,
                          n_cols=n),
        grid=(m // tm, n // tn),
        in_specs=in_specs,
        out_specs=out_specs,
        out_shape=out_shape,
        scratch_shapes=scratch,
        compiler_params=_params(2),
        name=name,
    )(*args)


def _attn_kernel(q_ref, k_ref, v_ref, base_ref, o_ref, bias_ref, *, qb, scale):
    seq = q_ref.shape[0]
    width = LEFT + qb

    @pl.when(pl.program_id(1) == 0)
    def _():
        toeplitz = pltpu.roll(jnp.broadcast_to(base_ref[...], (qb, base_ref.shape[-1])),
                              0, 1, stride=1, stride_axis=0)
        q_chunk = lax.broadcasted_iota(jnp.int32, (qb, width), 0) // CHUNK
        c_chunk = lax.broadcasted_iota(jnp.int32, (qb, width), 1) // CHUNK
        in_band = (c_chunk >= q_chunk) & (c_chunk <= q_chunk + LEFT_CHUNKS)
        bias_ref[...] = jnp.where(in_band, toeplitz[:, :width], NEG_INF)

    for i in range(seq // qb):
        q0 = i * qb
        k0 = max(0, q0 - LEFT)
        kw = q0 + qb - k0
        c0 = k0 - (q0 - LEFT)
        s = lax.dot_general(q_ref[q0:q0 + qb, :], k_ref[k0:k0 + kw, :], _NT,
                            preferred_element_type=F32)
        s = s * scale + bias_ref[:, c0:c0 + kw]
        m = jnp.max(s, axis=-1, keepdims=True)
        p = jnp.exp(s - m)
        l = jnp.sum(p, axis=-1, keepdims=True)
        o = jnp.dot(p.astype(BF16), v_ref[k0:k0 + kw, :], preferred_element_type=F32)
        o_ref[q0:q0 + qb, :] = (o / l).astype(o_ref.dtype)


def _attn_bias_base(rel_bias, qb):
    w = pl.next_power_of_2(LEFT + 2 * qb)
    j = jnp.arange(w)
    j = jnp.where(j < LEFT + qb, j, j - w)
    idx = jnp.clip(LEFT - j, -MAX_REL_DIST, MAX_REL_DIST) + MAX_REL_DIST
    return rel_bias[:, None, idx].astype(F32)


def _chunk_attention(proj, rel_bias, n_heads, qb=256):
    b, s, _ = proj.shape
    dh = ATTN_HEAD_DIM
    base = _attn_bias_base(rel_bias, qb)
    kern = functools.partial(_attn_kernel, qb=qb, scale=dh ** -0.5)
    return pl.pallas_call(
        kern,
        grid=(n_heads, b),
        in_specs=[pl.BlockSpec((None, s, dh), lambda h, bi: (bi, 0, h)),
                  pl.BlockSpec((None, s, dh), lambda h, bi: (bi, 0, n_heads + h)),
                  pl.BlockSpec((None, s, dh), lambda h, bi: (bi, 0, 2 * n_heads + h)),
                  pl.BlockSpec((None, 1, base.shape[-1]), lambda h, bi: (h, 0, 0))],
        out_specs=pl.BlockSpec((None, s, dh), lambda h, bi: (bi, 0, h)),
        out_shape=jax.ShapeDtypeStruct((b, s, n_heads * dh), BF16),
        scratch_shapes=[pltpu.VMEM((qb, LEFT + qb), F32)],
        compiler_params=_params(2),
        name="chunk_attention",
    )(proj, proj, proj, base)


def _ret_kernel(q_ref, k_ref, v_ref, g_ref, cos_ref, sin_ref, dec_ref, qd_ref, kd_ref, bd_ref,
                gnw_ref, o_ref, state_ref, *, rb, scale):
    seq = q_ref.shape[0]
    state_ref[...] = jnp.zeros_like(state_ref)
    half = q_ref.shape[1] // 2

    def body(n, carry):
        r0 = pl.multiple_of(n * rb, rb)
        rows = pl.ds(r0, rb)
        cos = cos_ref[rows, :]
        sin = sin_ref[rows, :]
        q = q_ref[rows, :].astype(F32)
        k = k_ref[rows, :].astype(F32)
        q = q * cos + pltpu.roll(q, half, 1) * sin
        k = (k * cos + pltpu.roll(k, half, 1) * sin) * scale
        v = v_ref[rows, :]
        a = lax.dot_general(q.astype(BF16), k.astype(BF16), _NT, preferred_element_type=F32)
        a = a * dec_ref[...]
        st = state_ref[...]
        y = jnp.dot(a.astype(BF16), v, preferred_element_type=F32)
        y = y + jnp.dot((q * qd_ref[...]).astype(BF16), st.astype(BF16),
                        preferred_element_type=F32)
        kd = (k * kd_ref[...]).astype(BF16)
        kv = lax.dot_general(kd, v, _TN, preferred_element_type=F32)
        state_ref[...] = st * bd_ref[...] + kv
        mu = jnp.mean(y, axis=-1, keepdims=True)
        yc = y - mu
        var = jnp.mean(yc * yc, axis=-1, keepdims=True)
        yn = yc * lax.rsqrt(var + EPS) * gnw_ref[...]
        g = g_ref[rows, :].astype(F32)
        o_ref[rows, :] = (g * (1.0 / (1.0 + jnp.exp(-g))) * yn).astype(o_ref.dtype)
        return carry

    lax.fori_loop(0, seq // rb, body, 0)


def _retention(proj, gn_w, n_heads, col0, rb=512):
    b, s, _ = proj.shape
    dk, dv = RET_QK_DIM, RET_V_DIM
    qk0 = col0 // dk
    v0 = (col0 + 2 * n_heads * dk) // dv
    inv_freq = 1.0 / (ROPE_BASE ** (jnp.arange(0, dk, 2, dtype=F32) / dk))
    ang = jnp.arange(s, dtype=F32)[:, None] * inv_freq[None, :]
    cos = jnp.concatenate([jnp.cos(ang), jnp.cos(ang)], axis=-1)
    sin = jnp.concatenate([-jnp.sin(ang), jnp.sin(ang)], axis=-1)
    log_gamma = jnp.log1p(-jnp.power(2.0, -5.0 - jnp.arange(n_heads, dtype=F32)))
    pos = jnp.arange(rb, dtype=F32)
    chunk_of = jnp.arange(rb) // CHUNK
    causal = (chunk_of[None, :] <= chunk_of[:, None]).astype(F32)
    dec = jnp.exp(log_gamma[:, None, None] * jnp.abs(pos[:, None] - pos[None, :])) * causal[None]
    qd = jnp.exp(log_gamma[:, None] * (pos + 1.0))[:, :, None]
    kd = jnp.exp(log_gamma[:, None] * (rb - 1.0 - pos))[:, :, None]
    bd = jnp.exp(log_gamma * rb)[:, None, None]
    kern = functools.partial(_ret_kernel, rb=rb, scale=dk ** -0.5)
    return pl.pallas_call(
        kern,
        grid=(b, n_heads),
        in_specs=[pl.BlockSpec((None, s, dk), lambda bi, h: (bi, 0, qk0 + h)),
                  pl.BlockSpec((None, s, dk), lambda bi, h: (bi, 0, qk0 + n_heads + h)),
                  pl.BlockSpec((None, s, dv), lambda bi, h: (bi, 0, v0 + h)),
                  pl.BlockSpec((None, s, dv), lambda bi, h: (bi, 0, v0 + n_heads + h)),
                  pl.BlockSpec((s, dk), lambda bi, h: (0, 0)),
                  pl.BlockSpec((s, dk), lambda bi, h: (0, 0)),
                  pl.BlockSpec((None, rb, rb), lambda bi, h: (h, 0, 0)),
                  pl.BlockSpec((None, rb, 1), lambda bi, h: (h, 0, 0)),
                  pl.BlockSpec((None, rb, 1), lambda bi, h: (h, 0, 0)),
                  pl.BlockSpec((None, 1, 1), lambda bi, h: (h, 0, 0)),
                  pl.BlockSpec((1, dv), lambda bi, h: (0, h))],
        out_specs=pl.BlockSpec((None, s, dv), lambda bi, h: (bi, 0, h)),
        out_shape=jax.ShapeDtypeStruct((b, s, n_heads * dv), BF16),
        scratch_shapes=[pltpu.VMEM((dk, dv), F32)],
        compiler_params=_params(2),
        name="retention",
    )(proj, proj, proj, proj, cos, sin, dec, qd, kd, bd, gn_w.reshape(1, -1))


def _fold_keys_kernel(wq_ref, k_ref, o_ref, wq16_ref):
    @pl.when(pl.program_id(1) == 0)
    def _():
        wq16_ref[...] = wq_ref[...].astype(BF16)

    o_ref[...] = lax.dot_general(wq16_ref[...], k_ref[...], _NT,
                                 preferred_element_type=F32).astype(o_ref.dtype)


def _fold_values_kernel(v_ref, wo_ref, o_ref, wo16_ref):
    @pl.when(pl.program_id(1) == 0)
    def _():
        wo16_ref[...] = wo_ref[...].astype(BF16)

    o_ref[...] = jnp.dot(v_ref[...], wo16_ref[...],
                         preferred_element_type=F32).astype(o_ref.dtype)


def _fold_memory(kv, wq, wo, n_heads):
    b, m, d2 = kv.shape
    d = d2 // 2
    dh = d // n_heads
    folded_k = pl.pallas_call(
        _fold_keys_kernel,
        grid=(n_heads, b),
        in_specs=[pl.BlockSpec((d, dh), lambda h, bi: (0, h)),
                  pl.BlockSpec((None, m, dh), lambda h, bi: (bi, 0, h))],
        out_specs=pl.BlockSpec((None, d, m), lambda h, bi: (bi, 0, h)),
        out_shape=jax.ShapeDtypeStruct((b, d, n_heads * m), BF16),
        scratch_shapes=[pltpu.VMEM((d, dh), BF16)],
        compiler_params=_params(2),
        name="fold_keys",
    )(wq, kv)
    folded_v = pl.pallas_call(
        _fold_values_kernel,
        grid=(n_heads, b),
        in_specs=[pl.BlockSpec((None, m, dh), lambda h, bi: (bi, 0, n_heads + h)),
                  pl.BlockSpec((dh, d), lambda h, bi: (h, 0))],
        out_specs=pl.BlockSpec((None, m, d), lambda h, bi: (bi, h, 0)),
        out_shape=jax.ShapeDtypeStruct((b, n_heads * m, d), BF16),
        scratch_shapes=[pltpu.VMEM((dh, d), BF16)],
        compiler_params=_params(2),
        name="fold_values",
    )(kv, wo)
    return folded_k, folded_v


def _mem_probs_kernel(x_ref, rstd_ref, fk_ref, o_ref, *, n_heads, scale):
    m = fk_ref.shape[1] // n_heads
    s = jnp.dot(x_ref[...], fk_ref[...], preferred_element_type=F32)
    s = s * (rstd_ref[...] * scale)
    for h in range(n_heads):
        sh = s[:, h * m:(h + 1) * m]
        mx = jnp.max(sh, axis=-1, keepdims=True)
        p = jnp.exp(sh - mx)
        o_ref[:, h * m:(h + 1) * m] = (p / jnp.sum(p, axis=-1, keepdims=True)).astype(o_ref.dtype)


def _mem_probs(x, rstd, folded_k, n_heads, tq=1024):
    b, s, d = x.shape
    hm = folded_k.shape[2]
    kern = functools.partial(_mem_probs_kernel, n_heads=n_heads, scale=(d // n_heads) ** -0.5)
    return pl.pallas_call(
        kern,
        grid=(b, s // tq),
        in_specs=[pl.BlockSpec((None, tq, d), lambda bi, i: (bi, i, 0)),
                  pl.BlockSpec((None, tq, 1), lambda bi, i: (bi, i, 0)),
                  pl.BlockSpec((None, d, hm), lambda bi, i: (bi, 0, 0))],
        out_specs=pl.BlockSpec((None, tq, hm), lambda bi, i: (bi, i, 0)),
        out_shape=jax.ShapeDtypeStruct((b, s, hm), BF16),
        compiler_params=_params(2),
        name="mem_probs",
    )(x, rstd, folded_k)


def _take_max(work, rows, exact):
    m = jnp.max(work, axis=0, keepdims=True)
    hit = work == m
    if not exact:
        return m, hit
    first = jnp.min(jnp.where(hit, rows, np.float32(work.shape[0])), axis=0, keepdims=True)
    return m, rows == first


def _top_rows(s, k, exact, want_rank=True):
    rows = lax.broadcasted_iota(jnp.int32, s.shape, 0).astype(F32)
    work = s
    rank = jnp.full(s.shape, NOT_RANKED, F32) if want_rank else None
    vals = []
    for r in range(k):
        m, sel = _take_max(work, rows, exact)
        if want_rank:
            rank = jnp.where(sel, np.float32(r), rank)
        work = jnp.where(sel, -jnp.inf, work)
        vals.append(m)
    taken = jnp.sum(jnp.where(work == -jnp.inf, 1.0, 0.0), axis=0, keepdims=True)
    return vals, rank, taken


def _candidate_cells(k):
    cells = []
    for ra in range(k):
        cells += [(ra, rb) for rb in range(k // (ra + 1))]
    single = [c for c in cells if k // (c[0] + 1) == 1]
    multi = [c for c in cells if c not in single]
    pad = (-len(multi)) % 8
    return multi + [None] * pad + single


def _route_kernel(q_ref, keys_ref, u_ref, v_ref, rank1_ref, e1_ref, cnt_ref, e0_ref, ub_ref, vb_ref):
    ub_ref[...] = u_ref[...].astype(ub_ref.dtype)
    vb_ref[...] = v_ref[...].astype(vb_ref.dtype)
    k = PEER_TOPK
    tl = q_ref.shape[0]
    cells = _candidate_cells(k)

    def route_group(g, exact):
        q = q_ref[g * LANES:(g + 1) * LANES, :]
        s0 = lax.dot_general(keys_ref[0].astype(BF16), q[:, :PEER_HALF], _NT,
                             preferred_element_type=F32)
        s1 = lax.dot_general(keys_ref[1].astype(BF16), q[:, PEER_HALF:], _NT,
                             preferred_element_type=F32)
        a, rank0, taken0 = _top_rows(s0, k, exact, want_rank=exact)
        b, rank1, taken1 = _top_rows(s1, k, exact)
        a_all, b_all = jnp.concatenate(a, axis=0), jnp.concatenate(b, axis=0)
        ea_all, eb_all = jnp.exp(a_all - a[0]), jnp.exp(b_all - b[0])
        neg = jnp.full_like(a[0], -jnp.inf)
        zero = jnp.zeros_like(a[0])
        n0, n1 = k, k // 2
        mid = cells[n0 + n1:len(cells) - k // 2]
        cand = jnp.concatenate(
            [a[0] + b_all, a[1] + b_all[:n1]]
            + [neg if c is None else a_all[c[0]:c[0] + 1] + b_all[c[1]:c[1] + 1] for c in mid]
            + [a_all[k // 2:] + b[0]], axis=0)
        wgt = jnp.concatenate(
            [ea_all[0:1] * eb_all, ea_all[1:2] * eb_all[:n1]]
            + [zero if c is None else ea_all[c[0]:c[0] + 1] * eb_all[c[1]:c[1] + 1] for c in mid]
            + [ea_all[k // 2:] * eb_all[0:1]], axis=0)
        rows = lax.broadcasted_iota(jnp.int32, cand.shape, 0).astype(F32)
        work = cand
        chosen = jnp.zeros(cand.shape, F32)
        for _ in range(k):
            _, sel = _take_max(work, rows, exact)
            chosen = jnp.where(sel, 1.0, chosen)
            work = jnp.where(sel, -jnp.inf, work)
        z = jnp.sum(chosen * wgt, axis=0, keepdims=True)
        cnt = jnp.zeros(s0.shape, F32)
        for ra in range(k):
            mine = [i for i, c in enumerate(cells) if c is not None and c[0] == ra]
            n_ra = jnp.sum(chosen[mine[0]:mine[-1] + 1], axis=0, keepdims=True)
            is_ra = rank0 == np.float32(ra) if exact else s0 == a[ra]
            cnt = jnp.where(is_ra, n_ra, cnt)
        lanes = slice(g * LANES, (g + 1) * LANES)
        rank1_ref[:, lanes] = rank1.astype(rank1_ref.dtype)
        e1_ref[:, lanes] = (jnp.exp(s1 - b[0]) / z).astype(e1_ref.dtype)
        cnt_ref[:, lanes] = cnt
        e0_ref[:, lanes] = jnp.exp(s0 - a[0])
        taken2 = jnp.sum(chosen, axis=0, keepdims=True)
        return (jnp.abs(taken0 - k) + jnp.abs(taken1 - k) + jnp.abs(taken2 - k))

    groups = range(tl // LANES)
    tie = sum(route_group(g, exact=False) for g in groups)

    @pl.when(jnp.max(tie) > 0.0)
    def _():
        for g in groups:
            route_group(g, exact=True)


def _peer_route(qp, subkeys, u, v, tl=512):
    t = qp.shape[0]
    hp, _, nk, half = subkeys.shape
    n_exp, d = u.shape
    steps = (t // tl) * hp
    rows = n_exp // steps
    assert rows * steps == n_exp
    out = lambda dt: jax.ShapeDtypeStruct((hp, nk, t), dt)
    ospec = pl.BlockSpec((None, nk, tl), lambda i, h: (h, 0, i))
    wspec = pl.BlockSpec((rows, d), lambda i, h: (i * hp + h, 0))
    wout = jax.ShapeDtypeStruct((n_exp, d), BF16)
    return pl.pallas_call(
        _route_kernel,
        grid=(t // tl, hp),
        in_specs=[pl.BlockSpec((tl, 2 * half), lambda i, h: (i, h)),
                  pl.BlockSpec((None, 2, nk, half), lambda i, h: (h, 0, 0, 0)),
                  wspec, wspec],
        out_specs=[ospec] * 4 + [wspec] * 2,
        out_shape=[out(BF16), out(BF16), out(F32), out(F32), wout, wout],
        compiler_params=_params(2),
        name="peer_route",
    )(qp, subkeys, u, v)


def _peer_kernel(x_ref, rstd_ref, u_ref, v_ref, rank1_ref, e1_ref, cnt_ref, e0_ref, o_ref, acc_ref,
                 xt_ref, *stage_refs, sub, n_pieces):
    e = pl.program_id(1)
    n_heads, nk, tm = rank1_ref.shape
    te, d = u_ref.shape
    n_sub = te // sub
    rows_per_sub = sub // nk
    kc = d // n_pieces
    rows_per_step = te // nk
    row0 = (e % (SUBLANES // rows_per_step)) * rows_per_step
    act_refs, wt_refs = stage_refs[:n_sub], stage_refs[n_sub:]

    @pl.when(e == 0)
    def _():
        acc_ref[...] = jnp.zeros_like(acc_ref)
        xt_ref[...] = x_ref[...].T

    def up_piece(j, p):
        part = jnp.dot(u_ref[j * sub:(j + 1) * sub, p * kc:(p + 1) * kc],
                       xt_ref[p * kc:(p + 1) * kc, :], preferred_element_type=F32)
        if p == 0:
            act_refs[j][...] = part
        else:
            act_refs[j][...] += part

    def gate_piece(j, p):
        ii, g = divmod(p, n_pieces // rows_per_sub)
        tg = tm // (n_pieces // rows_per_sub)
        lanes = slice(g * tg, (g + 1) * tg)
        row = j * rows_per_sub + ii
        gate = None
        for h in range(n_heads):
            cnt = cnt_ref[h, pl.ds(row0 + row, 1), :][:, lanes].astype(BF16)
            e0 = e0_ref[h, pl.ds(row0 + row, 1), :][:, lanes].astype(BF16)
            zero = jnp.zeros((), BF16)
            term = e0 * jnp.where(rank1_ref[h, :, lanes] < cnt, e1_ref[h, :, lanes], zero)
            gate = term if gate is None else gate + term
        act = act_refs[j][ii * nk:(ii + 1) * nk, lanes] * rstd_ref[:, lanes]
        act = 0.5 * act * (1.0 + lax.erf(act * np.float32(1.0 / np.sqrt(2.0))))
        wt_refs[j][lanes, ii * nk:(ii + 1) * nk] = (gate * act.astype(BF16)).T

    def down_piece(j, p):
        cols = slice(p * kc, (p + 1) * kc)
        acc_ref[:, cols] += jnp.dot(wt_refs[j][...], v_ref[j * sub:(j + 1) * sub, cols],
                                    preferred_element_type=F32)

    for step in range(n_sub + 2):
        for p in range(n_pieces):
            if step < n_sub:
                up_piece(step, p)
            if 0 <= step - 1 < n_sub:
                gate_piece(step - 1, p)
            if 0 <= step - 2 < n_sub:
                down_piece(step - 2, p)

    @pl.when(e == pl.num_programs(1) - 1)
    def _():
        o_ref[...] = acc_ref[...].astype(o_ref.dtype)


def _peer_experts(x, rstd, u, v, rank1, e1, cnt, e0, tm=512, te=512, sub=256, n_pieces=4):
    t, d = x.shape
    n_exp = u.shape[0]
    hp, nk, _ = rank1.shape
    rows = te // nk
    tab = pl.BlockSpec((hp, nk, tm), lambda i, e: (0, 0, i))
    assert SUBLANES % rows == 0
    row_tab = pl.BlockSpec((hp, SUBLANES, tm), lambda i, e: (0, (e * rows) // SUBLANES, i))
    return pl.pallas_call(
        functools.partial(_peer_kernel, sub=sub, n_pieces=n_pieces),
        grid=(t // tm, n_exp // te),
        in_specs=[pl.BlockSpec((tm, d), lambda i, e: (i, 0)),
                  pl.BlockSpec((1, tm), lambda i, e: (0, i)),
                  pl.BlockSpec((te, d), lambda i, e: (e, 0)),
                  pl.BlockSpec((te, d), lambda i, e: (e, 0)),
                  tab, tab, row_tab, row_tab],
        out_specs=pl.BlockSpec((tm, d), lambda i, e: (i, 0)),
        out_shape=jax.ShapeDtypeStruct((t, d), BF16),
        scratch_shapes=([pltpu.VMEM((tm, d), F32), pltpu.VMEM((d, tm), BF16)]
                        + [pltpu.VMEM((sub, tm), F32)] * (te // sub)
                        + [pltpu.VMEM((tm, sub), BF16)] * (te // sub)),
        compiler_params=_params(2),
        name="peer_experts",
    )(x, rstd, u, v, rank1, e1, cnt, e0)


def kernel(x, mem, norm1_w, w_in, attn_rel_bias, ret_gn_w, w_out, norm2_w, mem_norm_w, xattn_wq,
           xattn_wkv, xattn_wo, norm3_w, peer_wq, peer_subkeys, peer_u, peer_v, final_norm_w):
    b, s, d = x.shape
    t = b * s
    depth = w_in.shape[0]
    n_attn_heads = attn_rel_bias.shape[1]
    attn_width = n_attn_heads * ATTN_HEAD_DIM
    n_ret_heads = ret_gn_w.shape[1] // RET_V_DIM
    h = x.reshape(t, d)
    xn = _rmsnorm(h, norm1_w[0], BF16)
    for l in range(depth):
        proj = _matmul(xn, w_in[l], BF16, tn=768, name="in_proj").reshape(b, s, -1)
        a_out = _chunk_attention(proj, attn_rel_bias[l], n_attn_heads)
        r_out = _retention(proj, ret_gn_w[l], n_ret_heads, 3 * attn_width)
        h, hg, rstd = _matmul((a_out.reshape(t, -1), r_out.reshape(t, -1)), w_out[l], F32,
                              residual=h, norm_gain=norm2_w[l], name="out_proj")

        mem_n = _rmsnorm(mem.reshape(-1, d), mem_norm_w[l], BF16)
        kv = _matmul(mem_n, xattn_wkv[l], BF16, name="mem_kv").reshape(b, -1, 2 * d)
        folded_k, folded_v = _fold_memory(kv, xattn_wq[l], xattn_wo[l], XATTN_HEADS)
        probs = _mem_probs(hg.reshape(b, s, d), rstd.reshape(b, s, 1), folded_k, XATTN_HEADS)
        h, hg, rstd = _matmul(probs.reshape(t, -1), folded_v, F32, residual=h,
                              norm_gain=norm3_w[l], tn=1024, name="xattn_o")

        qp = _matmul(hg, peer_wq[l], BF16, row_scale=rstd, name="peer_q")
        rank1, e1, cnt, e0, u_b, v_b = _peer_route(qp, peer_subkeys[l], peer_u[l], peer_v[l])
        delta = _peer_experts(hg, rstd.reshape(1, t), u_b, v_b, rank1, e1, cnt, e0)
        if l + 1 < depth:
            h = h + delta.astype(F32)
            xn = _rmsnorm(h, norm1_w[l + 1], BF16)
    return _add_rmsnorm(h, delta, final_norm_w, F32).reshape(b, s, d)
```

```python
import functools

import numpy as np
import jax
import jax.numpy as jnp
from jax import lax
from jax.experimental import pallas as pl
from jax.experimental.pallas import tpu as pltpu

F32 = jnp.float32
BF16 = jnp.bfloat16

CHUNK = 64
LEFT_CHUNKS = 8
LEFT = LEFT_CHUNKS * CHUNK
ATTN_HEAD_DIM = 128
MAX_REL_DIST = 256
RET_V_DIM = 256
RET_QK_DIM = 128
ROPE_BASE = 10000.0
XATTN_HEADS = 4
PEER_HALF = 128
PEER_TOPK = 16
EPS = 1e-6
NEG_INF = -1e30
NOT_RANKED = 1e9

VMEM_LIMIT_BYTES = 56 * 1024 * 1024
LANES = 128
SUBLANES = 8

_NT = (((1,), (1,)), ((), ()))
_TN = (((0,), (0,)), ((), ()))


def _params(n_grid_dims):
    return pltpu.CompilerParams(
        dimension_semantics=("arbitrary",) * n_grid_dims,
        vmem_limit_bytes=VMEM_LIMIT_BYTES)


def _rmsnorm_kernel(x_ref, w_ref, o_ref):
    x = x_ref[...]
    ms = jnp.mean(x * x, axis=-1, keepdims=True)
    o_ref[...] = (x * lax.rsqrt(ms + EPS) * w_ref[...]).astype(o_ref.dtype)


def _rmsnorm(x, w, out_dtype, rows=512):
    m, d = x.shape
    return pl.pallas_call(
        _rmsnorm_kernel,
        grid=(m // rows,),
        in_specs=[pl.BlockSpec((rows, d), lambda i: (i, 0)),
                  pl.BlockSpec((1, d), lambda i: (0, 0))],
        out_specs=pl.BlockSpec((rows, d), lambda i: (i, 0)),
        out_shape=jax.ShapeDtypeStruct((m, d), out_dtype),
        compiler_params=_params(1),
        name="rmsnorm",
    )(x, w.reshape(1, d))


def _add_rmsnorm_kernel(x_ref, y_ref, w_ref, o_ref):
    x = x_ref[...] + y_ref[...].astype(F32)
    ms = jnp.mean(x * x, axis=-1, keepdims=True)
    o_ref[...] = (x * lax.rsqrt(ms + EPS) * w_ref[...]).astype(o_ref.dtype)


def _add_rmsnorm(x, y, w, out_dtype, rows=512):
    m, d = x.shape
    return pl.pallas_call(
        _add_rmsnorm_kernel,
        grid=(m // rows,),
        in_specs=[pl.BlockSpec((rows, d), lambda i: (i, 0)),
                  pl.BlockSpec((rows, d), lambda i: (i, 0)),
                  pl.BlockSpec((1, d), lambda i: (0, 0))],
        out_specs=pl.BlockSpec((rows, d), lambda i: (i, 0)),
        out_shape=jax.ShapeDtypeStruct((m, d), out_dtype),
        compiler_params=_params(1),
        name="add_rmsnorm",
    )(x, y, w.reshape(1, d))


def _mm_kernel(*refs, n_lhs, has_residual, has_row_scale, emit_norm, n_cols):
    lhs_refs, w_ref = refs[:n_lhs], refs[n_lhs]
    pos = n_lhs + 1
    acc, k0 = None, 0
    for a_ref in lhs_refs:
        k1 = k0 + a_ref.shape[1]
        w_blk = w_ref[k0:k1, :]
        if w_blk.dtype != BF16:
            w_blk = w_blk.astype(BF16)
        part = jnp.dot(a_ref[...], w_blk, preferred_element_type=F32)
        acc = part if acc is None else acc + part
        k0 = k1
    if has_row_scale:
        acc = acc * refs[pos][...]
        pos += 1
    if has_residual:
        acc = refs[pos][...] + acc
        pos += 1
    if not emit_norm:
        o_ref = refs[pos]
        o_ref[...] = acc.astype(o_ref.dtype)
        return
    gain_ref, o_ref, scaled_ref, rstd_ref, ssq_ref = refs[pos:pos + 5]
    j = pl.program_id(1)
    o_ref[...] = acc.astype(o_ref.dtype)
    scaled_ref[...] = (acc * gain_ref[...]).astype(scaled_ref.dtype)
    row_ssq = jnp.sum(acc * acc, axis=-1, keepdims=True)

    @pl.when(j == 0)
    def _():
        ssq_ref[...] = row_ssq

    @pl.when(j > 0)
    def _():
        ssq_ref[...] += row_ssq

    @pl.when(j == pl.num_programs(1) - 1)
    def _():
        rstd_ref[...] = lax.rsqrt(ssq_ref[...] / n_cols + EPS)


def _matmul(lhs, w, out_dtype, residual=None, row_scale=None, norm_gain=None, tm=1024, tn=512,
            name="matmul"):
    lhs = lhs if isinstance(lhs, (tuple, list)) else (lhs,)
    m = lhs[0].shape[0]
    k, n = w.shape[-2:]
    assert sum(a.shape[1] for a in lhs) == k
    tm, tn = min(tm, m), min(tn, n)
    in_specs = [pl.BlockSpec((tm, a.shape[1]), lambda i, j: (i, 0)) for a in lhs]
    if w.ndim == 2:
        in_specs.append(pl.BlockSpec((k, tn), lambda i, j: (0, j)))
    else:
        tiles_per_group = m // w.shape[0] // tm
        in_specs.append(pl.BlockSpec((None, k, tn), lambda i, j: (i // tiles_per_group, 0, j)))
    args = [*lhs, w]
    tile = pl.BlockSpec((tm, tn), lambda i, j: (i, j))
    per_row = pl.BlockSpec((tm, 1), lambda i, j: (i, 0))
    if row_scale is not None:
        in_specs.append(per_row)
        args.append(row_scale)
    if residual is not None:
        in_specs.append(tile)
        args.append(residual)
    out_specs, out_shape, scratch = tile, jax.ShapeDtypeStruct((m, n), out_dtype), []
    if norm_gain is not None:
        in_specs.append(pl.BlockSpec((1, tn), lambda i, j: (0, j)))
        args.append(norm_gain.reshape(1, n))
        out_specs = [tile, tile, per_row]
        out_shape = [out_shape, jax.ShapeDtypeStruct((m, n), BF16),
                     jax.ShapeDtypeStruct((m, 1), F32)]
        scratch = [pltpu.VMEM((tm, 1), F32)]
    return pl.pallas_call(
        functools.partial(_mm_kernel, n_lhs=len(lhs), has_residual=residual is not None,
                          has_row_scale=row_scale is not None, emit_norm=norm_gain is not None,
                          n_cols=n),
        grid=(m // tm, n // tn),
        in_specs=in_specs,
        out_specs=out_specs,
        out_shape=out_shape,
        scratch_shapes=scratch,
        compiler_params=_params(2),
        name=name,
    )(*args)


def _attn_kernel(q_ref, k_ref, v_ref, base_ref, o_ref, bias_ref, *, qb, scale, hpb):
    seq = q_ref.shape[0]
    width = LEFT + qb
    dh = q_ref.shape[1] // hpb

    @pl.when(pl.program_id(1) == 0)
    def _():
        q_chunk = lax.broadcasted_iota(jnp.int32, (qb, width), 0) // CHUNK
        c_chunk = lax.broadcasted_iota(jnp.int32, (qb, width), 1) // CHUNK
        in_band = (c_chunk >= q_chunk) & (c_chunk <= q_chunk + LEFT_CHUNKS)
        for hh in range(hpb):
            toeplitz = pltpu.roll(jnp.broadcast_to(base_ref[hh], (qb, base_ref.shape[-1])),
                                  0, 1, stride=1, stride_axis=0)
            bias_ref[hh] = jnp.where(in_band, toeplitz[:, :width], NEG_INF)

    for i in range(seq // qb):
        q0 = i * qb
        k0 = max(0, q0 - LEFT)
        kw = q0 + qb - k0
        c0 = k0 - (q0 - LEFT)
        for hh in range(hpb):
            cols = slice(hh * dh, (hh + 1) * dh)
            s = lax.dot_general(q_ref[q0:q0 + qb, cols], k_ref[k0:k0 + kw, cols], _NT,
                                preferred_element_type=F32)
            s = s * scale + bias_ref[hh, :, c0:c0 + kw]
            m = jnp.max(s, axis=-1, keepdims=True)
            p = jnp.exp(s - m)
            l = jnp.sum(p, axis=-1, keepdims=True)
            o = jnp.dot(p.astype(BF16), v_ref[k0:k0 + kw, cols], preferred_element_type=F32)
            o_ref[q0:q0 + qb, cols] = (o / l).astype(o_ref.dtype)


def _attn_bias_base(rel_bias, qb):
    w = pl.next_power_of_2(LEFT + 2 * qb)
    j = jnp.arange(w)
    j = jnp.where(j < LEFT + qb, j, j - w)
    idx = jnp.clip(LEFT - j, -MAX_REL_DIST, MAX_REL_DIST) + MAX_REL_DIST
    return rel_bias[:, None, idx].astype(F32)


def _chunk_attention(proj, rel_bias, n_heads, qb=256, hpb=4):
    b, s, _ = proj.shape
    dh = ATTN_HEAD_DIM
    base = _attn_bias_base(rel_bias, qb)
    groups = n_heads // hpb
    kern = functools.partial(_attn_kernel, qb=qb, scale=dh ** -0.5, hpb=hpb)
    return pl.pallas_call(
        kern,
        grid=(groups, b),
        in_specs=[pl.BlockSpec((None, s, hpb * dh), lambda h, bi: (bi, 0, h)),
                  pl.BlockSpec((None, s, hpb * dh), lambda h, bi: (bi, 0, groups + h)),
                  pl.BlockSpec((None, s, hpb * dh), lambda h, bi: (bi, 0, 2 * groups + h)),
                  pl.BlockSpec((hpb, 1, base.shape[-1]), lambda h, bi: (h, 0, 0))],
        out_specs=pl.BlockSpec((None, s, hpb * dh), lambda h, bi: (bi, 0, h)),
        out_shape=jax.ShapeDtypeStruct((b, s, n_heads * dh), BF16),
        scratch_shapes=[pltpu.VMEM((hpb, qb, LEFT + qb), F32)],
        compiler_params=_params(2),
        name="chunk_attention",
    )(proj, proj, proj, base)


def _ret_kernel(q_ref, k_ref, v_ref, g_ref, cos_ref, sin_ref, dec_ref, qd_ref, kd_ref, bd_ref,
                gnw_ref, o_ref, state_ref, *, rb, scale):
    seq = q_ref.shape[0]
    state_ref[...] = jnp.zeros_like(state_ref)
    half = q_ref.shape[1] // 2

    def body(n, carry):
        r0 = pl.multiple_of(n * rb, rb)
        rows = pl.ds(r0, rb)
        cos = cos_ref[rows, :]
        sin = sin_ref[rows, :]
        q = q_ref[rows, :].astype(F32)
        k = k_ref[rows, :].astype(F32)
        q = q * cos + pltpu.roll(q, half, 1) * sin
        k = (k * cos + pltpu.roll(k, half, 1) * sin) * scale
        v = v_ref[rows, :]
        a = lax.dot_general(q.astype(BF16), k.astype(BF16), _NT, preferred_element_type=F32)
        a = a * dec_ref[...]
        st = state_ref[...]
        y = jnp.dot(a.astype(BF16), v, preferred_element_type=F32)
        y = y + jnp.dot((q * qd_ref[...]).astype(BF16), st.astype(BF16),
                        preferred_element_type=F32)
        kd = (k * kd_ref[...]).astype(BF16)
        kv = lax.dot_general(kd, v, _TN, preferred_element_type=F32)
        state_ref[...] = st * bd_ref[...] + kv
        mu = jnp.mean(y, axis=-1, keepdims=True)
        yc = y - mu
        var = jnp.mean(yc * yc, axis=-1, keepdims=True)
        yn = yc * lax.rsqrt(var + EPS) * gnw_ref[...]
        g = g_ref[rows, :].astype(F32)
        o_ref[rows, :] = (g * (1.0 / (1.0 + jnp.exp(-g))) * yn).astype(o_ref.dtype)
        return carry

    lax.fori_loop(0, seq // rb, body, 0)


def _retention(proj, gn_w, n_heads, col0, rb=512):
    b, s, _ = proj.shape
    dk, dv = RET_QK_DIM, RET_V_DIM
    qk0 = col0 // dk
    v0 = (col0 + 2 * n_heads * dk) // dv
    inv_freq = 1.0 / (ROPE_BASE ** (jnp.arange(0, dk, 2, dtype=F32) / dk))
    ang = jnp.arange(s, dtype=F32)[:, None] * inv_freq[None, :]
    cos = jnp.concatenate([jnp.cos(ang), jnp.cos(ang)], axis=-1)
    sin = jnp.concatenate([-jnp.sin(ang), jnp.sin(ang)], axis=-1)
    log_gamma = jnp.log1p(-jnp.power(2.0, -5.0 - jnp.arange(n_heads, dtype=F32)))
    pos = jnp.arange(rb, dtype=F32)
    chunk_of = jnp.arange(rb) // CHUNK
    causal = (chunk_of[None, :] <= chunk_of[:, None]).astype(F32)
    dec = jnp.exp(log_gamma[:, None, None] * jnp.abs(pos[:, None] - pos[None, :])) * causal[None]
    qd = jnp.exp(log_gamma[:, None] * (pos + 1.0))[:, :, None]
    kd = jnp.exp(log_gamma[:, None] * (rb - 1.0 - pos))[:, :, None]
    bd = jnp.exp(log_gamma * rb)[:, None, None]
    kern = functools.partial(_ret_kernel, rb=rb, scale=dk ** -0.5)
    return pl.pallas_call(
        kern,
        grid=(b, n_heads),
        in_specs=[pl.BlockSpec((None, s, dk), lambda bi, h: (bi, 0, qk0 + h)),
                  pl.BlockSpec((None, s, dk), lambda bi, h: (bi, 0, qk0 + n_heads + h)),
                  pl.BlockSpec((None, s, dv), lambda bi, h: (bi, 0, v0 + h)),
                  pl.BlockSpec((None, s, dv), lambda bi, h: (bi, 0, v0 + n_heads + h)),
                  pl.BlockSpec((s, dk), lambda bi, h: (0, 0)),
                  pl.BlockSpec((s, dk), lambda bi, h: (0, 0)),
                  pl.BlockSpec((None, rb, rb), lambda bi, h: (h, 0, 0)),
                  pl.BlockSpec((None, rb, 1), lambda bi, h: (h, 0, 0)),
                  pl.BlockSpec((None, rb, 1), lambda bi, h: (h, 0, 0)),
                  pl.BlockSpec((None, 1, 1), lambda bi, h: (h, 0, 0)),
                  pl.BlockSpec((1, dv), lambda bi, h: (0, h))],
        out_specs=pl.BlockSpec((None, s, dv), lambda bi, h: (bi, 0, h)),
        out_shape=jax.ShapeDtypeStruct((b, s, n_heads * dv), BF16),
        scratch_shapes=[pltpu.VMEM((dk, dv), F32)],
        compiler_params=_params(2),
        name="retention",
    )(proj, proj, proj, proj, cos, sin, dec, qd, kd, bd, gn_w.reshape(1, -1))


def _fold_keys_kernel(wq_ref, k_ref, o_ref, wq16_ref):
    @pl.when(pl.program_id(1) == 0)
    def _():
        wq16_ref[...] = wq_ref[...].astype(BF16)

    o_ref[...] = lax.dot_general(wq16_ref[...], k_ref[...], _NT,
                                 preferred_element_type=F32).astype(o_ref.dtype)


def _fold_values_kernel(v_ref, wo_ref, o_ref, wo16_ref):
    @pl.when(pl.program_id(1) == 0)
    def _():
        wo16_ref[...] = wo_ref[...].astype(BF16)

    o_ref[...] = jnp.dot(v_ref[...], wo16_ref[...],
                         preferred_element_type=F32).astype(o_ref.dtype)


def _fold_memory(kv, wq, wo, n_heads):
    b, m, d2 = kv.shape
    d = d2 // 2
    dh = d // n_heads
    folded_k = pl.pallas_call(
        _fold_keys_kernel,
        grid=(n_heads, b),
        in_specs=[pl.BlockSpec((d, dh), lambda h, bi: (0, h)),
                  pl.BlockSpec((None, m, dh), lambda h, bi: (bi, 0, h))],
        out_specs=pl.BlockSpec((None, d, m), lambda h, bi: (bi, 0, h)),
        out_shape=jax.ShapeDtypeStruct((b, d, n_heads * m), BF16),
        scratch_shapes=[pltpu.VMEM((d, dh), BF16)],
        compiler_params=_params(2),
        name="fold_keys",
    )(wq, kv)
    folded_v = pl.pallas_call(
        _fold_values_kernel,
        grid=(n_heads, b),
        in_specs=[pl.BlockSpec((None, m, dh), lambda h, bi: (bi, 0, n_heads + h)),
                  pl.BlockSpec((dh, d), lambda h, bi: (h, 0))],
        out_specs=pl.BlockSpec((None, m, d), lambda h, bi: (bi, h, 0)),
        out_shape=jax.ShapeDtypeStruct((b, n_heads * m, d), BF16),
        scratch_shapes=[pltpu.VMEM((dh, d), BF16)],
        compiler_params=_params(2),
        name="fold_values",
    )(kv, wo)
    return folded_k, folded_v


def _mem_probs_kernel(x_ref, rstd_ref, fk_ref, o_ref, *, n_heads, scale):
    m = fk_ref.shape[1] // n_heads
    s = jnp.dot(x_ref[...], fk_ref[...], preferred_element_type=F32)
    s = s * (rstd_ref[...] * scale)
    for h in range(n_heads):
        sh = s[:, h * m:(h + 1) * m]
        mx = jnp.max(sh, axis=-1, keepdims=True)
        p = jnp.exp(sh - mx)
        o_ref[:, h * m:(h + 1) * m] = (p / jnp.sum(p, axis=-1, keepdims=True)).astype(o_ref.dtype)


def _mem_probs(x, rstd, folded_k, n_heads, tq=1024):
    b, s, d = x.shape
    hm = folded_k.shape[2]
    kern = functools.partial(_mem_probs_kernel, n_heads=n_heads, scale=(d // n_heads) ** -0.5)
    return pl.pallas_call(
        kern,
        grid=(b, s // tq),
        in_specs=[pl.BlockSpec((None, tq, d), lambda bi, i: (bi, i, 0)),
                  pl.BlockSpec((None, tq, 1), lambda bi, i: (bi, i, 0)),
                  pl.BlockSpec((None, d, hm), lambda bi, i: (bi, 0, 0))],
        out_specs=pl.BlockSpec((None, tq, hm), lambda bi, i: (bi, i, 0)),
        out_shape=jax.ShapeDtypeStruct((b, s, hm), BF16),
        compiler_params=_params(2),
        name="mem_probs",
    )(x, rstd, folded_k)


def _take_max(work, rows, exact):
    m = jnp.max(work, axis=0, keepdims=True)
    hit = work == m
    if not exact:
        return m, hit
    first = jnp.min(jnp.where(hit, rows, np.float32(work.shape[0])), axis=0, keepdims=True)
    return m, rows == first


def _top_rows(s, k, exact, want_rank=True):
    rows = lax.broadcasted_iota(jnp.int32, s.shape, 0).astype(F32)
    work = s
    rank = jnp.full(s.shape, NOT_RANKED, F32) if want_rank else None
    vals = []
    for r in range(k):
        m, sel = _take_max(work, rows, exact)
        if want_rank:
            rank = jnp.where(sel, np.float32(r), rank)
        work = jnp.where(sel, -jnp.inf, work)
        vals.append(m)
    taken = jnp.sum(jnp.where(work == -jnp.inf, 1.0, 0.0), axis=0, keepdims=True)
    return vals, rank, taken


def _candidate_cells(k):
    cells = []
    for ra in range(k):
        cells += [(ra, rb) for rb in range(k // (ra + 1))]
    single = [c for c in cells if k // (c[0] + 1) == 1]
    multi = [c for c in cells if c not in single]
    pad = (-len(multi)) % 8
    return multi + [None] * pad + single


def _route_kernel(q_ref, keys_ref, u_ref, v_ref, rank1_ref, e1_ref, cnt_ref, e0_ref, ub_ref, vb_ref):
    ub_ref[...] = u_ref[...].astype(ub_ref.dtype)
    vb_ref[...] = v_ref[...].astype(vb_ref.dtype)
    k = PEER_TOPK
    tl = q_ref.shape[0]
    cells = _candidate_cells(k)

    def route_group(g, exact):
        q = q_ref[g * LANES:(g + 1) * LANES, :]
        s0 = lax.dot_general(keys_ref[0].astype(BF16), q[:, :PEER_HALF], _NT,
                             preferred_element_type=F32)
        s1 = lax.dot_general(keys_ref[1].astype(BF16), q[:, PEER_HALF:], _NT,
                             preferred_element_type=F32)
        a, rank0, taken0 = _top_rows(s0, k, exact, want_rank=exact)
        b, rank1, taken1 = _top_rows(s1, k, exact)
        a_all, b_all = jnp.concatenate(a, axis=0), jnp.concatenate(b, axis=0)
        ea_all, eb_all = jnp.exp(a_all - a[0]), jnp.exp(b_all - b[0])
        neg = jnp.full_like(a[0], -jnp.inf)
        zero = jnp.zeros_like(a[0])
        n0, n1 = k, k // 2
        mid = cells[n0 + n1:len(cells) - k // 2]
        cand = jnp.concatenate(
            [a[0] + b_all, a[1] + b_all[:n1]]
            + [neg if c is None else a_all[c[0]:c[0] + 1] + b_all[c[1]:c[1] + 1] for c in mid]
            + [a_all[k // 2:] + b[0]], axis=0)
        wgt = jnp.concatenate(
            [ea_all[0:1] * eb_all, ea_all[1:2] * eb_all[:n1]]
            + [zero if c is None else ea_all[c[0]:c[0] + 1] * eb_all[c[1]:c[1] + 1] for c in mid]
            + [ea_all[k // 2:] * eb_all[0:1]], axis=0)
        rows = lax.broadcasted_iota(jnp.int32, cand.shape, 0).astype(F32)
        work = cand
        chosen = jnp.zeros(cand.shape, F32)
        for _ in range(k):
            _, sel = _take_max(work, rows, exact)
            chosen = jnp.where(sel, 1.0, chosen)
            work = jnp.where(sel, -jnp.inf, work)
        z = jnp.sum(chosen * wgt, axis=0, keepdims=True)
        cnt = jnp.zeros(s0.shape, F32)
        for ra in range(k):
            mine = [i for i, c in enumerate(cells) if c is not None and c[0] == ra]
            n_ra = jnp.sum(chosen[mine[0]:mine[-1] + 1], axis=0, keepdims=True)
            is_ra = rank0 == np.float32(ra) if exact else s0 == a[ra]
            cnt = jnp.where(is_ra, n_ra, cnt)
        lanes = slice(g * LANES, (g + 1) * LANES)
        rank1_ref[:, lanes] = rank1.astype(rank1_ref.dtype)
        e1_ref[:, lanes] = (jnp.exp(s1 - b[0]) / z).astype(e1_ref.dtype)
        cnt_ref[:, lanes] = cnt
        e0_ref[:, lanes] = jnp.exp(s0 - a[0])
        taken2 = jnp.sum(chosen, axis=0, keepdims=True)
        return (jnp.abs(taken0 - k) + jnp.abs(taken1 - k) + jnp.abs(taken2 - k))

    groups = range(tl // LANES)
    tie = sum(route_group(g, exact=False) for g in groups)

    @pl.when(jnp.max(tie) > 0.0)
    def _():
        for g in groups:
            route_group(g, exact=True)


def _peer_route(qp, subkeys, u, v, tl=512):
    t = qp.shape[0]
    hp, _, nk, half = subkeys.shape
    n_exp, d = u.shape
    steps = (t // tl) * hp
    rows = n_exp // steps
    assert rows * steps == n_exp
    out = lambda dt: jax.ShapeDtypeStruct((hp, nk, t), dt)
    ospec = pl.BlockSpec((None, nk, tl), lambda i, h: (h, 0, i))
    wspec = pl.BlockSpec((rows, d), lambda i, h: (i * hp + h, 0))
    wout = jax.ShapeDtypeStruct((n_exp, d), BF16)
    return pl.pallas_call(
        _route_kernel,
        grid=(t // tl, hp),
        in_specs=[pl.BlockSpec((tl, 2 * half), lambda i, h: (i, h)),
                  pl.BlockSpec((None, 2, nk, half), lambda i, h: (h, 0, 0, 0)),
                  wspec, wspec],
        out_specs=[ospec] * 4 + [wspec] * 2,
        out_shape=[out(BF16), out(BF16), out(F32), out(F32), wout, wout],
        compiler_params=_params(2),
        name="peer_route",
    )(qp, subkeys, u, v)


def _peer_kernel(x_ref, rstd_ref, u_ref, v_ref, rank1_ref, e1_ref, cnt_ref, e0_ref, o_ref, acc_ref,
                 xt_ref, *stage_refs, sub, n_pieces):
    e = pl.program_id(1)
    n_heads, nk, tm = rank1_ref.shape
    te, d = u_ref.shape
    n_sub = te // sub
    rows_per_sub = sub // nk
    kc = d // n_pieces
    rows_per_step = te // nk
    row0 = (e % (SUBLANES // rows_per_step)) * rows_per_step
    act_refs, wt_refs = stage_refs[:n_sub], stage_refs[n_sub:]

    @pl.when(e == 0)
    def _():
        acc_ref[...] = jnp.zeros_like(acc_ref)
        xt_ref[...] = x_ref[...].T

    def up_piece(j, p):
        part = jnp.dot(u_ref[j * sub:(j + 1) * sub, p * kc:(p + 1) * kc],
                       xt_ref[p * kc:(p + 1) * kc, :], preferred_element_type=F32)
        if p == 0:
            act_refs[j][...] = part
        else:
            act_refs[j][...] += part

    def gate_piece(j, p):
        ii, g = divmod(p, n_pieces // rows_per_sub)
        tg = tm // (n_pieces // rows_per_sub)
        lanes = slice(g * tg, (g + 1) * tg)
        row = j * rows_per_sub + ii
        gate = None
        for h in range(n_heads):
            cnt = cnt_ref[h, pl.ds(row0 + row, 1), :][:, lanes].astype(BF16)
            e0 = e0_ref[h, pl.ds(row0 + row, 1), :][:, lanes].astype(BF16)
            zero = jnp.zeros((), BF16)
            term = e0 * jnp.where(rank1_ref[h, :, lanes] < cnt, e1_ref[h, :, lanes], zero)
            gate = term if gate is None else gate + term
        act = act_refs[j][ii * nk:(ii + 1) * nk, lanes] * rstd_ref[:, lanes]
        act = 0.5 * act * (1.0 + lax.erf(act * np.float32(1.0 / np.sqrt(2.0))))
        wt_refs[j][lanes, ii * nk:(ii + 1) * nk] = (gate * act.astype(BF16)).T

    def down_piece(j, p):
        cols = slice(p * kc, (p + 1) * kc)
        acc_ref[:, cols] += jnp.dot(wt_refs[j][...], v_ref[j * sub:(j + 1) * sub, cols],
                                    preferred_element_type=F32)

    for step in range(n_sub + 2):
        for p in range(n_pieces):
            if step < n_sub:
                up_piece(step, p)
            if 0 <= step - 1 < n_sub:
                gate_piece(step - 1, p)
            if 0 <= step - 2 < n_sub:
                down_piece(step - 2, p)

    @pl.when(e == pl.num_programs(1) - 1)
    def _():
        o_ref[...] = acc_ref[...].astype(o_ref.dtype)


def _peer_experts(x, rstd, u, v, rank1, e1, cnt, e0, tm=512, te=512, sub=256, n_pieces=4):
    t, d = x.shape
    n_exp = u.shape[0]
    hp, nk, _ = rank1.shape
    rows = te // nk
    tab = pl.BlockSpec((hp, nk, tm), lambda i, e: (0, 0, i))
    assert SUBLANES % rows == 0
    row_tab = pl.BlockSpec((hp, SUBLANES, tm), lambda i, e: (0, (e * rows) // SUBLANES, i))
    return pl.pallas_call(
        functools.partial(_peer_kernel, sub=sub, n_pieces=n_pieces),
        grid=(t // tm, n_exp // te),
        in_specs=[pl.BlockSpec((tm, d), lambda i, e: (i, 0)),
                  pl.BlockSpec((1, tm), lambda i, e: (0, i)),
                  pl.BlockSpec((te, d), lambda i, e: (e, 0)),
                  pl.BlockSpec((te, d), lambda i, e: (e, 0)),
                  tab, tab, row_tab, row_tab],
        out_specs=pl.BlockSpec((tm, d), lambda i, e: (i, 0)),
        out_shape=jax.ShapeDtypeStruct((t, d), BF16),
        scratch_shapes=([pltpu.VMEM((tm, d), F32), pltpu.VMEM((d, tm), BF16)]
                        + [pltpu.VMEM((sub, tm), F32)] * (te // sub)
                        + [pltpu.VMEM((tm, sub), BF16)] * (te // sub)),
        compiler_params=_params(2),
        name="peer_experts",
    )(x, rstd, u, v, rank1, e1, cnt, e0)


def kernel(x, mem, norm1_w, w_in, attn_rel_bias, ret_gn_w, w_out, norm2_w, mem_norm_w, xattn_wq,
           xattn_wkv, xattn_wo, norm3_w, peer_wq, peer_subkeys, peer_u, peer_v, final_norm_w):
    b, s, d = x.shape
    t = b * s
    depth = w_in.shape[0]
    n_attn_heads = attn_rel_bias.shape[1]
    attn_width = n_attn_heads * ATTN_HEAD_DIM
    n_ret_heads = ret_gn_w.shape[1] // RET_V_DIM
    h = x.reshape(t, d)
    xn = _rmsnorm(h, norm1_w[0], BF16)
    for l in range(depth):
        proj = _matmul(xn, w_in[l], BF16, tn=768, name="in_proj").reshape(b, s, -1)
        a_out = _chunk_attention(proj, attn_rel_bias[l], n_attn_heads)
        r_out = _retention(proj, ret_gn_w[l], n_ret_heads, 3 * attn_width)
        h, hg, rstd = _matmul((a_out.reshape(t, -1), r_out.reshape(t, -1)), w_out[l], F32,
                              residual=h, norm_gain=norm2_w[l], name="out_proj")

        mem_n = _rmsnorm(mem.reshape(-1, d), mem_norm_w[l], BF16)
        kv = _matmul(mem_n, xattn_wkv[l], BF16, name="mem_kv").reshape(b, -1, 2 * d)
        folded_k, folded_v = _fold_memory(kv, xattn_wq[l], xattn_wo[l], XATTN_HEADS)
        probs = _mem_probs(hg.reshape(b, s, d), rstd.reshape(b, s, 1), folded_k, XATTN_HEADS)
        h, hg, rstd = _matmul(probs.reshape(t, -1), folded_v, F32, residual=h,
                              norm_gain=norm3_w[l], tn=1024, name="xattn_o")

        qp = _matmul(hg, peer_wq[l], BF16, row_scale=rstd, name="peer_q")
        rank1, e1, cnt, e0, u_b, v_b = _peer_route(qp, peer_subkeys[l], peer_u[l], peer_v[l])
        delta = _peer_experts(hg, rstd.reshape(1, t), u_b, v_b, rank1, e1, cnt, e0)
        if l + 1 < depth:
            h = h + delta.astype(F32)
            xn = _rmsnorm(h, norm1_w[l + 1], BF16)
    return _add_rmsnorm(h, delta, final_norm_w, F32).reshape(b, s, d)
```

```python
import functools

import numpy as np
import jax
import jax.numpy as jnp
from jax import lax
from jax.experimental import pallas as pl
from jax.experimental.pallas import tpu as pltpu

F32 = jnp.float32
BF16 = jnp.bfloat16

CHUNK = 64
LEFT_CHUNKS = 8
LEFT = LEFT_CHUNKS * CHUNK
ATTN_HEAD_DIM = 128
MAX_REL_DIST = 256
RET_V_DIM = 256
RET_QK_DIM = 128
ROPE_BASE = 10000.0
XATTN_HEADS = 4
PEER_HALF = 128
PEER_TOPK = 16
EPS = 1e-6
NEG_INF = -1e30
NOT_RANKED = 1e9

VMEM_LIMIT_BYTES = 56 * 1024 * 1024
LANES = 128
SUBLANES = 8

_NT = (((1,), (1,)), ((), ()))
_TN = (((0,), (0,)), ((), ()))


def _params(n_grid_dims):
    return pltpu.CompilerParams(
        dimension_semantics=("arbitrary",) * n_grid_dims,
        vmem_limit_bytes=VMEM_LIMIT_BYTES)


def _rmsnorm_kernel(x_ref, w_ref, o_ref):
    x = x_ref[...]
    ms = jnp.mean(x * x, axis=-1, keepdims=True)
    o_ref[...] = (x * lax.rsqrt(ms + EPS) * w_ref[...]).astype(o_ref.dtype)


def _rmsnorm(x, w, out_dtype, rows=512):
    m, d = x.shape
    return pl.pallas_call(
        _rmsnorm_kernel,
        grid=(m // rows,),
        in_specs=[pl.BlockSpec((rows, d), lambda i: (i, 0)),
                  pl.BlockSpec((1, d), lambda i: (0, 0))],
        out_specs=pl.BlockSpec((rows, d), lambda i: (i, 0)),
        out_shape=jax.ShapeDtypeStruct((m, d), out_dtype),
        compiler_params=_params(1),
        name="rmsnorm",
    )(x, w.reshape(1, d))


def _add_rmsnorm_kernel(x_ref, y_ref, w_ref, o_ref):
    x = x_ref[...] + y_ref[...].astype(F32)
    ms = jnp.mean(x * x, axis=-1, keepdims=True)
    o_ref[...] = (x * lax.rsqrt(ms + EPS) * w_ref[...]).astype(o_ref.dtype)


def _add_rmsnorm(x, y, w, out_dtype, rows=512):
    m, d = x.shape
    return pl.pallas_call(
        _add_rmsnorm_kernel,
        grid=(m // rows,),
        in_specs=[pl.BlockSpec((rows, d), lambda i: (i, 0)),
                  pl.BlockSpec((rows, d), lambda i: (i, 0)),
                  pl.BlockSpec((1, d), lambda i: (0, 0))],
        out_specs=pl.BlockSpec((rows, d), lambda i: (i, 0)),
        out_shape=jax.ShapeDtypeStruct((m, d), out_dtype),
        compiler_params=_params(1),
        name="add_rmsnorm",
    )(x, y, w.reshape(1, d))


def _mm_kernel(*refs, n_lhs, has_residual, has_row_scale, emit_norm, n_cols):
    lhs_refs, w_ref = refs[:n_lhs], refs[n_lhs]
    pos = n_lhs + 1
    acc, k0 = None, 0
    for a_ref in lhs_refs:
        k1 = k0 + a_ref.shape[1]
        w_blk = w_ref[k0:k1, :]
        if w_blk.dtype != BF16:
            w_blk = w_blk.astype(BF16)
        part = jnp.dot(a_ref[...], w_blk, preferred_element_type=F32)
        acc = part if acc is None else acc + part
        k0 = k1
    if has_row_scale:
        acc = acc * refs[pos][...]
        pos += 1
    if has_residual:
        acc = refs[pos][...] + acc
        pos += 1
    if not emit_norm:
        o_ref = refs[pos]
        o_ref[...] = acc.astype(o_ref.dtype)
        return
    gain_ref, o_ref, scaled_ref, rstd_ref, ssq_ref = refs[pos:pos + 5]
    j = pl.program_id(1)
    o_ref[...] = acc.astype(o_ref.dtype)
    scaled_ref[...] = (acc * gain_ref[...]).astype(scaled_ref.dtype)
    row_ssq = jnp.sum(acc * acc, axis=-1, keepdims=True)

    @pl.when(j == 0)
    def _():
        ssq_ref[...] = row_ssq

    @pl.when(j > 0)
    def _():
        ssq_ref[...] += row_ssq

    @pl.when(j == pl.num_programs(1) - 1)
    def _():
        rstd_ref[...] = lax.rsqrt(ssq_ref[...] / n_cols + EPS)


def _matmul(lhs, w, out_dtype, residual=None, row_scale=None, norm_gain=None, tm=1024, tn=512,
            name="matmul"):
    lhs = lhs if isinstance(lhs, (tuple, list)) else (lhs,)
    m = lhs[0].shape[0]
    k, n = w.shape[-2:]
    assert sum(a.shape[1] for a in lhs) == k
    tm, tn = min(tm, m), min(tn, n)
    in_specs = [pl.BlockSpec((tm, a.shape[1]), lambda i, j: (i, 0)) for a in lhs]
    if w.ndim == 2:
        in_specs.append(pl.BlockSpec((k, tn), lambda i, j: (0, j)))
    else:
        tiles_per_group = m // w.shape[0] // tm
        in_specs.append(pl.BlockSpec((None, k, tn), lambda i, j: (i // tiles_per_group, 0, j)))
    args = [*lhs, w]
    tile = pl.BlockSpec((tm, tn), lambda i, j: (i, j))
    per_row = pl.BlockSpec((tm, 1), lambda i, j: (i, 0))
    if row_scale is not None:
        in_specs.append(per_row)
        args.append(row_scale)
    if residual is not None:
        in_specs.append(tile)
        args.append(residual)
    out_specs, out_shape, scratch = tile, jax.ShapeDtypeStruct((m, n), out_dtype), []
    if norm_gain is not None:
        in_specs.append(pl.BlockSpec((1, tn), lambda i, j: (0, j)))
        args.append(norm_gain.reshape(1, n))
        out_specs = [tile, tile, per_row]
        out_shape = [out_shape, jax.ShapeDtypeStruct((m, n), BF16),
                     jax.ShapeDtypeStruct((m, 1), F32)]
        scratch = [pltpu.VMEM((tm, 1), F32)]
    return pl.pallas_call(
        functools.partial(_mm_kernel, n_lhs=len(lhs), has_residual=residual is not None,
                          has_row_scale=row_scale is not None, emit_norm=norm_gain is not None,
                          n_cols=n),
        grid=(m // tm, n // tn),
        in_specs=in_specs,
        out_specs=out_specs,
        out_shape=out_shape,
        scratch_shapes=scratch,
        compiler_params=_params(2),
        name=name,
    )(*args)


def _attn_kernel(q_ref, k_ref, v_ref, base_ref, o_ref, bias_ref, *, qb, scale, hpb):
    seq = q_ref.shape[0]
    width = LEFT + qb
    dh = q_ref.shape[1] // hpb

    @pl.when(pl.program_id(1) == 0)
    def _():
        q_chunk = lax.broadcasted_iota(jnp.int32, (qb, width), 0) // CHUNK
        c_chunk = lax.broadcasted_iota(jnp.int32, (qb, width), 1) // CHUNK
        in_band = (c_chunk >= q_chunk) & (c_chunk <= q_chunk + LEFT_CHUNKS)
        for hh in range(hpb):
            toeplitz = pltpu.roll(jnp.broadcast_to(base_ref[hh], (qb, base_ref.shape[-1])),
                                  0, 1, stride=1, stride_axis=0)
            bias_ref[hh] = jnp.where(in_band, toeplitz[:, :width], NEG_INF)

    for i in range(seq // qb):
        q0 = i * qb
        k0 = max(0, q0 - LEFT)
        kw = q0 + qb - k0
        c0 = k0 - (q0 - LEFT)
        for hh in range(hpb):
            cols = slice(hh * dh, (hh + 1) * dh)
            s = lax.dot_general(q_ref[q0:q0 + qb, cols], k_ref[k0:k0 + kw, cols], _NT,
                                preferred_element_type=F32)
            s = s * scale + bias_ref[hh, :, c0:c0 + kw]
            m = jnp.max(s, axis=-1, keepdims=True)
            p = jnp.exp(s - m)
            l = jnp.sum(p, axis=-1, keepdims=True)
            o = jnp.dot(p.astype(BF16), v_ref[k0:k0 + kw, cols], preferred_element_type=F32)
            o_ref[q0:q0 + qb, cols] = (o / l).astype(o_ref.dtype)


def _attn_bias_base(rel_bias, qb):
    w = pl.next_power_of_2(LEFT + 2 * qb)
    j = jnp.arange(w)
    j = jnp.where(j < LEFT + qb, j, j - w)
    idx = jnp.clip(LEFT - j, -MAX_REL_DIST, MAX_REL_DIST) + MAX_REL_DIST
    return rel_bias[:, None, idx].astype(F32)


def _chunk_attention(proj, rel_bias, n_heads, qb=256, hpb=4):
    b, s, _ = proj.shape
    dh = ATTN_HEAD_DIM
    base = _attn_bias_base(rel_bias, qb)
    groups = n_heads // hpb
    kern = functools.partial(_attn_kernel, qb=qb, scale=dh ** -0.5, hpb=hpb)
    return pl.pallas_call(
        kern,
        grid=(groups, b),
        in_specs=[pl.BlockSpec((None, s, hpb * dh), lambda h, bi: (bi, 0, h)),
                  pl.BlockSpec((None, s, hpb * dh), lambda h, bi: (bi, 0, groups + h)),
                  pl.BlockSpec((None, s, hpb * dh), lambda h, bi: (bi, 0, 2 * groups + h)),
                  pl.BlockSpec((hpb, 1, base.shape[-1]), lambda h, bi: (h, 0, 0))],
        out_specs=pl.BlockSpec((None, s, hpb * dh), lambda h, bi: (bi, 0, h)),
        out_shape=jax.ShapeDtypeStruct((b, s, n_heads * dh), BF16),
        scratch_shapes=[pltpu.VMEM((hpb, qb, LEFT + qb), F32)],
        compiler_params=_params(2),
        name="chunk_attention",
    )(proj, proj, proj, base)


def _ret_kernel(q_ref, k_ref, v_ref, g_ref, cos_ref, sin_ref, dec_ref, qd_ref, kd_ref, bd_ref,
                gnw_ref, o_ref, state_ref, *, rb, scale, hpb):
    seq = q_ref.shape[0]
    state_ref[...] = jnp.zeros_like(state_ref)
    dk = q_ref.shape[1] // hpb
    dv = v_ref.shape[1] // hpb
    half = dk // 2

    def body(n, carry):
        r0 = pl.multiple_of(n * rb, rb)
        rows = pl.ds(r0, rb)
        cos = cos_ref[rows, :]
        sin = sin_ref[rows, :]
        for hh in range(hpb):
            qk_cols = slice(hh * dk, (hh + 1) * dk)
            v_cols = slice(hh * dv, (hh + 1) * dv)
            q = q_ref[rows, qk_cols].astype(F32)
            k = k_ref[rows, qk_cols].astype(F32)
            q = q * cos + pltpu.roll(q, half, 1) * sin
            k = (k * cos + pltpu.roll(k, half, 1) * sin) * scale
            v = v_ref[rows, v_cols]
            a = lax.dot_general(q.astype(BF16), k.astype(BF16), _NT, preferred_element_type=F32)
            a = a * dec_ref[hh]
            st = state_ref[hh]
            y = jnp.dot(a.astype(BF16), v, preferred_element_type=F32)
            y = y + jnp.dot((q * qd_ref[hh]).astype(BF16), st.astype(BF16),
                            preferred_element_type=F32)
            kd = (k * kd_ref[hh]).astype(BF16)
            kv = lax.dot_general(kd, v, _TN, preferred_element_type=F32)
            state_ref[hh] = st * bd_ref[hh] + kv
            mu = jnp.mean(y, axis=-1, keepdims=True)
            yc = y - mu
            var = jnp.mean(yc * yc, axis=-1, keepdims=True)
            yn = yc * lax.rsqrt(var + EPS) * gnw_ref[:, v_cols]
            g = g_ref[rows, v_cols].astype(F32)
            o_ref[rows, v_cols] = (g * (1.0 / (1.0 + jnp.exp(-g))) * yn).astype(o_ref.dtype)
        return carry

    lax.fori_loop(0, seq // rb, body, 0)


def _retention(proj, gn_w, n_heads, col0, rb=512, hpb=4):
    b, s, _ = proj.shape
    dk, dv = RET_QK_DIM, RET_V_DIM
    groups = n_heads // hpb
    qk0 = col0 // (hpb * dk)
    v0 = (col0 + 2 * n_heads * dk) // (hpb * dv)
    inv_freq = 1.0 / (ROPE_BASE ** (jnp.arange(0, dk, 2, dtype=F32) / dk))
    ang = jnp.arange(s, dtype=F32)[:, None] * inv_freq[None, :]
    cos = jnp.concatenate([jnp.cos(ang), jnp.cos(ang)], axis=-1)
    sin = jnp.concatenate([-jnp.sin(ang), jnp.sin(ang)], axis=-1)
    log_gamma = jnp.log1p(-jnp.power(2.0, -5.0 - jnp.arange(n_heads, dtype=F32)))
    pos = jnp.arange(rb, dtype=F32)
    chunk_of = jnp.arange(rb) // CHUNK
    causal = (chunk_of[None, :] <= chunk_of[:, None]).astype(F32)
    dec = jnp.exp(log_gamma[:, None, None] * jnp.abs(pos[:, None] - pos[None, :])) * causal[None]
    qd = jnp.exp(log_gamma[:, None] * (pos + 1.0))[:, :, None]
    kd = jnp.exp(log_gamma[:, None] * (rb - 1.0 - pos))[:, :, None]
    bd = jnp.exp(log_gamma * rb)[:, None, None]
    kern = functools.partial(_ret_kernel, rb=rb, scale=dk ** -0.5, hpb=hpb)
    return pl.pallas_call(
        kern,
        grid=(b, groups),
        in_specs=[pl.BlockSpec((None, s, hpb * dk), lambda bi, h: (bi, 0, qk0 + h)),
                  pl.BlockSpec((None, s, hpb * dk), lambda bi, h: (bi, 0, qk0 + groups + h)),
                  pl.BlockSpec((None, s, hpb * dv), lambda bi, h: (bi, 0, v0 + h)),
                  pl.BlockSpec((None, s, hpb * dv), lambda bi, h: (bi, 0, v0 + groups + h)),
                  pl.BlockSpec((s, dk), lambda bi, h: (0, 0)),
                  pl.BlockSpec((s, dk), lambda bi, h: (0, 0)),
                  pl.BlockSpec((hpb, rb, rb), lambda bi, h: (h, 0, 0)),
                  pl.BlockSpec((hpb, rb, 1), lambda bi, h: (h, 0, 0)),
                  pl.BlockSpec((hpb, rb, 1), lambda bi, h: (h, 0, 0)),
                  pl.BlockSpec((hpb, 1, 1), lambda bi, h: (h, 0, 0)),
                  pl.BlockSpec((1, hpb * dv), lambda bi, h: (0, h))],
        out_specs=pl.BlockSpec((None, s, hpb * dv), lambda bi, h: (bi, 0, h)),
        out_shape=jax.ShapeDtypeStruct((b, s, n_heads * dv), BF16),
        scratch_shapes=[pltpu.VMEM((hpb, dk, dv), F32)],
        compiler_params=_params(2),
        name="retention",
    )(proj, proj, proj, proj, cos, sin, dec, qd, kd, bd, gn_w.reshape(1, -1))


def _fold_keys_kernel(wq_ref, k_ref, o_ref, wq16_ref):
    @pl.when(pl.program_id(1) == 0)
    def _():
        wq16_ref[...] = wq_ref[...].astype(BF16)

    o_ref[...] = lax.dot_general(wq16_ref[...], k_ref[...], _NT,
                                 preferred_element_type=F32).astype(o_ref.dtype)


def _fold_values_kernel(v_ref, wo_ref, o_ref, wo16_ref):
    @pl.when(pl.program_id(1) == 0)
    def _():
        wo16_ref[...] = wo_ref[...].astype(BF16)

    o_ref[...] = jnp.dot(v_ref[...], wo16_ref[...],
                         preferred_element_type=F32).astype(o_ref.dtype)


def _fold_memory(kv, wq, wo, n_heads):
    b, m, d2 = kv.shape
    d = d2 // 2
    dh = d // n_heads
    folded_k = pl.pallas_call(
        _fold_keys_kernel,
        grid=(n_heads, b),
        in_specs=[pl.BlockSpec((d, dh), lambda h, bi: (0, h)),
                  pl.BlockSpec((None, m, dh), lambda h, bi: (bi, 0, h))],
        out_specs=pl.BlockSpec((None, d, m), lambda h, bi: (bi, 0, h)),
        out_shape=jax.ShapeDtypeStruct((b, d, n_heads * m), BF16),
        scratch_shapes=[pltpu.VMEM((d, dh), BF16)],
        compiler_params=_params(2),
        name="fold_keys",
    )(wq, kv)
    folded_v = pl.pallas_call(
        _fold_values_kernel,
        grid=(n_heads, b),
        in_specs=[pl.BlockSpec((None, m, dh), lambda h, bi: (bi, 0, n_heads + h)),
                  pl.BlockSpec((dh, d), lambda h, bi: (h, 0))],
        out_specs=pl.BlockSpec((None, m, d), lambda h, bi: (bi, h, 0)),
        out_shape=jax.ShapeDtypeStruct((b, n_heads * m, d), BF16),
        scratch_shapes=[pltpu.VMEM((dh, d), BF16)],
        compiler_params=_params(2),
        name="fold_values",
    )(kv, wo)
    return folded_k, folded_v


def _mem_probs_kernel(x_ref, rstd_ref, fk_ref, o_ref, *, n_heads, scale):
    m = fk_ref.shape[1] // n_heads
    s = jnp.dot(x_ref[...], fk_ref[...], preferred_element_type=F32)
    s = s * (rstd_ref[...] * scale)
    for h in range(n_heads):
        sh = s[:, h * m:(h + 1) * m]
        mx = jnp.max(sh, axis=-1, keepdims=True)
        p = jnp.exp(sh - mx)
        o_ref[:, h * m:(h + 1) * m] = (p / jnp.sum(p, axis=-1, keepdims=True)).astype(o_ref.dtype)


def _mem_probs(x, rstd, folded_k, n_heads, tq=1024):
    b, s, d = x.shape
    hm = folded_k.shape[2]
    kern = functools.partial(_mem_probs_kernel, n_heads=n_heads, scale=(d // n_heads) ** -0.5)
    return pl.pallas_call(
        kern,
        grid=(b, s // tq),
        in_specs=[pl.BlockSpec((None, tq, d), lambda bi, i: (bi, i, 0)),
                  pl.BlockSpec((None, tq, 1), lambda bi, i: (bi, i, 0)),
                  pl.BlockSpec((None, d, hm), lambda bi, i: (bi, 0, 0))],
        out_specs=pl.BlockSpec((None, tq, hm), lambda bi, i: (bi, i, 0)),
        out_shape=jax.ShapeDtypeStruct((b, s, hm), BF16),
        compiler_params=_params(2),
        name="mem_probs",
    )(x, rstd, folded_k)


def _take_max(work, rows, exact):
    m = jnp.max(work, axis=0, keepdims=True)
    hit = work == m
    if not exact:
        return m, hit
    first = jnp.min(jnp.where(hit, rows, np.float32(work.shape[0])), axis=0, keepdims=True)
    return m, rows == first


def _top_rows(s, k, exact, want_rank=True):
    rows = lax.broadcasted_iota(jnp.int32, s.shape, 0).astype(F32)
    work = s
    rank = jnp.full(s.shape, NOT_RANKED, F32) if want_rank else None
    vals = []
    for r in range(k):
        m, sel = _take_max(work, rows, exact)
        if want_rank:
            rank = jnp.where(sel, np.float32(r), rank)
        work = jnp.where(sel, -jnp.inf, work)
        vals.append(m)
    taken = jnp.sum(jnp.where(work == -jnp.inf, 1.0, 0.0), axis=0, keepdims=True)
    return vals, rank, taken


def _candidate_cells(k):
    cells = []
    for ra in range(k):
        cells += [(ra, rb) for rb in range(k // (ra + 1))]
    single = [c for c in cells if k // (c[0] + 1) == 1]
    multi = [c for c in cells if c not in single]
    pad = (-len(multi)) % 8
    return multi + [None] * pad + single


def _route_kernel(q_ref, keys_ref, u_ref, v_ref, rank1_ref, e1_ref, cnt_ref, e0_ref, ub_ref, vb_ref):
    ub_ref[...] = u_ref[...].astype(ub_ref.dtype)
    vb_ref[...] = v_ref[...].astype(vb_ref.dtype)
    k = PEER_TOPK
    tl = q_ref.shape[0]
    cells = _candidate_cells(k)

    def route_group(g, exact):
        q = q_ref[g * LANES:(g + 1) * LANES, :]
        s0 = lax.dot_general(keys_ref[0].astype(BF16), q[:, :PEER_HALF], _NT,
                             preferred_element_type=F32)
        s1 = lax.dot_general(keys_ref[1].astype(BF16), q[:, PEER_HALF:], _NT,
                             preferred_element_type=F32)
        a, rank0, taken0 = _top_rows(s0, k, exact, want_rank=exact)
        b, rank1, taken1 = _top_rows(s1, k, exact)
        a_all, b_all = jnp.concatenate(a, axis=0), jnp.concatenate(b, axis=0)
        ea_all, eb_all = jnp.exp(a_all - a[0]), jnp.exp(b_all - b[0])
        neg = jnp.full_like(a[0], -jnp.inf)
        zero = jnp.zeros_like(a[0])
        n0, n1 = k, k // 2
        mid = cells[n0 + n1:len(cells) - k // 2]
        cand = jnp.concatenate(
            [a[0] + b_all, a[1] + b_all[:n1]]
            + [neg if c is None else a_all[c[0]:c[0] + 1] + b_all[c[1]:c[1] + 1] for c in mid]
            + [a_all[k // 2:] + b[0]], axis=0)
        wgt = jnp.concatenate(
            [ea_all[0:1] * eb_all, ea_all[1:2] * eb_all[:n1]]
            + [zero if c is None else ea_all[c[0]:c[0] + 1] * eb_all[c[1]:c[1] + 1] for c in mid]
            + [ea_all[k // 2:] * eb_all[0:1]], axis=0)
        rows = lax.broadcasted_iota(jnp.int32, cand.shape, 0).astype(F32)
        work = cand
        chosen = jnp.zeros(cand.shape, F32)
        for _ in range(k):
            _, sel = _take_max(work, rows, exact)
            chosen = jnp.where(sel, 1.0, chosen)
            work = jnp.where(sel, -jnp.inf, work)
        z = jnp.sum(chosen * wgt, axis=0, keepdims=True)
        cnt = jnp.zeros(s0.shape, F32)
        for ra in range(k):
            mine = [i for i, c in enumerate(cells) if c is not None and c[0] == ra]
            n_ra = jnp.sum(chosen[mine[0]:mine[-1] + 1], axis=0, keepdims=True)
            is_ra = rank0 == np.float32(ra) if exact else s0 == a[ra]
            cnt = jnp.where(is_ra, n_ra, cnt)
        lanes = slice(g * LANES, (g + 1) * LANES)
        rank1_ref[:, lanes] = rank1.astype(rank1_ref.dtype)
        e1_ref[:, lanes] = (jnp.exp(s1 - b[0]) / z).astype(e1_ref.dtype)
        cnt_ref[:, lanes] = cnt
        e0_ref[:, lanes] = jnp.exp(s0 - a[0])
        taken2 = jnp.sum(chosen, axis=0, keepdims=True)
        return (jnp.abs(taken0 - k) + jnp.abs(taken1 - k) + jnp.abs(taken2 - k))

    groups = range(tl // LANES)
    tie = sum(route_group(g, exact=False) for g in groups)

    @pl.when(jnp.max(tie) > 0.0)
    def _():
        for g in groups:
            route_group(g, exact=True)


def _peer_route(qp, subkeys, u, v, tl=512):
    t = qp.shape[0]
    hp, _, nk, half = subkeys.shape
    n_exp, d = u.shape
    steps = (t // tl) * hp
    rows = n_exp // steps
    assert rows * steps == n_exp
    out = lambda dt: jax.ShapeDtypeStruct((hp, nk, t), dt)
    ospec = pl.BlockSpec((None, nk, tl), lambda i, h: (h, 0, i))
    wspec = pl.BlockSpec((rows, d), lambda i, h: (i * hp + h, 0))
    wout = jax.ShapeDtypeStruct((n_exp, d), BF16)
    return pl.pallas_call(
        _route_kernel,
        grid=(t // tl, hp),
        in_specs=[pl.BlockSpec((tl, 2 * half), lambda i, h: (i, h)),
                  pl.BlockSpec((None, 2, nk, half), lambda i, h: (h, 0, 0, 0)),
                  wspec, wspec],
        out_specs=[ospec] * 4 + [wspec] * 2,
        out_shape=[out(BF16), out(BF16), out(F32), out(F32), wout, wout],
        compiler_params=_params(2),
        name="peer_route",
    )(qp, subkeys, u, v)


def _peer_kernel(x_ref, rstd_ref, u_ref, v_ref, rank1_ref, e1_ref, cnt_ref, e0_ref, o_ref, acc_ref,
                 xt_ref, *stage_refs, sub, n_pieces):
    e = pl.program_id(1)
    n_heads, nk, tm = rank1_ref.shape
    te, d = u_ref.shape
    n_sub = te // sub
    rows_per_sub = sub // nk
    kc = d // n_pieces
    rows_per_step = te // nk
    row0 = (e % (SUBLANES // rows_per_step)) * rows_per_step
    act_refs, wt_refs = stage_refs[:n_sub], stage_refs[n_sub:]

    @pl.when(e == 0)
    def _():
        acc_ref[...] = jnp.zeros_like(acc_ref)
        xt_ref[...] = x_ref[...].T

    def up_piece(j, p):
        part = jnp.dot(u_ref[j * sub:(j + 1) * sub, p * kc:(p + 1) * kc],
                       xt_ref[p * kc:(p + 1) * kc, :], preferred_element_type=F32)
        if p == 0:
            act_refs[j][...] = part
        else:
            act_refs[j][...] += part

    def gate_piece(j, p):
        ii, g = divmod(p, n_pieces // rows_per_sub)
        tg = tm // (n_pieces // rows_per_sub)
        lanes = slice(g * tg, (g + 1) * tg)
        row = j * rows_per_sub + ii
        gate = None
        for h in range(n_heads):
            cnt = cnt_ref[h, pl.ds(row0 + row, 1), :][:, lanes].astype(BF16)
            e0 = e0_ref[h, pl.ds(row0 + row, 1), :][:, lanes].astype(BF16)
            zero = jnp.zeros((), BF16)
            term = e0 * jnp.where(rank1_ref[h, :, lanes] < cnt, e1_ref[h, :, lanes], zero)
            gate = term if gate is None else gate + term
        act = act_refs[j][ii * nk:(ii + 1) * nk, lanes] * rstd_ref[:, lanes]
        act = 0.5 * act * (1.0 + lax.erf(act * np.float32(1.0 / np.sqrt(2.0))))
        wt_refs[j][lanes, ii * nk:(ii + 1) * nk] = (gate * act.astype(BF16)).T

    def down_piece(j, p):
        cols = slice(p * kc, (p + 1) * kc)
        acc_ref[:, cols] += jnp.dot(wt_refs[j][...], v_ref[j * sub:(j + 1) * sub, cols],
                                    preferred_element_type=F32)

    for step in range(n_sub + 2):
        for p in range(n_pieces):
            if step < n_sub:
                up_piece(step, p)
            if 0 <= step - 1 < n_sub:
                gate_piece(step - 1, p)
            if 0 <= step - 2 < n_sub:
                down_piece(step - 2, p)

    @pl.when(e == pl.num_programs(1) - 1)
    def _():
        o_ref[...] = acc_ref[...].astype(o_ref.dtype)


def _peer_experts(x, rstd, u, v, rank1, e1, cnt, e0, tm=512, te=512, sub=256, n_pieces=4):
    t, d = x.shape
    n_exp = u.shape[0]
    hp, nk, _ = rank1.shape
    rows = te // nk
    tab = pl.BlockSpec((hp, nk, tm), lambda i, e: (0, 0, i))
    assert SUBLANES % rows == 0
    row_tab = pl.BlockSpec((hp, SUBLANES, tm), lambda i, e: (0, (e * rows) // SUBLANES, i))
    return pl.pallas_call(
        functools.partial(_peer_kernel, sub=sub, n_pieces=n_pieces),
        grid=(t // tm, n_exp // te),
        in_specs=[pl.BlockSpec((tm, d), lambda i, e: (i, 0)),
                  pl.BlockSpec((1, tm), lambda i, e: (0, i)),
                  pl.BlockSpec((te, d), lambda i, e: (e, 0)),
                  pl.BlockSpec((te, d), lambda i, e: (e, 0)),
                  tab, tab, row_tab, row_tab],
        out_specs=pl.BlockSpec((tm, d), lambda i, e: (i, 0)),
        out_shape=jax.ShapeDtypeStruct((t, d), BF16),
        scratch_shapes=([pltpu.VMEM((tm, d), F32), pltpu.VMEM((d, tm), BF16)]
                        + [pltpu.VMEM((sub, tm), F32)] * (te // sub)
                        + [pltpu.VMEM((tm, sub), BF16)] * (te // sub)),
        compiler_params=_params(2),
        name="peer_experts",
    )(x, rstd, u, v, rank1, e1, cnt, e0)


def kernel(x, mem, norm1_w, w_in, attn_rel_bias, ret_gn_w, w_out, norm2_w, mem_norm_w, xattn_wq,
           xattn_wkv, xattn_wo, norm3_w, peer_wq, peer_subkeys, peer_u, peer_v, final_norm_w):
    b, s, d = x.shape
    t = b * s
    depth = w_in.shape[0]
    n_attn_heads = attn_rel_bias.shape[1]
    attn_width = n_attn_heads * ATTN_HEAD_DIM
    n_ret_heads = ret_gn_w.shape[1] // RET_V_DIM
    h = x.reshape(t, d)
    xn = _rmsnorm(h, norm1_w[0], BF16)
    for l in range(depth):
        proj = _matmul(xn, w_in[l], BF16, tn=768, name="in_proj").reshape(b, s, -1)
        a_out = _chunk_attention(proj, attn_rel_bias[l], n_attn_heads)
        r_out = _retention(proj, ret_gn_w[l], n_ret_heads, 3 * attn_width)
        h, hg, rstd = _matmul((a_out.reshape(t, -1), r_out.reshape(t, -1)), w_out[l], F32,
                              residual=h, norm_gain=norm2_w[l], name="out_proj")

        mem_n = _rmsnorm(mem.reshape(-1, d), mem_norm_w[l], BF16)
        kv = _matmul(mem_n, xattn_wkv[l], BF16, name="mem_kv").reshape(b, -1, 2 * d)
        folded_k, folded_v = _fold_memory(kv, xattn_wq[l], xattn_wo[l], XATTN_HEADS)
        probs = _mem_probs(hg.reshape(b, s, d), rstd.reshape(b, s, 1), folded_k, XATTN_HEADS)
        h, hg, rstd = _matmul(probs.reshape(t, -1), folded_v, F32, residual=h,
                              norm_gain=norm3_w[l], tn=1024, name="xattn_o")

        qp = _matmul(hg, peer_wq[l], BF16, row_scale=rstd, name="peer_q")
        rank1, e1, cnt, e0, u_b, v_b = _peer_route(qp, peer_subkeys[l], peer_u[l], peer_v[l])
        delta = _peer_experts(hg, rstd.reshape(1, t), u_b, v_b, rank1, e1, cnt, e0)
        if l + 1 < depth:
            h = h + delta.astype(F32)
            xn = _rmsnorm(h, norm1_w[l + 1], BF16)
    return _add_rmsnorm(h, delta, final_norm_w, F32).reshape(b, s, d)
```

```python
import functools

import numpy as np
import jax
import jax.numpy as jnp
from jax import lax
from jax.experimental import pallas as pl
from jax.experimental.pallas import tpu as pltpu

F32 = jnp.float32
BF16 = jnp.bfloat16

CHUNK = 64
LEFT_CHUNKS = 8
LEFT = LEFT_CHUNKS * CHUNK
ATTN_HEAD_DIM = 128
MAX_REL_DIST = 256
RET_V_DIM = 256
RET_QK_DIM = 128
ROPE_BASE = 10000.0
XATTN_HEADS = 4
PEER_HALF = 128
PEER_TOPK = 16
EPS = 1e-6
NEG_INF = -1e30
NOT_RANKED = 1e9

VMEM_LIMIT_BYTES = 56 * 1024 * 1024
LANES = 128
SUBLANES = 8

_NT = (((1,), (1,)), ((), ()))
_TN = (((0,), (0,)), ((), ()))


def _params(n_grid_dims):
    return pltpu.CompilerParams(
        dimension_semantics=("arbitrary",) * n_grid_dims,
        vmem_limit_bytes=VMEM_LIMIT_BYTES)


def _rmsnorm_kernel(x_ref, w_ref, o_ref):
    x = x_ref[...]
    ms = jnp.mean(x * x, axis=-1, keepdims=True)
    o_ref[...] = (x * lax.rsqrt(ms + EPS) * w_ref[...]).astype(o_ref.dtype)


def _rmsnorm(x, w, out_dtype, rows=512):
    m, d = x.shape
    return pl.pallas_call(
        _rmsnorm_kernel,
        grid=(m // rows,),
        in_specs=[pl.BlockSpec((rows, d), lambda i: (i, 0)),
                  pl.BlockSpec((1, d), lambda i: (0, 0))],
        out_specs=pl.BlockSpec((rows, d), lambda i: (i, 0)),
        out_shape=jax.ShapeDtypeStruct((m, d), out_dtype),
        compiler_params=_params(1),
        name="rmsnorm",
    )(x, w.reshape(1, d))


def _add_rmsnorm_kernel(x_ref, y_ref, w_ref, o_ref):
    x = x_ref[...] + y_ref[...].astype(F32)
    ms = jnp.mean(x * x, axis=-1, keepdims=True)
    o_ref[...] = (x * lax.rsqrt(ms + EPS) * w_ref[...]).astype(o_ref.dtype)


def _add_rmsnorm(x, y, w, out_dtype, rows=512):
    m, d = x.shape
    return pl.pallas_call(
        _add_rmsnorm_kernel,
        grid=(m // rows,),
        in_specs=[pl.BlockSpec((rows, d), lambda i: (i, 0)),
                  pl.BlockSpec((rows, d), lambda i: (i, 0)),
                  pl.BlockSpec((1, d), lambda i: (0, 0))],
        out_specs=pl.BlockSpec((rows, d), lambda i: (i, 0)),
        out_shape=jax.ShapeDtypeStruct((m, d), out_dtype),
        compiler_params=_params(1),
        name="add_rmsnorm",
    )(x, y, w.reshape(1, d))


def _mm_kernel(*refs, n_lhs, has_residual, has_row_scale, emit_norm, n_cols):
    lhs_refs, w_ref = refs[:n_lhs], refs[n_lhs]
    pos = n_lhs + 1
    acc, k0 = None, 0
    for a_ref in lhs_refs:
        k1 = k0 + a_ref.shape[1]
        w_blk = w_ref[k0:k1, :]
        if w_blk.dtype != BF16:
            w_blk = w_blk.astype(BF16)
        part = jnp.dot(a_ref[...], w_blk, preferred_element_type=F32)
        acc = part if acc is None else acc + part
        k0 = k1
    if has_row_scale:
        acc = acc * refs[pos][...]
        pos += 1
    if has_residual:
        acc = refs[pos][...] + acc
        pos += 1
    if not emit_norm:
        o_ref = refs[pos]
        o_ref[...] = acc.astype(o_ref.dtype)
        return
    gain_ref, o_ref, scaled_ref, rstd_ref, ssq_ref = refs[pos:pos + 5]
    j = pl.program_id(1)
    o_ref[...] = acc.astype(o_ref.dtype)
    scaled_ref[...] = (acc * gain_ref[...]).astype(scaled_ref.dtype)
    row_ssq = jnp.sum(acc * acc, axis=-1, keepdims=True)

    @pl.when(j == 0)
    def _():
        ssq_ref[...] = row_ssq

    @pl.when(j > 0)
    def _():
        ssq_ref[...] += row_ssq

    @pl.when(j == pl.num_programs(1) - 1)
    def _():
        rstd_ref[...] = lax.rsqrt(ssq_ref[...] / n_cols + EPS)


def _matmul(lhs, w, out_dtype, residual=None, row_scale=None, norm_gain=None, tm=1024, tn=512,
            name="matmul"):
    lhs = lhs if isinstance(lhs, (tuple, list)) else (lhs,)
    m = lhs[0].shape[0]
    k, n = w.shape[-2:]
    assert sum(a.shape[1] for a in lhs) == k
    tm, tn = min(tm, m), min(tn, n)
    in_specs = [pl.BlockSpec((tm, a.shape[1]), lambda i, j: (i, 0)) for a in lhs]
    if w.ndim == 2:
        in_specs.append(pl.BlockSpec((k, tn), lambda i, j: (0, j)))
    else:
        tiles_per_group = m // w.shape[0] // tm
        in_specs.append(pl.BlockSpec((None, k, tn), lambda i, j: (i // tiles_per_group, 0, j)))
    args = [*lhs, w]
    tile = pl.BlockSpec((tm, tn), lambda i, j: (i, j))
    per_row = pl.BlockSpec((tm, 1), lambda i, j: (i, 0))
    if row_scale is not None:
        in_specs.append(per_row)
        args.append(row_scale)
    if residual is not None:
        in_specs.append(tile)
        args.append(residual)
    out_specs, out_shape, scratch = tile, jax.ShapeDtypeStruct((m, n), out_dtype), []
    if norm_gain is not None:
        in_specs.append(pl.BlockSpec((1, tn), lambda i, j: (0, j)))
        args.append(norm_gain.reshape(1, n))
        out_specs = [tile, tile, per_row]
        out_shape = [out_shape, jax.ShapeDtypeStruct((m, n), BF16),
                     jax.ShapeDtypeStruct((m, 1), F32)]
        scratch = [pltpu.VMEM((tm, 1), F32)]
    return pl.pallas_call(
        functools.partial(_mm_kernel, n_lhs=len(lhs), has_residual=residual is not None,
                          has_row_scale=row_scale is not None, emit_norm=norm_gain is not None,
                          n_cols=n),
        grid=(m // tm, n // tn),
        in_specs=in_specs,
        out_specs=out_specs,
        out_shape=out_shape,
        scratch_shapes=scratch,
        compiler_params=_params(2),
        name=name,
    )(*args)


def _attn_kernel(q_ref, k_ref, v_ref, base_ref, o_ref, bias_ref, *, qb, scale, hpb):
    seq = q_ref.shape[0]
    width = LEFT + qb
    dh = q_ref.shape[1] // hpb

    @pl.when(pl.program_id(1) == 0)
    def _():
        q_chunk = lax.broadcasted_iota(jnp.int32, (qb, width), 0) // CHUNK
        c_chunk = lax.broadcasted_iota(jnp.int32, (qb, width), 1) // CHUNK
        in_band = (c_chunk >= q_chunk) & (c_chunk <= q_chunk + LEFT_CHUNKS)
        for hh in range(hpb):
            toeplitz = pltpu.roll(jnp.broadcast_to(base_ref[hh], (qb, base_ref.shape[-1])),
                                  0, 1, stride=1, stride_axis=0)
            bias_ref[hh] = jnp.where(in_band, toeplitz[:, :width], NEG_INF)

    for i in range(seq // qb):
        q0 = i * qb
        k0 = max(0, q0 - LEFT)
        kw = q0 + qb - k0
        c0 = k0 - (q0 - LEFT)
        for hh in range(hpb):
            cols = slice(hh * dh, (hh + 1) * dh)
            s = lax.dot_general(q_ref[q0:q0 + qb, cols], k_ref[k0:k0 + kw, cols], _NT,
                                preferred_element_type=F32)
            s = s * scale + bias_ref[hh, :, c0:c0 + kw]
            m = jnp.max(s, axis=-1, keepdims=True)
            p = jnp.exp(s - m)
            l = jnp.sum(p, axis=-1, keepdims=True)
            o = jnp.dot(p.astype(BF16), v_ref[k0:k0 + kw, cols], preferred_element_type=F32)
            o_ref[q0:q0 + qb, cols] = (o / l).astype(o_ref.dtype)


def _attn_bias_base(rel_bias, qb):
    w = pl.next_power_of_2(LEFT + 2 * qb)
    j = jnp.arange(w)
    j = jnp.where(j < LEFT + qb, j, j - w)
    idx = jnp.clip(LEFT - j, -MAX_REL_DIST, MAX_REL_DIST) + MAX_REL_DIST
    return rel_bias[:, None, idx].astype(F32)


def _chunk_attention(proj, rel_bias, n_heads, qb=256, hpb=4):
    b, s, _ = proj.shape
    dh = ATTN_HEAD_DIM
    base = _attn_bias_base(rel_bias, qb)
    groups = n_heads // hpb
    kern = functools.partial(_attn_kernel, qb=qb, scale=dh ** -0.5, hpb=hpb)
    return pl.pallas_call(
        kern,
        grid=(groups, b),
        in_specs=[pl.BlockSpec((None, s, hpb * dh), lambda h, bi: (bi, 0, h)),
                  pl.BlockSpec((None, s, hpb * dh), lambda h, bi: (bi, 0, groups + h)),
                  pl.BlockSpec((None, s, hpb * dh), lambda h, bi: (bi, 0, 2 * groups + h)),
                  pl.BlockSpec((hpb, 1, base.shape[-1]), lambda h, bi: (h, 0, 0))],
        out_specs=pl.BlockSpec((None, s, hpb * dh), lambda h, bi: (bi, 0, h)),
        out_shape=jax.ShapeDtypeStruct((b, s, n_heads * dh), BF16),
        scratch_shapes=[pltpu.VMEM((hpb, qb, LEFT + qb), F32)],
        compiler_params=_params(2),
        name="chunk_attention",
    )(proj, proj, proj, base)


def _ret_kernel(q_ref, k_ref, v_ref, g_ref, cos_ref, sin_ref, dec_ref, qd_ref, kd_ref, bd_ref,
                gnw_ref, o_ref, state_ref, *, rb, scale, hpb):
    seq = q_ref.shape[0]
    state_ref[...] = jnp.zeros_like(state_ref)
    dk = q_ref.shape[1] // hpb
    dv = v_ref.shape[1] // hpb
    half = dk // 2

    def body(n, carry):
        r0 = pl.multiple_of(n * rb, rb)
        rows = pl.ds(r0, rb)
        cos = cos_ref[rows, :]
        sin = sin_ref[rows, :]
        for hh in range(hpb):
            qk_cols = slice(hh * dk, (hh + 1) * dk)
            v_cols = slice(hh * dv, (hh + 1) * dv)
            q = q_ref[rows, qk_cols].astype(F32)
            k = k_ref[rows, qk_cols].astype(F32)
            q = q * cos + pltpu.roll(q, half, 1) * sin
            k = (k * cos + pltpu.roll(k, half, 1) * sin) * scale
            v = v_ref[rows, v_cols]
            a = lax.dot_general(q.astype(BF16), k.astype(BF16), _NT, preferred_element_type=F32)
            a = a * dec_ref[hh]
            st = state_ref[hh]
            y = jnp.dot(a.astype(BF16), v, preferred_element_type=F32)
            y = y + jnp.dot((q * qd_ref[hh]).astype(BF16), st.astype(BF16),
                            preferred_element_type=F32)
            kd = (k * kd_ref[hh]).astype(BF16)
            kv = lax.dot_general(kd, v, _TN, preferred_element_type=F32)
            state_ref[hh] = st * bd_ref[hh] + kv
            mu = jnp.mean(y, axis=-1, keepdims=True)
            yc = y - mu
            var = jnp.mean(yc * yc, axis=-1, keepdims=True)
            yn = yc * lax.rsqrt(var + EPS) * gnw_ref[:, v_cols]
            g = g_ref[rows, v_cols].astype(F32)
            o_ref[rows, v_cols] = (g * (1.0 / (1.0 + jnp.exp(-g))) * yn).astype(o_ref.dtype)
        return carry

    lax.fori_loop(0, seq // rb, body, 0)


def _retention(proj, gn_w, n_heads, col0, rb=512, hpb=4):
    b, s, _ = proj.shape
    dk, dv = RET_QK_DIM, RET_V_DIM
    groups = n_heads // hpb
    qk0 = col0 // (hpb * dk)
    v0 = (col0 + 2 * n_heads * dk) // (hpb * dv)
    inv_freq = 1.0 / (ROPE_BASE ** (jnp.arange(0, dk, 2, dtype=F32) / dk))
    ang = jnp.arange(s, dtype=F32)[:, None] * inv_freq[None, :]
    cos = jnp.concatenate([jnp.cos(ang), jnp.cos(ang)], axis=-1)
    sin = jnp.concatenate([-jnp.sin(ang), jnp.sin(ang)], axis=-1)
    log_gamma = jnp.log1p(-jnp.power(2.0, -5.0 - jnp.arange(n_heads, dtype=F32)))
    pos = jnp.arange(rb, dtype=F32)
    chunk_of = jnp.arange(rb) // CHUNK
    causal = (chunk_of[None, :] <= chunk_of[:, None]).astype(F32)
    dec = jnp.exp(log_gamma[:, None, None] * jnp.abs(pos[:, None] - pos[None, :])) * causal[None]
    qd = jnp.exp(log_gamma[:, None] * (pos + 1.0))[:, :, None]
    kd = jnp.exp(log_gamma[:, None] * (rb - 1.0 - pos))[:, :, None]
    bd = jnp.exp(log_gamma * rb)[:, None, None]
    kern = functools.partial(_ret_kernel, rb=rb, scale=dk ** -0.5, hpb=hpb)
    return pl.pallas_call(
        kern,
        grid=(b, groups),
        in_specs=[pl.BlockSpec((None, s, hpb * dk), lambda bi, h: (bi, 0, qk0 + h)),
                  pl.BlockSpec((None, s, hpb * dk), lambda bi, h: (bi, 0, qk0 + groups + h)),
                  pl.BlockSpec((None, s, hpb * dv), lambda bi, h: (bi, 0, v0 + h)),
                  pl.BlockSpec((None, s, hpb * dv), lambda bi, h: (bi, 0, v0 + groups + h)),
                  pl.BlockSpec((s, dk), lambda bi, h: (0, 0)),
                  pl.BlockSpec((s, dk), lambda bi, h: (0, 0)),
                  pl.BlockSpec((hpb, rb, rb), lambda bi, h: (h, 0, 0)),
                  pl.BlockSpec((hpb, rb, 1), lambda bi, h: (h, 0, 0)),
                  pl.BlockSpec((hpb, rb, 1), lambda bi, h: (h, 0, 0)),
                  pl.BlockSpec((hpb, 1, 1), lambda bi, h: (h, 0, 0)),
                  pl.BlockSpec((1, hpb * dv), lambda bi, h: (0, h))],
        out_specs=pl.BlockSpec((None, s, hpb * dv), lambda bi, h: (bi, 0, h)),
        out_shape=jax.ShapeDtypeStruct((b, s, n_heads * dv), BF16),
        scratch_shapes=[pltpu.VMEM((hpb, dk, dv), F32)],
        compiler_params=_params(2),
        name="retention",
    )(proj, proj, proj, proj, cos, sin, dec, qd, kd, bd, gn_w.reshape(1, -1))


def _fold_keys_kernel(wq_ref, k_ref, o_ref, wq16_ref):
    @pl.when(pl.program_id(1) == 0)
    def _():
        wq16_ref[...] = wq_ref[...].astype(BF16)

    o_ref[...] = lax.dot_general(wq16_ref[...], k_ref[...], _NT,
                                 preferred_element_type=F32).astype(o_ref.dtype)


def _fold_values_kernel(v_ref, wo_ref, o_ref, wo16_ref):
    @pl.when(pl.program_id(1) == 0)
    def _():
        wo16_ref[...] = wo_ref[...].astype(BF16)

    o_ref[...] = jnp.dot(v_ref[...], wo16_ref[...],
                         preferred_element_type=F32).astype(o_ref.dtype)


def _fold_memory(kv, wq, wo, n_heads):
    b, m, d2 = kv.shape
    d = d2 // 2
    dh = d // n_heads
    folded_k = pl.pallas_call(
        _fold_keys_kernel,
        grid=(n_heads, b),
        in_specs=[pl.BlockSpec((d, dh), lambda h, bi: (0, h)),
                  pl.BlockSpec((None, m, dh), lambda h, bi: (bi, 0, h))],
        out_specs=pl.BlockSpec((None, d, m), lambda h, bi: (bi, 0, h)),
        out_shape=jax.ShapeDtypeStruct((b, d, n_heads * m), BF16),
        scratch_shapes=[pltpu.VMEM((d, dh), BF16)],
        compiler_params=_params(2),
        name="fold_keys",
    )(wq, kv)
    folded_v = pl.pallas_call(
        _fold_values_kernel,
        grid=(n_heads, b),
        in_specs=[pl.BlockSpec((None, m, dh), lambda h, bi: (bi, 0, n_heads + h)),
                  pl.BlockSpec((dh, d), lambda h, bi: (h, 0))],
        out_specs=pl.BlockSpec((None, m, d), lambda h, bi: (bi, h, 0)),
        out_shape=jax.ShapeDtypeStruct((b, n_heads * m, d), BF16),
        scratch_shapes=[pltpu.VMEM((dh, d), BF16)],
        compiler_params=_params(2),
        name="fold_values",
    )(kv, wo)
    return folded_k, folded_v


def _mem_probs_kernel(x_ref, rstd_ref, fk_ref, o_ref, *, n_heads, scale):
    m = fk_ref.shape[1] // n_heads
    s = jnp.dot(x_ref[...], fk_ref[...], preferred_element_type=F32)
    s = s * (rstd_ref[...] * scale)
    for h in range(n_heads):
        sh = s[:, h * m:(h + 1) * m]
        mx = jnp.max(sh, axis=-1, keepdims=True)
        p = jnp.exp(sh - mx)
        o_ref[:, h * m:(h + 1) * m] = (p / jnp.sum(p, axis=-1, keepdims=True)).astype(o_ref.dtype)


def _mem_probs(x, rstd, folded_k, n_heads, tq=1024):
    b, s, d = x.shape
    hm = folded_k.shape[2]
    kern = functools.partial(_mem_probs_kernel, n_heads=n_heads, scale=(d // n_heads) ** -0.5)
    return pl.pallas_call(
        kern,
        grid=(b, s // tq),
        in_specs=[pl.BlockSpec((None, tq, d), lambda bi, i: (bi, i, 0)),
                  pl.BlockSpec((None, tq, 1), lambda bi, i: (bi, i, 0)),
                  pl.BlockSpec((None, d, hm), lambda bi, i: (bi, 0, 0))],
        out_specs=pl.BlockSpec((None, tq, hm), lambda bi, i: (bi, i, 0)),
        out_shape=jax.ShapeDtypeStruct((b, s, hm), BF16),
        compiler_params=_params(2),
        name="mem_probs",
    )(x, rstd, folded_k)


def _take_max(work, rows, exact):
    m = jnp.max(work, axis=0, keepdims=True)
    hit = work == m
    if not exact:
        return m, hit
    first = jnp.min(jnp.where(hit, rows, np.float32(work.shape[0])), axis=0, keepdims=True)
    return m, rows == first


def _top_rows(s, k, exact, want_rank=True):
    rows = lax.broadcasted_iota(jnp.int32, s.shape, 0).astype(F32)
    work = s
    rank = jnp.full(s.shape, NOT_RANKED, F32) if want_rank else None
    vals = []
    for r in range(k):
        m, sel = _take_max(work, rows, exact)
        if want_rank:
            rank = jnp.where(sel, np.float32(r), rank)
        work = jnp.where(sel, -jnp.inf, work)
        vals.append(m)
    taken = jnp.sum(jnp.where(work == -jnp.inf, 1.0, 0.0), axis=0, keepdims=True)
    return vals, rank, taken


def _candidate_cells(k):
    cells = []
    for ra in range(k):
        cells += [(ra, rb) for rb in range(k // (ra + 1))]
    single = [c for c in cells if k // (c[0] + 1) == 1]
    multi = [c for c in cells if c not in single]
    pad = (-len(multi)) % 8
    return multi + [None] * pad + single


def _route_kernel(q_ref, keys_ref, u_ref, v_ref, rank1_ref, e1_ref, cnt_ref, e0_ref, ub_ref, vb_ref):
    ub_ref[...] = u_ref[...].astype(ub_ref.dtype)
    vb_ref[...] = v_ref[...].astype(vb_ref.dtype)
    k = PEER_TOPK
    tl = q_ref.shape[0]
    cells = _candidate_cells(k)

    def route_group(g, exact):
        q = q_ref[g * LANES:(g + 1) * LANES, :]
        s0 = lax.dot_general(keys_ref[0].astype(BF16), q[:, :PEER_HALF], _NT,
                             preferred_element_type=F32)
        s1 = lax.dot_general(keys_ref[1].astype(BF16), q[:, PEER_HALF:], _NT,
                             preferred_element_type=F32)
        a, rank0, taken0 = _top_rows(s0, k, exact, want_rank=exact)
        b, rank1, taken1 = _top_rows(s1, k, exact)
        a_all, b_all = jnp.concatenate(a, axis=0), jnp.concatenate(b, axis=0)
        ea_all, eb_all = jnp.exp(a_all - a[0]), jnp.exp(b_all - b[0])
        neg = jnp.full_like(a[0], -jnp.inf)
        zero = jnp.zeros_like(a[0])
        n0, n1 = k, k // 2
        mid = cells[n0 + n1:len(cells) - k // 2]
        cand = jnp.concatenate(
            [a[0] + b_all, a[1] + b_all[:n1]]
            + [neg if c is None else a_all[c[0]:c[0] + 1] + b_all[c[1]:c[1] + 1] for c in mid]
            + [a_all[k // 2:] + b[0]], axis=0)
        wgt = jnp.concatenate(
            [ea_all[0:1] * eb_all, ea_all[1:2] * eb_all[:n1]]
            + [zero if c is None else ea_all[c[0]:c[0] + 1] * eb_all[c[1]:c[1] + 1] for c in mid]
            + [ea_all[k // 2:] * eb_all[0:1]], axis=0)
        rows = lax.broadcasted_iota(jnp.int32, cand.shape, 0).astype(F32)
        work = cand
        chosen = jnp.zeros(cand.shape, F32)
        for _ in range(k):
            _, sel = _take_max(work, rows, exact)
            chosen = jnp.where(sel, 1.0, chosen)
            work = jnp.where(sel, -jnp.inf, work)
        z = jnp.sum(chosen * wgt, axis=0, keepdims=True)
        cnt = jnp.zeros(s0.shape, F32)
        for ra in range(k):
            mine = [i for i, c in enumerate(cells) if c is not None and c[0] == ra]
            n_ra = jnp.sum(chosen[mine[0]:mine[-1] + 1], axis=0, keepdims=True)
            is_ra = rank0 == np.float32(ra) if exact else s0 == a[ra]
            cnt = jnp.where(is_ra, n_ra, cnt)
        lanes = slice(g * LANES, (g + 1) * LANES)
        rank1_ref[:, lanes] = rank1.astype(rank1_ref.dtype)
        e1_ref[:, lanes] = (jnp.exp(s1 - b[0]) / z).astype(e1_ref.dtype)
        cnt_ref[:, lanes] = cnt
        e0_ref[:, lanes] = jnp.exp(s0 - a[0])
        taken2 = jnp.sum(chosen, axis=0, keepdims=True)
        return (jnp.abs(taken0 - k) + jnp.abs(taken1 - k) + jnp.abs(taken2 - k))

    groups = range(tl // LANES)
    tie = sum(route_group(g, exact=False) for g in groups)

    @pl.when(jnp.max(tie) > 0.0)
    def _():
        for g in groups:
            route_group(g, exact=True)


def _peer_route(qp, subkeys, u, v, tl=512):
    t = qp.shape[0]
    hp, _, nk, half = subkeys.shape
    n_exp, d = u.shape
    steps = (t // tl) * hp
    rows = n_exp // steps
    assert rows * steps == n_exp
    out = lambda dt: jax.ShapeDtypeStruct((hp, nk, t), dt)
    ospec = pl.BlockSpec((None, nk, tl), lambda i, h: (h, 0, i))
    wspec = pl.BlockSpec((rows, d), lambda i, h: (i * hp + h, 0))
    wout = jax.ShapeDtypeStruct((n_exp, d), BF16)
    return pl.pallas_call(
        _route_kernel,
        grid=(t // tl, hp),
        in_specs=[pl.BlockSpec((tl, 2 * half), lambda i, h: (i, h)),
                  pl.BlockSpec((None, 2, nk, half), lambda i, h: (h, 0, 0, 0)),
                  wspec, wspec],
        out_specs=[ospec] * 4 + [wspec] * 2,
        out_shape=[out(BF16), out(BF16), out(F32), out(F32), wout, wout],
        compiler_params=_params(2),
        name="peer_route",
    )(qp, subkeys, u, v)


def _peer_kernel(x_ref, rstd_ref, u_ref, v_ref, rank1_ref, e1_ref, cnt_ref, e0_ref, o_ref, acc_ref,
                 xt_ref, *stage_refs, sub, n_pieces):
    e = pl.program_id(1)
    n_heads, nk, tm = rank1_ref.shape
    te, d = u_ref.shape
    n_sub = te // sub
    rows_per_sub = sub // nk
    kc = d // n_pieces
    rows_per_step = te // nk
    row0 = (e % (SUBLANES // rows_per_step)) * rows_per_step
    act_refs, wt_refs = stage_refs[:n_sub], stage_refs[n_sub:]

    @pl.when(e == 0)
    def _():
        acc_ref[...] = jnp.zeros_like(acc_ref)
        xt_ref[...] = x_ref[...].T

    def up(j):
        act_refs[j][...] = jnp.dot(u_ref[j * sub:(j + 1) * sub, :], xt_ref[...],
                                   preferred_element_type=F32)

    def gate_piece(j, p):
        ii, g = divmod(p, n_pieces // rows_per_sub)
        tg = tm // (n_pieces // rows_per_sub)
        lanes = slice(g * tg, (g + 1) * tg)
        row = j * rows_per_sub + ii
        gate = None
        for h in range(n_heads):
            cnt = cnt_ref[h, pl.ds(row0 + row, 1), :][:, lanes].astype(BF16)
            e0 = e0_ref[h, pl.ds(row0 + row, 1), :][:, lanes].astype(BF16)
            zero = jnp.zeros((), BF16)
            term = e0 * jnp.where(rank1_ref[h, :, lanes] < cnt, e1_ref[h, :, lanes], zero)
            gate = term if gate is None else gate + term
        act = act_refs[j][ii * nk:(ii + 1) * nk, lanes] * rstd_ref[:, lanes]
        act = 0.5 * act * (1.0 + lax.erf(act * np.float32(1.0 / np.sqrt(2.0))))
        wt_refs[j][lanes, ii * nk:(ii + 1) * nk] = (gate * act.astype(BF16)).T

    def down_piece(j, p):
        cols = slice(p * kc, (p + 1) * kc)
        acc_ref[:, cols] += jnp.dot(wt_refs[j][...], v_ref[j * sub:(j + 1) * sub, cols],
                                    preferred_element_type=F32)

    for step in range(n_sub + 2):
        if step < n_sub:
            up(step)
        for p in range(n_pieces):
            if 0 <= step - 1 < n_sub:
                gate_piece(step - 1, p)
            if 0 <= step - 2 < n_sub:
                down_piece(step - 2, p)

    @pl.when(e == pl.num_programs(1) - 1)
    def _():
        o_ref[...] = acc_ref[...].astype(o_ref.dtype)


def _peer_experts(x, rstd, u, v, rank1, e1, cnt, e0, tm=512, te=512, sub=256, n_pieces=4):
    t, d = x.shape
    n_exp = u.shape[0]
    hp, nk, _ = rank1.shape
    rows = te // nk
    tab = pl.BlockSpec((hp, nk, tm), lambda i, e: (0, 0, i))
    assert SUBLANES % rows == 0
    row_tab = pl.BlockSpec((hp, SUBLANES, tm), lambda i, e: (0, (e * rows) // SUBLANES, i))
    return pl.pallas_call(
        functools.partial(_peer_kernel, sub=sub, n_pieces=n_pieces),
        grid=(t // tm, n_exp // te),
        in_specs=[pl.BlockSpec((tm, d), lambda i, e: (i, 0)),
                  pl.BlockSpec((1, tm), lambda i, e: (0, i)),
                  pl.BlockSpec((te, d), lambda i, e: (e, 0)),
                  pl.BlockSpec((te, d), lambda i, e: (e, 0)),
                  tab, tab, row_tab, row_tab],
        out_specs=pl.BlockSpec((tm, d), lambda i, e: (i, 0)),
        out_shape=jax.ShapeDtypeStruct((t, d), BF16),
        scratch_shapes=([pltpu.VMEM((tm, d), F32), pltpu.VMEM((d, tm), BF16)]
                        + [pltpu.VMEM((sub, tm), F32)] * (te // sub)
                        + [pltpu.VMEM((tm, sub), BF16)] * (te // sub)),
        compiler_params=_params(2),
        name="peer_experts",
    )(x, rstd, u, v, rank1, e1, cnt, e0)


def kernel(x, mem, norm1_w, w_in, attn_rel_bias, ret_gn_w, w_out, norm2_w, mem_norm_w, xattn_wq,
           xattn_wkv, xattn_wo, norm3_w, peer_wq, peer_subkeys, peer_u, peer_v, final_norm_w):
    b, s, d = x.shape
    t = b * s
    depth = w_in.shape[0]
    n_attn_heads = attn_rel_bias.shape[1]
    attn_width = n_attn_heads * ATTN_HEAD_DIM
    n_ret_heads = ret_gn_w.shape[1] // RET_V_DIM
    h = x.reshape(t, d)
    xn = _rmsnorm(h, norm1_w[0], BF16)
    for l in range(depth):
        proj = _matmul(xn, w_in[l], BF16, tn=768, name="in_proj").reshape(b, s, -1)
        a_out = _chunk_attention(proj, attn_rel_bias[l], n_attn_heads)
        r_out = _retention(proj, ret_gn_w[l], n_ret_heads, 3 * attn_width)
        h, hg, rstd = _matmul((a_out.reshape(t, -1), r_out.reshape(t, -1)), w_out[l], F32,
                              residual=h, norm_gain=norm2_w[l], name="out_proj")

        mem_n = _rmsnorm(mem.reshape(-1, d), mem_norm_w[l], BF16)
        kv = _matmul(mem_n, xattn_wkv[l], BF16, name="mem_kv").reshape(b, -1, 2 * d)
        folded_k, folded_v = _fold_memory(kv, xattn_wq[l], xattn_wo[l], XATTN_HEADS)
        probs = _mem_probs(hg.reshape(b, s, d), rstd.reshape(b, s, 1), folded_k, XATTN_HEADS)
        h, hg, rstd = _matmul(probs.reshape(t, -1), folded_v, F32, residual=h,
                              norm_gain=norm3_w[l], tn=1024, name="xattn_o")

        qp = _matmul(hg, peer_wq[l], BF16, row_scale=rstd, name="peer_q")
        rank1, e1, cnt, e0, u_b, v_b = _peer_route(qp, peer_subkeys[l], peer_u[l], peer_v[l])
        delta = _peer_experts(hg, rstd.reshape(1, t), u_b, v_b, rank1, e1, cnt, e0)
        if l + 1 < depth:
            h = h + delta.astype(F32)
            xn = _rmsnorm(h, norm1_w[l + 1], BF16)
    return _add_rmsnorm(h, delta, final_norm_w, F32).reshape(b, s, d)
```

```python
import functools

import numpy as np
import jax
import jax.numpy as jnp
from jax import lax
from jax.experimental import pallas as pl
from jax.experimental.pallas import tpu as pltpu

F32 = jnp.float32
BF16 = jnp.bfloat16

CHUNK = 64
LEFT_CHUNKS = 8
LEFT = LEFT_CHUNKS * CHUNK
ATTN_HEAD_DIM = 128
MAX_REL_DIST = 256
RET_V_DIM = 256
RET_QK_DIM = 128
ROPE_BASE = 10000.0
XATTN_HEADS = 4
PEER_HALF = 128
PEER_TOPK = 16
EPS = 1e-6
NEG_INF = -1e30
NOT_RANKED = 1e9

VMEM_LIMIT_BYTES = 56 * 1024 * 1024
LANES = 128
SUBLANES = 8

_NT = (((1,), (1,)), ((), ()))
_TN = (((0,), (0,)), ((), ()))


def _params(n_grid_dims):
    return pltpu.CompilerParams(
        dimension_semantics=("arbitrary",) * n_grid_dims,
        vmem_limit_bytes=VMEM_LIMIT_BYTES)


def _rmsnorm_kernel(x_ref, w_ref, o_ref):
    x = x_ref[...]
    ms = jnp.mean(x * x, axis=-1, keepdims=True)
    o_ref[...] = (x * lax.rsqrt(ms + EPS) * w_ref[...]).astype(o_ref.dtype)


def _rmsnorm(x, w, out_dtype, rows=512):
    m, d = x.shape
    return pl.pallas_call(
        _rmsnorm_kernel,
        grid=(m // rows,),
        in_specs=[pl.BlockSpec((rows, d), lambda i: (i, 0)),
                  pl.BlockSpec((1, d), lambda i: (0, 0))],
        out_specs=pl.BlockSpec((rows, d), lambda i: (i, 0)),
        out_shape=jax.ShapeDtypeStruct((m, d), out_dtype),
        compiler_params=_params(1),
        name="rmsnorm",
    )(x, w.reshape(1, d))


def _add_rmsnorm_kernel(x_ref, y_ref, w_ref, o_ref):
    x = x_ref[...] + y_ref[...].astype(F32)
    ms = jnp.mean(x * x, axis=-1, keepdims=True)
    o_ref[...] = (x * lax.rsqrt(ms + EPS) * w_ref[...]).astype(o_ref.dtype)


def _add_rmsnorm(x, y, w, out_dtype, rows=512):
    m, d = x.shape
    return pl.pallas_call(
        _add_rmsnorm_kernel,
        grid=(m // rows,),
        in_specs=[pl.BlockSpec((rows, d), lambda i: (i, 0)),
                  pl.BlockSpec((rows, d), lambda i: (i, 0)),
                  pl.BlockSpec((1, d), lambda i: (0, 0))],
        out_specs=pl.BlockSpec((rows, d), lambda i: (i, 0)),
        out_shape=jax.ShapeDtypeStruct((m, d), out_dtype),
        compiler_params=_params(1),
        name="add_rmsnorm",
    )(x, y, w.reshape(1, d))


def _mm_kernel(*refs, n_lhs, has_residual, has_row_scale, emit_norm, n_cols):
    lhs_refs, w_ref = refs[:n_lhs], refs[n_lhs]
    pos = n_lhs + 1
    acc, k0 = None, 0
    for a_ref in lhs_refs:
        k1 = k0 + a_ref.shape[1]
        w_blk = w_ref[k0:k1, :]
        if w_blk.dtype != BF16:
            w_blk = w_blk.astype(BF16)
        part = jnp.dot(a_ref[...], w_blk, preferred_element_type=F32)
        acc = part if acc is None else acc + part
        k0 = k1
    if has_row_scale:
        acc = acc * refs[pos][...]
        pos += 1
    if has_residual:
        acc = refs[pos][...] + acc
        pos += 1
    if not emit_norm:
        o_ref = refs[pos]
        o_ref[...] = acc.astype(o_ref.dtype)
        return
    gain_ref, o_ref, scaled_ref, rstd_ref, ssq_ref = refs[pos:pos + 5]
    j = pl.program_id(1)
    o_ref[...] = acc.astype(o_ref.dtype)
    scaled_ref[...] = (acc * gain_ref[...]).astype(scaled_ref.dtype)
    row_ssq = jnp.sum(acc * acc, axis=-1, keepdims=True)

    @pl.when(j == 0)
    def _():
        ssq_ref[...] = row_ssq

    @pl.when(j > 0)
    def _():
        ssq_ref[...] += row_ssq

    @pl.when(j == pl.num_programs(1) - 1)
    def _():
        rstd_ref[...] = lax.rsqrt(ssq_ref[...] / n_cols + EPS)


def _matmul(lhs, w, out_dtype, residual=None, row_scale=None, norm_gain=None, tm=1024, tn=512,
            name="matmul"):
    lhs = lhs if isinstance(lhs, (tuple, list)) else (lhs,)
    m = lhs[0].shape[0]
    k, n = w.shape[-2:]
    assert sum(a.shape[1] for a in lhs) == k
    tm, tn = min(tm, m), min(tn, n)
    in_specs = [pl.BlockSpec((tm, a.shape[1]), lambda i, j: (i, 0)) for a in lhs]
    if w.ndim == 2:
        in_specs.append(pl.BlockSpec((k, tn), lambda i, j: (0, j)))
    else:
        tiles_per_group = m // w.shape[0] // tm
        in_specs.append(pl.BlockSpec((None, k, tn), lambda i, j: (i // tiles_per_group, 0, j)))
    args = [*lhs, w]
    tile = pl.BlockSpec((tm, tn), lambda i, j: (i, j))
    per_row = pl.BlockSpec((tm, 1), lambda i, j: (i, 0))
    if row_scale is not None:
        in_specs.append(per_row)
        args.append(row_scale)
    if residual is not None:
        in_specs.append(tile)
        args.append(residual)
    out_specs, out_shape, scratch = tile, jax.ShapeDtypeStruct((m, n), out_dtype), []
    if norm_gain is not None:
        in_specs.append(pl.BlockSpec((1, tn), lambda i, j: (0, j)))
        args.append(norm_gain.reshape(1, n))
        out_specs = [tile, tile, per_row]
        out_shape = [out_shape, jax.ShapeDtypeStruct((m, n), BF16),
                     jax.ShapeDtypeStruct((m, 1), F32)]
        scratch = [pltpu.VMEM((tm, 1), F32)]
    return pl.pallas_call(
        functools.partial(_mm_kernel, n_lhs=len(lhs), has_residual=residual is not None,
                          has_row_scale=row_scale is not None, emit_norm=norm_gain is not None,
                          n_cols=n),
        grid=(m // tm, n // tn),
        in_specs=in_specs,
        out_specs=out_specs,
        out_shape=out_shape,
        scratch_shapes=scratch,
        compiler_params=_params(2),
        name=name,
    )(*args)


def _attn_kernel(q_ref, k_ref, v_ref, base_ref, o_ref, bias_ref, *, qb, scale, hpb):
    seq = q_ref.shape[0]
    width = LEFT + qb
    dh = q_ref.shape[1] // hpb

    @pl.when(pl.program_id(1) == 0)
    def _():
        q_chunk = lax.broadcasted_iota(jnp.int32, (qb, width), 0) // CHUNK
        c_chunk = lax.broadcasted_iota(jnp.int32, (qb, width), 1) // CHUNK
        in_band = (c_chunk >= q_chunk) & (c_chunk <= q_chunk + LEFT_CHUNKS)
        for hh in range(hpb):
            toeplitz = pltpu.roll(jnp.broadcast_to(base_ref[hh], (qb, base_ref.shape[-1])),
                                  0, 1, stride=1, stride_axis=0)
            bias_ref[hh] = jnp.where(in_band, toeplitz[:, :width], NEG_INF)

    for i in range(seq // qb):
        q0 = i * qb
        k0 = max(0, q0 - LEFT)
        kw = q0 + qb - k0
        c0 = k0 - (q0 - LEFT)
        for hh in range(hpb):
            cols = slice(hh * dh, (hh + 1) * dh)
            s = lax.dot_general(q_ref[q0:q0 + qb, cols], k_ref[k0:k0 + kw, cols], _NT,
                                preferred_element_type=F32)
            s = s * scale + bias_ref[hh, :, c0:c0 + kw]
            m = jnp.max(s, axis=-1, keepdims=True)
            p = jnp.exp(s - m)
            l = jnp.sum(p, axis=-1, keepdims=True)
            o = jnp.dot(p.astype(BF16), v_ref[k0:k0 + kw, cols], preferred_element_type=F32)
            o_ref[q0:q0 + qb, cols] = (o / l).astype(o_ref.dtype)


def _attn_bias_base(rel_bias, qb):
    w = pl.next_power_of_2(LEFT + 2 * qb)
    j = jnp.arange(w)
    j = jnp.where(j < LEFT + qb, j, j - w)
    idx = jnp.clip(LEFT - j, -MAX_REL_DIST, MAX_REL_DIST) + MAX_REL_DIST
    return rel_bias[:, None, idx].astype(F32)


def _chunk_attention(proj, rel_bias, n_heads, qb=256, hpb=4):
    b, s, _ = proj.shape
    dh = ATTN_HEAD_DIM
    base = _attn_bias_base(rel_bias, qb)
    groups = n_heads // hpb
    kern = functools.partial(_attn_kernel, qb=qb, scale=dh ** -0.5, hpb=hpb)
    return pl.pallas_call(
        kern,
        grid=(groups, b),
        in_specs=[pl.BlockSpec((None, s, hpb * dh), lambda h, bi: (bi, 0, h)),
                  pl.BlockSpec((None, s, hpb * dh), lambda h, bi: (bi, 0, groups + h)),
                  pl.BlockSpec((None, s, hpb * dh), lambda h, bi: (bi, 0, 2 * groups + h)),
                  pl.BlockSpec((hpb, 1, base.shape[-1]), lambda h, bi: (h, 0, 0))],
        out_specs=pl.BlockSpec((None, s, hpb * dh), lambda h, bi: (bi, 0, h)),
        out_shape=jax.ShapeDtypeStruct((b, s, n_heads * dh), BF16),
        scratch_shapes=[pltpu.VMEM((hpb, qb, LEFT + qb), F32)],
        compiler_params=_params(2),
        name="chunk_attention",
    )(proj, proj, proj, base)


def _ret_kernel(q_ref, k_ref, v_ref, g_ref, cos_ref, sin_ref, dec_ref, qd_ref, kd_ref, bd_ref,
                gnw_ref, o_ref, state_ref, *, rb, scale, hpb):
    seq = q_ref.shape[0]
    state_ref[...] = jnp.zeros_like(state_ref)
    dk = q_ref.shape[1] // hpb
    dv = v_ref.shape[1] // hpb
    half = dk // 2

    def body(n, carry):
        r0 = pl.multiple_of(n * rb, rb)
        rows = pl.ds(r0, rb)
        cos = cos_ref[rows, :]
        sin = sin_ref[rows, :]
        for hh in range(hpb):
            qk_cols = slice(hh * dk, (hh + 1) * dk)
            v_cols = slice(hh * dv, (hh + 1) * dv)
            q = q_ref[rows, qk_cols].astype(F32)
            k = k_ref[rows, qk_cols].astype(F32)
            q = q * cos + pltpu.roll(q, half, 1) * sin
            k = (k * cos + pltpu.roll(k, half, 1) * sin) * scale
            v = v_ref[rows, v_cols]
            a = lax.dot_general(q.astype(BF16), k.astype(BF16), _NT, preferred_element_type=F32)
            a = a * dec_ref[hh]
            st = state_ref[hh]
            y = jnp.dot(a.astype(BF16), v, preferred_element_type=F32)
            y = y + jnp.dot((q * qd_ref[hh]).astype(BF16), st.astype(BF16),
                            preferred_element_type=F32)
            kd = (k * kd_ref[hh]).astype(BF16)
            kv = lax.dot_general(kd, v, _TN, preferred_element_type=F32)
            state_ref[hh] = st * bd_ref[hh] + kv
            mu = jnp.mean(y, axis=-1, keepdims=True)
            yc = y - mu
            var = jnp.mean(yc * yc, axis=-1, keepdims=True)
            yn = yc * lax.rsqrt(var + EPS) * gnw_ref[:, v_cols]
            g = g_ref[rows, v_cols].astype(F32)
            o_ref[rows, v_cols] = (g * (1.0 / (1.0 + jnp.exp(-g))) * yn).astype(o_ref.dtype)
        return carry

    lax.fori_loop(0, seq // rb, body, 0)


def _retention(proj, gn_w, n_heads, col0, rb=512, hpb=4):
    b, s, _ = proj.shape
    dk, dv = RET_QK_DIM, RET_V_DIM
    groups = n_heads // hpb
    qk0 = col0 // (hpb * dk)
    v0 = (col0 + 2 * n_heads * dk) // (hpb * dv)
    inv_freq = 1.0 / (ROPE_BASE ** (jnp.arange(0, dk, 2, dtype=F32) / dk))
    ang = jnp.arange(s, dtype=F32)[:, None] * inv_freq[None, :]
    cos = jnp.concatenate([jnp.cos(ang), jnp.cos(ang)], axis=-1)
    sin = jnp.concatenate([-jnp.sin(ang), jnp.sin(ang)], axis=-1)
    log_gamma = jnp.log1p(-jnp.power(2.0, -5.0 - jnp.arange(n_heads, dtype=F32)))
    pos = jnp.arange(rb, dtype=F32)
    chunk_of = jnp.arange(rb) // CHUNK
    causal = (chunk_of[None, :] <= chunk_of[:, None]).astype(F32)
    dec = jnp.exp(log_gamma[:, None, None] * jnp.abs(pos[:, None] - pos[None, :])) * causal[None]
    qd = jnp.exp(log_gamma[:, None] * (pos + 1.0))[:, :, None]
    kd = jnp.exp(log_gamma[:, None] * (rb - 1.0 - pos))[:, :, None]
    bd = jnp.exp(log_gamma * rb)[:, None, None]
    kern = functools.partial(_ret_kernel, rb=rb, scale=dk ** -0.5, hpb=hpb)
    return pl.pallas_call(
        kern,
        grid=(b, groups),
        in_specs=[pl.BlockSpec((None, s, hpb * dk), lambda bi, h: (bi, 0, qk0 + h)),
                  pl.BlockSpec((None, s, hpb * dk), lambda bi, h: (bi, 0, qk0 + groups + h)),
                  pl.BlockSpec((None, s, hpb * dv), lambda bi, h: (bi, 0, v0 + h)),
                  pl.BlockSpec((None, s, hpb * dv), lambda bi, h: (bi, 0, v0 + groups + h)),
                  pl.BlockSpec((s, dk), lambda bi, h: (0, 0)),
                  pl.BlockSpec((s, dk), lambda bi, h: (0, 0)),
                  pl.BlockSpec((hpb, rb, rb), lambda bi, h: (h, 0, 0)),
                  pl.BlockSpec((hpb, rb, 1), lambda bi, h: (h, 0, 0)),
                  pl.BlockSpec((hpb, rb, 1), lambda bi, h: (h, 0, 0)),
                  pl.BlockSpec((hpb, 1, 1), lambda bi, h: (h, 0, 0)),
                  pl.BlockSpec((1, hpb * dv), lambda bi, h: (0, h))],
        out_specs=pl.BlockSpec((None, s, hpb * dv), lambda bi, h: (bi, 0, h)),
        out_shape=jax.ShapeDtypeStruct((b, s, n_heads * dv), BF16),
        scratch_shapes=[pltpu.VMEM((hpb, dk, dv), F32)],
        compiler_params=_params(2),
        name="retention",
    )(proj, proj, proj, proj, cos, sin, dec, qd, kd, bd, gn_w.reshape(1, -1))


def _fold_keys_kernel(wq_ref, k_ref, o_ref, wq16_ref):
    @pl.when(pl.program_id(1) == 0)
    def _():
        wq16_ref[...] = wq_ref[...].astype(BF16)

    o_ref[...] = lax.dot_general(wq16_ref[...], k_ref[...], _NT,
                                 preferred_element_type=F32).astype(o_ref.dtype)


def _fold_values_kernel(v_ref, wo_ref, o_ref, wo16_ref):
    @pl.when(pl.program_id(1) == 0)
    def _():
        wo16_ref[...] = wo_ref[...].astype(BF16)

    o_ref[...] = jnp.dot(v_ref[...], wo16_ref[...],
                         preferred_element_type=F32).astype(o_ref.dtype)


def _fold_memory(kv, wq, wo, n_heads):
    b, m, d2 = kv.shape
    d = d2 // 2
    dh = d // n_heads
    folded_k = pl.pallas_call(
        _fold_keys_kernel,
        grid=(n_heads, b),
        in_specs=[pl.BlockSpec((d, dh), lambda h, bi: (0, h)),
                  pl.BlockSpec((None, m, dh), lambda h, bi: (bi, 0, h))],
        out_specs=pl.BlockSpec((None, d, m), lambda h, bi: (bi, 0, h)),
        out_shape=jax.ShapeDtypeStruct((b, d, n_heads * m), BF16),
        scratch_shapes=[pltpu.VMEM((d, dh), BF16)],
        compiler_params=_params(2),
        name="fold_keys",
    )(wq, kv)
    folded_v = pl.pallas_call(
        _fold_values_kernel,
        grid=(n_heads, b),
        in_specs=[pl.BlockSpec((None, m, dh), lambda h, bi: (bi, 0, n_heads + h)),
                  pl.BlockSpec((dh, d), lambda h, bi: (h, 0))],
        out_specs=pl.BlockSpec((None, m, d), lambda h, bi: (bi, h, 0)),
        out_shape=jax.ShapeDtypeStruct((b, n_heads * m, d), BF16),
        scratch_shapes=[pltpu.VMEM((dh, d), BF16)],
        compiler_params=_params(2),
        name="fold_values",
    )(kv, wo)
    return folded_k, folded_v


def _mem_probs_kernel(x_ref, rstd_ref, fk_ref, o_ref, *, n_heads, scale):
    m = fk_ref.shape[1] // n_heads
    s = jnp.dot(x_ref[...], fk_ref[...], preferred_element_type=F32)
    s = s * (rstd_ref[...] * scale)
    for h in range(n_heads):
        sh = s[:, h * m:(h + 1) * m]
        mx = jnp.max(sh, axis=-1, keepdims=True)
        p = jnp.exp(sh - mx)
        o_ref[:, h * m:(h + 1) * m] = (p / jnp.sum(p, axis=-1, keepdims=True)).astype(o_ref.dtype)


def _mem_probs(x, rstd, folded_k, n_heads, tq=1024):
    b, s, d = x.shape
    hm = folded_k.shape[2]
    kern = functools.partial(_mem_probs_kernel, n_heads=n_heads, scale=(d // n_heads) ** -0.5)
    return pl.pallas_call(
        kern,
        grid=(b, s // tq),
        in_specs=[pl.BlockSpec((None, tq, d), lambda bi, i: (bi, i, 0)),
                  pl.BlockSpec((None, tq, 1), lambda bi, i: (bi, i, 0)),
                  pl.BlockSpec((None, d, hm), lambda bi, i: (bi, 0, 0))],
        out_specs=pl.BlockSpec((None, tq, hm), lambda bi, i: (bi, i, 0)),
        out_shape=jax.ShapeDtypeStruct((b, s, hm), BF16),
        compiler_params=_params(2),
        name="mem_probs",
    )(x, rstd, folded_k)


def _take_max(work, rows, exact):
    m = jnp.max(work, axis=0, keepdims=True)
    hit = work == m
    if not exact:
        return m, hit
    first = jnp.min(jnp.where(hit, rows, np.float32(work.shape[0])), axis=0, keepdims=True)
    return m, rows == first


def _top_rows(s, k, exact, want_rank=True):
    rows = lax.broadcasted_iota(jnp.int32, s.shape, 0).astype(F32)
    work = s
    rank = jnp.full(s.shape, NOT_RANKED, F32) if want_rank else None
    vals = []
    for r in range(k):
        m, sel = _take_max(work, rows, exact)
        if want_rank:
            rank = jnp.where(sel, np.float32(r), rank)
        work = jnp.where(sel, -jnp.inf, work)
        vals.append(m)
    taken = jnp.sum(jnp.where(work == -jnp.inf, 1.0, 0.0), axis=0, keepdims=True)
    return vals, rank, taken


def _candidate_cells(k):
    cells = []
    for ra in range(k):
        cells += [(ra, rb) for rb in range(k // (ra + 1))]
    single = [c for c in cells if k // (c[0] + 1) == 1]
    multi = [c for c in cells if c not in single]
    pad = (-len(multi)) % 8
    return multi + [None] * pad + single


def _route_kernel(q_ref, keys_ref, u_ref, v_ref, rank1_ref, e1_ref, cnt_ref, e0_ref, ub_ref, vb_ref):
    ub_ref[...] = u_ref[...].astype(ub_ref.dtype)
    vb_ref[...] = v_ref[...].astype(vb_ref.dtype)
    k = PEER_TOPK
    tl = q_ref.shape[0]
    cells = _candidate_cells(k)

    def route_group(g, exact):
        q = q_ref[g * LANES:(g + 1) * LANES, :]
        s0 = lax.dot_general(keys_ref[0].astype(BF16), q[:, :PEER_HALF], _NT,
                             preferred_element_type=F32)
        s1 = lax.dot_general(keys_ref[1].astype(BF16), q[:, PEER_HALF:], _NT,
                             preferred_element_type=F32)
        a, rank0, taken0 = _top_rows(s0, k, exact, want_rank=exact)
        b, rank1, taken1 = _top_rows(s1, k, exact)
        a_all, b_all = jnp.concatenate(a, axis=0), jnp.concatenate(b, axis=0)
        ea_all, eb_all = jnp.exp(a_all - a[0]), jnp.exp(b_all - b[0])
        neg = jnp.full_like(a[0], -jnp.inf)
        zero = jnp.zeros_like(a[0])
        n0, n1 = k, k // 2
        mid = cells[n0 + n1:len(cells) - k // 2]
        cand = jnp.concatenate(
            [a[0] + b_all, a[1] + b_all[:n1]]
            + [neg if c is None else a_all[c[0]:c[0] + 1] + b_all[c[1]:c[1] + 1] for c in mid]
            + [a_all[k // 2:] + b[0]], axis=0)
        wgt = jnp.concatenate(
            [ea_all[0:1] * eb_all, ea_all[1:2] * eb_all[:n1]]
            + [zero if c is None else ea_all[c[0]:c[0] + 1] * eb_all[c[1]:c[1] + 1] for c in mid]
            + [ea_all[k // 2:] * eb_all[0:1]], axis=0)
        rows = lax.broadcasted_iota(jnp.int32, cand.shape, 0).astype(F32)
        work = cand
        chosen = jnp.zeros(cand.shape, F32)
        for _ in range(k):
            _, sel = _take_max(work, rows, exact)
            chosen = jnp.where(sel, 1.0, chosen)
            work = jnp.where(sel, -jnp.inf, work)
        z = jnp.sum(chosen * wgt, axis=0, keepdims=True)
        cnt = jnp.zeros(s0.shape, F32)
        for ra in range(k):
            mine = [i for i, c in enumerate(cells) if c is not None and c[0] == ra]
            n_ra = jnp.sum(chosen[mine[0]:mine[-1] + 1], axis=0, keepdims=True)
            is_ra = rank0 == np.float32(ra) if exact else s0 == a[ra]
            cnt = jnp.where(is_ra, n_ra, cnt)
        lanes = slice(g * LANES, (g + 1) * LANES)
        rank1_ref[:, lanes] = rank1.astype(rank1_ref.dtype)
        e1_ref[:, lanes] = (jnp.exp(s1 - b[0]) / z).astype(e1_ref.dtype)
        cnt_ref[:, lanes] = cnt
        e0_ref[:, lanes] = jnp.exp(s0 - a[0])
        taken2 = jnp.sum(chosen, axis=0, keepdims=True)
        return (jnp.abs(taken0 - k) + jnp.abs(taken1 - k) + jnp.abs(taken2 - k))

    groups = range(tl // LANES)
    tie = sum(route_group(g, exact=False) for g in groups)

    @pl.when(jnp.max(tie) > 0.0)
    def _():
        for g in groups:
            route_group(g, exact=True)


def _peer_route(qp, subkeys, u, v, tl=1024):
    t = qp.shape[0]
    hp, _, nk, half = subkeys.shape
    n_exp, d = u.shape
    steps = (t // tl) * hp
    rows = n_exp // steps
    assert rows * steps == n_exp
    out = lambda dt: jax.ShapeDtypeStruct((hp, nk, t), dt)
    ospec = pl.BlockSpec((None, nk, tl), lambda i, h: (h, 0, i))
    wspec = pl.BlockSpec((rows, d), lambda i, h: (i * hp + h, 0))
    wout = jax.ShapeDtypeStruct((n_exp, d), BF16)
    return pl.pallas_call(
        _route_kernel,
        grid=(t // tl, hp),
        in_specs=[pl.BlockSpec((tl, 2 * half), lambda i, h: (i, h)),
                  pl.BlockSpec((None, 2, nk, half), lambda i, h: (h, 0, 0, 0)),
                  wspec, wspec],
        out_specs=[ospec] * 4 + [wspec] * 2,
        out_shape=[out(BF16), out(BF16), out(F32), out(F32), wout, wout],
        compiler_params=_params(2),
        name="peer_route",
    )(qp, subkeys, u, v)


def _peer_kernel(x_ref, rstd_ref, u_ref, v_ref, rank1_ref, e1_ref, cnt_ref, e0_ref, o_ref, acc_ref,
                 xt_ref, *stage_refs, sub, n_pieces):
    e = pl.program_id(1)
    n_heads, nk, tm = rank1_ref.shape
    te, d = u_ref.shape
    n_sub = te // sub
    rows_per_sub = sub // nk
    kc = d // n_pieces
    rows_per_step = te // nk
    row0 = (e % (SUBLANES // rows_per_step)) * rows_per_step
    act_refs, wt_refs = stage_refs[:n_sub], stage_refs[n_sub:]

    @pl.when(e == 0)
    def _():
        acc_ref[...] = jnp.zeros_like(acc_ref)
        xt_ref[...] = x_ref[...].T

    def up_piece(j, p):
        part = jnp.dot(u_ref[j * sub:(j + 1) * sub, p * kc:(p + 1) * kc],
                       xt_ref[p * kc:(p + 1) * kc, :], preferred_element_type=F32)
        if p == 0:
            act_refs[j][...] = part
        else:
            act_refs[j][...] += part

    def gate_piece(j, p):
        ii, g = divmod(p, n_pieces // rows_per_sub)
        tg = tm // (n_pieces // rows_per_sub)
        lanes = slice(g * tg, (g + 1) * tg)
        row = j * rows_per_sub + ii
        gate = None
        for h in range(n_heads):
            cnt = cnt_ref[h, pl.ds(row0 + row, 1), :][:, lanes].astype(BF16)
            e0 = e0_ref[h, pl.ds(row0 + row, 1), :][:, lanes].astype(BF16)
            zero = jnp.zeros((), BF16)
            term = e0 * jnp.where(rank1_ref[h, :, lanes] < cnt, e1_ref[h, :, lanes], zero)
            gate = term if gate is None else gate + term
        act = act_refs[j][ii * nk:(ii + 1) * nk, lanes] * rstd_ref[:, lanes]
        act = 0.5 * act * (1.0 + lax.erf(act * np.float32(1.0 / np.sqrt(2.0))))
        wt_refs[j][lanes, ii * nk:(ii + 1) * nk] = (gate * act.astype(BF16)).T

    def down_piece(j, p):
        cols = slice(p * kc, (p + 1) * kc)
        acc_ref[:, cols] += jnp.dot(wt_refs[j][...], v_ref[j * sub:(j + 1) * sub, cols],
                                    preferred_element_type=F32)

    for step in range(n_sub + 2):
        for p in range(n_pieces):
            if step < n_sub:
                up_piece(step, p)
            if 0 <= step - 1 < n_sub:
                gate_piece(step - 1, p)
            if 0 <= step - 2 < n_sub:
                down_piece(step - 2, p)

    @pl.when(e == pl.num_programs(1) - 1)
    def _():
        o_ref[...] = acc_ref[...].astype(o_ref.dtype)


def _peer_experts(x, rstd, u, v, rank1, e1, cnt, e0, tm=512, te=512, sub=256, n_pieces=4):
    t, d = x.shape
    n_exp = u.shape[0]
    hp, nk, _ = rank1.shape
    rows = te // nk
    tab = pl.BlockSpec((hp, nk, tm), lambda i, e: (0, 0, i))
    assert SUBLANES % rows == 0
    row_tab = pl.BlockSpec((hp, SUBLANES, tm), lambda i, e: (0, (e * rows) // SUBLANES, i))
    return pl.pallas_call(
        functools.partial(_peer_kernel, sub=sub, n_pieces=n_pieces),
        grid=(t // tm, n_exp // te),
        in_specs=[pl.BlockSpec((tm, d), lambda i, e: (i, 0)),
                  pl.BlockSpec((1, tm), lambda i, e: (0, i)),
                  pl.BlockSpec((te, d), lambda i, e: (e, 0)),
                  pl.BlockSpec((te, d), lambda i, e: (e, 0)),
                  tab, tab, row_tab, row_tab],
        out_specs=pl.BlockSpec((tm, d), lambda i, e: (i, 0)),
        out_shape=jax.ShapeDtypeStruct((t, d), BF16),
        scratch_shapes=([pltpu.VMEM((tm, d), F32), pltpu.VMEM((d, tm), BF16)]
                        + [pltpu.VMEM((sub, tm), F32)] * (te // sub)
                        + [pltpu.VMEM((tm, sub), BF16)] * (te // sub)),
        compiler_params=_params(2),
        name="peer_experts",
    )(x, rstd, u, v, rank1, e1, cnt, e0)


def kernel(x, mem, norm1_w, w_in, attn_rel_bias, ret_gn_w, w_out, norm2_w, mem_norm_w, xattn_wq,
           xattn_wkv, xattn_wo, norm3_w, peer_wq, peer_subkeys, peer_u, peer_v, final_norm_w):
    b, s, d = x.shape
    t = b * s
    depth = w_in.shape[0]
    n_attn_heads = attn_rel_bias.shape[1]
    attn_width = n_attn_heads * ATTN_HEAD_DIM
    n_ret_heads = ret_gn_w.shape[1] // RET_V_DIM
    h = x.reshape(t, d)
    xn = _rmsnorm(h, norm1_w[0], BF16)
    for l in range(depth):
        proj = _matmul(xn, w_in[l], BF16, tn=768, name="in_proj").reshape(b, s, -1)
        a_out = _chunk_attention(proj, attn_rel_bias[l], n_attn_heads)
        r_out = _retention(proj, ret_gn_w[l], n_ret_heads, 3 * attn_width)
        h, hg, rstd = _matmul((a_out.reshape(t, -1), r_out.reshape(t, -1)), w_out[l], F32,
                              residual=h, norm_gain=norm2_w[l], name="out_proj")

        mem_n = _rmsnorm(mem.reshape(-1, d), mem_norm_w[l], BF16)
        kv = _matmul(mem_n, xattn_wkv[l], BF16, name="mem_kv").reshape(b, -1, 2 * d)
        folded_k, folded_v = _fold_memory(kv, xattn_wq[l], xattn_wo[l], XATTN_HEADS)
        probs = _mem_probs(hg.reshape(b, s, d), rstd.reshape(b, s, 1), folded_k, XATTN_HEADS)
        h, hg, rstd = _matmul(probs.reshape(t, -1), folded_v, F32, residual=h,
                              norm_gain=norm3_w[l], tn=1024, name="xattn_o")

        qp = _matmul(hg, peer_wq[l], BF16, row_scale=rstd, name="peer_q")
        rank1, e1, cnt, e0, u_b, v_b = _peer_route(qp, peer_subkeys[l], peer_u[l], peer_v[l])
        delta = _peer_experts(hg, rstd.reshape(1, t), u_b, v_b, rank1, e1, cnt, e0)
        if l + 1 < depth:
            h = h + delta.astype(F32)
            xn = _rmsnorm(h, norm1_w[l + 1], BF16)
    return _add_rmsnorm(h, delta, final_norm_w, F32).reshape(b, s, d)
```

```python
import functools

import numpy as np
import jax
import jax.numpy as jnp
from jax import lax
from jax.experimental import pallas as pl
from jax.experimental.pallas import tpu as pltpu

F32 = jnp.float32
BF16 = jnp.bfloat16

CHUNK = 64
LEFT_CHUNKS = 8
LEFT = LEFT_CHUNKS * CHUNK
ATTN_HEAD_DIM = 128
MAX_REL_DIST = 256
RET_V_DIM = 256
RET_QK_DIM = 128
ROPE_BASE = 10000.0
XATTN_HEADS = 4
PEER_HALF = 128
PEER_TOPK = 16
EPS = 1e-6
NEG_INF = -1e30
NOT_RANKED = 1e9

VMEM_LIMIT_BYTES = 56 * 1024 * 1024
LANES = 128
SUBLANES = 8

_NT = (((1,), (1,)), ((), ()))
_TN = (((0,), (0,)), ((), ()))


def _params(n_grid_dims):
    return pltpu.CompilerParams(
        dimension_semantics=("arbitrary",) * n_grid_dims,
        vmem_limit_bytes=VMEM_LIMIT_BYTES)


def _rmsnorm_kernel(x_ref, w_ref, o_ref):
    x = x_ref[...]
    ms = jnp.mean(x * x, axis=-1, keepdims=True)
    o_ref[...] = (x * lax.rsqrt(ms + EPS) * w_ref[...]).astype(o_ref.dtype)


def _rmsnorm(x, w, out_dtype, rows=512):
    m, d = x.shape
    return pl.pallas_call(
        _rmsnorm_kernel,
        grid=(m // rows,),
        in_specs=[pl.BlockSpec((rows, d), lambda i: (i, 0)),
                  pl.BlockSpec((1, d), lambda i: (0, 0))],
        out_specs=pl.BlockSpec((rows, d), lambda i: (i, 0)),
        out_shape=jax.ShapeDtypeStruct((m, d), out_dtype),
        compiler_params=_params(1),
        name="rmsnorm",
    )(x, w.reshape(1, d))


def _add_rmsnorm_kernel(x_ref, y_ref, w_ref, o_ref):
    x = x_ref[...] + y_ref[...].astype(F32)
    ms = jnp.mean(x * x, axis=-1, keepdims=True)
    o_ref[...] = (x * lax.rsqrt(ms + EPS) * w_ref[...]).astype(o_ref.dtype)


def _add_rmsnorm(x, y, w, out_dtype, rows=512):
    m, d = x.shape
    return pl.pallas_call(
        _add_rmsnorm_kernel,
        grid=(m // rows,),
        in_specs=[pl.BlockSpec((rows, d), lambda i: (i, 0)),
                  pl.BlockSpec((rows, d), lambda i: (i, 0)),
                  pl.BlockSpec((1, d), lambda i: (0, 0))],
        out_specs=pl.BlockSpec((rows, d), lambda i: (i, 0)),
        out_shape=jax.ShapeDtypeStruct((m, d), out_dtype),
        compiler_params=_params(1),
        name="add_rmsnorm",
    )(x, y, w.reshape(1, d))


def _mm_kernel(*refs, n_lhs, has_residual, has_row_scale, emit_norm, n_cols):
    lhs_refs, w_ref = refs[:n_lhs], refs[n_lhs]
    pos = n_lhs + 1
    acc, k0 = None, 0
    for a_ref in lhs_refs:
        k1 = k0 + a_ref.shape[1]
        w_blk = w_ref[k0:k1, :]
        if w_blk.dtype != BF16:
            w_blk = w_blk.astype(BF16)
        part = jnp.dot(a_ref[...], w_blk, preferred_element_type=F32)
        acc = part if acc is None else acc + part
        k0 = k1
    if has_row_scale:
        acc = acc * refs[pos][...]
        pos += 1
    if has_residual:
        acc = refs[pos][...] + acc
        pos += 1
    if not emit_norm:
        o_ref = refs[pos]
        o_ref[...] = acc.astype(o_ref.dtype)
        return
    gain_ref, o_ref, scaled_ref, rstd_ref, ssq_ref = refs[pos:pos + 5]
    j = pl.program_id(1)
    o_ref[...] = acc.astype(o_ref.dtype)
    scaled_ref[...] = (acc * gain_ref[...]).astype(scaled_ref.dtype)
    row_ssq = jnp.sum(acc * acc, axis=-1, keepdims=True)

    @pl.when(j == 0)
    def _():
        ssq_ref[...] = row_ssq

    @pl.when(j > 0)
    def _():
        ssq_ref[...] += row_ssq

    @pl.when(j == pl.num_programs(1) - 1)
    def _():
        rstd_ref[...] = lax.rsqrt(ssq_ref[...] / n_cols + EPS)


def _matmul(lhs, w, out_dtype, residual=None, row_scale=None, norm_gain=None, tm=1024, tn=512,
            name="matmul"):
    lhs = lhs if isinstance(lhs, (tuple, list)) else (lhs,)
    m = lhs[0].shape[0]
    k, n = w.shape[-2:]
    assert sum(a.shape[1] for a in lhs) == k
    tm, tn = min(tm, m), min(tn, n)
    in_specs = [pl.BlockSpec((tm, a.shape[1]), lambda i, j: (i, 0)) for a in lhs]
    if w.ndim == 2:
        in_specs.append(pl.BlockSpec((k, tn), lambda i, j: (0, j)))
    else:
        tiles_per_group = m // w.shape[0] // tm
        in_specs.append(pl.BlockSpec((None, k, tn), lambda i, j: (i // tiles_per_group, 0, j)))
    args = [*lhs, w]
    tile = pl.BlockSpec((tm, tn), lambda i, j: (i, j))
    per_row = pl.BlockSpec((tm, 1), lambda i, j: (i, 0))
    if row_scale is not None:
        in_specs.append(per_row)
        args.append(row_scale)
    if residual is not None:
        in_specs.append(tile)
        args.append(residual)
    out_specs, out_shape, scratch = tile, jax.ShapeDtypeStruct((m, n), out_dtype), []
    if norm_gain is not None:
        in_specs.append(pl.BlockSpec((1, tn), lambda i, j: (0, j)))
        args.append(norm_gain.reshape(1, n))
        out_specs = [tile, tile, per_row]
        out_shape = [out_shape, jax.ShapeDtypeStruct((m, n), BF16),
                     jax.ShapeDtypeStruct((m, 1), F32)]
        scratch = [pltpu.VMEM((tm, 1), F32)]
    return pl.pallas_call(
        functools.partial(_mm_kernel, n_lhs=len(lhs), has_residual=residual is not None,
                          has_row_scale=row_scale is not None, emit_norm=norm_gain is not None,
                          n_cols=n),
        grid=(m // tm, n // tn),
        in_specs=in_specs,
        out_specs=out_specs,
        out_shape=out_shape,
        scratch_shapes=scratch,
        compiler_params=_params(2),
        name=name,
    )(*args)


def _attn_kernel(q_ref, k_ref, v_ref, base_ref, o_ref, bias_ref, *, qb, scale, hpb):
    seq = q_ref.shape[0]
    width = LEFT + qb
    dh = q_ref.shape[1] // hpb

    @pl.when(pl.program_id(1) == 0)
    def _():
        q_chunk = lax.broadcasted_iota(jnp.int32, (qb, width), 0) // CHUNK
        c_chunk = lax.broadcasted_iota(jnp.int32, (qb, width), 1) // CHUNK
        in_band = (c_chunk >= q_chunk) & (c_chunk <= q_chunk + LEFT_CHUNKS)
        for hh in range(hpb):
            toeplitz = pltpu.roll(jnp.broadcast_to(base_ref[hh], (qb, base_ref.shape[-1])),
                                  0, 1, stride=1, stride_axis=0)
            bias_ref[hh] = jnp.where(in_band, toeplitz[:, :width], NEG_INF)

    for i in range(seq // qb):
        q0 = i * qb
        k0 = max(0, q0 - LEFT)
        kw = q0 + qb - k0
        c0 = k0 - (q0 - LEFT)
        for hh in range(hpb):
            cols = slice(hh * dh, (hh + 1) * dh)
            s = lax.dot_general(q_ref[q0:q0 + qb, cols], k_ref[k0:k0 + kw, cols], _NT,
                                preferred_element_type=F32)
            s = s * scale + bias_ref[hh, :, c0:c0 + kw]
            m = jnp.max(s, axis=-1, keepdims=True)
            p = jnp.exp(s - m)
            l = jnp.sum(p, axis=-1, keepdims=True)
            o = jnp.dot(p.astype(BF16), v_ref[k0:k0 + kw, cols], preferred_element_type=F32)
            o_ref[q0:q0 + qb, cols] = (o / l).astype(o_ref.dtype)


def _attn_bias_base(rel_bias, qb):
    w = pl.next_power_of_2(LEFT + 2 * qb)
    j = jnp.arange(w)
    j = jnp.where(j < LEFT + qb, j, j - w)
    idx = jnp.clip(LEFT - j, -MAX_REL_DIST, MAX_REL_DIST) + MAX_REL_DIST
    return rel_bias[:, None, idx].astype(F32)


def _chunk_attention(proj, rel_bias, n_heads, qb=256, hpb=4):
    b, s, _ = proj.shape
    dh = ATTN_HEAD_DIM
    base = _attn_bias_base(rel_bias, qb)
    groups = n_heads // hpb
    kern = functools.partial(_attn_kernel, qb=qb, scale=dh ** -0.5, hpb=hpb)
    return pl.pallas_call(
        kern,
        grid=(groups, b),
        in_specs=[pl.BlockSpec((None, s, hpb * dh), lambda h, bi: (bi, 0, h)),
                  pl.BlockSpec((None, s, hpb * dh), lambda h, bi: (bi, 0, groups + h)),
                  pl.BlockSpec((None, s, hpb * dh), lambda h, bi: (bi, 0, 2 * groups + h)),
                  pl.BlockSpec((hpb, 1, base.shape[-1]), lambda h, bi: (h, 0, 0))],
        out_specs=pl.BlockSpec((None, s, hpb * dh), lambda h, bi: (bi, 0, h)),
        out_shape=jax.ShapeDtypeStruct((b, s, n_heads * dh), BF16),
        scratch_shapes=[pltpu.VMEM((hpb, qb, LEFT + qb), F32)],
        compiler_params=_params(2),
        name="chunk_attention",
    )(proj, proj, proj, base)


def _ret_kernel(q_ref, k_ref, v_ref, g_ref, cos_ref, sin_ref, dec_ref, qd_ref, kd_ref, bd_ref,
                gnw_ref, o_ref, state_ref, *, rb, scale, hpb):
    seq = q_ref.shape[0]
    state_ref[...] = jnp.zeros_like(state_ref)
    dk = q_ref.shape[1] // hpb
    dv = v_ref.shape[1] // hpb
    half = dk // 2

    def body(n, carry):
        r0 = pl.multiple_of(n * rb, rb)
        rows = pl.ds(r0, rb)
        cos = cos_ref[rows, :]
        sin = sin_ref[rows, :]
        for hh in range(hpb):
            qk_cols = slice(hh * dk, (hh + 1) * dk)
            v_cols = slice(hh * dv, (hh + 1) * dv)
            q = q_ref[rows, qk_cols].astype(F32)
            k = k_ref[rows, qk_cols].astype(F32)
            q = q * cos + pltpu.roll(q, half, 1) * sin
            k = (k * cos + pltpu.roll(k, half, 1) * sin) * scale
            v = v_ref[rows, v_cols]
            a = lax.dot_general(q.astype(BF16), k.astype(BF16), _NT, preferred_element_type=F32)
            a = a * dec_ref[hh]
            st = state_ref[hh]
            y = jnp.dot(a.astype(BF16), v, preferred_element_type=F32)
            y = y + jnp.dot((q * qd_ref[hh]).astype(BF16), st.astype(BF16),
                            preferred_element_type=F32)
            kd = (k * kd_ref[hh]).astype(BF16)
            kv = lax.dot_general(kd, v, _TN, preferred_element_type=F32)
            state_ref[hh] = st * bd_ref[hh] + kv
            mu = jnp.mean(y, axis=-1, keepdims=True)
            yc = y - mu
            var = jnp.mean(yc * yc, axis=-1, keepdims=True)
            yn = yc * lax.rsqrt(var + EPS) * gnw_ref[:, v_cols]
            g = g_ref[rows, v_cols].astype(F32)
            o_ref[rows, v_cols] = (g * (1.0 / (1.0 + jnp.exp(-g))) * yn).astype(o_ref.dtype)
        return carry

    lax.fori_loop(0, seq // rb, body, 0)


def _retention(proj, gn_w, n_heads, col0, rb=512, hpb=4):
    b, s, _ = proj.shape
    dk, dv = RET_QK_DIM, RET_V_DIM
    groups = n_heads // hpb
    qk0 = col0 // (hpb * dk)
    v0 = (col0 + 2 * n_heads * dk) // (hpb * dv)
    inv_freq = 1.0 / (ROPE_BASE ** (jnp.arange(0, dk, 2, dtype=F32) / dk))
    ang = jnp.arange(s, dtype=F32)[:, None] * inv_freq[None, :]
    cos = jnp.concatenate([jnp.cos(ang), jnp.cos(ang)], axis=-1)
    sin = jnp.concatenate([-jnp.sin(ang), jnp.sin(ang)], axis=-1)
    log_gamma = jnp.log1p(-jnp.power(2.0, -5.0 - jnp.arange(n_heads, dtype=F32)))
    pos = jnp.arange(rb, dtype=F32)
    chunk_of = jnp.arange(rb) // CHUNK
    causal = (chunk_of[None, :] <= chunk_of[:, None]).astype(F32)
    dec = jnp.exp(log_gamma[:, None, None] * jnp.abs(pos[:, None] - pos[None, :])) * causal[None]
    qd = jnp.exp(log_gamma[:, None] * (pos + 1.0))[:, :, None]
    kd = jnp.exp(log_gamma[:, None] * (rb - 1.0 - pos))[:, :, None]
    bd = jnp.exp(log_gamma * rb)[:, None, None]
    kern = functools.partial(_ret_kernel, rb=rb, scale=dk ** -0.5, hpb=hpb)
    return pl.pallas_call(
        kern,
        grid=(b, groups),
        in_specs=[pl.BlockSpec((None, s, hpb * dk), lambda bi, h: (bi, 0, qk0 + h)),
                  pl.BlockSpec((None, s, hpb * dk), lambda bi, h: (bi, 0, qk0 + groups + h)),
                  pl.BlockSpec((None, s, hpb * dv), lambda bi, h: (bi, 0, v0 + h)),
                  pl.BlockSpec((None, s, hpb * dv), lambda bi, h: (bi, 0, v0 + groups + h)),
                  pl.BlockSpec((s, dk), lambda bi, h: (0, 0)),
                  pl.BlockSpec((s, dk), lambda bi, h: (0, 0)),
                  pl.BlockSpec((hpb, rb, rb), lambda bi, h: (h, 0, 0)),
                  pl.BlockSpec((hpb, rb, 1), lambda bi, h: (h, 0, 0)),
                  pl.BlockSpec((hpb, rb, 1), lambda bi, h: (h, 0, 0)),
                  pl.BlockSpec((hpb, 1, 1), lambda bi, h: (h, 0, 0)),
                  pl.BlockSpec((1, hpb * dv), lambda bi, h: (0, h))],
        out_specs=pl.BlockSpec((None, s, hpb * dv), lambda bi, h: (bi, 0, h)),
        out_shape=jax.ShapeDtypeStruct((b, s, n_heads * dv), BF16),
        scratch_shapes=[pltpu.VMEM((hpb, dk, dv), F32)],
        compiler_params=_params(2),
        name="retention",
    )(proj, proj, proj, proj, cos, sin, dec, qd, kd, bd, gn_w.reshape(1, -1))


def _fold_keys_kernel(wq_ref, k_ref, o_ref, wq16_ref):
    @pl.when(pl.program_id(1) == 0)
    def _():
        wq16_ref[...] = wq_ref[...].astype(BF16)

    o_ref[...] = lax.dot_general(wq16_ref[...], k_ref[...], _NT,
                                 preferred_element_type=F32).astype(o_ref.dtype)


def _fold_values_kernel(v_ref, wo_ref, o_ref, wo16_ref):
    @pl.when(pl.program_id(1) == 0)
    def _():
        wo16_ref[...] = wo_ref[...].astype(BF16)

    o_ref[...] = jnp.dot(v_ref[...], wo16_ref[...],
                         preferred_element_type=F32).astype(o_ref.dtype)


def _fold_memory(kv, wq, wo, n_heads):
    b, m, d2 = kv.shape
    d = d2 // 2
    dh = d // n_heads
    folded_k = pl.pallas_call(
        _fold_keys_kernel,
        grid=(n_heads, b),
        in_specs=[pl.BlockSpec((d, dh), lambda h, bi: (0, h)),
                  pl.BlockSpec((None, m, dh), lambda h, bi: (bi, 0, h))],
        out_specs=pl.BlockSpec((None, d, m), lambda h, bi: (bi, 0, h)),
        out_shape=jax.ShapeDtypeStruct((b, d, n_heads * m), BF16),
        scratch_shapes=[pltpu.VMEM((d, dh), BF16)],
        compiler_params=_params(2),
        name="fold_keys",
    )(wq, kv)
    folded_v = pl.pallas_call(
        _fold_values_kernel,
        grid=(n_heads, b),
        in_specs=[pl.BlockSpec((None, m, dh), lambda h, bi: (bi, 0, n_heads + h)),
                  pl.BlockSpec((dh, d), lambda h, bi: (h, 0))],
        out_specs=pl.BlockSpec((None, m, d), lambda h, bi: (bi, h, 0)),
        out_shape=jax.ShapeDtypeStruct((b, n_heads * m, d), BF16),
        scratch_shapes=[pltpu.VMEM((dh, d), BF16)],
        compiler_params=_params(2),
        name="fold_values",
    )(kv, wo)
    return folded_k, folded_v


def _mem_attend_kernel(x_ref, rstd_ref, fk_ref, fv_ref, res_ref, gain_ref, o_ref, scaled_ref,
                       rstd_out_ref, probs_ref, ssq_ref, *, n_heads, scale, n_cols):
    j = pl.program_id(1)

    @pl.when(j == 0)
    def _():
        m = fk_ref.shape[1] // n_heads
        s = jnp.dot(x_ref[...], fk_ref[...], preferred_element_type=F32)
        s = s * (rstd_ref[...] * scale)
        for h in range(n_heads):
            sh = s[:, h * m:(h + 1) * m]
            mx = jnp.max(sh, axis=-1, keepdims=True)
            p = jnp.exp(sh - mx)
            probs_ref[:, h * m:(h + 1) * m] = (
                p / jnp.sum(p, axis=-1, keepdims=True)).astype(probs_ref.dtype)

    acc = res_ref[...] + jnp.dot(probs_ref[...], fv_ref[...], preferred_element_type=F32)
    o_ref[...] = acc
    scaled_ref[...] = (acc * gain_ref[...]).astype(scaled_ref.dtype)
    row_ssq = jnp.sum(acc * acc, axis=-1, keepdims=True)

    @pl.when(j == 0)
    def _():
        ssq_ref[...] = row_ssq

    @pl.when(j > 0)
    def _():
        ssq_ref[...] += row_ssq

    @pl.when(j == pl.num_programs(1) - 1)
    def _():
        rstd_out_ref[...] = lax.rsqrt(ssq_ref[...] / n_cols + EPS)


def _mem_attend(x, rstd, folded_k, folded_v, residual, norm_gain, n_heads, tm=512, tn=1024):
    t, d = x.shape
    nb, _, hm = folded_k.shape
    tiles_per_batch = t // nb // tm
    tile = pl.BlockSpec((tm, tn), lambda i, j: (i, j))
    per_row = pl.BlockSpec((tm, 1), lambda i, j: (i, 0))
    kern = functools.partial(_mem_attend_kernel, n_heads=n_heads, scale=(d // n_heads) ** -0.5,
                             n_cols=d)
    return pl.pallas_call(
        kern,
        grid=(t // tm, d // tn),
        in_specs=[pl.BlockSpec((tm, d), lambda i, j: (i, 0)),
                  per_row,
                  pl.BlockSpec((None, d, hm), lambda i, j: (i // tiles_per_batch, 0, 0)),
                  pl.BlockSpec((None, hm, tn), lambda i, j: (i // tiles_per_batch, 0, j)),
                  tile,
                  pl.BlockSpec((1, tn), lambda i, j: (0, j))],
        out_specs=[tile, tile, per_row],
        out_shape=[jax.ShapeDtypeStruct((t, d), F32), jax.ShapeDtypeStruct((t, d), BF16),
                   jax.ShapeDtypeStruct((t, 1), F32)],
        scratch_shapes=[pltpu.VMEM((tm, hm), BF16), pltpu.VMEM((tm, 1), F32)],
        compiler_params=_params(2),
        name="mem_attend",
    )(x, rstd, folded_k, folded_v, residual, norm_gain.reshape(1, d))


def _take_max(work, rows, exact):
    m = jnp.max(work, axis=0, keepdims=True)
    hit = work == m
    if not exact:
        return m, hit
    first = jnp.min(jnp.where(hit, rows, np.float32(work.shape[0])), axis=0, keepdims=True)
    return m, rows == first


def _top_rows(s, k, exact, want_rank=True):
    rows = lax.broadcasted_iota(jnp.int32, s.shape, 0).astype(F32)
    work = s
    rank = jnp.full(s.shape, NOT_RANKED, F32) if want_rank else None
    vals = []
    for r in range(k):
        m, sel = _take_max(work, rows, exact)
        if want_rank:
            rank = jnp.where(sel, np.float32(r), rank)
        work = jnp.where(sel, -jnp.inf, work)
        vals.append(m)
    taken = jnp.sum(jnp.where(work == -jnp.inf, 1.0, 0.0), axis=0, keepdims=True)
    return vals, rank, taken


def _candidate_cells(k):
    cells = []
    for ra in range(k):
        cells += [(ra, rb) for rb in range(k // (ra + 1))]
    single = [c for c in cells if k // (c[0] + 1) == 1]
    multi = [c for c in cells if c not in single]
    pad = (-len(multi)) % 8
    return multi + [None] * pad + single


def _route_kernel(q_ref, keys_ref, u_ref, v_ref, rank1_ref, e1_ref, cnt_ref, e0_ref, ub_ref, vb_ref):
    ub_ref[...] = u_ref[...].astype(ub_ref.dtype)
    vb_ref[...] = v_ref[...].astype(vb_ref.dtype)
    k = PEER_TOPK
    tl = q_ref.shape[0]
    cells = _candidate_cells(k)

    def route_group(g, exact):
        q = q_ref[g * LANES:(g + 1) * LANES, :]
        s0 = lax.dot_general(keys_ref[0].astype(BF16), q[:, :PEER_HALF], _NT,
                             preferred_element_type=F32)
        s1 = lax.dot_general(keys_ref[1].astype(BF16), q[:, PEER_HALF:], _NT,
                             preferred_element_type=F32)
        a, rank0, taken0 = _top_rows(s0, k, exact, want_rank=exact)
        b, rank1, taken1 = _top_rows(s1, k, exact)
        a_all, b_all = jnp.concatenate(a, axis=0), jnp.concatenate(b, axis=0)
        ea_all, eb_all = jnp.exp(a_all - a[0]), jnp.exp(b_all - b[0])
        neg = jnp.full_like(a[0], -jnp.inf)
        zero = jnp.zeros_like(a[0])
        n0, n1 = k, k // 2
        mid = cells[n0 + n1:len(cells) - k // 2]
        cand = jnp.concatenate(
            [a[0] + b_all, a[1] + b_all[:n1]]
            + [neg if c is None else a_all[c[0]:c[0] + 1] + b_all[c[1]:c[1] + 1] for c in mid]
            + [a_all[k // 2:] + b[0]], axis=0)
        wgt = jnp.concatenate(
            [ea_all[0:1] * eb_all, ea_all[1:2] * eb_all[:n1]]
            + [zero if c is None else ea_all[c[0]:c[0] + 1] * eb_all[c[1]:c[1] + 1] for c in mid]
            + [ea_all[k // 2:] * eb_all[0:1]], axis=0)
        rows = lax.broadcasted_iota(jnp.int32, cand.shape, 0).astype(F32)
        work = cand
        chosen = jnp.zeros(cand.shape, F32)
        for _ in range(k):
            _, sel = _take_max(work, rows, exact)
            chosen = jnp.where(sel, 1.0, chosen)
            work = jnp.where(sel, -jnp.inf, work)
        z = jnp.sum(chosen * wgt, axis=0, keepdims=True)
        cnt = jnp.zeros(s0.shape, F32)
        for ra in range(k):
            mine = [i for i, c in enumerate(cells) if c is not None and c[0] == ra]
            n_ra = jnp.sum(chosen[mine[0]:mine[-1] + 1], axis=0, keepdims=True)
            is_ra = rank0 == np.float32(ra) if exact else s0 == a[ra]
            cnt = jnp.where(is_ra, n_ra, cnt)
        lanes = slice(g * LANES, (g + 1) * LANES)
        rank1_ref[:, lanes] = rank1.astype(rank1_ref.dtype)
        e1_ref[:, lanes] = (jnp.exp(s1 - b[0]) / z).astype(e1_ref.dtype)
        cnt_ref[:, lanes] = cnt
        e0_ref[:, lanes] = jnp.exp(s0 - a[0])
        taken2 = jnp.sum(chosen, axis=0, keepdims=True)
        return (jnp.abs(taken0 - k) + jnp.abs(taken1 - k) + jnp.abs(taken2 - k))

    groups = range(tl // LANES)
    tie = sum(route_group(g, exact=False) for g in groups)

    @pl.when(jnp.max(tie) > 0.0)
    def _():
        for g in groups:
            route_group(g, exact=True)


def _peer_route(qp, subkeys, u, v, tl=1024):
    t = qp.shape[0]
    hp, _, nk, half = subkeys.shape
    n_exp, d = u.shape
    steps = (t // tl) * hp
    rows = n_exp // steps
    assert rows * steps == n_exp
    out = lambda dt: jax.ShapeDtypeStruct((hp, nk, t), dt)
    ospec = pl.BlockSpec((None, nk, tl), lambda i, h: (h, 0, i))
    wspec = pl.BlockSpec((rows, d), lambda i, h: (i * hp + h, 0))
    wout = jax.ShapeDtypeStruct((n_exp, d), BF16)
    return pl.pallas_call(
        _route_kernel,
        grid=(t // tl, hp),
        in_specs=[pl.BlockSpec((tl, 2 * half), lambda i, h: (i, h)),
                  pl.BlockSpec((None, 2, nk, half), lambda i, h: (h, 0, 0, 0)),
                  wspec, wspec],
        out_specs=[ospec] * 4 + [wspec] * 2,
        out_shape=[out(BF16), out(BF16), out(F32), out(F32), wout, wout],
        compiler_params=_params(2),
        name="peer_route",
    )(qp, subkeys, u, v)


def _peer_kernel(x_ref, rstd_ref, u_ref, v_ref, rank1_ref, e1_ref, cnt_ref, e0_ref, o_ref, acc_ref,
                 xt_ref, *stage_refs, sub, n_pieces):
    e = pl.program_id(1)
    n_heads, nk, tm = rank1_ref.shape
    te, d = u_ref.shape
    n_sub = te // sub
    rows_per_sub = sub // nk
    kc = d // n_pieces
    rows_per_step = te // nk
    row0 = (e % (SUBLANES // rows_per_step)) * rows_per_step
    act_refs, wt_refs = stage_refs[:n_sub], stage_refs[n_sub:]

    @pl.when(e == 0)
    def _():
        acc_ref[...] = jnp.zeros_like(acc_ref)
        xt_ref[...] = x_ref[...].T

    def up_piece(j, p):
        part = jnp.dot(u_ref[j * sub:(j + 1) * sub, p * kc:(p + 1) * kc],
                       xt_ref[p * kc:(p + 1) * kc, :], preferred_element_type=F32)
        if p == 0:
            act_refs[j][...] = part
        else:
            act_refs[j][...] += part

    def gate_piece(j, p):
        ii, g = divmod(p, n_pieces // rows_per_sub)
        tg = tm // (n_pieces // rows_per_sub)
        lanes = slice(g * tg, (g + 1) * tg)
        row = j * rows_per_sub + ii
        gate = None
        for h in range(n_heads):
            cnt = cnt_ref[h, pl.ds(row0 + row, 1), :][:, lanes].astype(BF16)
            e0 = e0_ref[h, pl.ds(row0 + row, 1), :][:, lanes].astype(BF16)
            zero = jnp.zeros((), BF16)
            term = e0 * jnp.where(rank1_ref[h, :, lanes] < cnt, e1_ref[h, :, lanes], zero)
            gate = term if gate is None else gate + term
        act = act_refs[j][ii * nk:(ii + 1) * nk, lanes] * rstd_ref[:, lanes]
        act = 0.5 * act * (1.0 + lax.erf(act * np.float32(1.0 / np.sqrt(2.0))))
        wt_refs[j][lanes, ii * nk:(ii + 1) * nk] = (gate * act.astype(BF16)).T

    def down_piece(j, p):
        cols = slice(p * kc, (p + 1) * kc)
        acc_ref[:, cols] += jnp.dot(wt_refs[j][...], v_ref[j * sub:(j + 1) * sub, cols],
                                    preferred_element_type=F32)

    for step in range(n_sub + 2):
        for p in range(n_pieces):
            if step < n_sub:
                up_piece(step, p)
            if 0 <= step - 1 < n_sub:
                gate_piece(step - 1, p)
            if 0 <= step - 2 < n_sub:
                down_piece(step - 2, p)

    @pl.when(e == pl.num_programs(1) - 1)
    def _():
        o_ref[...] = acc_ref[...].astype(o_ref.dtype)


def _peer_experts(x, rstd, u, v, rank1, e1, cnt, e0, tm=512, te=512, sub=256, n_pieces=4):
    t, d = x.shape
    n_exp = u.shape[0]
    hp, nk, _ = rank1.shape
    rows = te // nk
    tab = pl.BlockSpec((hp, nk, tm), lambda i, e: (0, 0, i))
    assert SUBLANES % rows == 0
    row_tab = pl.BlockSpec((hp, SUBLANES, tm), lambda i, e: (0, (e * rows) // SUBLANES, i))
    return pl.pallas_call(
        functools.partial(_peer_kernel, sub=sub, n_pieces=n_pieces),
        grid=(t // tm, n_exp // te),
        in_specs=[pl.BlockSpec((tm, d), lambda i, e: (i, 0)),
                  pl.BlockSpec((1, tm), lambda i, e: (0, i)),
                  pl.BlockSpec((te, d), lambda i, e: (e, 0)),
                  pl.BlockSpec((te, d), lambda i, e: (e, 0)),
                  tab, tab, row_tab, row_tab],
        out_specs=pl.BlockSpec((tm, d), lambda i, e: (i, 0)),
        out_shape=jax.ShapeDtypeStruct((t, d), BF16),
        scratch_shapes=([pltpu.VMEM((tm, d), F32), pltpu.VMEM((d, tm), BF16)]
                        + [pltpu.VMEM((sub, tm), F32)] * (te // sub)
                        + [pltpu.VMEM((tm, sub), BF16)] * (te // sub)),
        compiler_params=_params(2),
        name="peer_experts",
    )(x, rstd, u, v, rank1, e1, cnt, e0)


def kernel(x, mem, norm1_w, w_in, attn_rel_bias, ret_gn_w, w_out, norm2_w, mem_norm_w, xattn_wq,
           xattn_wkv, xattn_wo, norm3_w, peer_wq, peer_subkeys, peer_u, peer_v, final_norm_w):
    b, s, d = x.shape
    t = b * s
    depth = w_in.shape[0]
    n_attn_heads = attn_rel_bias.shape[1]
    attn_width = n_attn_heads * ATTN_HEAD_DIM
    n_ret_heads = ret_gn_w.shape[1] // RET_V_DIM
    h = x.reshape(t, d)
    xn = _rmsnorm(h, norm1_w[0], BF16)
    for l in range(depth):
        proj = _matmul(xn, w_in[l], BF16, tn=768, name="in_proj").reshape(b, s, -1)
        a_out = _chunk_attention(proj, attn_rel_bias[l], n_attn_heads)
        r_out = _retention(proj, ret_gn_w[l], n_ret_heads, 3 * attn_width)
        h, hg, rstd = _matmul((a_out.reshape(t, -1), r_out.reshape(t, -1)), w_out[l], F32,
                              residual=h, norm_gain=norm2_w[l], name="out_proj")

        mem_n = _rmsnorm(mem.reshape(-1, d), mem_norm_w[l], BF16)
        kv = _matmul(mem_n, xattn_wkv[l], BF16, name="mem_kv").reshape(b, -1, 2 * d)
        folded_k, folded_v = _fold_memory(kv, xattn_wq[l], xattn_wo[l], XATTN_HEADS)
        h, hg, rstd = _mem_attend(hg, rstd, folded_k, folded_v, h, norm3_w[l], XATTN_HEADS)

        qp = _matmul(hg, peer_wq[l], BF16, row_scale=rstd, name="peer_q")
        rank1, e1, cnt, e0, u_b, v_b = _peer_route(qp, peer_subkeys[l], peer_u[l], peer_v[l])
        delta = _peer_experts(hg, rstd.reshape(1, t), u_b, v_b, rank1, e1, cnt, e0)
        if l + 1 < depth:
            h = h + delta.astype(F32)
            xn = _rmsnorm(h, norm1_w[l + 1], BF16)
    return _add_rmsnorm(h, delta, final_norm_w, F32).reshape(b, s, d)
```
